```python
import math
import jax, jax.numpy as jnp
from jax import lax
import numpy as np

D_MODEL = 1024
BATCH = 4
SEQ = 8192
DEPTH = 1

MOBA_HEADS = 8
MOBA_HEAD_DIM = 64
MOBA_BLOCK = 256
MOBA_TOPK = 3
MLA_HEADS = 8
MLA_Q_RANK = 256
MLA_KV_RANK = 128
MLA_NOPE_DIM = 64
MLA_ROPE_DIM = 32
MLA_V_DIM = 64
ROPE_THETA = 10000.0
REL_BUCKETS = 32
REL_MAX_DIST = 128
D_FF = 2816
CONV_WIDTH = 3
N_BRANCH = 2
Q_BLOCK = 128
EPS = 1e-6

MOBA_WIDTH = MOBA_HEADS * MOBA_HEAD_DIM
MLA_QK_DIM = MLA_NOPE_DIM + MLA_ROPE_DIM
MLA_WIDTH = MLA_HEADS * MLA_V_DIM
IN_COLS = 3 * MOBA_WIDTH + MLA_Q_RANK + MLA_KV_RANK + MLA_ROPE_DIM + N_BRANCH * D_MODEL
IN_SPLITS = [MOBA_WIDTH, 2 * MOBA_WIDTH, 3 * MOBA_WIDTH,
             3 * MOBA_WIDTH + MLA_Q_RANK,
             3 * MOBA_WIDTH + MLA_Q_RANK + MLA_KV_RANK,
             3 * MOBA_WIDTH + MLA_Q_RANK + MLA_KV_RANK + MLA_ROPE_DIM]

kernel_name = "hybrid_moba_mla_gated_convffn"


def rms_norm(x, g):
    x32 = x.astype(jnp.float32)
    y = x32 * lax.rsqrt(jnp.mean(x32 * x32, axis=-1, keepdims=True) + EPS)
    return (y * g.astype(jnp.float32)).astype(x.dtype)


def t5_bucket(rel):
    n = jnp.maximum(rel, 0)
    max_exact = REL_BUCKETS // 2
    nf = jnp.maximum(n, 1).astype(jnp.float32)
    large = max_exact + (jnp.log(nf / max_exact) / math.log(REL_MAX_DIST / max_exact)
                         * (REL_BUCKETS - max_exact)).astype(jnp.int32)
    large = jnp.minimum(large, REL_BUCKETS - 1)
    return jnp.where(n < max_exact, n, large)


def rope_tables(seq, dim, dtype):
    inv_freq = ROPE_THETA ** (-jnp.arange(0, dim, 2, dtype=jnp.float32) / dim)
    ang = jnp.arange(seq, dtype=jnp.float32)[:, None] * inv_freq[None, :]
    return jnp.cos(ang).astype(dtype), jnp.sin(ang).astype(dtype)


def apply_rope(x, cos, sin):
    half = x.shape[-1] // 2
    x1, x2 = x[..., :half], x[..., half:]
    return jnp.concatenate([x1 * cos - x2 * sin, x2 * cos + x1 * sin], axis=-1)


def to_heads(t, n_heads):
    b, s, _ = t.shape
    return t.reshape(b, s, n_heads, -1).transpose(0, 2, 1, 3)


def merge_heads(t):
    b, h, s, d = t.shape
    return t.transpose(0, 2, 1, 3).reshape(b, s, h * d)


def moba_attention(q, k, v, rel_bias):
    b, h, s, dh = q.shape
    nb = -(-s // MOBA_BLOCK)
    pad = nb * MOBA_BLOCK - s
    k_p = jnp.pad(k, ((0, 0), (0, 0), (0, pad), (0, 0)))
    v_p = jnp.pad(v, ((0, 0), (0, 0), (0, pad), (0, 0)))
    k_blocks = k_p.reshape(b, h, nb, MOBA_BLOCK, dh)
    v_blocks = v_p.reshape(b, h, nb, MOBA_BLOCK, dh)
    k_mean = jnp.mean(k_blocks, axis=3)

    pos = jnp.arange(s, dtype=jnp.int32)
    q_blk = pos // MOBA_BLOCK
    gate = jnp.einsum('bhsd,bhnd->bhsn', q, k_mean).astype(jnp.float32)
    past = jnp.arange(nb, dtype=jnp.int32)[None, :] < q_blk[:, None]
    gate = jnp.where(past, gate, -jnp.inf)
    kk = min(MOBA_TOPK, nb)
    _, idx = lax.top_k(gate, kk)

    nc = s // Q_BLOCK
    q_c = q.reshape(b, h, nc, Q_BLOCK, dh).transpose(2, 0, 1, 3, 4)
    idx_c = idx.reshape(b, h, nc, Q_BLOCK, kk).transpose(2, 0, 1, 3, 4)
    bias_h = rel_bias.T
    bi = jnp.arange(b)[:, None, None, None]
    hi = jnp.arange(h)[None, :, None, None]
    offs = jnp.arange(MOBA_BLOCK, dtype=jnp.int32)
    scale = MOBA_HEAD_DIM ** -0.5

    def chunk(args):
        qc, ic, c = args
        q_pos = c * Q_BLOCK + jnp.arange(Q_BLOCK, dtype=jnp.int32)
        qb = (c * Q_BLOCK) // MOBA_BLOCK
        kb = k_blocks[bi, hi, ic]
        vb = v_blocks[bi, hi, ic]
        k_pos_sel = ic[..., None] * MOBA_BLOCK + offs
        rel_sel = q_pos[:, None, None] - k_pos_sel
        s_sel = (jnp.einsum('bhqd,bhqjkd->bhqjk', qc, kb).astype(jnp.float32) * scale
                 + bias_h[hi[..., None], t5_bucket(rel_sel)].astype(jnp.float32))
        s_sel = jnp.where((ic < qb)[..., None], s_sel, -jnp.inf)
        s_sel = s_sel.reshape(b, h, Q_BLOCK, kk * MOBA_BLOCK)
        start = qb * MOBA_BLOCK
        k_own = lax.dynamic_slice_in_dim(k_p, start, MOBA_BLOCK, axis=2)
        v_own = lax.dynamic_slice_in_dim(v_p, start, MOBA_BLOCK, axis=2)
        rel_own = q_pos[:, None] - (start + offs)[None, :]
        s_own = (jnp.einsum('bhqd,bhkd->bhqk', qc, k_own).astype(jnp.float32) * scale
                 + bias_h[:, t5_bucket(rel_own)].astype(jnp.float32))
        s_own = jnp.where(rel_own >= 0, s_own, -jnp.inf)
        p = jax.nn.softmax(jnp.concatenate([s_sel, s_own], axis=-1), axis=-1).astype(v.dtype)
        p_sel = p[..., :kk * MOBA_BLOCK].reshape(b, h, Q_BLOCK, kk, MOBA_BLOCK)
        p_own = p[..., kk * MOBA_BLOCK:]
        return (jnp.einsum('bhqjk,bhqjkd->bhqd', p_sel, vb)
                + jnp.einsum('bhqk,bhkd->bhqd', p_own, v_own))

    out = lax.map(chunk, (q_c, idx_c, jnp.arange(nc, dtype=jnp.int32)))
    return out.transpose(1, 2, 0, 3, 4).reshape(b, h, s, dh)


def mla_attention(q_nope, q_rope, k_nope, k_rope, v):
    b, h, s, _ = q_nope.shape
    nc = s // Q_BLOCK
    qn_c = q_nope.reshape(b, h, nc, Q_BLOCK, -1).transpose(2, 0, 1, 3, 4)
    qr_c = q_rope.reshape(b, h, nc, Q_BLOCK, -1).transpose(2, 0, 1, 3, 4)
    k_pos = jnp.arange(s, dtype=jnp.int32)
    scale = MLA_QK_DIM ** -0.5

    def chunk(args):
        qn, qr, c = args
        q_pos = c * Q_BLOCK + jnp.arange(Q_BLOCK, dtype=jnp.int32)
        sc = (jnp.einsum('bhqd,bhkd->bhqk', qn, k_nope)
              + jnp.einsum('bhqr,bkr->bhqk', qr, k_rope)).astype(jnp.float32) * scale
        sc = jnp.where(k_pos[None, :] <= q_pos[:, None], sc, -jnp.inf)
        p = jax.nn.softmax(sc, axis=-1).astype(v.dtype)
        return jnp.einsum('bhqk,bhkd->bhqd', p, v)

    out = lax.map(chunk, (qn_c, qr_c, jnp.arange(nc, dtype=jnp.int32)))
    return out.transpose(1, 2, 0, 3, 4).reshape(b, h, s, -1)


def causal_dwconv(u, w, bias):
    s = u.shape[1]
    up = jnp.pad(u, ((0, 0), (CONV_WIDTH - 1, 0), (0, 0)))
    y = bias
    for j in range(CONV_WIDTH):
        y = y + w[j] * up[:, j:j + s]
    return y


def setup_inputs(seed: int = 0) -> dict:
    key = jax.random.key(seed)
    ks = jax.random.split(key, 20)

    def nrm(k, shape, scale):
        return jax.random.normal(k, shape, jnp.float32) * scale

    L = DEPTH
    conv_w = nrm(ks[14], (L, CONV_WIDTH, 2 * D_FF), 0.2)
    conv_w = conv_w.at[:, CONV_WIDTH - 1].add(1.0)
    return {
        "x": nrm(ks[0], (BATCH, SEQ, D_MODEL), 1.0),
        "norm_attn_g": 1.0 + nrm(ks[1], (L, D_MODEL), 0.02),
        "w_in": nrm(ks[2], (L, D_MODEL, IN_COLS), D_MODEL ** -0.5),
        "b_gate": nrm(ks[3], (L, N_BRANCH * D_MODEL), 0.1),
        "q_norm_g": 1.0 + nrm(ks[4], (L, MLA_Q_RANK), 0.02),
        "w_uq": nrm(ks[5], (L, MLA_Q_RANK, MLA_HEADS * MLA_QK_DIM), MLA_Q_RANK ** -0.5),
        "kv_norm_g": 1.0 + nrm(ks[6], (L, MLA_KV_RANK), 0.02),
        "w_ukv": nrm(ks[7], (L, MLA_KV_RANK, MLA_HEADS * (MLA_NOPE_DIM + MLA_V_DIM)), MLA_KV_RANK ** -0.5),
        "rel_bias": nrm(ks[8], (REL_BUCKETS, MOBA_HEADS), 0.3),
        "w_branch_moba": nrm(ks[9], (L, MOBA_WIDTH, D_MODEL), MOBA_WIDTH ** -0.5),
        "w_branch_mla": nrm(ks[10], (L, MLA_WIDTH, D_MODEL), MLA_WIDTH ** -0.5),
        "w_out": nrm(ks[11], (L, D_MODEL, D_MODEL), D_MODEL ** -0.5),
        "norm_ffn_g": 1.0 + nrm(ks[12], (L, D_MODEL), 0.02),
        "w_up": nrm(ks[13], (L, D_MODEL, 2 * D_FF), D_MODEL ** -0.5),
        "conv_w": conv_w,
        "conv_b": nrm(ks[15], (L, 2 * D_FF), 0.02),
        "w_down": nrm(ks[16], (L, D_FF, D_MODEL), D_FF ** -0.5),
        "norm_final_g": 1.0 + nrm(ks[17], (D_MODEL,), 0.02),
    }


def reference(x, norm_attn_g, w_in, b_gate, q_norm_g, w_uq, kv_norm_g, w_ukv, rel_bias,
              w_branch_moba, w_branch_mla, w_out, norm_ffn_g, w_up, conv_w, conv_b, w_down,
              norm_final_g):
    b, s, _ = x.shape
    cos, sin = rope_tables(s, MLA_ROPE_DIM, x.dtype)
    h = x
    for l in range(DEPTH):
        xn = rms_norm(h, norm_attn_g[l])
        proj = xn @ w_in[l]
        q_a, k_a, v_a, c_q, c_kv, k_rope, gates = jnp.split(proj, IN_SPLITS, axis=-1)

        y_a = merge_heads(moba_attention(to_heads(q_a, MOBA_HEADS), to_heads(k_a, MOBA_HEADS),
                                         to_heads(v_a, MOBA_HEADS), rel_bias))

        q = (rms_norm(c_q, q_norm_g[l]) @ w_uq[l]).reshape(b, s, MLA_HEADS, MLA_QK_DIM)
        q_nope = q[..., :MLA_NOPE_DIM]
        q_rope = apply_rope(q[..., MLA_NOPE_DIM:], cos[:, None, :], sin[:, None, :])
        kv = (rms_norm(c_kv, kv_norm_g[l]) @ w_ukv[l]).reshape(b, s, MLA_HEADS, MLA_NOPE_DIM + MLA_V_DIM)
        k_nope = kv[..., :MLA_NOPE_DIM]
        v_b = kv[..., MLA_NOPE_DIM:]
        k_r = apply_rope(k_rope, cos, sin)
        y_b = merge_heads(mla_attention(q_nope.transpose(0, 2, 1, 3), q_rope.transpose(0, 2, 1, 3),
                                        k_nope.transpose(0, 2, 1, 3), k_r,
                                        v_b.transpose(0, 2, 1, 3)))

        g = jax.nn.sigmoid(gates + b_gate[l]).reshape(b, s, N_BRANCH, D_MODEL)
        mixed = g[..., 0, :] * (y_a @ w_branch_moba[l]) + g[..., 1, :] * (y_b @ w_branch_mla[l])
        h = h + mixed @ w_out[l]

        hn = rms_norm(h, norm_ffn_g[l])
        u = causal_dwconv(hn @ w_up[l], conv_w[l], conv_b[l])
        u_gate, u_val = u[..., :D_FF], u[..., D_FF:]
        h = h + (jax.nn.silu(u_gate) * u_val) @ w_down[l]
    return rms_norm(h, norm_final_g)
```

```python
import functools
import math

import numpy as np
import jax
import jax.numpy as jnp
from jax import lax
from jax.experimental import pallas as pl
from jax.experimental.pallas import tpu as pltpu

MOBA_HEADS = 8
MOBA_HEAD_DIM = 64
MOBA_BLOCK = 256
MOBA_TOPK = 3
MLA_HEADS = 8
MLA_Q_RANK = 256
MLA_KV_RANK = 128
MLA_NOPE_DIM = 64
MLA_ROPE_DIM = 32
MLA_V_DIM = 64
ROPE_THETA = 10000.0
REL_BUCKETS = 32
REL_MAX_DIST = 128
CONV_WIDTH = 3
N_BRANCH = 2
EPS = 1e-6

MOBA_WIDTH = MOBA_HEADS * MOBA_HEAD_DIM
MLA_QK_DIM = MLA_NOPE_DIM + MLA_ROPE_DIM
MLA_WIDTH = MLA_HEADS * MLA_V_DIM

LANES = 128
SUBLANES = 8
VMEM_LIMIT_BYTES = 56 * 1024 * 1024

NEG = -1e30

F32 = jnp.float32
BF16 = jnp.bfloat16


def _dot(a, b):
    return jnp.dot(a, b, preferred_element_type=F32)


def _dot_nt(a, b):
    return lax.dot_general(a, b, (((1,), (1,)), ((), ())), preferred_element_type=F32)


def _rms(x, g):
    return x * lax.rsqrt(jnp.mean(x * x, axis=-1, keepdims=True) + EPS) * g


def _sigmoid(z):
    return 1.0 / (1.0 + jnp.exp(-z))


def _resident(shape):
    nd = len(shape)
    return pl.BlockSpec(shape, lambda *_: (0,) * nd, pipeline_mode=pl.Buffered(1))


_C_QA = 0
_C_KA = _C_QA + MOBA_WIDTH
_C_VA = _C_KA + MOBA_WIDTH
_C_CQ = _C_VA + MOBA_HEADS * LANES
_C_CKV = _C_CQ + MLA_Q_RANK
_C_KRA = _C_CKV + MLA_KV_RANK
_C_KRB = _C_KRA + LANES
_C_G = _C_KRB + LANES
_C_END = _C_G + N_BRANCH * 1024


def _ones_pattern(width):
    lane = lax.broadcasted_iota(jnp.int32, (1, width), 1) & (2 * LANES - 1)
    return jnp.where((lane == MLA_V_DIM) | (lane == LANES), 1.0, 0.0).astype(F32)


def _front_kernel(x_ref, gattn_ref, w1_ref, bg_ref, qng_ref, wuq_ref, kvng_ref, wkv_ref,
                  cos_ref, sin_ref,
                  qa_ref, ka_ref, kmean_ref, va_ref, qm_ref, km_ref, vm_ref, g_ref,
                  *, d_model, mla_scale):
    tm = x_ref.shape[0]
    hw = MLA_HEADS * LANES
    xn = _rms(x_ref[...], gattn_ref[...]).astype(BF16)

    def proj(a, b):
        return _dot(xn, w1_ref[:, a:b])

    qa_ref[...] = (proj(_C_QA, _C_KA) * (MOBA_HEAD_DIM ** -0.5)).astype(BF16)
    ka = proj(_C_KA, _C_VA)
    ka_ref[...] = ka.astype(BF16)
    for i in range(tm // MOBA_BLOCK):
        kmean_ref[i] = jnp.mean(ka[i * MOBA_BLOCK:(i + 1) * MOBA_BLOCK], axis=0, keepdims=True)
    ones = _ones_pattern(hw)
    va_ref[...] = (proj(_C_VA, _C_CQ) + ones).astype(BF16)

    cosm = cos_ref[...]
    sinm = sin_ref[...]

    cqn = _rms(proj(_C_CQ, _C_CKV), qng_ref[...]).astype(BF16)
    qq = _dot(cqn, wuq_ref[...])
    for h in range(MLA_HEADS):
        a = qq[:, h * LANES:(h + 1) * LANES]
        b = qq[:, hw + h * LANES:hw + (h + 1) * LANES]
        qm_ref[:, h * LANES:(h + 1) * LANES] = ((a * cosm + b * sinm) * mla_scale).astype(BF16)

    ckvn = _rms(proj(_C_CKV, _C_KRA), kvng_ref[...]).astype(BF16)
    kv = _dot(ckvn, wkv_ref[...])
    kr = proj(_C_KRA, _C_KRB) * cosm + proj(_C_KRB, _C_G) * sinm
    for h in range(MLA_HEADS):
        km_ref[:, h * LANES:(h + 1) * LANES] = (kv[:, h * LANES:(h + 1) * LANES] + kr).astype(BF16)
    vm_ref[...] = (kv[:, hw:] + ones).astype(BF16)

    g_ref[...] = _sigmoid(proj(_C_G, _C_END) + bg_ref[...]).astype(BF16)


def _front(x2, gattn, w1, bg, qng, wuq2, kvng, wkv2, cosm, sinm, *, seq, tm):
    n, d = x2.shape
    hw = MLA_HEADS * LANES
    nblk = n // MOBA_BLOCK
    steps_per_seq = seq // tm
    row = lambda i: (i, 0)
    out_shape = (
        jax.ShapeDtypeStruct((n, MOBA_WIDTH), BF16),
        jax.ShapeDtypeStruct((n, MOBA_WIDTH), BF16),
        jax.ShapeDtypeStruct((nblk, 1, MOBA_WIDTH), F32),
        jax.ShapeDtypeStruct((n, hw), BF16),
        jax.ShapeDtypeStruct((n, hw), BF16),
        jax.ShapeDtypeStruct((n, hw), BF16),
        jax.ShapeDtypeStruct((n, hw), BF16),
        jax.ShapeDtypeStruct((n, N_BRANCH * d), BF16),
    )
    in_specs = [
        pl.BlockSpec((tm, d), row),
        _resident(gattn.shape), _resident(w1.shape), _resident(bg.shape), _resident(qng.shape),
        _resident(wuq2.shape), _resident(kvng.shape), _resident(wkv2.shape),
        pl.BlockSpec((tm, LANES), lambda i: (i % steps_per_seq, 0)),
        pl.BlockSpec((tm, LANES), lambda i: (i % steps_per_seq, 0)),
    ]
    out_specs = (
        pl.BlockSpec((tm, MOBA_WIDTH), row),
        pl.BlockSpec((tm, MOBA_WIDTH), row),
        pl.BlockSpec((tm // MOBA_BLOCK, 1, MOBA_WIDTH), lambda i: (i, 0, 0)),
        pl.BlockSpec((tm, hw), row),
        pl.BlockSpec((tm, hw), row),
        pl.BlockSpec((tm, hw), row),
        pl.BlockSpec((tm, hw), row),
        pl.BlockSpec((tm, N_BRANCH * d), row),
    )
    return pl.pallas_call(
        functools.partial(_front_kernel, d_model=d, mla_scale=MLA_QK_DIM ** -0.5),
        grid=(n // tm,),
        in_specs=in_specs,
        out_specs=out_specs,
        out_shape=out_shape,
        compiler_params=pltpu.CompilerParams(
            dimension_semantics=("arbitrary",), vmem_limit_bytes=VMEM_LIMIT_BYTES),
        name="front",
    )(x2, gattn, w1, bg, qng, wuq2, kvng, wkv2, cosm, sinm)


def _moba_kernel(q_ref, k_ref, v_ref, kmean_ref, town_ref, tprev_ref, cfar_ref, o_ref,
                 m_s, acc_s):
    blk = MOBA_BLOCK
    qb = pl.program_id(2)
    nb = kmean_ref.shape[0]
    q2 = q_ref[...].astype(F32)
    lane = lax.broadcasted_iota(jnp.int32, q2.shape, 1)
    kmean = kmean_ref[...].astype(BF16)
    col = lax.broadcasted_iota(jnp.int32, (blk, nb), 1).astype(F32)
    outs = []
    for hh in range(2):
        in_head = (lane >= hh * MOBA_HEAD_DIM) & (lane < (hh + 1) * MOBA_HEAD_DIM)
        qh = jnp.where(in_head, q2, 0.0).astype(BF16)

        gate = jnp.where(col < qb.astype(F32), _dot_nt(qh, kmean), -jnp.inf)
        sel = jnp.zeros((blk, nb), F32)
        for _ in range(min(MOBA_TOPK, nb)):
            best = jnp.max(gate, axis=1, keepdims=True)
            first = jnp.min(jnp.where(gate == best, col, float(nb)), axis=1, keepdims=True)
            pick = col == first
            sel = jnp.where(pick, 1.0, sel)
            gate = jnp.where(pick, -jnp.inf, gate)
        sel = jnp.where(col < qb.astype(F32), sel, 0.0)

        def sel_col(j):
            return jnp.max(jnp.where(col == j.astype(F32), sel, 0.0), axis=1, keepdims=True) > 0.5

        def kv_block(j):
            start = pl.multiple_of(j * blk, blk)
            return (k_ref[pl.ds(start, blk), :],
                    v_ref[pl.ds(start, blk), hh * LANES:(hh + 1) * LANES])

        k_j, v_j = kv_block(qb)
        s = _dot_nt(qh, k_j) + town_ref[hh]
        m0 = jnp.max(s, axis=1, keepdims=True)
        m_s[...] = m0
        acc_s[...] = _dot(jnp.exp(s - m0).astype(BF16), v_j)

        def update(s, j, shift):
            _, v_j = kv_block(j)
            chosen = sel_col(j)
            m_old = m_s[...]
            m_new = jnp.maximum(m_old, jnp.where(chosen, jnp.max(s, axis=1, keepdims=True) + shift, NEG))
            p = jnp.exp(s - (m_new - shift)).astype(BF16)
            pv = _dot(p, v_j)
            acc_s[...] = jnp.exp(m_old - m_new) * acc_s[...] + jnp.where(chosen, pv, 0.0)
            m_s[...] = m_new

        @pl.when(qb >= 1)
        def _():
            k_j, _ = kv_block(qb - 1)
            update(_dot_nt(qh, k_j) + tprev_ref[hh], qb - 1, 0.0)

        cfar = cfar_ref[hh][:, 0:1]

        def far(j, carry):
            k_j, _ = kv_block(j)
            update(_dot_nt(qh, k_j), j, cfar)
            return carry

        lax.fori_loop(0, qb - 1, far, 0)
        outs.append(acc_s[...])

    inv0 = 1.0 / outs[0][:, MOBA_HEAD_DIM:MOBA_HEAD_DIM + 1]
    inv1 = 1.0 / outs[1][:, 0:1]
    o_ref[...] = jnp.where(lane < MOBA_HEAD_DIM, outs[0] * inv0, outs[1] * inv1).astype(o_ref.dtype)


def _moba(qa, ka, va, kmean, town, tprev, cfar, *, batch, seq):
    n = qa.shape[0]
    blk = MOBA_BLOCK
    nb = seq // blk
    pairs = MOBA_HEADS // 2
    return pl.pallas_call(
        _moba_kernel,
        grid=(batch, pairs, nb),
        in_specs=[
            pl.BlockSpec((blk, LANES), lambda b, p, i: (b * nb + i, p)),
            pl.BlockSpec((seq, LANES), lambda b, p, i: (b, p)),
            pl.BlockSpec((seq, 2 * LANES), lambda b, p, i: (b, p)),
            pl.BlockSpec((None, nb, LANES), lambda b, p, i: (b, 0, p)),
            pl.BlockSpec((2, blk, blk), lambda b, p, i: (p, 0, 0)),
            pl.BlockSpec((2, blk, blk), lambda b, p, i: (p, 0, 0)),
            pl.BlockSpec((2, 1, LANES), lambda b, p, i: (p, 0, 0)),
        ],
        out_specs=pl.BlockSpec((blk, LANES), lambda b, p, i: (b * nb + i, p)),
        out_shape=jax.ShapeDtypeStruct((n, MOBA_WIDTH), BF16),
        scratch_shapes=[pltpu.VMEM((blk, 1), F32), pltpu.VMEM((blk, LANES), F32)],
        compiler_params=pltpu.CompilerParams(
            dimension_semantics=("arbitrary", "arbitrary", "arbitrary"),
            vmem_limit_bytes=VMEM_LIMIT_BYTES),
        name="moba",
    )(qa, ka, va, kmean, town, tprev, cfar)


def _mla_kernel(q_ref, k_ref, v_ref, o_ref, *, tq):
    qi = pl.program_id(2)
    row = lax.broadcasted_iota(jnp.int32, (tq, tq), 0)
    colq = lax.broadcasted_iota(jnp.int32, (tq, tq), 1)
    lane = lax.broadcasted_iota(jnp.int32, (tq, LANES), 1)
    outs = []
    for hh in range(2):
        hs = slice(hh * LANES, (hh + 1) * LANES)
        qh = q_ref[:, hs]

        def kv_tile(j):
            start = pl.multiple_of(j * tq, tq)
            return k_ref[pl.ds(start, tq), hs], v_ref[pl.ds(start, tq), hs]

        k_j, v_j = kv_tile(qi)
        s = jnp.where(colq <= row, _dot_nt(qh, k_j), NEG)
        m0 = jnp.max(s, axis=1, keepdims=True)
        acc0 = _dot(jnp.exp(s - m0).astype(BF16), v_j)

        def body(j, carry):
            m_old, acc = carry
            k_j, v_j = kv_tile(j)
            s = _dot_nt(qh, k_j)
            m_new = jnp.maximum(m_old, jnp.max(s, axis=1, keepdims=True))
            p = jnp.exp(s - m_new).astype(BF16)
            return m_new, jnp.exp(m_old - m_new) * acc + _dot(p, v_j)

        _, acc = lax.fori_loop(0, qi, body, (m0, acc0))
        outs.append(acc)

    inv0 = 1.0 / outs[0][:, MLA_V_DIM:MLA_V_DIM + 1]
    inv1 = 1.0 / outs[1][:, 0:1]
    o_ref[...] = jnp.where(lane < MLA_V_DIM, outs[0] * inv0, outs[1] * inv1).astype(o_ref.dtype)


def _mla(qm, km, vm, *, batch, seq, tq):
    n = qm.shape[0]
    nq = seq // tq
    pairs = MLA_HEADS // 2
    return pl.pallas_call(
        functools.partial(_mla_kernel, tq=tq),
        grid=(batch, pairs, nq),
        in_specs=[
            pl.BlockSpec((tq, 2 * LANES), lambda b, p, i: (b * nq + i, p)),
            pl.BlockSpec((seq, 2 * LANES), lambda b, p, i: (b, p)),
            pl.BlockSpec((seq, 2 * LANES), lambda b, p, i: (b, p)),
        ],
        out_specs=pl.BlockSpec((tq, LANES), lambda b, p, i: (b * nq + i, p)),
        out_shape=jax.ShapeDtypeStruct((n, MLA_WIDTH), BF16),
        compiler_params=pltpu.CompilerParams(
            dimension_semantics=("arbitrary", "arbitrary", "arbitrary"),
            vmem_limit_bytes=VMEM_LIMIT_BYTES),
        name="mla",
    )(qm, km, vm)


def _back_kernel(x_ref, ya_ref, yb_ref, g_ref, wa_ref, wb_ref, wo_ref, gffn_ref, wup_ref,
                 cw_ref, cb_ref, wdn_ref, gfin_ref, o_ref,
                 carry_ref, ubuf_g, ubuf_v, acc_ref, *, steps_per_seq, d_ff, fc, final):
    tm, d = x_ref.shape
    halo = SUBLANES

    @pl.when(pl.program_id(0) % steps_per_seq == 0)
    def _():
        carry_ref[...] = jnp.zeros_like(carry_ref)

    g = g_ref[...].astype(F32)
    mixed = g[:, :d] * _dot(ya_ref[...], wa_ref[...]) + g[:, d:] * _dot(yb_ref[...], wb_ref[...])
    h1 = x_ref[...] + _dot(mixed.astype(BF16), wo_ref[...])
    hn = _rms(h1, gffn_ref[...]).astype(BF16)
    acc_ref[...] = jnp.zeros_like(acc_ref)

    def conv(col0, ubuf):
        cols = pl.ds(col0, fc)
        u = _dot(hn, wup_ref[:, cols])
        ubuf[0:halo, :] = carry_ref[:, cols]
        ubuf[halo:halo + tm, :] = u
        carry_ref[:, cols] = u[tm - halo:tm, :]
        w = cw_ref[:, cols]
        y = cb_ref[:, cols]
        for t in range(CONV_WIDTH - 1):
            off = halo - (CONV_WIDTH - 1) + t
            y = y + w[t:t + 1, :] * ubuf[off:off + tm, :]
        return y + w[CONV_WIDTH - 1:CONV_WIDTH, :] * u

    def chunk(c, carry):
        yg = conv(pl.multiple_of(c * fc, fc), ubuf_g)
        yv = conv(pl.multiple_of(d_ff + c * fc, fc), ubuf_v)
        act = (yg * _sigmoid(yg) * yv).astype(BF16)
        acc_ref[...] += _dot(act, wdn_ref[pl.ds(pl.multiple_of(c * fc, fc), fc), :])
        return carry

    lax.fori_loop(0, d_ff // fc, chunk, 0)
    h2 = h1 + acc_ref[...]
    o_ref[...] = _rms(h2, gfin_ref[...]) if final else h2


def _back(x2, ya, yb, g, wa, wb, wo, gffn, wup, cw, cb, wdn, gfin, *, seq, tm, fc, final):
    n, d = x2.shape
    d_ff = wdn.shape[0]
    row = lambda i: (i, 0)
    return pl.pallas_call(
        functools.partial(_back_kernel, steps_per_seq=seq // tm, d_ff=d_ff, fc=fc, final=final),
        grid=(n // tm,),
        in_specs=[
            pl.BlockSpec((tm, d), row),
            pl.BlockSpec((tm, ya.shape[1]), row),
            pl.BlockSpec((tm, yb.shape[1]), row),
            pl.BlockSpec((tm, g.shape[1]), row),
            _resident(wa.shape), _resident(wb.shape), _resident(wo.shape), _resident(gffn.shape),
            _resident(wup.shape), _resident(cw.shape), _resident(cb.shape), _resident(wdn.shape),
            _resident(gfin.shape),
        ],
        out_specs=pl.BlockSpec((tm, d), row),
        out_shape=jax.ShapeDtypeStruct((n, d), F32),
        scratch_shapes=[
            pltpu.VMEM((SUBLANES, 2 * d_ff), F32),
            pltpu.VMEM((SUBLANES + tm, fc), F32),
            pltpu.VMEM((SUBLANES + tm, fc), F32),
            pltpu.VMEM((tm, d), F32),
        ],
        compiler_params=pltpu.CompilerParams(
            dimension_semantics=("arbitrary",), vmem_limit_bytes=VMEM_LIMIT_BYTES),
        name="back",
    )(x2, ya, yb, g, wa, wb, wo, gffn, wup, cw, cb, wdn, gfin)


def _t5_bucket_np(rel):
    n = np.maximum(rel, 0)
    max_exact = REL_BUCKETS // 2
    nf = np.maximum(n, 1).astype(np.float32)
    large = max_exact + (np.log(nf / np.float32(max_exact)) / np.float32(math.log(REL_MAX_DIST / max_exact))
                         * np.float32(REL_BUCKETS - max_exact)).astype(np.int32)
    large = np.minimum(large, REL_BUCKETS - 1)
    return np.where(n < max_exact, n, large)


def _moba_bias_tables(rel_bias):
    blk = MOBA_BLOCK
    assert REL_MAX_DIST <= blk + 1
    off = np.arange(blk, dtype=np.int32)
    rel_own = off[:, None] - off[None, :]
    bias_h = rel_bias.T.astype(F32)
    town = jnp.where(jnp.asarray(rel_own >= 0), bias_h[:, _t5_bucket_np(rel_own)], NEG)
    tprev = bias_h[:, _t5_bucket_np(rel_own + blk)]
    cfar = bias_h[:, int(_t5_bucket_np(np.int32(blk + 1)))]
    cfar = jnp.broadcast_to(cfar[:, None, None], (MOBA_HEADS, 1, LANES))
    return town, tprev, cfar


def _pad_heads(w, n_heads, width, odd_shift):
    r = w.shape[0]
    w = w.reshape(r, n_heads, width)
    out = jnp.zeros((r, n_heads, LANES), w.dtype)
    out = out.at[:, :, :width].set(w)
    if odd_shift:
        shifted = jnp.zeros((r, n_heads, LANES), w.dtype).at[:, :, LANES - width:].set(w)
        odd = (jnp.arange(n_heads) % 2 == 1)[None, :, None]
        out = jnp.where(odd, shifted, out)
    return out.reshape(r, n_heads * LANES)


def _rope_slot(w_rope, swap):
    half = MLA_ROPE_DIM // 2
    if swap:
        w_rope = jnp.concatenate([w_rope[:, half:], w_rope[:, :half]], axis=1)
    r = w_rope.shape[0]
    return jnp.zeros((r, LANES), w_rope.dtype).at[:, MLA_NOPE_DIM:MLA_NOPE_DIM + MLA_ROPE_DIM].set(w_rope)


def _front_weights(w_in, w_uq, w_ukv):
    d = w_in.shape[0]
    s0 = MOBA_WIDTH
    s1 = 2 * MOBA_WIDTH
    s2 = 3 * MOBA_WIDTH
    s3 = s2 + MLA_Q_RANK
    s4 = s3 + MLA_KV_RANK
    s5 = s4 + MLA_ROPE_DIM
    w1 = jnp.concatenate([
        w_in[:, :s0], w_in[:, s0:s1],
        _pad_heads(w_in[:, s1:s2], MOBA_HEADS, MOBA_HEAD_DIM, True),
        w_in[:, s2:s3], w_in[:, s3:s4],
        _rope_slot(w_in[:, s4:s5], False), _rope_slot(w_in[:, s4:s5], True),
        w_in[:, s5:],
    ], axis=1).astype(BF16)
    assert w1.shape == (d, _C_END)

    r = w_uq.shape[0]
    uq = w_uq.reshape(r, MLA_HEADS, MLA_QK_DIM)
    nope, rope = uq[:, :, :MLA_NOPE_DIM], uq[:, :, MLA_NOPE_DIM:]
    half = MLA_ROPE_DIM // 2
    zpad = jnp.zeros((r, MLA_HEADS, LANES - MLA_QK_DIM), w_uq.dtype)
    direct = jnp.concatenate([nope, rope, zpad], axis=2)
    swapped = jnp.concatenate([jnp.zeros_like(nope), rope[:, :, half:], rope[:, :, :half], zpad], axis=2)
    wuq2 = jnp.concatenate([direct.reshape(r, -1), swapped.reshape(r, -1)], axis=1).astype(BF16)

    r = w_ukv.shape[0]
    ukv = w_ukv.reshape(r, MLA_HEADS, MLA_NOPE_DIM + MLA_V_DIM)
    wk = _pad_heads(ukv[:, :, :MLA_NOPE_DIM].reshape(r, -1), MLA_HEADS, MLA_NOPE_DIM, False)
    wv = _pad_heads(ukv[:, :, MLA_NOPE_DIM:].reshape(r, -1), MLA_HEADS, MLA_V_DIM, True)
    wkv2 = jnp.concatenate([wk, wv], axis=1).astype(BF16)
    return w1, wuq2, wkv2


def _rope_lane_tables(seq):
    dim = MLA_ROPE_DIM
    inv_freq = ROPE_THETA ** (-jnp.arange(0, dim, 2, dtype=F32) / dim)
    ang = jnp.arange(seq, dtype=F32)[:, None] * inv_freq[None, :]
    cos, sin = jnp.cos(ang), jnp.sin(ang)
    tail = jnp.zeros((seq, LANES - MLA_QK_DIM), F32)
    cosm = jnp.concatenate([jnp.ones((seq, MLA_NOPE_DIM), F32), cos, cos, tail], axis=1)
    sinm = jnp.concatenate([jnp.zeros((seq, MLA_NOPE_DIM), F32), -sin, sin, tail], axis=1)
    return cosm, sinm


def _tiles(seq):
    tm = 512 if seq % 512 == 0 else MOBA_BLOCK
    return tm, tm, 256


def kernel(x, norm_attn_g, w_in, b_gate, q_norm_g, w_uq, kv_norm_g, w_ukv, rel_bias,
           w_branch_moba, w_branch_mla, w_out, norm_ffn_g, w_up, conv_w, conv_b, w_down,
           norm_final_g):
    batch, seq, d = x.shape
    depth = w_in.shape[0]
    assert seq % MOBA_BLOCK == 0
    tm, tq, fc = _tiles(seq)
    d_ff = w_down.shape[1]
    assert d_ff % fc == 0
    n = batch * seq

    cosm, sinm = _rope_lane_tables(seq)
    town, tprev, cfar = _moba_bias_tables(rel_bias)
    row = lambda v: v.reshape(1, -1).astype(F32)

    h = x.reshape(n, d)
    for l in range(depth):
        w1, wuq2, wkv2 = _front_weights(w_in[l], w_uq[l], w_ukv[l])
        qa, ka, kmean, va, qm, km, vm, g = _front(
            h, row(norm_attn_g[l]), w1, row(b_gate[l]), row(q_norm_g[l]), wuq2,
            row(kv_norm_g[l]), wkv2, cosm, sinm, seq=seq, tm=tm)
        kmean = kmean.reshape(batch, seq // MOBA_BLOCK, MOBA_WIDTH)
        ya = _moba(qa, ka, va, kmean, town, tprev, cfar, batch=batch, seq=seq)
        yb = _mla(qm, km, vm, batch=batch, seq=seq, tq=tq)
        h = _back(h, ya, yb, g, w_branch_moba[l].astype(BF16), w_branch_mla[l].astype(BF16),
                  w_out[l].astype(BF16), row(norm_ffn_g[l]), w_up[l].astype(BF16),
                  conv_w[l].astype(F32), row(conv_b[l]), w_down[l].astype(BF16),
                  row(norm_final_g), seq=seq, tm=tm, fc=fc, final=(l == depth - 1))
    return h.reshape(batch, seq, d)
```

```python
import functools
import math

import numpy as np
import jax
import jax.numpy as jnp
from jax import lax
from jax.experimental import pallas as pl
from jax.experimental.pallas import tpu as pltpu

MOBA_HEADS = 8
MOBA_HEAD_DIM = 64
MOBA_BLOCK = 256
MOBA_TOPK = 3
MLA_HEADS = 8
MLA_Q_RANK = 256
MLA_KV_RANK = 128
MLA_NOPE_DIM = 64
MLA_ROPE_DIM = 32
MLA_V_DIM = 64
ROPE_THETA = 10000.0
REL_BUCKETS = 32
REL_MAX_DIST = 128
CONV_WIDTH = 3
N_BRANCH = 2
EPS = 1e-6

MOBA_WIDTH = MOBA_HEADS * MOBA_HEAD_DIM
MLA_QK_DIM = MLA_NOPE_DIM + MLA_ROPE_DIM
MLA_WIDTH = MLA_HEADS * MLA_V_DIM

LANES = 128
SUBLANES = 8
VMEM_LIMIT_BYTES = 56 * 1024 * 1024

NEG = -1e30

F32 = jnp.float32
BF16 = jnp.bfloat16


def _dot(a, b):
    return jnp.dot(a, b, preferred_element_type=F32)


def _dot_nt(a, b):
    return lax.dot_general(a, b, (((1,), (1,)), ((), ())), preferred_element_type=F32)


def _rms(x, g):
    return x * lax.rsqrt(jnp.mean(x * x, axis=-1, keepdims=True) + EPS) * g


def _sigmoid(z):
    return 1.0 / (1.0 + jnp.exp(-z))


def _resident(shape):
    nd = len(shape)
    return pl.BlockSpec(shape, lambda *_: (0,) * nd, pipeline_mode=pl.Buffered(1))


_C_QA = 0
_C_KA = _C_QA + MOBA_WIDTH
_C_CQ = _C_KA + MOBA_WIDTH
_C_CKV = _C_CQ + MLA_Q_RANK
_C_KRA = _C_CKV + MLA_KV_RANK
_C_KRB = _C_KRA + LANES
_C_G = _C_KRB + LANES
_C_END = _C_G + N_BRANCH * 1024

V_ROWS = LANES


def _ones_rows(shape):
    r = lax.broadcasted_iota(jnp.int32, shape, 0) & (V_ROWS - 1)
    return jnp.where(r == MLA_V_DIM, 1.0, 0.0).astype(F32)


def _front_kernel(x_ref, gattn_ref, w1_ref, wvt_ref, bg_ref, qng_ref, wuq_ref, kvng_ref, wkv_ref,
                  wkvt_ref, cos_ref, sin_ref,
                  qa_ref, ka_ref, kmean_ref, vat_ref, qm_ref, km_ref, vmt_ref, g_ref,
                  *, d_model, mla_scale):
    tm = x_ref.shape[0]
    hw = MLA_HEADS * LANES
    xn = _rms(x_ref[...], gattn_ref[...]).astype(BF16)

    def proj(a, b):
        return _dot(xn, w1_ref[:, a:b])

    qa_ref[...] = (proj(_C_QA, _C_KA) * (MOBA_HEAD_DIM ** -0.5)).astype(BF16)
    ka = proj(_C_KA, _C_CQ)
    ka_ref[...] = ka.astype(BF16)
    for i in range(tm // MOBA_BLOCK):
        kmean_ref[i] = jnp.mean(ka[i * MOBA_BLOCK:(i + 1) * MOBA_BLOCK], axis=0, keepdims=True)
    ones = _ones_rows(vat_ref.shape)
    vat_ref[...] = (_dot_nt(wvt_ref[...], xn) + ones).astype(BF16)

    cosm = cos_ref[...]
    sinm = sin_ref[...]

    cqn = _rms(proj(_C_CQ, _C_CKV), qng_ref[...]).astype(BF16)
    qq = _dot(cqn, wuq_ref[...])
    for h in range(MLA_HEADS):
        a = qq[:, h * LANES:(h + 1) * LANES]
        b = qq[:, hw + h * LANES:hw + (h + 1) * LANES]
        qm_ref[:, h * LANES:(h + 1) * LANES] = ((a * cosm + b * sinm) * mla_scale).astype(BF16)

    ckvn = _rms(proj(_C_CKV, _C_KRA), kvng_ref[...]).astype(BF16)
    kv = _dot(ckvn, wkv_ref[...])
    kr = proj(_C_KRA, _C_KRB) * cosm + proj(_C_KRB, _C_G) * sinm
    for h in range(MLA_HEADS):
        km_ref[:, h * LANES:(h + 1) * LANES] = (kv[:, h * LANES:(h + 1) * LANES] + kr).astype(BF16)
    vmt_ref[...] = (_dot_nt(wkvt_ref[...], ckvn) + ones).astype(BF16)

    g_ref[...] = _sigmoid(proj(_C_G, _C_END) + bg_ref[...]).astype(BF16)


def _front(x2, gattn, w1, wvt, bg, qng, wuq2, kvng, wkv, wkvt, cosm, sinm, *, seq, tm):
    n, d = x2.shape
    batch = n // seq
    hw = MLA_HEADS * LANES
    nblk = n // MOBA_BLOCK
    steps_per_seq = seq // tm
    row = lambda i: (i, 0)
    tcol = lambda i: (i // steps_per_seq, 0, i % steps_per_seq)
    out_shape = (
        jax.ShapeDtypeStruct((n, MOBA_WIDTH), BF16),
        jax.ShapeDtypeStruct((n, MOBA_WIDTH), BF16),
        jax.ShapeDtypeStruct((nblk, 1, MOBA_WIDTH), F32),
        jax.ShapeDtypeStruct((batch, MOBA_HEADS * V_ROWS, seq), BF16),
        jax.ShapeDtypeStruct((n, hw), BF16),
        jax.ShapeDtypeStruct((n, hw), BF16),
        jax.ShapeDtypeStruct((batch, MLA_HEADS * V_ROWS, seq), BF16),
        jax.ShapeDtypeStruct((n, N_BRANCH * d), BF16),
    )
    in_specs = [
        pl.BlockSpec((tm, d), row),
        _resident(gattn.shape), _resident(w1.shape), _resident(wvt.shape), _resident(bg.shape),
        _resident(qng.shape), _resident(wuq2.shape), _resident(kvng.shape), _resident(wkv.shape),
        _resident(wkvt.shape),
        pl.BlockSpec((tm, LANES), lambda i: (i % steps_per_seq, 0)),
        pl.BlockSpec((tm, LANES), lambda i: (i % steps_per_seq, 0)),
    ]
    out_specs = (
        pl.BlockSpec((tm, MOBA_WIDTH), row),
        pl.BlockSpec((tm, MOBA_WIDTH), row),
        pl.BlockSpec((tm // MOBA_BLOCK, 1, MOBA_WIDTH), lambda i: (i, 0, 0)),
        pl.BlockSpec((None, MOBA_HEADS * V_ROWS, tm), tcol),
        pl.BlockSpec((tm, hw), row),
        pl.BlockSpec((tm, hw), row),
        pl.BlockSpec((None, MLA_HEADS * V_ROWS, tm), tcol),
        pl.BlockSpec((tm, N_BRANCH * d), row),
    )
    return pl.pallas_call(
        functools.partial(_front_kernel, d_model=d, mla_scale=MLA_QK_DIM ** -0.5),
        grid=(n // tm,),
        in_specs=in_specs,
        out_specs=out_specs,
        out_shape=out_shape,
        compiler_params=pltpu.CompilerParams(
            dimension_semantics=("arbitrary",), vmem_limit_bytes=VMEM_LIMIT_BYTES),
        name="front",
    )(x2, gattn, w1, wvt, bg, qng, wuq2, kvng, wkv, wkvt, cosm, sinm)


HEADS_PER_STEP = 2


def _softmax_pv(st, vt_blk):
    rm = jnp.max(st, axis=0, keepdims=True)
    pt = jnp.exp(st - rm).astype(BF16)
    return rm, _dot(vt_blk, pt)


def _block_partial(k_blk, q_h, vt_blk, bias_t=None, mask=None):
    st = _dot_nt(k_blk, q_h)
    if bias_t is not None:
        st = st + bias_t
    if mask is not None:
        st = jnp.where(mask, st, NEG)
    return _softmax_pv(st, vt_blk)


def _merge_partial(m_ref, acc_ref, rm, ot, chosen=None):
    m_old = m_ref[...]
    m_new = jnp.maximum(m_old, rm if chosen is None else jnp.where(chosen, rm, NEG))
    w = jnp.exp(rm - m_new)
    if chosen is not None:
        w = jnp.where(chosen, w, 0.0)
    acc_ref[...] = jnp.exp(m_old - m_new) * acc_ref[...] + w * ot
    m_ref[...] = m_new


def _normalised_heads(acc_s, v_dim):
    rows = []
    for hh in range(HEADS_PER_STEP):
        acc = acc_s[hh]
        rows.append(acc[0:v_dim, :] * (1.0 / acc[v_dim:v_dim + 1, :]))
    return jnp.concatenate(rows, axis=0)


def _moba_kernel(q_ref, k_ref, vt_ref, kmean_ref, town_ref, tprev_ref, cfar_ref, o_ref,
                 m_s, acc_s, sel_s, *, unroll):
    blk = MOBA_BLOCK
    qb = pl.program_id(2)
    nb = kmean_ref.shape[0]
    q2 = q_ref[...].astype(F32)
    lane = lax.broadcasted_iota(jnp.int32, q2.shape, 1)
    kmean = kmean_ref[...].astype(BF16)
    blk_id = lax.broadcasted_iota(jnp.int32, (nb, blk), 0).astype(F32)
    qbf = qb.astype(F32)

    def kv_block(hh, j):
        start = pl.multiple_of(j * blk, blk)
        return (k_ref[pl.ds(start, blk), :],
                vt_ref[hh * V_ROWS:(hh + 1) * V_ROWS, pl.ds(start, blk)])

    q_heads, prev_chosen = [], []
    for hh in range(HEADS_PER_STEP):
        in_head = (lane >= hh * MOBA_HEAD_DIM) & (lane < (hh + 1) * MOBA_HEAD_DIM)
        qh = jnp.where(in_head, q2, 0.0).astype(BF16)
        q_heads.append(qh)

        gate = jnp.where(blk_id < qbf, _dot_nt(kmean, qh), -jnp.inf)
        sel = jnp.zeros((nb, blk), F32)
        for _ in range(min(MOBA_TOPK, nb)):
            best = jnp.max(gate, axis=0, keepdims=True)
            first = jnp.min(jnp.where(gate == best, blk_id, float(nb)), axis=0, keepdims=True)
            pick = blk_id == first
            sel = jnp.where(pick, 1.0, sel)
            gate = jnp.where(pick, -jnp.inf, gate)
        prev_chosen.append(
            jnp.max(jnp.where(blk_id == qbf - 1.0, sel, 0.0), axis=0, keepdims=True) > 0.5)
        sel_s[hh] = jnp.where(blk_id < qbf - 1.0, sel, 0.0)

        k_j, vt_j = kv_block(hh, qb)
        rm, ot = _block_partial(k_j, qh, vt_j, bias_t=town_ref[hh])
        m_s[hh] = rm
        acc_s[hh] = ot

    @pl.when(qb >= 1)
    def _():
        parts = []
        for hh in range(HEADS_PER_STEP):
            k_j, vt_j = kv_block(hh, qb - 1)
            parts.append(_block_partial(k_j, q_heads[hh], vt_j, bias_t=tprev_ref[hh]))
        for hh, (rm, ot) in enumerate(parts):
            _merge_partial(m_s.at[hh], acc_s.at[hh], rm, ot, prev_chosen[hh])

    def far(g, carry):
        chains = [(hh, jnp.minimum(g * unroll + u, nb - 1))
                  for u in range(unroll) for hh in range(HEADS_PER_STEP)]
        scores = [_dot_nt(kv_block(hh, j)[0], q_heads[hh]) for hh, j in chains]
        parts = [_softmax_pv(st, kv_block(hh, j)[1]) for st, (hh, j) in zip(scores, chains)]
        for (hh, j), (rm, ot) in zip(chains, parts):
            chosen = sel_s[hh, pl.ds(j, 1), :] > 0.5
            _merge_partial(m_s.at[hh], acc_s.at[hh], rm + cfar_ref[hh][:, 0:1], ot, chosen)
        return carry

    lax.fori_loop(0, (qb - 1 + unroll - 1) // unroll, far, 0)
    o_ref[...] = _normalised_heads(acc_s, MOBA_HEAD_DIM).T.astype(o_ref.dtype)


def _moba(qa, ka, vat, kmean, town, tprev, cfar, *, batch, seq, unroll):
    n = qa.shape[0]
    blk = MOBA_BLOCK
    nb = seq // blk
    pairs = MOBA_HEADS // HEADS_PER_STEP
    return pl.pallas_call(
        functools.partial(_moba_kernel, unroll=unroll),
        grid=(batch, pairs, nb),
        in_specs=[
            pl.BlockSpec((blk, LANES), lambda b, p, i: (b * nb + i, p)),
            pl.BlockSpec((seq, LANES), lambda b, p, i: (b, p)),
            pl.BlockSpec((None, HEADS_PER_STEP * V_ROWS, seq), lambda b, p, i: (b, p, 0)),
            pl.BlockSpec((None, nb, LANES), lambda b, p, i: (b, 0, p)),
            pl.BlockSpec((HEADS_PER_STEP, blk, blk), lambda b, p, i: (p, 0, 0)),
            pl.BlockSpec((HEADS_PER_STEP, blk, blk), lambda b, p, i: (p, 0, 0)),
            pl.BlockSpec((HEADS_PER_STEP, 1, LANES), lambda b, p, i: (p, 0, 0)),
        ],
        out_specs=pl.BlockSpec((blk, LANES), lambda b, p, i: (b * nb + i, p)),
        out_shape=jax.ShapeDtypeStruct((n, MOBA_WIDTH), BF16),
        scratch_shapes=[
            pltpu.VMEM((HEADS_PER_STEP, 1, blk), F32),
            pltpu.VMEM((HEADS_PER_STEP, V_ROWS, blk), F32),
            pltpu.VMEM((HEADS_PER_STEP, nb, blk), F32),
        ],
        compiler_params=pltpu.CompilerParams(
            dimension_semantics=("arbitrary", "arbitrary", "arbitrary"),
            vmem_limit_bytes=VMEM_LIMIT_BYTES),
        name="moba",
    )(qa, ka, vat, kmean, town, tprev, cfar)


def _mla_kernel(q_ref, k_ref, vt_ref, o_ref, m_s, acc_s, *, tq, tk, unroll):
    qi = pl.program_id(2)
    n_diag = tq // tk
    key = lax.broadcasted_iota(jnp.int32, (tk, tq), 0)
    qry = lax.broadcasted_iota(jnp.int32, (tk, tq), 1)
    q_heads = [q_ref[:, hh * LANES:(hh + 1) * LANES] for hh in range(HEADS_PER_STEP)]

    def kv_tile(hh, j):
        start = pl.multiple_of(j * tk, tk)
        return (k_ref[pl.ds(start, tk), hh * LANES:(hh + 1) * LANES],
                vt_ref[hh * V_ROWS:(hh + 1) * V_ROWS, pl.ds(start, tk)])

    for d in range(n_diag):
        parts = []
        for hh in range(HEADS_PER_STEP):
            k_j, vt_j = kv_tile(hh, qi * n_diag + d)
            parts.append(_block_partial(k_j, q_heads[hh], vt_j, mask=key + d * tk <= qry))
        for hh, (rm, ot) in enumerate(parts):
            if d == 0:
                m_s[hh] = rm
                acc_s[hh] = ot
            else:
                _merge_partial(m_s.at[hh], acc_s.at[hh], rm, ot)

    def body(g, carry):
        chains = [(hh, g * unroll + u) for u in range(unroll) for hh in range(HEADS_PER_STEP)]
        scores = [_dot_nt(kv_tile(hh, j)[0], q_heads[hh]) for hh, j in chains]
        parts = [_softmax_pv(st, kv_tile(hh, j)[1]) for st, (hh, j) in zip(scores, chains)]
        for (hh, _), (rm, ot) in zip(chains, parts):
            _merge_partial(m_s.at[hh], acc_s.at[hh], rm, ot)
        return carry

    lax.fori_loop(0, (qi * n_diag) // unroll, body, 0)
    o_ref[...] = _normalised_heads(acc_s, MLA_V_DIM).T.astype(o_ref.dtype)


def _mla(qm, km, vmt, *, batch, seq, tq, tk, unroll):
    n = qm.shape[0]
    nq = seq // tq
    assert tq % tk == 0 and (tq // tk) % unroll == 0
    pairs = MLA_HEADS // HEADS_PER_STEP
    return pl.pallas_call(
        functools.partial(_mla_kernel, tq=tq, tk=tk, unroll=unroll),
        grid=(batch, pairs, nq),
        in_specs=[
            pl.BlockSpec((tq, HEADS_PER_STEP * LANES), lambda b, p, i: (b * nq + i, p)),
            pl.BlockSpec((seq, HEADS_PER_STEP * LANES), lambda b, p, i: (b, p)),
            pl.BlockSpec((None, HEADS_PER_STEP * V_ROWS, seq), lambda b, p, i: (b, p, 0)),
        ],
        out_specs=pl.BlockSpec((tq, HEADS_PER_STEP * MLA_V_DIM), lambda b, p, i: (b * nq + i, p)),
        out_shape=jax.ShapeDtypeStruct((n, MLA_WIDTH), BF16),
        scratch_shapes=[
            pltpu.VMEM((HEADS_PER_STEP, 1, tq), F32),
            pltpu.VMEM((HEADS_PER_STEP, V_ROWS, tq), F32),
        ],
        compiler_params=pltpu.CompilerParams(
            dimension_semantics=("arbitrary", "arbitrary", "arbitrary"),
            vmem_limit_bytes=VMEM_LIMIT_BYTES),
        name="mla",
    )(qm, km, vmt)


def _back_kernel(x_ref, ya_ref, yb_ref, g_ref, wa_ref, wb_ref, wo_ref, gffn_ref, wup_ref,
                 cw_ref, cb_ref, wdn_ref, gfin_ref, o_ref,
                 carry_ref, ubuf_g, ubuf_v, acc_ref, *, steps_per_seq, d_ff, fc, final):
    tm, d = x_ref.shape
    halo = SUBLANES

    @pl.when(pl.program_id(0) % steps_per_seq == 0)
    def _():
        carry_ref[...] = jnp.zeros_like(carry_ref)

    g = g_ref[...].astype(F32)
    mixed = g[:, :d] * _dot(ya_ref[...], wa_ref[...]) + g[:, d:] * _dot(yb_ref[...], wb_ref[...])
    h1 = x_ref[...] + _dot(mixed.astype(BF16), wo_ref[...])
    hn = _rms(h1, gffn_ref[...]).astype(BF16)
    acc_ref[...] = jnp.zeros_like(acc_ref)

    def conv(col0, ubuf):
        cols = pl.ds(col0, fc)
        u = _dot(hn, wup_ref[:, cols])
        ubuf[0:halo, :] = carry_ref[:, cols]
        ubuf[halo:halo + tm, :] = u
        carry_ref[:, cols] = u[tm - halo:tm, :]
        w = cw_ref[:, cols]
        y = cb_ref[:, cols]
        for t in range(CONV_WIDTH - 1):
            off = halo - (CONV_WIDTH - 1) + t
            y = y + w[t:t + 1, :] * ubuf[off:off + tm, :]
        return y + w[CONV_WIDTH - 1:CONV_WIDTH, :] * u

    def chunk(c, carry):
        yg = conv(pl.multiple_of(c * fc, fc), ubuf_g)
        yv = conv(pl.multiple_of(d_ff + c * fc, fc), ubuf_v)
        act = (yg * _sigmoid(yg) * yv).astype(BF16)
        acc_ref[...] += _dot(act, wdn_ref[pl.ds(pl.multiple_of(c * fc, fc), fc), :])
        return carry

    lax.fori_loop(0, d_ff // fc, chunk, 0)
    h2 = h1 + acc_ref[...]
    o_ref[...] = _rms(h2, gfin_ref[...]) if final else h2


def _back(x2, ya, yb, g, wa, wb, wo, gffn, wup, cw, cb, wdn, gfin, *, seq, tm, fc, final):
    n, d = x2.shape
    d_ff = wdn.shape[0]
    row = lambda i: (i, 0)
    return pl.pallas_call(
        functools.partial(_back_kernel, steps_per_seq=seq // tm, d_ff=d_ff, fc=fc, final=final),
        grid=(n // tm,),
        in_specs=[
            pl.BlockSpec((tm, d), row),
            pl.BlockSpec((tm, ya.shape[1]), row),
            pl.BlockSpec((tm, yb.shape[1]), row),
            pl.BlockSpec((tm, g.shape[1]), row),
            _resident(wa.shape), _resident(wb.shape), _resident(wo.shape), _resident(gffn.shape),
            _resident(wup.shape), _resident(cw.shape), _resident(cb.shape), _resident(wdn.shape),
            _resident(gfin.shape),
        ],
        out_specs=pl.BlockSpec((tm, d), row),
        out_shape=jax.ShapeDtypeStruct((n, d), F32),
        scratch_shapes=[
            pltpu.VMEM((SUBLANES, 2 * d_ff), F32),
            pltpu.VMEM((SUBLANES + tm, fc), F32),
            pltpu.VMEM((SUBLANES + tm, fc), F32),
            pltpu.VMEM((tm, d), F32),
        ],
        compiler_params=pltpu.CompilerParams(
            dimension_semantics=("arbitrary",), vmem_limit_bytes=VMEM_LIMIT_BYTES),
        name="back",
    )(x2, ya, yb, g, wa, wb, wo, gffn, wup, cw, cb, wdn, gfin)


def _t5_bucket_np(rel):
    n = np.maximum(rel, 0)
    max_exact = REL_BUCKETS // 2
    nf = np.maximum(n, 1).astype(np.float32)
    large = max_exact + (np.log(nf / np.float32(max_exact)) / np.float32(math.log(REL_MAX_DIST / max_exact))
                         * np.float32(REL_BUCKETS - max_exact)).astype(np.int32)
    large = np.minimum(large, REL_BUCKETS - 1)
    return np.where(n < max_exact, n, large)


def _moba_bias_tables(rel_bias):
    blk = MOBA_BLOCK
    assert REL_MAX_DIST <= blk + 1
    off = np.arange(blk, dtype=np.int32)
    rel_own = off[None, :] - off[:, None]
    bias_h = rel_bias.T.astype(F32)

    def lookup(bucket):
        onehot = jnp.asarray(bucket)[None, :, :] == jnp.arange(REL_BUCKETS)[:, None, None]
        return jnp.sum(jnp.where(onehot[None], bias_h[:, :, None, None], 0.0), axis=1)

    town = jnp.where(jnp.asarray(rel_own >= 0), lookup(_t5_bucket_np(rel_own)), NEG)
    tprev = lookup(_t5_bucket_np(rel_own + blk))
    cfar = bias_h[:, int(_t5_bucket_np(np.int32(blk + 1)))]
    cfar = jnp.broadcast_to(cfar[:, None, None], (MOBA_HEADS, 1, LANES))
    return town, tprev, cfar


def _pad_heads(w, n_heads, width):
    r = w.shape[0]
    w = w.reshape(r, n_heads, width)
    out = jnp.zeros((r, n_heads, LANES), w.dtype).at[:, :, :width].set(w)
    return out.reshape(r, n_heads * LANES)


def _rope_slot(w_rope, swap):
    half = MLA_ROPE_DIM // 2
    if swap:
        w_rope = jnp.concatenate([w_rope[:, half:], w_rope[:, :half]], axis=1)
    r = w_rope.shape[0]
    return jnp.zeros((r, LANES), w_rope.dtype).at[:, MLA_NOPE_DIM:MLA_NOPE_DIM + MLA_ROPE_DIM].set(w_rope)


def _front_weights(w_in, w_uq, w_ukv):
    d = w_in.shape[0]
    s0 = MOBA_WIDTH
    s1 = 2 * MOBA_WIDTH
    s2 = 3 * MOBA_WIDTH
    s3 = s2 + MLA_Q_RANK
    s4 = s3 + MLA_KV_RANK
    s5 = s4 + MLA_ROPE_DIM
    w1 = jnp.concatenate([
        w_in[:, :s0], w_in[:, s0:s1],
        w_in[:, s2:s3], w_in[:, s3:s4],
        _rope_slot(w_in[:, s4:s5], False), _rope_slot(w_in[:, s4:s5], True),
        w_in[:, s5:],
    ], axis=1).astype(BF16)
    assert w1.shape == (d, _C_END)
    wvt = _pad_heads(w_in[:, s1:s2], MOBA_HEADS, MOBA_HEAD_DIM).T.astype(BF16)

    r = w_uq.shape[0]
    uq = w_uq.reshape(r, MLA_HEADS, MLA_QK_DIM)
    nope, rope = uq[:, :, :MLA_NOPE_DIM], uq[:, :, MLA_NOPE_DIM:]
    half = MLA_ROPE_DIM // 2
    zpad = jnp.zeros((r, MLA_HEADS, LANES - MLA_QK_DIM), w_uq.dtype)
    direct = jnp.concatenate([nope, rope, zpad], axis=2)
    swapped = jnp.concatenate([jnp.zeros_like(nope), rope[:, :, half:], rope[:, :, :half], zpad], axis=2)
    wuq2 = jnp.concatenate([direct.reshape(r, -1), swapped.reshape(r, -1)], axis=1).astype(BF16)

    r = w_ukv.shape[0]
    ukv = w_ukv.reshape(r, MLA_HEADS, MLA_NOPE_DIM + MLA_V_DIM)
    wk = _pad_heads(ukv[:, :, :MLA_NOPE_DIM].reshape(r, -1), MLA_HEADS, MLA_NOPE_DIM).astype(BF16)
    wkvt = _pad_heads(ukv[:, :, MLA_NOPE_DIM:].reshape(r, -1), MLA_HEADS, MLA_V_DIM).T.astype(BF16)
    return w1, wvt, wuq2, wk, wkvt


def _rope_lane_tables(seq):
    dim = MLA_ROPE_DIM
    inv_freq = ROPE_THETA ** (-jnp.arange(0, dim, 2, dtype=F32) / dim)
    ang = jnp.arange(seq, dtype=F32)[:, None] * inv_freq[None, :]
    cos, sin = jnp.cos(ang), jnp.sin(ang)
    tail = jnp.zeros((seq, LANES - MLA_QK_DIM), F32)
    cosm = jnp.concatenate([jnp.ones((seq, MLA_NOPE_DIM), F32), cos, cos, tail], axis=1)
    sinm = jnp.concatenate([jnp.zeros((seq, MLA_NOPE_DIM), F32), -sin, sin, tail], axis=1)
    return cosm, sinm


class _Tiles:
    def __init__(self, seq):
        self.tm = 512 if seq % 512 == 0 else MOBA_BLOCK
        self.fc = 256
        self.mla_tq = self.tm
        self.mla_tk = 256
        self.mla_unroll = self.mla_tq // self.mla_tk
        self.moba_unroll = 4


def kernel(x, norm_attn_g, w_in, b_gate, q_norm_g, w_uq, kv_norm_g, w_ukv, rel_bias,
           w_branch_moba, w_branch_mla, w_out, norm_ffn_g, w_up, conv_w, conv_b, w_down,
           norm_final_g):
    batch, seq, d = x.shape
    depth = w_in.shape[0]
    assert seq % MOBA_BLOCK == 0
    t = _Tiles(seq)
    d_ff = w_down.shape[1]
    assert d_ff % t.fc == 0
    n = batch * seq

    cosm, sinm = _rope_lane_tables(seq)
    town, tprev, cfar = _moba_bias_tables(rel_bias)
    row = lambda v: v.reshape(1, -1).astype(F32)

    h = x.reshape(n, d)
    for l in range(depth):
        w1, wvt, wuq2, wk, wkvt = _front_weights(w_in[l], w_uq[l], w_ukv[l])
        qa, ka, kmean, vat, qm, km, vmt, g = _front(
            h, row(norm_attn_g[l]), w1, wvt, row(b_gate[l]), row(q_norm_g[l]), wuq2,
            row(kv_norm_g[l]), wk, wkvt, cosm, sinm, seq=seq, tm=t.tm)
        kmean = kmean.reshape(batch, seq // MOBA_BLOCK, MOBA_WIDTH)
        ya = _moba(qa, ka, vat, kmean, town, tprev, cfar, batch=batch, seq=seq,
                   unroll=t.moba_unroll)
        yb = _mla(qm, km, vmt, batch=batch, seq=seq, tq=t.mla_tq, tk=t.mla_tk,
                  unroll=t.mla_unroll)
        h = _back(h, ya, yb, g, w_branch_moba[l].astype(BF16), w_branch_mla[l].astype(BF16),
                  w_out[l].astype(BF16), row(norm_ffn_g[l]), w_up[l].astype(BF16),
                  conv_w[l].astype(F32), row(conv_b[l]), w_down[l].astype(BF16),
                  row(norm_final_g), seq=seq, tm=t.tm, fc=t.fc, final=(l == depth - 1))
    return h.reshape(batch, seq, d)
```

```python
import functools
import math

import numpy as np
import jax
import jax.numpy as jnp
from jax import lax
from jax.experimental import pallas as pl
from jax.experimental.pallas import tpu as pltpu

MOBA_HEADS = 8
MOBA_HEAD_DIM = 64
MOBA_BLOCK = 256
MOBA_TOPK = 3
MLA_HEADS = 8
MLA_Q_RANK = 256
MLA_KV_RANK = 128
MLA_NOPE_DIM = 64
MLA_ROPE_DIM = 32
MLA_V_DIM = 64
ROPE_THETA = 10000.0
REL_BUCKETS = 32
REL_MAX_DIST = 128
CONV_WIDTH = 3
N_BRANCH = 2
EPS = 1e-6

MOBA_WIDTH = MOBA_HEADS * MOBA_HEAD_DIM
MLA_QK_DIM = MLA_NOPE_DIM + MLA_ROPE_DIM
MLA_WIDTH = MLA_HEADS * MLA_V_DIM

LANES = 128
SUBLANES = 8
VMEM_LIMIT_BYTES = 56 * 1024 * 1024

NEG = -1e30

F32 = jnp.float32
BF16 = jnp.bfloat16


def _dot(a, b):
    return jnp.dot(a, b, preferred_element_type=F32)


def _dot_nt(a, b):
    return lax.dot_general(a, b, (((1,), (1,)), ((), ())), preferred_element_type=F32)


def _rms(x, g):
    return x * lax.rsqrt(jnp.mean(x * x, axis=-1, keepdims=True) + EPS) * g


def _sigmoid(z):
    return 1.0 / (1.0 + jnp.exp(-z))


def _resident(shape):
    nd = len(shape)
    return pl.BlockSpec(shape, lambda *_: (0,) * nd, pipeline_mode=pl.Buffered(1))


_C_QA = 0
_C_KA = _C_QA + MOBA_WIDTH
_C_CQ = _C_KA + MOBA_WIDTH
_C_CKV = _C_CQ + MLA_Q_RANK
_C_KRA = _C_CKV + MLA_KV_RANK
_C_KRB = _C_KRA + LANES
_C_G = _C_KRB + LANES
_C_END = _C_G + N_BRANCH * 1024

V_ROWS = LANES


def _ones_rows(shape):
    r = lax.broadcasted_iota(jnp.int32, shape, 0) & (V_ROWS - 1)
    return jnp.where(r == MLA_V_DIM, 1.0, 0.0).astype(F32)


def _front_kernel(x_ref, gattn_ref, w1_ref, wvt_ref, bg_ref, qng_ref, wuq_ref, kvng_ref, wkv_ref,
                  wkvt_ref, cos_ref, sin_ref,
                  qa_ref, ka_ref, kmean_ref, vat_ref, qm_ref, km_ref, vmt_ref, g_ref,
                  *, d_model, mla_scale):
    tm = x_ref.shape[0]
    hw = MLA_HEADS * LANES
    xn = _rms(x_ref[...], gattn_ref[...]).astype(BF16)

    def proj(a, b):
        return _dot(xn, w1_ref[:, a:b])

    qa_ref[...] = (proj(_C_QA, _C_KA) * (MOBA_HEAD_DIM ** -0.5 * LOG2E)).astype(BF16)
    ka = proj(_C_KA, _C_CQ)
    ka_ref[...] = ka.astype(BF16)
    for i in range(tm // MOBA_BLOCK):
        kmean_ref[i] = jnp.mean(ka[i * MOBA_BLOCK:(i + 1) * MOBA_BLOCK], axis=0, keepdims=True)
    ones = _ones_rows(vat_ref.shape)
    vat_ref[...] = (_dot_nt(wvt_ref[...], xn) + ones).astype(BF16)

    cosm = cos_ref[...]
    sinm = sin_ref[...]

    cqn = _rms(proj(_C_CQ, _C_CKV), qng_ref[...]).astype(BF16)
    qq = _dot(cqn, wuq_ref[...])
    for h in range(MLA_HEADS):
        a = qq[:, h * LANES:(h + 1) * LANES]
        b = qq[:, hw + h * LANES:hw + (h + 1) * LANES]
        qm_ref[:, h * LANES:(h + 1) * LANES] = ((a * cosm + b * sinm) * mla_scale).astype(BF16)

    ckvn = _rms(proj(_C_CKV, _C_KRA), kvng_ref[...]).astype(BF16)
    kv = _dot(ckvn, wkv_ref[...])
    kr = proj(_C_KRA, _C_KRB) * cosm + proj(_C_KRB, _C_G) * sinm
    for h in range(MLA_HEADS):
        km_ref[:, h * LANES:(h + 1) * LANES] = (kv[:, h * LANES:(h + 1) * LANES] + kr).astype(BF16)
    vmt_ref[...] = (_dot_nt(wkvt_ref[...], ckvn) + ones).astype(BF16)

    g_ref[...] = _sigmoid(proj(_C_G, _C_END) + bg_ref[...]).astype(BF16)


def _front(x2, gattn, w1, wvt, bg, qng, wuq2, kvng, wkv, wkvt, cosm, sinm, *, seq, tm):
    n, d = x2.shape
    batch = n // seq
    hw = MLA_HEADS * LANES
    nblk = n // MOBA_BLOCK
    steps_per_seq = seq // tm
    row = lambda i: (i, 0)
    tcol = lambda i: (i // steps_per_seq, 0, i % steps_per_seq)
    out_shape = (
        jax.ShapeDtypeStruct((n, MOBA_WIDTH), BF16),
        jax.ShapeDtypeStruct((n, MOBA_WIDTH), BF16),
        jax.ShapeDtypeStruct((nblk, 1, MOBA_WIDTH), F32),
        jax.ShapeDtypeStruct((batch, MOBA_HEADS * V_ROWS, seq), BF16),
        jax.ShapeDtypeStruct((n, hw), BF16),
        jax.ShapeDtypeStruct((n, hw), BF16),
        jax.ShapeDtypeStruct((batch, MLA_HEADS * V_ROWS, seq), BF16),
        jax.ShapeDtypeStruct((n, N_BRANCH * d), BF16),
    )
    in_specs = [
        pl.BlockSpec((tm, d), row),
        _resident(gattn.shape), _resident(w1.shape), _resident(wvt.shape), _resident(bg.shape),
        _resident(qng.shape), _resident(wuq2.shape), _resident(kvng.shape), _resident(wkv.shape),
        _resident(wkvt.shape),
        pl.BlockSpec((tm, LANES), lambda i: (i % steps_per_seq, 0)),
        pl.BlockSpec((tm, LANES), lambda i: (i % steps_per_seq, 0)),
    ]
    out_specs = (
        pl.BlockSpec((tm, MOBA_WIDTH), row),
        pl.BlockSpec((tm, MOBA_WIDTH), row),
        pl.BlockSpec((tm // MOBA_BLOCK, 1, MOBA_WIDTH), lambda i: (i, 0, 0)),
        pl.BlockSpec((None, MOBA_HEADS * V_ROWS, tm), tcol),
        pl.BlockSpec((tm, hw), row),
        pl.BlockSpec((tm, hw), row),
        pl.BlockSpec((None, MLA_HEADS * V_ROWS, tm), tcol),
        pl.BlockSpec((tm, N_BRANCH * d), row),
    )
    return pl.pallas_call(
        functools.partial(_front_kernel, d_model=d, mla_scale=MLA_QK_DIM ** -0.5 * LOG2E),
        grid=(n // tm,),
        in_specs=in_specs,
        out_specs=out_specs,
        out_shape=out_shape,
        compiler_params=pltpu.CompilerParams(
            dimension_semantics=("arbitrary",), vmem_limit_bytes=VMEM_LIMIT_BYTES),
        name="front",
    )(x2, gattn, w1, wvt, bg, qng, wuq2, kvng, wkv, wkvt, cosm, sinm)


HEADS_PER_STEP = 4
LOG2E = math.log2(math.e)
BIG = 1e30


def _online_softmax_group(m_ref, acc_ref, score_blocks, vt_grp, chosen=None, shifts=None):
    n = len(score_blocks)
    chosen = chosen or [None] * n
    shifts = shifts or [None] * n
    m_old = m_ref[...]
    m_new = m_old
    for st, ch, sh in zip(score_blocks, chosen, shifts):
        rm = jnp.max(st, axis=0, keepdims=True)
        if sh is not None:
            rm = rm + sh
        m_new = jnp.maximum(m_new, rm if ch is None else jnp.where(ch, rm, NEG))
    probs = []
    for st, ch, sh in zip(score_blocks, chosen, shifts):
        off = m_new if sh is None else m_new - sh
        if ch is not None:
            off = jnp.where(ch, off, BIG)
        probs.append(jnp.exp2(st - off).astype(BF16))
    ot = _dot(vt_grp, jnp.concatenate(probs, axis=0))
    acc_ref[...] = jnp.exp2(m_old - m_new) * acc_ref[...] + ot
    m_ref[...] = m_new


def _reset_softmax_state(m_s, acc_s):
    m_s[...] = jnp.full(m_s.shape, NEG, F32)
    acc_s[...] = jnp.zeros(acc_s.shape, F32)


def _normalised_heads(acc_s, v_dim):
    rows = []
    for hh in range(HEADS_PER_STEP):
        acc = acc_s[hh]
        rows.append(acc[0:v_dim, :] * (1.0 / acc[v_dim:v_dim + 1, :]))
    return jnp.concatenate(rows, axis=0)


def _moba_kernel(q_ref, k_ref, vt_ref, kmean_ref, near_ref, cfar_ref, o_ref,
                 m_s, acc_s, sel_s, *, unroll):
    blk = MOBA_BLOCK
    qb = pl.program_id(2)
    nb = kmean_ref.shape[0]
    lane = lax.broadcasted_iota(jnp.int32, (blk, LANES), 1)
    blk_id = lax.broadcasted_iota(jnp.int32, (nb, blk), 0).astype(F32)
    qbf = qb.astype(F32)
    has_prev = qb >= 1
    _reset_softmax_state(m_s, acc_s)

    def lane_group(hh):
        return slice((hh // 2) * LANES, (hh // 2 + 1) * LANES)

    def k_block(hh, j):
        return k_ref[pl.ds(pl.multiple_of(j * blk, blk), blk), lane_group(hh)]

    def vt_blocks(hh, j, n):
        return vt_ref[hh * V_ROWS:(hh + 1) * V_ROWS, pl.ds(pl.multiple_of(j * blk, blk), n * blk)]

    q_heads, prev_chosen = [], []
    for hh in range(HEADS_PER_STEP):
        in_head = (lane >= (hh % 2) * MOBA_HEAD_DIM) & (lane < (hh % 2 + 1) * MOBA_HEAD_DIM)
        qh = jnp.where(in_head, q_ref[:, lane_group(hh)].astype(F32), 0.0).astype(BF16)
        q_heads.append(qh)

        kmean = kmean_ref[:, lane_group(hh)].astype(BF16)
        gate = jnp.where(blk_id < qbf, _dot_nt(kmean, qh), -jnp.inf)
        sel = jnp.zeros((nb, blk), F32)
        for _ in range(min(MOBA_TOPK, nb)):
            best = jnp.max(gate, axis=0, keepdims=True)
            first = jnp.min(jnp.where(gate == best, blk_id, float(nb)), axis=0, keepdims=True)
            pick = blk_id == first
            sel = jnp.where(pick, 1.0, sel)
            gate = jnp.where(pick, -jnp.inf, gate)
        prev_chosen.append(jnp.max(jnp.where(blk_id == qbf - 1.0, sel, 0.0), axis=0, keepdims=True))
        sel_s[hh] = jnp.where(blk_id < qbf - 1.0, sel, 0.0)

    lo = jnp.maximum(qb - 1, 0)
    tab = jnp.where(has_prev, 0, 1)
    scores = [[_dot_nt(k_block(hh, lo + u), q_heads[hh]) + near_ref[hh, tab + u] for u in range(2)]
              for hh in range(HEADS_PER_STEP)]
    for hh in range(HEADS_PER_STEP):
        chosen = [jnp.where(has_prev, prev_chosen[hh], 1.0) > 0.5,
                  jnp.where(has_prev, jnp.ones((1, blk), F32), 0.0) > 0.5]
        _online_softmax_group(m_s.at[hh], acc_s.at[hh], scores[hh], vt_blocks(hh, lo, 2), chosen)

    def far(g, carry):
        first = g * unroll
        start = jnp.minimum(first, nb - unroll)
        scores = [[_dot_nt(k_block(hh, start + u), q_heads[hh]) for u in range(unroll)]
                  for hh in range(HEADS_PER_STEP)]
        for hh in range(HEADS_PER_STEP):
            chosen = [jnp.where(start + u >= first, sel_s[hh, pl.ds(start + u, 1), :], 0.0) > 0.5
                      for u in range(unroll)]
            shifts = [cfar_ref[hh][:, 0:1]] * unroll
            _online_softmax_group(m_s.at[hh], acc_s.at[hh], scores[hh],
                                  vt_blocks(hh, start, unroll), chosen, shifts)
        return carry

    lax.fori_loop(0, (qb - 1 + unroll - 1) // unroll, far, 0)
    o_ref[...] = _normalised_heads(acc_s, MOBA_HEAD_DIM).T.astype(o_ref.dtype)


def _moba(qa, ka, vat, kmean, near, cfar, *, batch, seq, unroll):
    n = qa.shape[0]
    blk = MOBA_BLOCK
    nb = seq // blk
    assert nb >= unroll and nb >= 2
    steps = MOBA_HEADS // HEADS_PER_STEP
    width = HEADS_PER_STEP * MOBA_HEAD_DIM
    return pl.pallas_call(
        functools.partial(_moba_kernel, unroll=unroll),
        grid=(batch, steps, nb),
        in_specs=[
            pl.BlockSpec((blk, width), lambda b, p, i: (b * nb + i, p)),
            pl.BlockSpec((seq, width), lambda b, p, i: (b, p)),
            pl.BlockSpec((None, HEADS_PER_STEP * V_ROWS, seq), lambda b, p, i: (b, p, 0)),
            pl.BlockSpec((None, nb, width), lambda b, p, i: (b, 0, p)),
            pl.BlockSpec((HEADS_PER_STEP, 3, blk, blk), lambda b, p, i: (p, 0, 0, 0)),
            pl.BlockSpec((HEADS_PER_STEP, 1, LANES), lambda b, p, i: (p, 0, 0)),
        ],
        out_specs=pl.BlockSpec((blk, width), lambda b, p, i: (b * nb + i, p)),
        out_shape=jax.ShapeDtypeStruct((n, MOBA_WIDTH), BF16),
        scratch_shapes=[
            pltpu.VMEM((HEADS_PER_STEP, 1, blk), F32),
            pltpu.VMEM((HEADS_PER_STEP, V_ROWS, blk), F32),
            pltpu.VMEM((HEADS_PER_STEP, nb, blk), F32),
        ],
        compiler_params=pltpu.CompilerParams(
            dimension_semantics=("arbitrary", "arbitrary", "arbitrary"),
            vmem_limit_bytes=VMEM_LIMIT_BYTES),
        name="moba",
    )(qa, ka, vat, kmean, near, cfar)


def _mla_kernel(q_ref, k_ref, vt_ref, o_ref, m_s, acc_s, *, tq, tk, unroll):
    qi = pl.program_id(2)
    n_diag = tq // tk
    key = lax.broadcasted_iota(jnp.int32, (tk, tq), 0)
    qry = lax.broadcasted_iota(jnp.int32, (tk, tq), 1)
    q_heads = [q_ref[:, hh * LANES:(hh + 1) * LANES] for hh in range(HEADS_PER_STEP)]

    _reset_softmax_state(m_s, acc_s)

    def k_tile(hh, j):
        return k_ref[pl.ds(pl.multiple_of(j * tk, tk), tk), hh * LANES:(hh + 1) * LANES]

    def vt_tiles(hh, j, n):
        return vt_ref[hh * V_ROWS:(hh + 1) * V_ROWS, pl.ds(pl.multiple_of(j * tk, tk), n * tk)]

    def group(first, n, masked):
        scores = [[_dot_nt(k_tile(hh, first + u), q_heads[hh]) for u in range(n)]
                  for hh in range(HEADS_PER_STEP)]
        for hh in range(HEADS_PER_STEP):
            blocks = scores[hh]
            if masked:
                blocks = [jnp.where(key + u * tk <= qry, st, NEG) for u, st in enumerate(blocks)]
            _online_softmax_group(m_s.at[hh], acc_s.at[hh], blocks, vt_tiles(hh, first, n))

    group(qi * n_diag, n_diag, True)

    def body(g, carry):
        group(g * unroll, unroll, False)
        return carry

    lax.fori_loop(0, (qi * n_diag) // unroll, body, 0)
    o_ref[...] = _normalised_heads(acc_s, MLA_V_DIM).T.astype(o_ref.dtype)


def _mla(qm, km, vmt, *, batch, seq, tq, tk, unroll):
    n = qm.shape[0]
    nq = seq // tq
    assert tq % tk == 0 and (tq // tk) % unroll == 0
    pairs = MLA_HEADS // HEADS_PER_STEP
    return pl.pallas_call(
        functools.partial(_mla_kernel, tq=tq, tk=tk, unroll=unroll),
        grid=(batch, pairs, nq),
        in_specs=[
            pl.BlockSpec((tq, HEADS_PER_STEP * LANES), lambda b, p, i: (b * nq + i, p)),
            pl.BlockSpec((seq, HEADS_PER_STEP * LANES), lambda b, p, i: (b, p)),
            pl.BlockSpec((None, HEADS_PER_STEP * V_ROWS, seq), lambda b, p, i: (b, p, 0)),
        ],
        out_specs=pl.BlockSpec((tq, HEADS_PER_STEP * MLA_V_DIM), lambda b, p, i: (b * nq + i, p)),
        out_shape=jax.ShapeDtypeStruct((n, MLA_WIDTH), BF16),
        scratch_shapes=[
            pltpu.VMEM((HEADS_PER_STEP, 1, tq), F32),
            pltpu.VMEM((HEADS_PER_STEP, V_ROWS, tq), F32),
        ],
        compiler_params=pltpu.CompilerParams(
            dimension_semantics=("arbitrary", "arbitrary", "arbitrary"),
            vmem_limit_bytes=VMEM_LIMIT_BYTES),
        name="mla",
    )(qm, km, vmt)


def _back_kernel(x_ref, ya_ref, yb_ref, g_ref, wa_ref, wb_ref, wo_ref, gffn_ref, wup_ref,
                 cw_ref, cb_ref, wdn_ref, gfin_ref, o_ref,
                 carry_ref, ubuf_g, ubuf_v, acc_ref, *, steps_per_seq, d_ff, fc, final):
    tm, d = x_ref.shape
    halo = SUBLANES

    @pl.when(pl.program_id(0) % steps_per_seq == 0)
    def _():
        carry_ref[...] = jnp.zeros_like(carry_ref)

    g = g_ref[...].astype(F32)
    mixed = g[:, :d] * _dot(ya_ref[...], wa_ref[...]) + g[:, d:] * _dot(yb_ref[...], wb_ref[...])
    h1 = x_ref[...] + _dot(mixed.astype(BF16), wo_ref[...])
    hn = _rms(h1, gffn_ref[...]).astype(BF16)
    acc_ref[...] = jnp.zeros_like(acc_ref)

    def conv(col0, ubuf):
        cols = pl.ds(col0, fc)
        u = _dot(hn, wup_ref[:, cols])
        ubuf[0:halo, :] = carry_ref[:, cols]
        ubuf[halo:halo + tm, :] = u
        carry_ref[:, cols] = u[tm - halo:tm, :]
        w = cw_ref[:, cols]
        y = cb_ref[:, cols]
        for t in range(CONV_WIDTH - 1):
            off = halo - (CONV_WIDTH - 1) + t
            y = y + w[t:t + 1, :] * ubuf[off:off + tm, :]
        return y + w[CONV_WIDTH - 1:CONV_WIDTH, :] * u

    def chunk(c, carry):
        yg = conv(pl.multiple_of(c * fc, fc), ubuf_g)
        yv = conv(pl.multiple_of(d_ff + c * fc, fc), ubuf_v)
        act = (yg * _sigmoid(yg) * yv).astype(BF16)
        acc_ref[...] += _dot(act, wdn_ref[pl.ds(pl.multiple_of(c * fc, fc), fc), :])
        return carry

    lax.fori_loop(0, d_ff // fc, chunk, 0)
    h2 = h1 + acc_ref[...]
    o_ref[...] = _rms(h2, gfin_ref[...]) if final else h2


def _back(x2, ya, yb, g, wa, wb, wo, gffn, wup, cw, cb, wdn, gfin, *, seq, tm, fc, final):
    n, d = x2.shape
    d_ff = wdn.shape[0]
    row = lambda i: (i, 0)
    return pl.pallas_call(
        functools.partial(_back_kernel, steps_per_seq=seq // tm, d_ff=d_ff, fc=fc, final=final),
        grid=(n // tm,),
        in_specs=[
            pl.BlockSpec((tm, d), row),
            pl.BlockSpec((tm, ya.shape[1]), row),
            pl.BlockSpec((tm, yb.shape[1]), row),
            pl.BlockSpec((tm, g.shape[1]), row),
            _resident(wa.shape), _resident(wb.shape), _resident(wo.shape), _resident(gffn.shape),
            _resident(wup.shape), _resident(cw.shape), _resident(cb.shape), _resident(wdn.shape),
            _resident(gfin.shape),
        ],
        out_specs=pl.BlockSpec((tm, d), row),
        out_shape=jax.ShapeDtypeStruct((n, d), F32),
        scratch_shapes=[
            pltpu.VMEM((SUBLANES, 2 * d_ff), F32),
            pltpu.VMEM((SUBLANES + tm, fc), F32),
            pltpu.VMEM((SUBLANES + tm, fc), F32),
            pltpu.VMEM((tm, d), F32),
        ],
        compiler_params=pltpu.CompilerParams(
            dimension_semantics=("arbitrary",), vmem_limit_bytes=VMEM_LIMIT_BYTES),
        name="back",
    )(x2, ya, yb, g, wa, wb, wo, gffn, wup, cw, cb, wdn, gfin)


def _t5_bucket_np(rel):
    n = np.maximum(rel, 0)
    max_exact = REL_BUCKETS // 2
    nf = np.maximum(n, 1).astype(np.float32)
    large = max_exact + (np.log(nf / np.float32(max_exact)) / np.float32(math.log(REL_MAX_DIST / max_exact))
                         * np.float32(REL_BUCKETS - max_exact)).astype(np.int32)
    large = np.minimum(large, REL_BUCKETS - 1)
    return np.where(n < max_exact, n, large)


def _moba_bias_tables(rel_bias):
    blk = MOBA_BLOCK
    assert REL_MAX_DIST <= blk + 1
    off = np.arange(blk, dtype=np.int32)
    rel_own = off[None, :] - off[:, None]
    bias_h = rel_bias.T.astype(F32)

    def lookup(bucket):
        onehot = jnp.asarray(bucket)[None, :, :] == jnp.arange(REL_BUCKETS)[:, None, None]
        return jnp.sum(jnp.where(onehot[None], bias_h[:, :, None, None], 0.0), axis=1)

    town = jnp.where(jnp.asarray(rel_own >= 0), lookup(_t5_bucket_np(rel_own)) * LOG2E, NEG)
    tprev = lookup(_t5_bucket_np(rel_own + blk)) * LOG2E
    near = jnp.stack([tprev, town, jnp.zeros_like(town)], axis=1)
    cfar = bias_h[:, int(_t5_bucket_np(np.int32(blk + 1)))] * LOG2E
    cfar = jnp.broadcast_to(cfar[:, None, None], (MOBA_HEADS, 1, LANES))
    return near, cfar


def _pad_heads(w, n_heads, width):
    r = w.shape[0]
    w = w.reshape(r, n_heads, width)
    out = jnp.zeros((r, n_heads, LANES), w.dtype).at[:, :, :width].set(w)
    return out.reshape(r, n_heads * LANES)


def _rope_slot(w_rope, swap):
    half = MLA_ROPE_DIM // 2
    if swap:
        w_rope = jnp.concatenate([w_rope[:, half:], w_rope[:, :half]], axis=1)
    r = w_rope.shape[0]
    return jnp.zeros((r, LANES), w_rope.dtype).at[:, MLA_NOPE_DIM:MLA_NOPE_DIM + MLA_ROPE_DIM].set(w_rope)


def _front_weights(w_in, w_uq, w_ukv):
    d = w_in.shape[0]
    s0 = MOBA_WIDTH
    s1 = 2 * MOBA_WIDTH
    s2 = 3 * MOBA_WIDTH
    s3 = s2 + MLA_Q_RANK
    s4 = s3 + MLA_KV_RANK
    s5 = s4 + MLA_ROPE_DIM
    w1 = jnp.concatenate([
        w_in[:, :s0], w_in[:, s0:s1],
        w_in[:, s2:s3], w_in[:, s3:s4],
        _rope_slot(w_in[:, s4:s5], False), _rope_slot(w_in[:, s4:s5], True),
        w_in[:, s5:],
    ], axis=1).astype(BF16)
    assert w1.shape == (d, _C_END)
    wvt = _pad_heads(w_in[:, s1:s2], MOBA_HEADS, MOBA_HEAD_DIM).T.astype(BF16)

    r = w_uq.shape[0]
    uq = w_uq.reshape(r, MLA_HEADS, MLA_QK_DIM)
    nope, rope = uq[:, :, :MLA_NOPE_DIM], uq[:, :, MLA_NOPE_DIM:]
    half = MLA_ROPE_DIM // 2
    zpad = jnp.zeros((r, MLA_HEADS, LANES - MLA_QK_DIM), w_uq.dtype)
    direct = jnp.concatenate([nope, rope, zpad], axis=2)
    swapped = jnp.concatenate([jnp.zeros_like(nope), rope[:, :, half:], rope[:, :, :half], zpad], axis=2)
    wuq2 = jnp.concatenate([direct.reshape(r, -1), swapped.reshape(r, -1)], axis=1).astype(BF16)

    r = w_ukv.shape[0]
    ukv = w_ukv.reshape(r, MLA_HEADS, MLA_NOPE_DIM + MLA_V_DIM)
    wk = _pad_heads(ukv[:, :, :MLA_NOPE_DIM].reshape(r, -1), MLA_HEADS, MLA_NOPE_DIM).astype(BF16)
    wkvt = _pad_heads(ukv[:, :, MLA_NOPE_DIM:].reshape(r, -1), MLA_HEADS, MLA_V_DIM).T.astype(BF16)
    return w1, wvt, wuq2, wk, wkvt


def _rope_lane_tables(seq):
    dim = MLA_ROPE_DIM
    inv_freq = ROPE_THETA ** (-jnp.arange(0, dim, 2, dtype=F32) / dim)
    ang = jnp.arange(seq, dtype=F32)[:, None] * inv_freq[None, :]
    cos, sin = jnp.cos(ang), jnp.sin(ang)
    tail = jnp.zeros((seq, LANES - MLA_QK_DIM), F32)
    cosm = jnp.concatenate([jnp.ones((seq, MLA_NOPE_DIM), F32), cos, cos, tail], axis=1)
    sinm = jnp.concatenate([jnp.zeros((seq, MLA_NOPE_DIM), F32), -sin, sin, tail], axis=1)
    return cosm, sinm


class _Tiles:
    def __init__(self, seq):
        self.tm = 512 if seq % 512 == 0 else MOBA_BLOCK
        self.fc = 256
        self.mla_tq = self.tm
        self.mla_tk = 256
        self.mla_unroll = self.mla_tq // self.mla_tk
        self.moba_unroll = 4


def kernel(x, norm_attn_g, w_in, b_gate, q_norm_g, w_uq, kv_norm_g, w_ukv, rel_bias,
           w_branch_moba, w_branch_mla, w_out, norm_ffn_g, w_up, conv_w, conv_b, w_down,
           norm_final_g):
    batch, seq, d = x.shape
    depth = w_in.shape[0]
    assert seq % MOBA_BLOCK == 0
    t = _Tiles(seq)
    d_ff = w_down.shape[1]
    assert d_ff % t.fc == 0
    n = batch * seq

    cosm, sinm = _rope_lane_tables(seq)
    near, cfar = _moba_bias_tables(rel_bias)
    row = lambda v: v.reshape(1, -1).astype(F32)

    h = x.reshape(n, d)
    for l in range(depth):
        w1, wvt, wuq2, wk, wkvt = _front_weights(w_in[l], w_uq[l], w_ukv[l])
        qa, ka, kmean, vat, qm, km, vmt, g = _front(
            h, row(norm_attn_g[l]), w1, wvt, row(b_gate[l]), row(q_norm_g[l]), wuq2,
            row(kv_norm_g[l]), wk, wkvt, cosm, sinm, seq=seq, tm=t.tm)
        kmean = kmean.reshape(batch, seq // MOBA_BLOCK, MOBA_WIDTH)
        ya = _moba(qa, ka, vat, kmean, near, cfar, batch=batch, seq=seq,
                   unroll=t.moba_unroll)
        yb = _mla(qm, km, vmt, batch=batch, seq=seq, tq=t.mla_tq, tk=t.mla_tk,
                  unroll=t.mla_unroll)
        h = _back(h, ya, yb, g, w_branch_moba[l].astype(BF16), w_branch_mla[l].astype(BF16),
                  w_out[l].astype(BF16), row(norm_ffn_g[l]), w_up[l].astype(BF16),
                  conv_w[l].astype(F32), row(conv_b[l]), w_down[l].astype(BF16),
                  row(norm_final_g), seq=seq, tm=t.tm, fc=t.fc, final=(l == depth - 1))
    return h.reshape(batch, seq, d)
```

```python
import functools
import math

import numpy as np
import jax
import jax.numpy as jnp
from jax import lax
from jax.experimental import pallas as pl
from jax.experimental.pallas import tpu as pltpu

MOBA_HEADS = 8
MOBA_HEAD_DIM = 64
MOBA_BLOCK = 256
MOBA_TOPK = 3
MLA_HEADS = 8
MLA_Q_RANK = 256
MLA_KV_RANK = 128
MLA_NOPE_DIM = 64
MLA_ROPE_DIM = 32
MLA_V_DIM = 64
ROPE_THETA = 10000.0
REL_BUCKETS = 32
REL_MAX_DIST = 128
CONV_WIDTH = 3
N_BRANCH = 2
EPS = 1e-6

MOBA_WIDTH = MOBA_HEADS * MOBA_HEAD_DIM
MLA_QK_DIM = MLA_NOPE_DIM + MLA_ROPE_DIM
MLA_WIDTH = MLA_HEADS * MLA_V_DIM

LANES = 128
SUBLANES = 8
VMEM_LIMIT_BYTES = 56 * 1024 * 1024

NEG = -1e30

F32 = jnp.float32
BF16 = jnp.bfloat16


def _dot(a, b):
    return jnp.dot(a, b, preferred_element_type=F32)


def _dot_nt(a, b):
    return lax.dot_general(a, b, (((1,), (1,)), ((), ())), preferred_element_type=F32)


def _rms(x, g):
    return x * lax.rsqrt(jnp.mean(x * x, axis=-1, keepdims=True) + EPS) * g


def _sigmoid(z):
    return 1.0 / (1.0 + jnp.exp(-z))


def _resident(shape):
    nd = len(shape)
    return pl.BlockSpec(shape, lambda *_: (0,) * nd, pipeline_mode=pl.Buffered(1))


_C_QA = 0
_C_KA = _C_QA + MOBA_WIDTH
_C_CQ = _C_KA + MOBA_WIDTH
_C_CKV = _C_CQ + MLA_Q_RANK
_C_KRA = _C_CKV + MLA_KV_RANK
_C_KRB = _C_KRA + LANES
_C_G = _C_KRB + LANES
_C_END = _C_G + N_BRANCH * 1024

V_ROWS = LANES


def _ones_rows(shape):
    r = lax.broadcasted_iota(jnp.int32, shape, 0) & (V_ROWS - 1)
    return jnp.where(r == MLA_V_DIM, 1.0, 0.0).astype(F32)


def _front_kernel(x_ref, gattn_ref, w1_ref, wvt_ref, bg_ref, qng_ref, wuq_ref, kvng_ref, wkv_ref,
                  wkvt_ref, cos_ref, sin_ref,
                  qa_ref, ka_ref, kmean_ref, vat_ref, qm_ref, km_ref, vmt_ref, g_ref,
                  *, d_model, mla_scale):
    tm = x_ref.shape[0]
    hw = MLA_HEADS * LANES
    xn = _rms(x_ref[...], gattn_ref[...]).astype(BF16)

    def proj(a, b):
        return _dot(xn, w1_ref[:, a:b])

    qa_ref[...] = (proj(_C_QA, _C_KA) * (MOBA_HEAD_DIM ** -0.5 * LOG2E)).astype(BF16)
    ka = proj(_C_KA, _C_CQ)
    ka_ref[...] = ka.astype(BF16)
    for i in range(tm // MOBA_BLOCK):
        kmean_ref[i] = jnp.mean(ka[i * MOBA_BLOCK:(i + 1) * MOBA_BLOCK], axis=0, keepdims=True)
    ones = _ones_rows(vat_ref.shape)
    vat_ref[...] = (_dot_nt(wvt_ref[...], xn) + ones).astype(BF16)

    cosm = cos_ref[...]
    sinm = sin_ref[...]

    cqn = _rms(proj(_C_CQ, _C_CKV), qng_ref[...]).astype(BF16)
    qq = _dot(cqn, wuq_ref[...])
    for h in range(MLA_HEADS):
        a = qq[:, h * LANES:(h + 1) * LANES]
        b = qq[:, hw + h * LANES:hw + (h + 1) * LANES]
        qm_ref[:, h * LANES:(h + 1) * LANES] = ((a * cosm + b * sinm) * mla_scale).astype(BF16)

    ckvn = _rms(proj(_C_CKV, _C_KRA), kvng_ref[...]).astype(BF16)
    kv = _dot(ckvn, wkv_ref[...])
    kr = proj(_C_KRA, _C_KRB) * cosm + proj(_C_KRB, _C_G) * sinm
    for h in range(MLA_HEADS):
        km_ref[:, h * LANES:(h + 1) * LANES] = (kv[:, h * LANES:(h + 1) * LANES] + kr).astype(BF16)
    vmt_ref[...] = (_dot_nt(wkvt_ref[...], ckvn) + ones).astype(BF16)

    g_ref[...] = _sigmoid(proj(_C_G, _C_END) + bg_ref[...]).astype(BF16)


def _front(x2, gattn, w1, wvt, bg, qng, wuq2, kvng, wkv, wkvt, cosm, sinm, *, seq, tm):
    n, d = x2.shape
    batch = n // seq
    hw = MLA_HEADS * LANES
    nblk = n // MOBA_BLOCK
    steps_per_seq = seq // tm
    row = lambda i: (i, 0)
    tcol = lambda i: (i // steps_per_seq, 0, i % steps_per_seq)
    out_shape = (
        jax.ShapeDtypeStruct((n, MOBA_WIDTH), BF16),
        jax.ShapeDtypeStruct((n, MOBA_WIDTH), BF16),
        jax.ShapeDtypeStruct((nblk, 1, MOBA_WIDTH), F32),
        jax.ShapeDtypeStruct((batch, MOBA_HEADS * V_ROWS, seq), BF16),
        jax.ShapeDtypeStruct((n, hw), BF16),
        jax.ShapeDtypeStruct((n, hw), BF16),
        jax.ShapeDtypeStruct((batch, MLA_HEADS * V_ROWS, seq), BF16),
        jax.ShapeDtypeStruct((n, N_BRANCH * d), BF16),
    )
    in_specs = [
        pl.BlockSpec((tm, d), row),
        _resident(gattn.shape), _resident(w1.shape), _resident(wvt.shape), _resident(bg.shape),
        _resident(qng.shape), _resident(wuq2.shape), _resident(kvng.shape), _resident(wkv.shape),
        _resident(wkvt.shape),
        pl.BlockSpec((tm, LANES), lambda i: (i % steps_per_seq, 0)),
        pl.BlockSpec((tm, LANES), lambda i: (i % steps_per_seq, 0)),
    ]
    out_specs = (
        pl.BlockSpec((tm, MOBA_WIDTH), row),
        pl.BlockSpec((tm, MOBA_WIDTH), row),
        pl.BlockSpec((tm // MOBA_BLOCK, 1, MOBA_WIDTH), lambda i: (i, 0, 0)),
        pl.BlockSpec((None, MOBA_HEADS * V_ROWS, tm), tcol),
        pl.BlockSpec((tm, hw), row),
        pl.BlockSpec((tm, hw), row),
        pl.BlockSpec((None, MLA_HEADS * V_ROWS, tm), tcol),
        pl.BlockSpec((tm, N_BRANCH * d), row),
    )
    return pl.pallas_call(
        functools.partial(_front_kernel, d_model=d, mla_scale=MLA_QK_DIM ** -0.5 * LOG2E),
        grid=(n // tm,),
        in_specs=in_specs,
        out_specs=out_specs,
        out_shape=out_shape,
        compiler_params=pltpu.CompilerParams(
            dimension_semantics=("arbitrary",), vmem_limit_bytes=VMEM_LIMIT_BYTES),
        name="front",
    )(x2, gattn, w1, wvt, bg, qng, wuq2, kvng, wkv, wkvt, cosm, sinm)


HEADS_PER_STEP = 4
LOG2E = math.log2(math.e)
BIG = 1e30


def _online_softmax_group(m_ref, acc_ref, score_blocks, vt_grp, chosen=None, shifts=None):
    n = len(score_blocks)
    chosen = chosen or [None] * n
    shifts = shifts or [None] * n
    m_old = m_ref[...]
    m_new = m_old
    for st, ch, sh in zip(score_blocks, chosen, shifts):
        rm = jnp.max(st, axis=0, keepdims=True)
        if sh is not None:
            rm = rm + sh
        m_new = jnp.maximum(m_new, rm if ch is None else jnp.where(ch, rm, NEG))
    probs = []
    for st, ch, sh in zip(score_blocks, chosen, shifts):
        off = m_new if sh is None else m_new - sh
        if ch is not None:
            off = jnp.where(ch, off, BIG)
        probs.append(jnp.exp2(st - off).astype(BF16))
    ot = _dot(vt_grp, jnp.concatenate(probs, axis=0))
    acc_ref[...] = jnp.exp2(m_old - m_new) * acc_ref[...] + ot
    m_ref[...] = m_new


def _reset_softmax_state(m_s, acc_s):
    m_s[...] = jnp.full(m_s.shape, NEG, F32)
    acc_s[...] = jnp.zeros(acc_s.shape, F32)


def _normalised_heads(acc_s, v_dim):
    rows = []
    for hh in range(HEADS_PER_STEP):
        acc = acc_s[hh]
        rows.append(acc[0:v_dim, :] * (1.0 / acc[v_dim:v_dim + 1, :]))
    return jnp.concatenate(rows, axis=0)


def _moba_kernel(q_ref, k_ref, vt_ref, kmean_ref, near_ref, cfar_ref, o_ref,
                 m_s, acc_s, sel_s, *, unroll):
    blk = MOBA_BLOCK
    qb = pl.program_id(2)
    nb = kmean_ref.shape[0]
    lane = lax.broadcasted_iota(jnp.int32, (blk, LANES), 1)
    blk_id = lax.broadcasted_iota(jnp.int32, (nb, blk), 0).astype(F32)
    qbf = qb.astype(F32)
    has_prev = qb >= 1
    _reset_softmax_state(m_s, acc_s)

    def lane_group(hh):
        return slice((hh // 2) * LANES, (hh // 2 + 1) * LANES)

    def k_block(hh, j):
        return k_ref[pl.ds(pl.multiple_of(j * blk, blk), blk), lane_group(hh)]

    def vt_blocks(hh, j, n):
        return vt_ref[hh * V_ROWS:(hh + 1) * V_ROWS, pl.ds(pl.multiple_of(j * blk, blk), n * blk)]

    q_heads, prev_chosen = [], []
    for hh in range(HEADS_PER_STEP):
        in_head = (lane >= (hh % 2) * MOBA_HEAD_DIM) & (lane < (hh % 2 + 1) * MOBA_HEAD_DIM)
        qh = jnp.where(in_head, q_ref[:, lane_group(hh)].astype(F32), 0.0).astype(BF16)
        q_heads.append(qh)

        kmean = kmean_ref[:, lane_group(hh)].astype(BF16)
        gate = jnp.where(blk_id < qbf, _dot_nt(kmean, qh), -jnp.inf)
        sel = jnp.zeros((nb, blk), F32)
        for _ in range(min(MOBA_TOPK, nb)):
            best = jnp.max(gate, axis=0, keepdims=True)
            first = jnp.min(jnp.where(gate == best, blk_id, float(nb)), axis=0, keepdims=True)
            pick = blk_id == first
            sel = jnp.where(pick, 1.0, sel)
            gate = jnp.where(pick, -jnp.inf, gate)
        prev_chosen.append(jnp.max(jnp.where(blk_id == qbf - 1.0, sel, 0.0), axis=0, keepdims=True))
        sel_s[hh] = jnp.where(blk_id < qbf - 1.0, sel, 0.0)

    lo = jnp.maximum(qb - 1, 0)
    tab = jnp.where(has_prev, 0, 1)
    scores = [[_dot_nt(k_block(hh, lo + u), q_heads[hh]) + near_ref[hh, tab + u] for u in range(2)]
              for hh in range(HEADS_PER_STEP)]
    for hh in range(HEADS_PER_STEP):
        chosen = [jnp.where(has_prev, prev_chosen[hh], 1.0) > 0.5,
                  jnp.where(has_prev, jnp.ones((1, blk), F32), 0.0) > 0.5]
        _online_softmax_group(m_s.at[hh], acc_s.at[hh], scores[hh], vt_blocks(hh, lo, 2), chosen)

    def far(g, carry):
        first = g * unroll
        start = jnp.minimum(first, nb - unroll)
        scores = [[_dot_nt(k_block(hh, start + u), q_heads[hh]) for u in range(unroll)]
                  for hh in range(HEADS_PER_STEP)]
        for hh in range(HEADS_PER_STEP):
            chosen = [jnp.where(start + u >= first, sel_s[hh, pl.ds(start + u, 1), :], 0.0) > 0.5
                      for u in range(unroll)]
            shifts = [cfar_ref[hh][:, 0:1]] * unroll
            _online_softmax_group(m_s.at[hh], acc_s.at[hh], scores[hh],
                                  vt_blocks(hh, start, unroll), chosen, shifts)
        return carry

    lax.fori_loop(0, (qb - 1 + unroll - 1) // unroll, far, 0)
    o_ref[...] = _normalised_heads(acc_s, MOBA_HEAD_DIM).T.astype(o_ref.dtype)


def _moba(qa, ka, vat, kmean, near, cfar, *, batch, seq, unroll):
    n = qa.shape[0]
    blk = MOBA_BLOCK
    nb = seq // blk
    assert nb >= unroll and nb >= 2
    steps = MOBA_HEADS // HEADS_PER_STEP
    width = HEADS_PER_STEP * MOBA_HEAD_DIM
    return pl.pallas_call(
        functools.partial(_moba_kernel, unroll=unroll),
        grid=(batch, steps, nb),
        in_specs=[
            pl.BlockSpec((blk, width), lambda b, p, i: (b * nb + i, p)),
            pl.BlockSpec((seq, width), lambda b, p, i: (b, p)),
            pl.BlockSpec((None, HEADS_PER_STEP * V_ROWS, seq), lambda b, p, i: (b, p, 0)),
            pl.BlockSpec((None, nb, width), lambda b, p, i: (b, 0, p)),
            pl.BlockSpec((HEADS_PER_STEP, 3, blk, blk), lambda b, p, i: (p, 0, 0, 0)),
            pl.BlockSpec((HEADS_PER_STEP, 1, LANES), lambda b, p, i: (p, 0, 0)),
        ],
        out_specs=pl.BlockSpec((blk, width), lambda b, p, i: (b * nb + i, p)),
        out_shape=jax.ShapeDtypeStruct((n, MOBA_WIDTH), BF16),
        scratch_shapes=[
            pltpu.VMEM((HEADS_PER_STEP, 1, blk), F32),
            pltpu.VMEM((HEADS_PER_STEP, V_ROWS, blk), F32),
            pltpu.VMEM((HEADS_PER_STEP, nb, blk), F32),
        ],
        compiler_params=pltpu.CompilerParams(
            dimension_semantics=("arbitrary", "arbitrary", "arbitrary"),
            vmem_limit_bytes=VMEM_LIMIT_BYTES),
        name="moba",
    )(qa, ka, vat, kmean, near, cfar)


def _mla_kernel(q_ref, k_ref, vt_ref, o_ref, m_s, acc_s, *, tq, tk, unroll):
    qi = pl.program_id(2)
    n_diag = tq // tk
    key = lax.broadcasted_iota(jnp.int32, (tk, tq), 0)
    qry = lax.broadcasted_iota(jnp.int32, (tk, tq), 1)
    q_heads = [q_ref[:, hh * LANES:(hh + 1) * LANES] for hh in range(HEADS_PER_STEP)]

    _reset_softmax_state(m_s, acc_s)

    def k_tile(hh, j):
        return k_ref[pl.ds(pl.multiple_of(j * tk, tk), tk), hh * LANES:(hh + 1) * LANES]

    def vt_tiles(hh, j, n):
        return vt_ref[hh * V_ROWS:(hh + 1) * V_ROWS, pl.ds(pl.multiple_of(j * tk, tk), n * tk)]

    def group(first, n, masked):
        scores = [[_dot_nt(k_tile(hh, first + u), q_heads[hh]) for u in range(n)]
                  for hh in range(HEADS_PER_STEP)]
        for hh in range(HEADS_PER_STEP):
            blocks = scores[hh]
            if masked:
                blocks = [jnp.where(key + u * tk <= qry, st, NEG) for u, st in enumerate(blocks)]
            _online_softmax_group(m_s.at[hh], acc_s.at[hh], blocks, vt_tiles(hh, first, n))

    group(qi * n_diag, n_diag, True)

    def body(g, carry):
        group(g * unroll, unroll, False)
        return carry

    lax.fori_loop(0, (qi * n_diag) // unroll, body, 0)
    o_ref[...] = _normalised_heads(acc_s, MLA_V_DIM).T.astype(o_ref.dtype)


def _mla(qm, km, vmt, *, batch, seq, tq, tk, unroll):
    n = qm.shape[0]
    nq = seq // tq
    assert tq % tk == 0 and (tq // tk) % unroll == 0
    pairs = MLA_HEADS // HEADS_PER_STEP
    return pl.pallas_call(
        functools.partial(_mla_kernel, tq=tq, tk=tk, unroll=unroll),
        grid=(batch, pairs, nq),
        in_specs=[
            pl.BlockSpec((tq, HEADS_PER_STEP * LANES), lambda b, p, i: (b * nq + i, p)),
            pl.BlockSpec((seq, HEADS_PER_STEP * LANES), lambda b, p, i: (b, p)),
            pl.BlockSpec((None, HEADS_PER_STEP * V_ROWS, seq), lambda b, p, i: (b, p, 0)),
        ],
        out_specs=pl.BlockSpec((tq, HEADS_PER_STEP * MLA_V_DIM), lambda b, p, i: (b * nq + i, p)),
        out_shape=jax.ShapeDtypeStruct((n, MLA_WIDTH), BF16),
        scratch_shapes=[
            pltpu.VMEM((HEADS_PER_STEP, 1, tq), F32),
            pltpu.VMEM((HEADS_PER_STEP, V_ROWS, tq), F32),
        ],
        compiler_params=pltpu.CompilerParams(
            dimension_semantics=("arbitrary", "arbitrary", "arbitrary"),
            vmem_limit_bytes=VMEM_LIMIT_BYTES),
        name="mla",
    )(qm, km, vmt)


def _back_kernel(x_ref, ya_ref, yb_ref, g_ref, wa_ref, wb_ref, wo_ref, gffn_ref, wup_ref,
                 cw_ref, cb_ref, wdn_ref, gfin_ref, o_ref,
                 carry_ref, ubufs_a, ubufs_b, act_ref, *, steps_per_seq, d_ff, fc, final):
    tm, d = x_ref.shape
    halo = SUBLANES
    n_chunks = d_ff // fc

    @pl.when(pl.program_id(0) % steps_per_seq == 0)
    def _():
        carry_ref[...] = jnp.zeros_like(carry_ref)

    g = g_ref[...].astype(F32)
    mixed = g[:, :d] * _dot(ya_ref[...], wa_ref[...]) + g[:, d:] * _dot(yb_ref[...], wb_ref[...])
    h1 = x_ref[...] + _dot(mixed.astype(BF16), wo_ref[...])
    hn = _rms(h1, gffn_ref[...]).astype(BF16)

    def columns(c):
        return pl.multiple_of(c * fc, fc), pl.multiple_of(d_ff + c * fc, fc)

    def up_project(c, ubufs):
        for half, col0 in enumerate(columns(c)):
            cols = pl.ds(col0, fc)
            u = _dot(hn, wup_ref[:, cols])
            ubufs[half, 0:halo, :] = carry_ref[:, cols]
            ubufs[half, halo:halo + tm, :] = u
            carry_ref[:, cols] = u[tm - halo:tm, :]

    def conv(col0, ubuf):
        cols = pl.ds(col0, fc)
        w = cw_ref[:, cols]
        y = cb_ref[:, cols]
        rows = ubuf[...]
        for t in range(CONV_WIDTH):
            back = CONV_WIDTH - 1 - t
            shifted = pltpu.roll(rows, back, axis=0) if back else rows
            y = y + w[t:t + 1, :] * shifted[halo:, :]
        return y

    def gate(c, ubufs):
        col_g, col_v = columns(c)
        yg = conv(col_g, ubufs.at[0])
        yv = conv(col_v, ubufs.at[1])
        act_ref[:, pl.ds(col_g, fc)] = (yg * _sigmoid(yg) * yv).astype(BF16)

    up_project(0, ubufs_a)

    def pair(k, carry):
        up_project(2 * k + 1, ubufs_b)
        gate(2 * k, ubufs_a)
        up_project(2 * k + 2, ubufs_a)
        gate(2 * k + 1, ubufs_b)
        return carry

    lax.fori_loop(0, (n_chunks - 1) // 2, pair, 0)
    if n_chunks % 2:
        gate(n_chunks - 1, ubufs_a)
    else:
        up_project(n_chunks - 1, ubufs_b)
        gate(n_chunks - 2, ubufs_a)
        gate(n_chunks - 1, ubufs_b)

    h2 = h1 + _dot(act_ref[...], wdn_ref[...])
    o_ref[...] = _rms(h2, gfin_ref[...]) if final else h2


def _back(x2, ya, yb, g, wa, wb, wo, gffn, wup, cw, cb, wdn, gfin, *, seq, tm, fc, final):
    n, d = x2.shape
    d_ff = wdn.shape[0]
    row = lambda i: (i, 0)
    return pl.pallas_call(
        functools.partial(_back_kernel, steps_per_seq=seq // tm, d_ff=d_ff, fc=fc, final=final),
        grid=(n // tm,),
        in_specs=[
            pl.BlockSpec((tm, d), row),
            pl.BlockSpec((tm, ya.shape[1]), row),
            pl.BlockSpec((tm, yb.shape[1]), row),
            pl.BlockSpec((tm, g.shape[1]), row),
            _resident(wa.shape), _resident(wb.shape), _resident(wo.shape), _resident(gffn.shape),
            _resident(wup.shape), _resident(cw.shape), _resident(cb.shape), _resident(wdn.shape),
            _resident(gfin.shape),
        ],
        out_specs=pl.BlockSpec((tm, d), row),
        out_shape=jax.ShapeDtypeStruct((n, d), F32),
        scratch_shapes=[
            pltpu.VMEM((SUBLANES, 2 * d_ff), F32),
            pltpu.VMEM((2, SUBLANES + tm, fc), F32),
            pltpu.VMEM((2, SUBLANES + tm, fc), F32),
            pltpu.VMEM((tm, d_ff), BF16),
        ],
        compiler_params=pltpu.CompilerParams(
            dimension_semantics=("arbitrary",), vmem_limit_bytes=VMEM_LIMIT_BYTES),
        name="back",
    )(x2, ya, yb, g, wa, wb, wo, gffn, wup, cw, cb, wdn, gfin)


def _t5_bucket_np(rel):
    n = np.maximum(rel, 0)
    max_exact = REL_BUCKETS // 2
    nf = np.maximum(n, 1).astype(np.float32)
    large = max_exact + (np.log(nf / np.float32(max_exact)) / np.float32(math.log(REL_MAX_DIST / max_exact))
                         * np.float32(REL_BUCKETS - max_exact)).astype(np.int32)
    large = np.minimum(large, REL_BUCKETS - 1)
    return np.where(n < max_exact, n, large)


def _moba_bias_tables(rel_bias):
    blk = MOBA_BLOCK
    assert REL_MAX_DIST <= blk + 1
    off = np.arange(blk, dtype=np.int32)
    rel_own = off[None, :] - off[:, None]
    bias_h = rel_bias.T.astype(F32)

    def lookup(bucket):
        onehot = jnp.asarray(bucket)[None, :, :] == jnp.arange(REL_BUCKETS)[:, None, None]
        return jnp.sum(jnp.where(onehot[None], bias_h[:, :, None, None], 0.0), axis=1)

    town = jnp.where(jnp.asarray(rel_own >= 0), lookup(_t5_bucket_np(rel_own)) * LOG2E, NEG)
    tprev = lookup(_t5_bucket_np(rel_own + blk)) * LOG2E
    near = jnp.stack([tprev, town, jnp.zeros_like(town)], axis=1)
    cfar = bias_h[:, int(_t5_bucket_np(np.int32(blk + 1)))] * LOG2E
    cfar = jnp.broadcast_to(cfar[:, None, None], (MOBA_HEADS, 1, LANES))
    return near, cfar


def _pad_heads(w, n_heads, width):
    r = w.shape[0]
    w = w.reshape(r, n_heads, width)
    out = jnp.zeros((r, n_heads, LANES), w.dtype).at[:, :, :width].set(w)
    return out.reshape(r, n_heads * LANES)


def _rope_slot(w_rope, swap):
    half = MLA_ROPE_DIM // 2
    if swap:
        w_rope = jnp.concatenate([w_rope[:, half:], w_rope[:, :half]], axis=1)
    r = w_rope.shape[0]
    return jnp.zeros((r, LANES), w_rope.dtype).at[:, MLA_NOPE_DIM:MLA_NOPE_DIM + MLA_ROPE_DIM].set(w_rope)


def _front_weights(w_in, w_uq, w_ukv):
    d = w_in.shape[0]
    s0 = MOBA_WIDTH
    s1 = 2 * MOBA_WIDTH
    s2 = 3 * MOBA_WIDTH
    s3 = s2 + MLA_Q_RANK
    s4 = s3 + MLA_KV_RANK
    s5 = s4 + MLA_ROPE_DIM
    w1 = jnp.concatenate([
        w_in[:, :s0], w_in[:, s0:s1],
        w_in[:, s2:s3], w_in[:, s3:s4],
        _rope_slot(w_in[:, s4:s5], False), _rope_slot(w_in[:, s4:s5], True),
        w_in[:, s5:],
    ], axis=1).astype(BF16)
    assert w1.shape == (d, _C_END)
    wvt = _pad_heads(w_in[:, s1:s2], MOBA_HEADS, MOBA_HEAD_DIM).T.astype(BF16)

    r = w_uq.shape[0]
    uq = w_uq.reshape(r, MLA_HEADS, MLA_QK_DIM)
    nope, rope = uq[:, :, :MLA_NOPE_DIM], uq[:, :, MLA_NOPE_DIM:]
    half = MLA_ROPE_DIM // 2
    zpad = jnp.zeros((r, MLA_HEADS, LANES - MLA_QK_DIM), w_uq.dtype)
    direct = jnp.concatenate([nope, rope, zpad], axis=2)
    swapped = jnp.concatenate([jnp.zeros_like(nope), rope[:, :, half:], rope[:, :, :half], zpad], axis=2)
    wuq2 = jnp.concatenate([direct.reshape(r, -1), swapped.reshape(r, -1)], axis=1).astype(BF16)

    r = w_ukv.shape[0]
    ukv = w_ukv.reshape(r, MLA_HEADS, MLA_NOPE_DIM + MLA_V_DIM)
    wk = _pad_heads(ukv[:, :, :MLA_NOPE_DIM].reshape(r, -1), MLA_HEADS, MLA_NOPE_DIM).astype(BF16)
    wkvt = _pad_heads(ukv[:, :, MLA_NOPE_DIM:].reshape(r, -1), MLA_HEADS, MLA_V_DIM).T.astype(BF16)
    return w1, wvt, wuq2, wk, wkvt


def _rope_lane_tables(seq):
    dim = MLA_ROPE_DIM
    inv_freq = ROPE_THETA ** (-jnp.arange(0, dim, 2, dtype=F32) / dim)
    ang = jnp.arange(seq, dtype=F32)[:, None] * inv_freq[None, :]
    cos, sin = jnp.cos(ang), jnp.sin(ang)
    tail = jnp.zeros((seq, LANES - MLA_QK_DIM), F32)
    cosm = jnp.concatenate([jnp.ones((seq, MLA_NOPE_DIM), F32), cos, cos, tail], axis=1)
    sinm = jnp.concatenate([jnp.zeros((seq, MLA_NOPE_DIM), F32), -sin, sin, tail], axis=1)
    return cosm, sinm


class _Tiles:
    def __init__(self, seq):
        self.tm = 512 if seq % 512 == 0 else MOBA_BLOCK
        self.fc = 256
        self.mla_tq = self.tm
        self.mla_tk = 256
        self.mla_unroll = self.mla_tq // self.mla_tk
        self.moba_unroll = 4


def kernel(x, norm_attn_g, w_in, b_gate, q_norm_g, w_uq, kv_norm_g, w_ukv, rel_bias,
           w_branch_moba, w_branch_mla, w_out, norm_ffn_g, w_up, conv_w, conv_b, w_down,
           norm_final_g):
    batch, seq, d = x.shape
    depth = w_in.shape[0]
    assert seq % MOBA_BLOCK == 0
    t = _Tiles(seq)
    d_ff = w_down.shape[1]
    assert d_ff % t.fc == 0
    n = batch * seq

    cosm, sinm = _rope_lane_tables(seq)
    near, cfar = _moba_bias_tables(rel_bias)
    row = lambda v: v.reshape(1, -1).astype(F32)

    h = x.reshape(n, d)
    for l in range(depth):
        w1, wvt, wuq2, wk, wkvt = _front_weights(w_in[l], w_uq[l], w_ukv[l])
        qa, ka, kmean, vat, qm, km, vmt, g = _front(
            h, row(norm_attn_g[l]), w1, wvt, row(b_gate[l]), row(q_norm_g[l]), wuq2,
            row(kv_norm_g[l]), wk, wkvt, cosm, sinm, seq=seq, tm=t.tm)
        kmean = kmean.reshape(batch, seq // MOBA_BLOCK, MOBA_WIDTH)
        ya = _moba(qa, ka, vat, kmean, near, cfar, batch=batch, seq=seq,
                   unroll=t.moba_unroll)
        yb = _mla(qm, km, vmt, batch=batch, seq=seq, tq=t.mla_tq, tk=t.mla_tk,
                  unroll=t.mla_unroll)
        h = _back(h, ya, yb, g, w_branch_moba[l].astype(BF16), w_branch_mla[l].astype(BF16),
                  w_out[l].astype(BF16), row(norm_ffn_g[l]), w_up[l].astype(BF16),
                  conv_w[l].astype(F32), row(conv_b[l]), w_down[l].astype(BF16),
                  row(norm_final_g), seq=seq, tm=t.tm, fc=t.fc, final=(l == depth - 1))
    return h.reshape(batch, seq, d)
```

```python
import functools
import math

import numpy as np
import jax
import jax.numpy as jnp
from jax import lax
from jax.experimental import pallas as pl
from jax.experimental.pallas import tpu as pltpu

MOBA_HEADS = 8
MOBA_HEAD_DIM = 64
MOBA_BLOCK = 256
MOBA_TOPK = 3
MLA_HEADS = 8
MLA_Q_RANK = 256
MLA_KV_RANK = 128
MLA_NOPE_DIM = 64
MLA_ROPE_DIM = 32
MLA_V_DIM = 64
ROPE_THETA = 10000.0
REL_BUCKETS = 32
REL_MAX_DIST = 128
CONV_WIDTH = 3
N_BRANCH = 2
EPS = 1e-6

MOBA_WIDTH = MOBA_HEADS * MOBA_HEAD_DIM
MLA_QK_DIM = MLA_NOPE_DIM + MLA_ROPE_DIM
MLA_WIDTH = MLA_HEADS * MLA_V_DIM

LANES = 128
SUBLANES = 8
VMEM_LIMIT_BYTES = 56 * 1024 * 1024

NEG = -1e30

F32 = jnp.float32
BF16 = jnp.bfloat16


def _dot(a, b):
    return jnp.dot(a, b, preferred_element_type=F32)


def _dot_nt(a, b):
    return lax.dot_general(a, b, (((1,), (1,)), ((), ())), preferred_element_type=F32)


def _rms(x, g):
    return x * lax.rsqrt(jnp.mean(x * x, axis=-1, keepdims=True) + EPS) * g


def _sigmoid(z):
    return 1.0 / (1.0 + jnp.exp(-z))


def _resident(shape):
    nd = len(shape)
    return pl.BlockSpec(shape, lambda *_: (0,) * nd, pipeline_mode=pl.Buffered(1))


def _per_batch(shape, index_map):
    return pl.BlockSpec(shape, index_map, pipeline_mode=pl.Buffered(1))


_C_QA = 0
_C_KA = _C_QA + MOBA_WIDTH
_C_CQ = _C_KA + MOBA_WIDTH
_C_CKV = _C_CQ + MLA_Q_RANK
_C_KRA = _C_CKV + MLA_KV_RANK
_C_KRB = _C_KRA + LANES
_C_G = _C_KRB + LANES
_C_END = _C_G + N_BRANCH * 1024

BF16_SUBLANES = 16
V_ROWS = MLA_V_DIM + BF16_SUBLANES
assert MOBA_HEAD_DIM == MLA_V_DIM


def _ones_rows(n_rows):
    r = lax.broadcasted_iota(jnp.int32, (n_rows, 1), 0).astype(F32)
    within = r - jnp.floor((r + 0.5) * (1.0 / V_ROWS)) * V_ROWS
    return jnp.where(within == MLA_V_DIM, 1.0, 0.0).astype(F32)


def _front_kernel(x_ref, gattn_ref, w1_ref, wvt_ref, bg_ref, qng_ref, wuq_ref, kvng_ref, wkv_ref,
                  wkvt_ref, cos_ref, sin_ref,
                  qa_ref, ka_ref, kmean_ref, vat_ref, qm_ref, km_ref, vmt_ref, g_ref,
                  *, d_model, mla_scale):
    tm = x_ref.shape[0]
    hw = MLA_HEADS * LANES
    xn = _rms(x_ref[...], gattn_ref[...]).astype(BF16)

    def proj(a, b):
        return _dot(xn, w1_ref[:, a:b])

    qa_ref[...] = (proj(_C_QA, _C_KA) * (MOBA_HEAD_DIM ** -0.5 * LOG2E)).astype(BF16)
    ka = proj(_C_KA, _C_CQ)
    ka_ref[...] = ka.astype(BF16)
    for i in range(tm // MOBA_BLOCK):
        kmean_ref[i] = jnp.mean(ka[i * MOBA_BLOCK:(i + 1) * MOBA_BLOCK], axis=0, keepdims=True)
    ones = _ones_rows(vat_ref.shape[0])
    vat_ref[...] = (_dot_nt(wvt_ref[...], xn) + ones).astype(BF16)

    cosm = cos_ref[...]
    sinm = sin_ref[...]

    cqn = _rms(proj(_C_CQ, _C_CKV), qng_ref[...]).astype(BF16)
    qq = _dot(cqn, wuq_ref[...])
    for h in range(MLA_HEADS):
        a = qq[:, h * LANES:(h + 1) * LANES]
        b = qq[:, hw + h * LANES:hw + (h + 1) * LANES]
        qm_ref[:, h * LANES:(h + 1) * LANES] = ((a * cosm + b * sinm) * mla_scale).astype(BF16)

    ckvn = _rms(proj(_C_CKV, _C_KRA), kvng_ref[...]).astype(BF16)
    kv = _dot(ckvn, wkv_ref[...])
    kr = proj(_C_KRA, _C_KRB) * cosm + proj(_C_KRB, _C_G) * sinm
    for h in range(MLA_HEADS):
        km_ref[:, h * LANES:(h + 1) * LANES] = (kv[:, h * LANES:(h + 1) * LANES] + kr).astype(BF16)
    vmt_ref[...] = (_dot_nt(wkvt_ref[...], ckvn) + ones).astype(BF16)

    g_ref[...] = _sigmoid(proj(_C_G, _C_END) + bg_ref[...]).astype(BF16)


def _front(x2, gattn, w1, wvt, bg, qng, wuq2, kvng, wkv, wkvt, cosm, sinm, *, seq, tm):
    n, d = x2.shape
    batch = n // seq
    hw = MLA_HEADS * LANES
    nblk = n // MOBA_BLOCK
    steps_per_seq = seq // tm
    row = lambda i: (i, 0)
    tcol = lambda i: (i // steps_per_seq, 0, i % steps_per_seq)
    out_shape = (
        jax.ShapeDtypeStruct((n, MOBA_WIDTH), BF16),
        jax.ShapeDtypeStruct((n, MOBA_WIDTH), BF16),
        jax.ShapeDtypeStruct((nblk, 1, MOBA_WIDTH), F32),
        jax.ShapeDtypeStruct((batch, MOBA_HEADS * V_ROWS, seq), BF16),
        jax.ShapeDtypeStruct((n, hw), BF16),
        jax.ShapeDtypeStruct((n, hw), BF16),
        jax.ShapeDtypeStruct((batch, MLA_HEADS * V_ROWS, seq), BF16),
        jax.ShapeDtypeStruct((n, N_BRANCH * d), BF16),
    )
    in_specs = [
        pl.BlockSpec((tm, d), row),
        _resident(gattn.shape), _resident(w1.shape), _resident(wvt.shape), _resident(bg.shape),
        _resident(qng.shape), _resident(wuq2.shape), _resident(kvng.shape), _resident(wkv.shape),
        _resident(wkvt.shape),
        pl.BlockSpec((tm, LANES), lambda i: (i % steps_per_seq, 0)),
        pl.BlockSpec((tm, LANES), lambda i: (i % steps_per_seq, 0)),
    ]
    out_specs = (
        pl.BlockSpec((tm, MOBA_WIDTH), row),
        pl.BlockSpec((tm, MOBA_WIDTH), row),
        pl.BlockSpec((tm // MOBA_BLOCK, 1, MOBA_WIDTH), lambda i: (i, 0, 0)),
        pl.BlockSpec((None, MOBA_HEADS * V_ROWS, tm), tcol),
        pl.BlockSpec((tm, hw), row),
        pl.BlockSpec((tm, hw), row),
        pl.BlockSpec((None, MLA_HEADS * V_ROWS, tm), tcol),
        pl.BlockSpec((tm, N_BRANCH * d), row),
    )
    return pl.pallas_call(
        functools.partial(_front_kernel, d_model=d, mla_scale=MLA_QK_DIM ** -0.5 * LOG2E),
        grid=(n // tm,),
        in_specs=in_specs,
        out_specs=out_specs,
        out_shape=out_shape,
        compiler_params=pltpu.CompilerParams(
            dimension_semantics=("arbitrary",), vmem_limit_bytes=VMEM_LIMIT_BYTES),
        name="front",
    )(x2, gattn, w1, wvt, bg, qng, wuq2, kvng, wkv, wkvt, cosm, sinm)


HEADS_PER_STEP = 8
LOG2E = math.log2(math.e)
BIG = 1e30


def _online_softmax_group(m_ref, acc_ref, score_blocks, vt_grp, chosen=None, shifts=None):
    n = len(score_blocks)
    chosen = chosen or [None] * n
    shifts = shifts or [None] * n
    m_old = m_ref[...]
    m_new = m_old
    for st, ch, sh in zip(score_blocks, chosen, shifts):
        rm = jnp.max(st, axis=0, keepdims=True)
        if sh is not None:
            rm = rm + sh
        m_new = jnp.maximum(m_new, rm if ch is None else jnp.where(ch, rm, NEG))
    probs = []
    for st, ch, sh in zip(score_blocks, chosen, shifts):
        off = m_new if sh is None else m_new - sh
        if ch is not None:
            off = jnp.where(ch, off, BIG)
        probs.append(jnp.exp2(st - off).astype(BF16))
    ot = _dot(vt_grp, jnp.concatenate(probs, axis=0))
    acc_ref[...] = jnp.exp2(m_old - m_new) * acc_ref[...] + ot
    m_ref[...] = m_new


def _reset_softmax_state(m_s, acc_s):
    m_s[...] = jnp.full(m_s.shape, NEG, F32)
    acc_s[...] = jnp.zeros(acc_s.shape, F32)


def _normalised_heads(acc_s, v_dim):
    rows = []
    for hh in range(HEADS_PER_STEP):
        acc = acc_s[hh]
        rows.append(acc[0:v_dim, :] * (1.0 / acc[v_dim:v_dim + 1, :]))
    return jnp.concatenate(rows, axis=0)


def _moba_kernel(q_ref, k_ref, vt_ref, kmean_ref, near_ref, cfar_ref, o_ref,
                 m_s, acc_s, sel_s, *, unroll):
    blk = MOBA_BLOCK
    qb = pl.program_id(2)
    nb = kmean_ref.shape[0]
    lane = lax.broadcasted_iota(jnp.int32, (blk, LANES), 1)
    blk_id = lax.broadcasted_iota(jnp.int32, (nb, blk), 0).astype(F32)
    qbf = qb.astype(F32)
    has_prev = qb >= 1
    _reset_softmax_state(m_s, acc_s)

    def lane_group(hh):
        return slice((hh // 2) * LANES, (hh // 2 + 1) * LANES)

    def k_block(hh, j):
        return k_ref[pl.ds(pl.multiple_of(j * blk, blk), blk), lane_group(hh)]

    def vt_blocks(hh, j, n):
        return vt_ref[hh * V_ROWS:(hh + 1) * V_ROWS, pl.ds(pl.multiple_of(j * blk, blk), n * blk)]

    q_heads, prev_chosen = [], []
    for hh in range(HEADS_PER_STEP):
        in_head = (lane >= (hh % 2) * MOBA_HEAD_DIM) & (lane < (hh % 2 + 1) * MOBA_HEAD_DIM)
        qh = jnp.where(in_head, q_ref[:, lane_group(hh)].astype(F32), 0.0).astype(BF16)
        q_heads.append(qh)

        kmean = kmean_ref[:, lane_group(hh)].astype(BF16)
        gate = jnp.where(blk_id < qbf, _dot_nt(kmean, qh), -jnp.inf)
        sel = jnp.zeros((nb, blk), F32)
        for _ in range(min(MOBA_TOPK, nb)):
            best = jnp.max(gate, axis=0, keepdims=True)
            first = jnp.min(jnp.where(gate == best, blk_id, float(nb)), axis=0, keepdims=True)
            pick = blk_id == first
            sel = jnp.where(pick, 1.0, sel)
            gate = jnp.where(pick, -jnp.inf, gate)
        prev_chosen.append(jnp.max(jnp.where(blk_id == qbf - 1.0, sel, 0.0), axis=0, keepdims=True))
        sel_s[hh] = jnp.where(blk_id < qbf - 1.0, sel, 0.0)

    lo = jnp.maximum(qb - 1, 0)
    tab = jnp.where(has_prev, 0, 1)
    scores = [[_dot_nt(k_block(hh, lo + u), q_heads[hh]) + near_ref[hh, tab + u] for u in range(2)]
              for hh in range(HEADS_PER_STEP)]
    for hh in range(HEADS_PER_STEP):
        chosen = [jnp.where(has_prev, prev_chosen[hh], 1.0) > 0.5,
                  jnp.where(has_prev, jnp.ones((1, blk), F32), 0.0) > 0.5]
        _online_softmax_group(m_s.at[hh], acc_s.at[hh], scores[hh], vt_blocks(hh, lo, 2), chosen)

    def far(g, carry):
        first = g * unroll
        start = jnp.minimum(first, nb - unroll)
        scores = [[_dot_nt(k_block(hh, start + u), q_heads[hh]) for u in range(unroll)]
                  for hh in range(HEADS_PER_STEP)]
        for hh in range(HEADS_PER_STEP):
            chosen = [jnp.where(start + u >= first, sel_s[hh, pl.ds(start + u, 1), :], 0.0) > 0.5
                      for u in range(unroll)]
            shifts = [cfar_ref[hh][:, 0:1]] * unroll
            _online_softmax_group(m_s.at[hh], acc_s.at[hh], scores[hh],
                                  vt_blocks(hh, start, unroll), chosen, shifts)
        return carry

    lax.fori_loop(0, (qb - 1 + unroll - 1) // unroll, far, 0)
    o_ref[...] = _normalised_heads(acc_s, MOBA_HEAD_DIM).T.astype(o_ref.dtype)


def _moba(qa, ka, vat, kmean, near, cfar, *, batch, seq, unroll):
    n = qa.shape[0]
    blk = MOBA_BLOCK
    nb = seq // blk
    assert nb >= unroll and nb >= 2
    steps = MOBA_HEADS // HEADS_PER_STEP
    width = HEADS_PER_STEP * MOBA_HEAD_DIM
    return pl.pallas_call(
        functools.partial(_moba_kernel, unroll=unroll),
        grid=(batch, steps, nb),
        in_specs=[
            pl.BlockSpec((blk, width), lambda b, p, i: (b * nb + i, p)),
            _per_batch((seq, width), lambda b, p, i: (b, p)),
            _per_batch((None, HEADS_PER_STEP * V_ROWS, seq), lambda b, p, i: (b, p, 0)),
            pl.BlockSpec((None, nb, width), lambda b, p, i: (b, 0, p)),
            _per_batch((HEADS_PER_STEP, 3, blk, blk), lambda b, p, i: (p, 0, 0, 0)),
            pl.BlockSpec((HEADS_PER_STEP, 1, LANES), lambda b, p, i: (p, 0, 0)),
        ],
        out_specs=pl.BlockSpec((blk, width), lambda b, p, i: (b * nb + i, p)),
        out_shape=jax.ShapeDtypeStruct((n, MOBA_WIDTH), BF16),
        scratch_shapes=[
            pltpu.VMEM((HEADS_PER_STEP, 1, blk), F32),
            pltpu.VMEM((HEADS_PER_STEP, V_ROWS, blk), F32),
            pltpu.VMEM((HEADS_PER_STEP, nb, blk), F32),
        ],
        compiler_params=pltpu.CompilerParams(
            dimension_semantics=("arbitrary", "arbitrary", "arbitrary"),
            vmem_limit_bytes=VMEM_LIMIT_BYTES),
        name="moba",
    )(qa, ka, vat, kmean, near, cfar)


def _mla_kernel(q_ref, k_ref, vt_ref, o_ref, m_s, acc_s, *, tq, tk, unroll):
    qi = pl.program_id(2)
    n_diag = tq // tk
    key = lax.broadcasted_iota(jnp.int32, (tk, tq), 0)
    qry = lax.broadcasted_iota(jnp.int32, (tk, tq), 1)
    q_heads = [q_ref[:, hh * LANES:(hh + 1) * LANES] for hh in range(HEADS_PER_STEP)]

    _reset_softmax_state(m_s, acc_s)

    def k_tile(hh, j):
        return k_ref[pl.ds(pl.multiple_of(j * tk, tk), tk), hh * LANES:(hh + 1) * LANES]

    def vt_tiles(hh, j, n):
        return vt_ref[hh * V_ROWS:(hh + 1) * V_ROWS, pl.ds(pl.multiple_of(j * tk, tk), n * tk)]

    def group(first, n, masked):
        scores = [[_dot_nt(k_tile(hh, first + u), q_heads[hh]) for u in range(n)]
                  for hh in range(HEADS_PER_STEP)]
        for hh in range(HEADS_PER_STEP):
            blocks = scores[hh]
            if masked:
                blocks = [jnp.where(key + u * tk <= qry, st, NEG) for u, st in enumerate(blocks)]
            _online_softmax_group(m_s.at[hh], acc_s.at[hh], blocks, vt_tiles(hh, first, n))

    group(qi * n_diag, n_diag, True)

    def body(g, carry):
        group(g * unroll, unroll, False)
        return carry

    lax.fori_loop(0, (qi * n_diag) // unroll, body, 0)
    o_ref[...] = _normalised_heads(acc_s, MLA_V_DIM).T.astype(o_ref.dtype)


def _mla(qm, km, vmt, *, batch, seq, tq, tk, unroll):
    n = qm.shape[0]
    nq = seq // tq
    assert tq % tk == 0 and (tq // tk) % unroll == 0
    pairs = MLA_HEADS // HEADS_PER_STEP
    return pl.pallas_call(
        functools.partial(_mla_kernel, tq=tq, tk=tk, unroll=unroll),
        grid=(batch, pairs, nq),
        in_specs=[
            pl.BlockSpec((tq, HEADS_PER_STEP * LANES), lambda b, p, i: (b * nq + i, p)),
            _per_batch((seq, HEADS_PER_STEP * LANES), lambda b, p, i: (b, p)),
            _per_batch((None, HEADS_PER_STEP * V_ROWS, seq), lambda b, p, i: (b, p, 0)),
        ],
        out_specs=pl.BlockSpec((tq, HEADS_PER_STEP * MLA_V_DIM), lambda b, p, i: (b * nq + i, p)),
        out_shape=jax.ShapeDtypeStruct((n, MLA_WIDTH), BF16),
        scratch_shapes=[
            pltpu.VMEM((HEADS_PER_STEP, 1, tq), F32),
            pltpu.VMEM((HEADS_PER_STEP, V_ROWS, tq), F32),
        ],
        compiler_params=pltpu.CompilerParams(
            dimension_semantics=("arbitrary", "arbitrary", "arbitrary"),
            vmem_limit_bytes=VMEM_LIMIT_BYTES),
        name="mla",
    )(qm, km, vmt)


def _back_kernel(x_ref, ya_ref, yb_ref, g_ref, wa_ref, wb_ref, wo_ref, gffn_ref, wup_ref,
                 cw_ref, cb_ref, wdn_ref, gfin_ref, o_ref,
                 carry_ref, ubufs_a, ubufs_b, act_ref, *, steps_per_seq, d_ff, fc, final):
    tm, d = x_ref.shape
    halo = SUBLANES
    n_chunks = d_ff // fc

    @pl.when(pl.program_id(0) % steps_per_seq == 0)
    def _():
        carry_ref[...] = jnp.zeros_like(carry_ref)

    g = g_ref[...].astype(F32)
    mixed = g[:, :d] * _dot(ya_ref[...], wa_ref[...]) + g[:, d:] * _dot(yb_ref[...], wb_ref[...])
    h1 = x_ref[...] + _dot(mixed.astype(BF16), wo_ref[...])
    hn = _rms(h1, gffn_ref[...]).astype(BF16)

    def columns(c):
        return pl.multiple_of(c * fc, fc), pl.multiple_of(d_ff + c * fc, fc)

    def up_project(c, ubufs):
        for half, col0 in enumerate(columns(c)):
            cols = pl.ds(col0, fc)
            u = _dot(hn, wup_ref[:, cols])
            ubufs[half, 0:halo, :] = carry_ref[:, cols]
            ubufs[half, halo:halo + tm, :] = u
            carry_ref[:, cols] = u[tm - halo:tm, :]

    def conv(col0, ubuf):
        cols = pl.ds(col0, fc)
        w = cw_ref[:, cols]
        y = cb_ref[:, cols]
        rows = ubuf[...]
        for t in range(CONV_WIDTH):
            back = CONV_WIDTH - 1 - t
            shifted = pltpu.roll(rows, back, axis=0) if back else rows
            y = y + w[t:t + 1, :] * shifted[halo:, :]
        return y

    def gate(c, ubufs):
        col_g, col_v = columns(c)
        yg = conv(col_g, ubufs.at[0])
        yv = conv(col_v, ubufs.at[1])
        act_ref[:, pl.ds(col_g, fc)] = (yg * _sigmoid(yg) * yv).astype(BF16)

    up_project(0, ubufs_a)

    def pair(k, carry):
        up_project(2 * k + 1, ubufs_b)
        gate(2 * k, ubufs_a)
        up_project(2 * k + 2, ubufs_a)
        gate(2 * k + 1, ubufs_b)
        return carry

    lax.fori_loop(0, (n_chunks - 1) // 2, pair, 0)
    if n_chunks % 2:
        gate(n_chunks - 1, ubufs_a)
    else:
        up_project(n_chunks - 1, ubufs_b)
        gate(n_chunks - 2, ubufs_a)
        gate(n_chunks - 1, ubufs_b)

    h2 = h1 + _dot(act_ref[...], wdn_ref[...])
    o_ref[...] = _rms(h2, gfin_ref[...]) if final else h2


def _back(x2, ya, yb, g, wa, wb, wo, gffn, wup, cw, cb, wdn, gfin, *, seq, tm, fc, final):
    n, d = x2.shape
    d_ff = wdn.shape[0]
    row = lambda i: (i, 0)
    return pl.pallas_call(
        functools.partial(_back_kernel, steps_per_seq=seq // tm, d_ff=d_ff, fc=fc, final=final),
        grid=(n // tm,),
        in_specs=[
            pl.BlockSpec((tm, d), row),
            pl.BlockSpec((tm, ya.shape[1]), row),
            pl.BlockSpec((tm, yb.shape[1]), row),
            pl.BlockSpec((tm, g.shape[1]), row),
            _resident(wa.shape), _resident(wb.shape), _resident(wo.shape), _resident(gffn.shape),
            _resident(wup.shape), _resident(cw.shape), _resident(cb.shape), _resident(wdn.shape),
            _resident(gfin.shape),
        ],
        out_specs=pl.BlockSpec((tm, d), row),
        out_shape=jax.ShapeDtypeStruct((n, d), F32),
        scratch_shapes=[
            pltpu.VMEM((SUBLANES, 2 * d_ff), F32),
            pltpu.VMEM((2, SUBLANES + tm, fc), F32),
            pltpu.VMEM((2, SUBLANES + tm, fc), F32),
            pltpu.VMEM((tm, d_ff), BF16),
        ],
        compiler_params=pltpu.CompilerParams(
            dimension_semantics=("arbitrary",), vmem_limit_bytes=VMEM_LIMIT_BYTES),
        name="back",
    )(x2, ya, yb, g, wa, wb, wo, gffn, wup, cw, cb, wdn, gfin)


def _t5_bucket_np(rel):
    n = np.maximum(rel, 0)
    max_exact = REL_BUCKETS // 2
    nf = np.maximum(n, 1).astype(np.float32)
    large = max_exact + (np.log(nf / np.float32(max_exact)) / np.float32(math.log(REL_MAX_DIST / max_exact))
                         * np.float32(REL_BUCKETS - max_exact)).astype(np.int32)
    large = np.minimum(large, REL_BUCKETS - 1)
    return np.where(n < max_exact, n, large)


def _moba_bias_tables(rel_bias):
    blk = MOBA_BLOCK
    assert REL_MAX_DIST <= blk + 1
    off = np.arange(blk, dtype=np.int32)
    rel_own = off[None, :] - off[:, None]
    bias_h = rel_bias.T.astype(F32)

    def lookup(bucket):
        onehot = jnp.asarray(bucket)[None, :, :] == jnp.arange(REL_BUCKETS)[:, None, None]
        return jnp.sum(jnp.where(onehot[None], bias_h[:, :, None, None], 0.0), axis=1)

    town = jnp.where(jnp.asarray(rel_own >= 0), lookup(_t5_bucket_np(rel_own)) * LOG2E, NEG)
    tprev = lookup(_t5_bucket_np(rel_own + blk)) * LOG2E
    near = jnp.stack([tprev, town, jnp.zeros_like(town)], axis=1)
    cfar = bias_h[:, int(_t5_bucket_np(np.int32(blk + 1)))] * LOG2E
    cfar = jnp.broadcast_to(cfar[:, None, None], (MOBA_HEADS, 1, LANES))
    return near, cfar


def _pad_heads(w, n_heads, width, padded=LANES):
    r = w.shape[0]
    w = w.reshape(r, n_heads, width)
    out = jnp.zeros((r, n_heads, padded), w.dtype).at[:, :, :width].set(w)
    return out.reshape(r, n_heads * padded)


def _rope_slot(w_rope, swap):
    half = MLA_ROPE_DIM // 2
    if swap:
        w_rope = jnp.concatenate([w_rope[:, half:], w_rope[:, :half]], axis=1)
    r = w_rope.shape[0]
    return jnp.zeros((r, LANES), w_rope.dtype).at[:, MLA_NOPE_DIM:MLA_NOPE_DIM + MLA_ROPE_DIM].set(w_rope)


def _front_weights(w_in, w_uq, w_ukv):
    d = w_in.shape[0]
    s0 = MOBA_WIDTH
    s1 = 2 * MOBA_WIDTH
    s2 = 3 * MOBA_WIDTH
    s3 = s2 + MLA_Q_RANK
    s4 = s3 + MLA_KV_RANK
    s5 = s4 + MLA_ROPE_DIM
    w1 = jnp.concatenate([
        w_in[:, :s0], w_in[:, s0:s1],
        w_in[:, s2:s3], w_in[:, s3:s4],
        _rope_slot(w_in[:, s4:s5], False), _rope_slot(w_in[:, s4:s5], True),
        w_in[:, s5:],
    ], axis=1).astype(BF16)
    assert w1.shape == (d, _C_END)
    wvt = _pad_heads(w_in[:, s1:s2], MOBA_HEADS, MOBA_HEAD_DIM, V_ROWS).T.astype(BF16)

    r = w_uq.shape[0]
    uq = w_uq.reshape(r, MLA_HEADS, MLA_QK_DIM)
    nope, rope = uq[:, :, :MLA_NOPE_DIM], uq[:, :, MLA_NOPE_DIM:]
    half = MLA_ROPE_DIM // 2
    zpad = jnp.zeros((r, MLA_HEADS, LANES - MLA_QK_DIM), w_uq.dtype)
    direct = jnp.concatenate([nope, rope, zpad], axis=2)
    swapped = jnp.concatenate([jnp.zeros_like(nope), rope[:, :, half:], rope[:, :, :half], zpad], axis=2)
    wuq2 = jnp.concatenate([direct.reshape(r, -1), swapped.reshape(r, -1)], axis=1).astype(BF16)

    r = w_ukv.shape[0]
    ukv = w_ukv.reshape(r, MLA_HEADS, MLA_NOPE_DIM + MLA_V_DIM)
    wk = _pad_heads(ukv[:, :, :MLA_NOPE_DIM].reshape(r, -1), MLA_HEADS, MLA_NOPE_DIM).astype(BF16)
    wkvt = _pad_heads(ukv[:, :, MLA_NOPE_DIM:].reshape(r, -1), MLA_HEADS, MLA_V_DIM, V_ROWS).T.astype(BF16)
    return w1, wvt, wuq2, wk, wkvt


def _rope_lane_tables(seq):
    dim = MLA_ROPE_DIM
    inv_freq = ROPE_THETA ** (-jnp.arange(0, dim, 2, dtype=F32) / dim)
    ang = jnp.arange(seq, dtype=F32)[:, None] * inv_freq[None, :]
    cos, sin = jnp.cos(ang), jnp.sin(ang)
    tail = jnp.zeros((seq, LANES - MLA_QK_DIM), F32)
    cosm = jnp.concatenate([jnp.ones((seq, MLA_NOPE_DIM), F32), cos, cos, tail], axis=1)
    sinm = jnp.concatenate([jnp.zeros((seq, MLA_NOPE_DIM), F32), -sin, sin, tail], axis=1)
    return cosm, sinm


class _Tiles:
    def __init__(self, seq):
        self.tm = 512 if seq % 512 == 0 else MOBA_BLOCK
        self.fc = 256
        self.mla_tq = self.tm
        self.mla_tk = 256
        self.mla_unroll = self.mla_tq // self.mla_tk
        self.moba_unroll = 4


def kernel(x, norm_attn_g, w_in, b_gate, q_norm_g, w_uq, kv_norm_g, w_ukv, rel_bias,
           w_branch_moba, w_branch_mla, w_out, norm_ffn_g, w_up, conv_w, conv_b, w_down,
           norm_final_g):
    batch, seq, d = x.shape
    depth = w_in.shape[0]
    assert seq % MOBA_BLOCK == 0
    t = _Tiles(seq)
    d_ff = w_down.shape[1]
    assert d_ff % t.fc == 0
    n = batch * seq

    cosm, sinm = _rope_lane_tables(seq)
    near, cfar = _moba_bias_tables(rel_bias)
    row = lambda v: v.reshape(1, -1).astype(F32)

    h = x.reshape(n, d)
    for l in range(depth):
        w1, wvt, wuq2, wk, wkvt = _front_weights(w_in[l], w_uq[l], w_ukv[l])
        qa, ka, kmean, vat, qm, km, vmt, g = _front(
            h, row(norm_attn_g[l]), w1, wvt, row(b_gate[l]), row(q_norm_g[l]), wuq2,
            row(kv_norm_g[l]), wk, wkvt, cosm, sinm, seq=seq, tm=t.tm)
        kmean = kmean.reshape(batch, seq // MOBA_BLOCK, MOBA_WIDTH)
        ya = _moba(qa, ka, vat, kmean, near, cfar, batch=batch, seq=seq,
                   unroll=t.moba_unroll)
        yb = _mla(qm, km, vmt, batch=batch, seq=seq, tq=t.mla_tq, tk=t.mla_tk,
                  unroll=t.mla_unroll)
        h = _back(h, ya, yb, g, w_branch_moba[l].astype(BF16), w_branch_mla[l].astype(BF16),
                  w_out[l].astype(BF16), row(norm_ffn_g[l]), w_up[l].astype(BF16),
                  conv_w[l].astype(F32), row(conv_b[l]), w_down[l].astype(BF16),
                  row(norm_final_g), seq=seq, tm=t.tm, fc=t.fc, final=(l == depth - 1))
    return h.reshape(batch, seq, d)
```

```python
import functools
import math

import numpy as np
import jax
import jax.numpy as jnp
from jax import lax
from jax.experimental import pallas as pl
from jax.experimental.pallas import tpu as pltpu

MOBA_HEADS = 8
MOBA_HEAD_DIM = 64
MOBA_BLOCK = 256
MOBA_TOPK = 3
MLA_HEADS = 8
MLA_Q_RANK = 256
MLA_KV_RANK = 128
MLA_NOPE_DIM = 64
MLA_ROPE_DIM = 32
MLA_V_DIM = 64
ROPE_THETA = 10000.0
REL_BUCKETS = 32
REL_MAX_DIST = 128
CONV_WIDTH = 3
N_BRANCH = 2
EPS = 1e-6

MOBA_WIDTH = MOBA_HEADS * MOBA_HEAD_DIM
MLA_QK_DIM = MLA_NOPE_DIM + MLA_ROPE_DIM
MLA_WIDTH = MLA_HEADS * MLA_V_DIM

LANES = 128
SUBLANES = 8
VMEM_LIMIT_BYTES = 56 * 1024 * 1024

NEG = -1e30

F32 = jnp.float32
BF16 = jnp.bfloat16


def _dot(a, b):
    return jnp.dot(a, b, preferred_element_type=F32)


def _dot_nt(a, b):
    return lax.dot_general(a, b, (((1,), (1,)), ((), ())), preferred_element_type=F32)


def _rms(x, g):
    return x * lax.rsqrt(jnp.mean(x * x, axis=-1, keepdims=True) + EPS) * g


def _sigmoid(z):
    return 1.0 / (1.0 + jnp.exp(-z))


def _resident(shape):
    nd = len(shape)
    return pl.BlockSpec(shape, lambda *_: (0,) * nd, pipeline_mode=pl.Buffered(1))


def _per_batch(shape, index_map):
    return pl.BlockSpec(shape, index_map, pipeline_mode=pl.Buffered(1))


_C_QA = 0
_C_KA = _C_QA + MOBA_WIDTH
_C_CQ = _C_KA + MOBA_WIDTH
_C_CKV = _C_CQ + MLA_Q_RANK
_C_KRA = _C_CKV + MLA_KV_RANK
_C_KRB = _C_KRA + LANES
_C_G = _C_KRB + LANES
_C_END = _C_G + N_BRANCH * 1024

BF16_SUBLANES = 16
V_ROWS = MLA_V_DIM + BF16_SUBLANES
assert MOBA_HEAD_DIM == MLA_V_DIM


def _ones_rows(n_rows):
    r = lax.broadcasted_iota(jnp.int32, (n_rows, 1), 0).astype(F32)
    within = r - jnp.floor((r + 0.5) * (1.0 / V_ROWS)) * V_ROWS
    return jnp.where(within == MLA_V_DIM, 1.0, 0.0).astype(F32)


def _front_kernel(x_ref, gattn_ref, w1_ref, wvt_ref, bg_ref, qng_ref, wuq_ref, kvng_ref, wkv_ref,
                  wkvt_ref, cos_ref, sin_ref,
                  qa_ref, ka_ref, kmean_ref, vat_ref, qm_ref, km_ref, vmt_ref, g_ref,
                  *, d_model, mla_scale):
    tm = x_ref.shape[0]
    hw = MLA_HEADS * LANES
    xn = _rms(x_ref[...], gattn_ref[...]).astype(BF16)

    def proj(a, b):
        return _dot(xn, w1_ref[:, a:b])

    qa_ref[...] = (proj(_C_QA, _C_KA) * (MOBA_HEAD_DIM ** -0.5 * LOG2E)).astype(BF16)
    ka = proj(_C_KA, _C_CQ)
    ka_ref[...] = ka.astype(BF16)
    for i in range(tm // MOBA_BLOCK):
        kmean_ref[i] = jnp.mean(ka[i * MOBA_BLOCK:(i + 1) * MOBA_BLOCK], axis=0, keepdims=True)
    ones = _ones_rows(vat_ref.shape[0])
    vat_ref[...] = (_dot_nt(wvt_ref[...], xn) + ones).astype(BF16)

    cosm = cos_ref[...]
    sinm = sin_ref[...]

    cqn = _rms(proj(_C_CQ, _C_CKV), qng_ref[...]).astype(BF16)
    qq = _dot(cqn, wuq_ref[...])
    for h in range(MLA_HEADS):
        a = qq[:, h * LANES:(h + 1) * LANES]
        b = qq[:, hw + h * LANES:hw + (h + 1) * LANES]
        qm_ref[:, h * LANES:(h + 1) * LANES] = ((a * cosm + b * sinm) * mla_scale).astype(BF16)

    ckvn = _rms(proj(_C_CKV, _C_KRA), kvng_ref[...]).astype(BF16)
    kv = _dot(ckvn, wkv_ref[...])
    kr = proj(_C_KRA, _C_KRB) * cosm + proj(_C_KRB, _C_G) * sinm
    for h in range(MLA_HEADS):
        km_ref[:, h * LANES:(h + 1) * LANES] = (kv[:, h * LANES:(h + 1) * LANES] + kr).astype(BF16)
    vmt_ref[...] = (_dot_nt(wkvt_ref[...], ckvn) + ones).astype(BF16)

    g_ref[...] = _sigmoid(proj(_C_G, _C_END) + bg_ref[...]).astype(BF16)


def _front(x2, gattn, w1, wvt, bg, qng, wuq2, kvng, wkv, wkvt, cosm, sinm, *, seq, tm):
    n, d = x2.shape
    batch = n // seq
    hw = MLA_HEADS * LANES
    nblk = n // MOBA_BLOCK
    steps_per_seq = seq // tm
    row = lambda i: (i, 0)
    tcol = lambda i: (i // steps_per_seq, 0, i % steps_per_seq)
    out_shape = (
        jax.ShapeDtypeStruct((n, MOBA_WIDTH), BF16),
        jax.ShapeDtypeStruct((n, MOBA_WIDTH), BF16),
        jax.ShapeDtypeStruct((nblk, 1, MOBA_WIDTH), F32),
        jax.ShapeDtypeStruct((batch, MOBA_HEADS * V_ROWS, seq), BF16),
        jax.ShapeDtypeStruct((n, hw), BF16),
        jax.ShapeDtypeStruct((n, hw), BF16),
        jax.ShapeDtypeStruct((batch, MLA_HEADS * V_ROWS, seq), BF16),
        jax.ShapeDtypeStruct((n, N_BRANCH * d), BF16),
    )
    in_specs = [
        pl.BlockSpec((tm, d), row),
        _resident(gattn.shape), _resident(w1.shape), _resident(wvt.shape), _resident(bg.shape),
        _resident(qng.shape), _resident(wuq2.shape), _resident(kvng.shape), _resident(wkv.shape),
        _resident(wkvt.shape),
        pl.BlockSpec((tm, LANES), lambda i: (i % steps_per_seq, 0)),
        pl.BlockSpec((tm, LANES), lambda i: (i % steps_per_seq, 0)),
    ]
    out_specs = (
        pl.BlockSpec((tm, MOBA_WIDTH), row),
        pl.BlockSpec((tm, MOBA_WIDTH), row),
        pl.BlockSpec((tm // MOBA_BLOCK, 1, MOBA_WIDTH), lambda i: (i, 0, 0)),
        pl.BlockSpec((None, MOBA_HEADS * V_ROWS, tm), tcol),
        pl.BlockSpec((tm, hw), row),
        pl.BlockSpec((tm, hw), row),
        pl.BlockSpec((None, MLA_HEADS * V_ROWS, tm), tcol),
        pl.BlockSpec((tm, N_BRANCH * d), row),
    )
    return pl.pallas_call(
        functools.partial(_front_kernel, d_model=d, mla_scale=MLA_QK_DIM ** -0.5 * LOG2E),
        grid=(n // tm,),
        in_specs=in_specs,
        out_specs=out_specs,
        out_shape=out_shape,
        compiler_params=pltpu.CompilerParams(
            dimension_semantics=("arbitrary",), vmem_limit_bytes=VMEM_LIMIT_BYTES),
        name="front",
    )(x2, gattn, w1, wvt, bg, qng, wuq2, kvng, wkv, wkvt, cosm, sinm)


HEADS_PER_STEP = 8
LOG2E = math.log2(math.e)
BIG = 1e30
RESCALE_SLACK = 64.0


def _online_softmax_group(m_ref, acc_ref, score_blocks, vt_grp, chosen=None, shifts=None):
    n = len(score_blocks)
    chosen = chosen or [None] * n
    shifts = shifts or [None] * n
    m_old = m_ref[...]
    m_new = m_old
    for st, ch, sh in zip(score_blocks, chosen, shifts):
        rm = jnp.max(st, axis=0, keepdims=True)
        if sh is not None:
            rm = rm + sh
        m_new = jnp.maximum(m_new, rm if ch is None else jnp.where(ch, rm, NEG))
    probs = []
    for st, ch, sh in zip(score_blocks, chosen, shifts):
        off = m_new if sh is None else m_new - sh
        if ch is not None:
            off = jnp.where(ch, off, BIG)
        probs.append(jnp.exp2(st - off).astype(BF16))
    ot = _dot(vt_grp, jnp.concatenate(probs, axis=0))
    acc_ref[...] = jnp.exp2(m_old - m_new) * acc_ref[...] + ot
    m_ref[...] = m_new


def _streamed_softmax_group(m_ref, acc_ref, score_blocks, vt_grp, chosen=None, shifts=None):
    n = len(score_blocks)
    chosen = chosen or [None] * n
    shifts = shifts or [None] * n
    m_old = m_ref[...]
    seen = jnp.full(m_old.shape, NEG, F32)
    probs = []
    for st, ch, sh in zip(score_blocks, chosen, shifts):
        off = m_old if sh is None else m_old - sh
        if ch is not None:
            off = jnp.where(ch, off, BIG)
        probs.append(jnp.exp2(st - off).astype(BF16))
        rm = jnp.max(st, axis=0, keepdims=True)
        if sh is not None:
            rm = rm + sh
        seen = jnp.maximum(seen, rm if ch is None else jnp.where(ch, rm, NEG))
    late = seen > m_old + RESCALE_SLACK
    ot = _dot(vt_grp, jnp.concatenate(probs, axis=0))
    acc_ref[...] = acc_ref[...] + jnp.where(late, 0.0, ot)
    return jnp.where(late, 1.0, 0.0)


def _reset_softmax_state(m_s, acc_s):
    m_s[...] = jnp.full(m_s.shape, NEG, F32)
    acc_s[...] = jnp.zeros(acc_s.shape, F32)


def _normalised_heads(acc_s, v_dim):
    rows = []
    for hh in range(HEADS_PER_STEP):
        acc = acc_s[hh]
        rows.append(acc[0:v_dim, :] * (1.0 / acc[v_dim:v_dim + 1, :]))
    return jnp.concatenate(rows, axis=0)


def _moba_kernel(q_ref, k_ref, vt_ref, kmean_ref, near_ref, cfar_ref, o_ref,
                 m_s, acc_s, sel_s, *, unroll):
    blk = MOBA_BLOCK
    qb = pl.program_id(2)
    nb = kmean_ref.shape[0]
    lane = lax.broadcasted_iota(jnp.int32, (blk, LANES), 1)
    blk_id = lax.broadcasted_iota(jnp.int32, (nb, blk), 0).astype(F32)
    qbf = qb.astype(F32)
    has_prev = qb >= 1
    _reset_softmax_state(m_s, acc_s)

    def lane_group(hh):
        return slice((hh // 2) * LANES, (hh // 2 + 1) * LANES)

    def k_block(hh, j):
        return k_ref[pl.ds(pl.multiple_of(j * blk, blk), blk), lane_group(hh)]

    def vt_blocks(hh, j, n):
        return vt_ref[hh * V_ROWS:(hh + 1) * V_ROWS, pl.ds(pl.multiple_of(j * blk, blk), n * blk)]

    q_heads, prev_chosen = [], []
    for hh in range(HEADS_PER_STEP):
        in_head = (lane >= (hh % 2) * MOBA_HEAD_DIM) & (lane < (hh % 2 + 1) * MOBA_HEAD_DIM)
        qh = jnp.where(in_head, q_ref[:, lane_group(hh)].astype(F32), 0.0).astype(BF16)
        q_heads.append(qh)

        kmean = kmean_ref[:, lane_group(hh)].astype(BF16)
        gate = jnp.where(blk_id < qbf, _dot_nt(kmean, qh), -jnp.inf)
        sel = jnp.zeros((nb, blk), F32)
        for _ in range(min(MOBA_TOPK, nb)):
            best = jnp.max(gate, axis=0, keepdims=True)
            first = jnp.min(jnp.where(gate == best, blk_id, float(nb)), axis=0, keepdims=True)
            pick = blk_id == first
            sel = jnp.where(pick, 1.0, sel)
            gate = jnp.where(pick, -jnp.inf, gate)
        prev_chosen.append(jnp.max(jnp.where(blk_id == qbf - 1.0, sel, 0.0), axis=0, keepdims=True))
        sel_s[hh] = jnp.where(blk_id < qbf - 1.0, sel, 0.0)

    lo = jnp.maximum(qb - 1, 0)
    tab = jnp.where(has_prev, 0, 1)
    scores = [[_dot_nt(k_block(hh, lo + u), q_heads[hh]) + near_ref[hh, tab + u] for u in range(2)]
              for hh in range(HEADS_PER_STEP)]
    for hh in range(HEADS_PER_STEP):
        chosen = [jnp.where(has_prev, prev_chosen[hh], 1.0) > 0.5,
                  jnp.where(has_prev, jnp.ones((1, blk), F32), 0.0) > 0.5]
        _online_softmax_group(m_s.at[hh], acc_s.at[hh], scores[hh], vt_blocks(hh, lo, 2), chosen)

    def far(g, carry):
        first = g * unroll
        start = jnp.minimum(first, nb - unroll)

        def far_scores(hh):
            return [_dot_nt(k_block(hh, start + u), q_heads[hh]) for u in range(unroll)]

        def far_chosen(hh):
            return [jnp.where(start + u >= first, sel_s[hh, pl.ds(start + u, 1), :], 0.0) > 0.5
                    for u in range(unroll)]

        late = jnp.zeros((1, blk), F32)
        pending = None
        for hh in range(HEADS_PER_STEP + 1):
            scores = far_scores(hh) if hh < HEADS_PER_STEP else None
            if pending is not None:
                ph, blocks = pending
                late = jnp.maximum(late, _streamed_softmax_group(
                    m_s.at[ph], acc_s.at[ph], blocks, vt_blocks(ph, start, unroll),
                    far_chosen(ph), [cfar_ref[ph][:, 0:1]] * unroll))
            pending = (hh, scores)

        @pl.when(jnp.max(late) > 0.5)
        def _():
            for hh in range(HEADS_PER_STEP):
                blocks = far_scores(hh)
                chosen = far_chosen(hh)
                shift = cfar_ref[hh][:, 0:1]
                seen = functools.reduce(jnp.maximum, [
                    jnp.where(ch, jnp.max(st, axis=0, keepdims=True) + shift, NEG)
                    for st, ch in zip(blocks, chosen)])
                beyond = seen > m_s[hh] + RESCALE_SLACK
                redo = [jnp.where(beyond, jnp.where(ch, 1.0, 0.0), 0.0) > 0.5 for ch in chosen]
                _online_softmax_group(m_s.at[hh], acc_s.at[hh], blocks,
                                      vt_blocks(hh, start, unroll), redo, [shift] * unroll)
        return carry

    lax.fori_loop(0, (qb - 1 + unroll - 1) // unroll, far, 0)
    o_ref[...] = _normalised_heads(acc_s, MOBA_HEAD_DIM).T.astype(o_ref.dtype)


def _moba(qa, ka, vat, kmean, near, cfar, *, batch, seq, unroll):
    n = qa.shape[0]
    blk = MOBA_BLOCK
    nb = seq // blk
    assert nb >= unroll and nb >= 2
    steps = MOBA_HEADS // HEADS_PER_STEP
    width = HEADS_PER_STEP * MOBA_HEAD_DIM
    return pl.pallas_call(
        functools.partial(_moba_kernel, unroll=unroll),
        grid=(batch, steps, nb),
        in_specs=[
            pl.BlockSpec((blk, width), lambda b, p, i: (b * nb + i, p)),
            _per_batch((seq, width), lambda b, p, i: (b, p)),
            _per_batch((None, HEADS_PER_STEP * V_ROWS, seq), lambda b, p, i: (b, p, 0)),
            pl.BlockSpec((None, nb, width), lambda b, p, i: (b, 0, p)),
            _per_batch((HEADS_PER_STEP, 3, blk, blk), lambda b, p, i: (p, 0, 0, 0)),
            pl.BlockSpec((HEADS_PER_STEP, 1, LANES), lambda b, p, i: (p, 0, 0)),
        ],
        out_specs=pl.BlockSpec((blk, width), lambda b, p, i: (b * nb + i, p)),
        out_shape=jax.ShapeDtypeStruct((n, MOBA_WIDTH), BF16),
        scratch_shapes=[
            pltpu.VMEM((HEADS_PER_STEP, 1, blk), F32),
            pltpu.VMEM((HEADS_PER_STEP, V_ROWS, blk), F32),
            pltpu.VMEM((HEADS_PER_STEP, nb, blk), F32),
        ],
        compiler_params=pltpu.CompilerParams(
            dimension_semantics=("arbitrary", "arbitrary", "arbitrary"),
            vmem_limit_bytes=VMEM_LIMIT_BYTES),
        name="moba",
    )(qa, ka, vat, kmean, near, cfar)


def _mla_kernel(q_ref, k_ref, vt_ref, o_ref, m_s, acc_s, *, tq, tk, unroll):
    qi = pl.program_id(2)
    n_diag = tq // tk
    key = lax.broadcasted_iota(jnp.int32, (tk, tq), 0)
    qry = lax.broadcasted_iota(jnp.int32, (tk, tq), 1)
    q_heads = [q_ref[:, hh * LANES:(hh + 1) * LANES] for hh in range(HEADS_PER_STEP)]

    _reset_softmax_state(m_s, acc_s)

    def k_tile(hh, j):
        return k_ref[pl.ds(pl.multiple_of(j * tk, tk), tk), hh * LANES:(hh + 1) * LANES]

    def vt_tiles(hh, j, n):
        return vt_ref[hh * V_ROWS:(hh + 1) * V_ROWS, pl.ds(pl.multiple_of(j * tk, tk), n * tk)]

    def group(first, n, masked):
        scores = [[_dot_nt(k_tile(hh, first + u), q_heads[hh]) for u in range(n)]
                  for hh in range(HEADS_PER_STEP)]
        for hh in range(HEADS_PER_STEP):
            blocks = scores[hh]
            if masked:
                blocks = [jnp.where(key + u * tk <= qry, st, NEG) for u, st in enumerate(blocks)]
            _online_softmax_group(m_s.at[hh], acc_s.at[hh], blocks, vt_tiles(hh, first, n))

    group(qi * n_diag, n_diag, True)

    def body(g, carry):
        first = g * unroll
        late = jnp.zeros((1, tq), F32)
        pending = None
        for hh in range(HEADS_PER_STEP + 1):
            scores = None
            if hh < HEADS_PER_STEP:
                scores = [_dot_nt(k_tile(hh, first + u), q_heads[hh]) for u in range(unroll)]
            if pending is not None:
                ph, blocks = pending
                late = jnp.maximum(late, _streamed_softmax_group(
                    m_s.at[ph], acc_s.at[ph], blocks, vt_tiles(ph, first, unroll)))
            pending = (hh, scores)

        @pl.when(jnp.max(late) > 0.5)
        def _():
            for hh in range(HEADS_PER_STEP):
                blocks = [_dot_nt(k_tile(hh, first + u), q_heads[hh]) for u in range(unroll)]
                seen = functools.reduce(
                    jnp.maximum, [jnp.max(st, axis=0, keepdims=True) for st in blocks])
                redo = [seen > m_s[hh] + RESCALE_SLACK] * unroll
                _online_softmax_group(m_s.at[hh], acc_s.at[hh], blocks,
                                      vt_tiles(hh, first, unroll), redo)
        return carry

    lax.fori_loop(0, (qi * n_diag) // unroll, body, 0)
    o_ref[...] = _normalised_heads(acc_s, MLA_V_DIM).T.astype(o_ref.dtype)


def _mla(qm, km, vmt, *, batch, seq, tq, tk, unroll):
    n = qm.shape[0]
    nq = seq // tq
    assert tq % tk == 0 and (tq // tk) % unroll == 0
    pairs = MLA_HEADS // HEADS_PER_STEP
    return pl.pallas_call(
        functools.partial(_mla_kernel, tq=tq, tk=tk, unroll=unroll),
        grid=(batch, pairs, nq),
        in_specs=[
            pl.BlockSpec((tq, HEADS_PER_STEP * LANES), lambda b, p, i: (b * nq + i, p)),
            _per_batch((seq, HEADS_PER_STEP * LANES), lambda b, p, i: (b, p)),
            _per_batch((None, HEADS_PER_STEP * V_ROWS, seq), lambda b, p, i: (b, p, 0)),
        ],
        out_specs=pl.BlockSpec((tq, HEADS_PER_STEP * MLA_V_DIM), lambda b, p, i: (b * nq + i, p)),
        out_shape=jax.ShapeDtypeStruct((n, MLA_WIDTH), BF16),
        scratch_shapes=[
            pltpu.VMEM((HEADS_PER_STEP, 1, tq), F32),
            pltpu.VMEM((HEADS_PER_STEP, V_ROWS, tq), F32),
        ],
        compiler_params=pltpu.CompilerParams(
            dimension_semantics=("arbitrary", "arbitrary", "arbitrary"),
            vmem_limit_bytes=VMEM_LIMIT_BYTES),
        name="mla",
    )(qm, km, vmt)


def _back_kernel(x_ref, ya_ref, yb_ref, g_ref, wa_ref, wb_ref, wo_ref, gffn_ref, wup_ref,
                 cw_ref, cb_ref, wdn_ref, gfin_ref, o_ref,
                 carry_ref, ubufs_a, ubufs_b, act_ref, *, steps_per_seq, d_ff, fc, final):
    tm, d = x_ref.shape
    halo = SUBLANES
    n_chunks = d_ff // fc

    @pl.when(pl.program_id(0) % steps_per_seq == 0)
    def _():
        carry_ref[...] = jnp.zeros_like(carry_ref)

    g = g_ref[...].astype(F32)
    mixed = g[:, :d] * _dot(ya_ref[...], wa_ref[...]) + g[:, d:] * _dot(yb_ref[...], wb_ref[...])
    h1 = x_ref[...] + _dot(mixed.astype(BF16), wo_ref[...])
    hn = _rms(h1, gffn_ref[...]).astype(BF16)

    def columns(c):
        return pl.multiple_of(c * fc, fc), pl.multiple_of(d_ff + c * fc, fc)

    def up_project(c, ubufs):
        for half, col0 in enumerate(columns(c)):
            cols = pl.ds(col0, fc)
            u = _dot(hn, wup_ref[:, cols])
            ubufs[half, 0:halo, :] = carry_ref[:, cols]
            ubufs[half, halo:halo + tm, :] = u
            carry_ref[:, cols] = u[tm - halo:tm, :]

    def conv(col0, ubuf):
        cols = pl.ds(col0, fc)
        w = cw_ref[:, cols]
        y = cb_ref[:, cols]
        rows = ubuf[...]
        for t in range(CONV_WIDTH):
            back = CONV_WIDTH - 1 - t
            shifted = pltpu.roll(rows, back, axis=0) if back else rows
            y = y + w[t:t + 1, :] * shifted[halo:, :]
        return y

    def gate(c, ubufs):
        col_g, col_v = columns(c)
        yg = conv(col_g, ubufs.at[0])
        yv = conv(col_v, ubufs.at[1])
        act_ref[:, pl.ds(col_g, fc)] = (yg * _sigmoid(yg) * yv).astype(BF16)

    up_project(0, ubufs_a)

    def pair(k, carry):
        up_project(2 * k + 1, ubufs_b)
        gate(2 * k, ubufs_a)
        up_project(2 * k + 2, ubufs_a)
        gate(2 * k + 1, ubufs_b)
        return carry

    lax.fori_loop(0, (n_chunks - 1) // 2, pair, 0)
    if n_chunks % 2:
        gate(n_chunks - 1, ubufs_a)
    else:
        up_project(n_chunks - 1, ubufs_b)
        gate(n_chunks - 2, ubufs_a)
        gate(n_chunks - 1, ubufs_b)

    h2 = h1 + _dot(act_ref[...], wdn_ref[...])
    o_ref[...] = _rms(h2, gfin_ref[...]) if final else h2


def _back(x2, ya, yb, g, wa, wb, wo, gffn, wup, cw, cb, wdn, gfin, *, seq, tm, fc, final):
    n, d = x2.shape
    d_ff = wdn.shape[0]
    row = lambda i: (i, 0)
    return pl.pallas_call(
        functools.partial(_back_kernel, steps_per_seq=seq // tm, d_ff=d_ff, fc=fc, final=final),
        grid=(n // tm,),
        in_specs=[
            pl.BlockSpec((tm, d), row),
            pl.BlockSpec((tm, ya.shape[1]), row),
            pl.BlockSpec((tm, yb.shape[1]), row),
            pl.BlockSpec((tm, g.shape[1]), row),
            _resident(wa.shape), _resident(wb.shape), _resident(wo.shape), _resident(gffn.shape),
            _resident(wup.shape), _resident(cw.shape), _resident(cb.shape), _resident(wdn.shape),
            _resident(gfin.shape),
        ],
        out_specs=pl.BlockSpec((tm, d), row),
        out_shape=jax.ShapeDtypeStruct((n, d), F32),
        scratch_shapes=[
            pltpu.VMEM((SUBLANES, 2 * d_ff), F32),
            pltpu.VMEM((2, SUBLANES + tm, fc), F32),
            pltpu.VMEM((2, SUBLANES + tm, fc), F32),
            pltpu.VMEM((tm, d_ff), BF16),
        ],
        compiler_params=pltpu.CompilerParams(
            dimension_semantics=("arbitrary",), vmem_limit_bytes=VMEM_LIMIT_BYTES),
        name="back",
    )(x2, ya, yb, g, wa, wb, wo, gffn, wup, cw, cb, wdn, gfin)


def _t5_bucket_np(rel):
    n = np.maximum(rel, 0)
    max_exact = REL_BUCKETS // 2
    nf = np.maximum(n, 1).astype(np.float32)
    large = max_exact + (np.log(nf / np.float32(max_exact)) / np.float32(math.log(REL_MAX_DIST / max_exact))
                         * np.float32(REL_BUCKETS - max_exact)).astype(np.int32)
    large = np.minimum(large, REL_BUCKETS - 1)
    return np.where(n < max_exact, n, large)


def _moba_bias_tables(rel_bias):
    blk = MOBA_BLOCK
    assert REL_MAX_DIST <= blk + 1
    off = np.arange(blk, dtype=np.int32)
    rel_own = off[None, :] - off[:, None]
    bias_h = rel_bias.T.astype(F32)

    def lookup(bucket):
        onehot = jnp.asarray(bucket)[None, :, :] == jnp.arange(REL_BUCKETS)[:, None, None]
        return jnp.sum(jnp.where(onehot[None], bias_h[:, :, None, None], 0.0), axis=1)

    town = jnp.where(jnp.asarray(rel_own >= 0), lookup(_t5_bucket_np(rel_own)) * LOG2E, NEG)
    tprev = lookup(_t5_bucket_np(rel_own + blk)) * LOG2E
    near = jnp.stack([tprev, town, jnp.zeros_like(town)], axis=1)
    cfar = bias_h[:, int(_t5_bucket_np(np.int32(blk + 1)))] * LOG2E
    cfar = jnp.broadcast_to(cfar[:, None, None], (MOBA_HEADS, 1, LANES))
    return near, cfar


def _pad_heads(w, n_heads, width, padded=LANES):
    r = w.shape[0]
    w = w.reshape(r, n_heads, width)
    out = jnp.zeros((r, n_heads, padded), w.dtype).at[:, :, :width].set(w)
    return out.reshape(r, n_heads * padded)


def _rope_slot(w_rope, swap):
    half = MLA_ROPE_DIM // 2
    if swap:
        w_rope = jnp.concatenate([w_rope[:, half:], w_rope[:, :half]], axis=1)
    r = w_rope.shape[0]
    return jnp.zeros((r, LANES), w_rope.dtype).at[:, MLA_NOPE_DIM:MLA_NOPE_DIM + MLA_ROPE_DIM].set(w_rope)


def _front_weights(w_in, w_uq, w_ukv):
    d = w_in.shape[0]
    s0 = MOBA_WIDTH
    s1 = 2 * MOBA_WIDTH
    s2 = 3 * MOBA_WIDTH
    s3 = s2 + MLA_Q_RANK
    s4 = s3 + MLA_KV_RANK
    s5 = s4 + MLA_ROPE_DIM
    w1 = jnp.concatenate([
        w_in[:, :s0], w_in[:, s0:s1],
        w_in[:, s2:s3], w_in[:, s3:s4],
        _rope_slot(w_in[:, s4:s5], False), _rope_slot(w_in[:, s4:s5], True),
        w_in[:, s5:],
    ], axis=1).astype(BF16)
    assert w1.shape == (d, _C_END)
    wvt = _pad_heads(w_in[:, s1:s2], MOBA_HEADS, MOBA_HEAD_DIM, V_ROWS).T.astype(BF16)

    r = w_uq.shape[0]
    uq = w_uq.reshape(r, MLA_HEADS, MLA_QK_DIM)
    nope, rope = uq[:, :, :MLA_NOPE_DIM], uq[:, :, MLA_NOPE_DIM:]
    half = MLA_ROPE_DIM // 2
    zpad = jnp.zeros((r, MLA_HEADS, LANES - MLA_QK_DIM), w_uq.dtype)
    direct = jnp.concatenate([nope, rope, zpad], axis=2)
    swapped = jnp.concatenate([jnp.zeros_like(nope), rope[:, :, half:], rope[:, :, :half], zpad], axis=2)
    wuq2 = jnp.concatenate([direct.reshape(r, -1), swapped.reshape(r, -1)], axis=1).astype(BF16)

    r = w_ukv.shape[0]
    ukv = w_ukv.reshape(r, MLA_HEADS, MLA_NOPE_DIM + MLA_V_DIM)
    wk = _pad_heads(ukv[:, :, :MLA_NOPE_DIM].reshape(r, -1), MLA_HEADS, MLA_NOPE_DIM).astype(BF16)
    wkvt = _pad_heads(ukv[:, :, MLA_NOPE_DIM:].reshape(r, -1), MLA_HEADS, MLA_V_DIM, V_ROWS).T.astype(BF16)
    return w1, wvt, wuq2, wk, wkvt


def _rope_lane_tables(seq):
    dim = MLA_ROPE_DIM
    inv_freq = ROPE_THETA ** (-jnp.arange(0, dim, 2, dtype=F32) / dim)
    ang = jnp.arange(seq, dtype=F32)[:, None] * inv_freq[None, :]
    cos, sin = jnp.cos(ang), jnp.sin(ang)
    tail = jnp.zeros((seq, LANES - MLA_QK_DIM), F32)
    cosm = jnp.concatenate([jnp.ones((seq, MLA_NOPE_DIM), F32), cos, cos, tail], axis=1)
    sinm = jnp.concatenate([jnp.zeros((seq, MLA_NOPE_DIM), F32), -sin, sin, tail], axis=1)
    return cosm, sinm


class _Tiles:
    def __init__(self, seq):
        self.tm = 512 if seq % 512 == 0 else MOBA_BLOCK
        self.fc = 256
        self.mla_tq = self.tm
        self.mla_tk = 256
        self.mla_unroll = self.mla_tq // self.mla_tk
        self.moba_unroll = 4


def kernel(x, norm_attn_g, w_in, b_gate, q_norm_g, w_uq, kv_norm_g, w_ukv, rel_bias,
           w_branch_moba, w_branch_mla, w_out, norm_ffn_g, w_up, conv_w, conv_b, w_down,
           norm_final_g):
    batch, seq, d = x.shape
    depth = w_in.shape[0]
    assert seq % MOBA_BLOCK == 0
    t = _Tiles(seq)
    d_ff = w_down.shape[1]
    assert d_ff % t.fc == 0
    n = batch * seq

    cosm, sinm = _rope_lane_tables(seq)
    near, cfar = _moba_bias_tables(rel_bias)
    row = lambda v: v.reshape(1, -1).astype(F32)

    h = x.reshape(n, d)
    for l in range(depth):
        w1, wvt, wuq2, wk, wkvt = _front_weights(w_in[l], w_uq[l], w_ukv[l])
        qa, ka, kmean, vat, qm, km, vmt, g = _front(
            h, row(norm_attn_g[l]), w1, wvt, row(b_gate[l]), row(q_norm_g[l]), wuq2,
            row(kv_norm_g[l]), wk, wkvt, cosm, sinm, seq=seq, tm=t.tm)
        kmean = kmean.reshape(batch, seq // MOBA_BLOCK, MOBA_WIDTH)
        ya = _moba(qa, ka, vat, kmean, near, cfar, batch=batch, seq=seq,
                   unroll=t.moba_unroll)
        yb = _mla(qm, km, vmt, batch=batch, seq=seq, tq=t.mla_tq, tk=t.mla_tk,
                  unroll=t.mla_unroll)
        h = _back(h, ya, yb, g, w_branch_moba[l].astype(BF16), w_branch_mla[l].astype(BF16),
                  w_out[l].astype(BF16), row(norm_ffn_g[l]), w_up[l].astype(BF16),
                  conv_w[l].astype(F32), row(conv_b[l]), w_down[l].astype(BF16),
                  row(norm_final_g), seq=seq, tm=t.tm, fc=t.fc, final=(l == depth - 1))
    return h.reshape(batch, seq, d)
```

```python
import functools
import math

import numpy as np
import jax
import jax.numpy as jnp
from jax import lax
from jax.experimental import pallas as pl
from jax.experimental.pallas import tpu as pltpu

MOBA_HEADS = 8
MOBA_HEAD_DIM = 64
MOBA_BLOCK = 256
MOBA_TOPK = 3
MLA_HEADS = 8
MLA_Q_RANK = 256
MLA_KV_RANK = 128
MLA_NOPE_DIM = 64
MLA_ROPE_DIM = 32
MLA_V_DIM = 64
ROPE_THETA = 10000.0
REL_BUCKETS = 32
REL_MAX_DIST = 128
CONV_WIDTH = 3
N_BRANCH = 2
EPS = 1e-6

MOBA_WIDTH = MOBA_HEADS * MOBA_HEAD_DIM
MLA_QK_DIM = MLA_NOPE_DIM + MLA_ROPE_DIM
MLA_WIDTH = MLA_HEADS * MLA_V_DIM

LANES = 128
SUBLANES = 8
VMEM_LIMIT_BYTES = 56 * 1024 * 1024

NEG = -1e30

F32 = jnp.float32
BF16 = jnp.bfloat16


def _dot(a, b):
    return jnp.dot(a, b, preferred_element_type=F32)


def _dot_nt(a, b):
    return lax.dot_general(a, b, (((1,), (1,)), ((), ())), preferred_element_type=F32)


def _rms(x, g):
    return x * lax.rsqrt(jnp.mean(x * x, axis=-1, keepdims=True) + EPS) * g


def _sigmoid(z):
    return 1.0 / (1.0 + jnp.exp(-z))


def _resident(shape):
    nd = len(shape)
    return pl.BlockSpec(shape, lambda *_: (0,) * nd, pipeline_mode=pl.Buffered(1))


def _per_batch(shape, index_map):
    return pl.BlockSpec(shape, index_map, pipeline_mode=pl.Buffered(1))


_C_QA = 0
_C_KA = _C_QA + MOBA_WIDTH
_C_CQ = _C_KA + MOBA_WIDTH
_C_CKV = _C_CQ + MLA_Q_RANK
_C_KRA = _C_CKV + MLA_KV_RANK
_C_KRB = _C_KRA + LANES
_C_G = _C_KRB + LANES
_C_END = _C_G + N_BRANCH * 1024

BF16_SUBLANES = 16
V_ROWS = MLA_V_DIM + BF16_SUBLANES
assert MOBA_HEAD_DIM == MLA_V_DIM


def _ones_rows(n_rows):
    r = lax.broadcasted_iota(jnp.int32, (n_rows, 1), 0).astype(F32)
    within = r - jnp.floor((r + 0.5) * (1.0 / V_ROWS)) * V_ROWS
    return jnp.where(within == MLA_V_DIM, 1.0, 0.0).astype(F32)


def _front_kernel(x_ref, gattn_ref, w1_ref, wvt_ref, bg_ref, qng_ref, wuq_ref, kvng_ref, wkv_ref,
                  wkvt_ref, cos_ref, sin_ref,
                  qa_ref, ka_ref, kmean_ref, vat_ref, qm_ref, km_ref, vmt_ref, g_ref,
                  *, d_model, mla_scale):
    tm = x_ref.shape[0]
    hw = MLA_HEADS * LANES
    xn = _rms(x_ref[...], gattn_ref[...]).astype(BF16)

    def proj(a, b):
        return _dot(xn, w1_ref[:, a:b])

    qa_ref[...] = (proj(_C_QA, _C_KA) * (MOBA_HEAD_DIM ** -0.5 * LOG2E)).astype(BF16)
    ka = proj(_C_KA, _C_CQ)
    ka_ref[...] = ka.astype(BF16)
    for i in range(tm // MOBA_BLOCK):
        kmean_ref[i] = jnp.mean(ka[i * MOBA_BLOCK:(i + 1) * MOBA_BLOCK], axis=0, keepdims=True)
    ones = _ones_rows(vat_ref.shape[0])
    vat_ref[...] = (_dot_nt(wvt_ref[...], xn) + ones).astype(BF16)

    cosm = cos_ref[...]
    sinm = sin_ref[...]

    cqn = _rms(proj(_C_CQ, _C_CKV), qng_ref[...]).astype(BF16)
    qq = _dot(cqn, wuq_ref[...])
    for h in range(MLA_HEADS):
        a = qq[:, h * LANES:(h + 1) * LANES]
        b = qq[:, hw + h * LANES:hw + (h + 1) * LANES]
        qm_ref[:, h * LANES:(h + 1) * LANES] = ((a * cosm + b * sinm) * mla_scale).astype(BF16)

    ckvn = _rms(proj(_C_CKV, _C_KRA), kvng_ref[...]).astype(BF16)
    kv = _dot(ckvn, wkv_ref[...])
    kr = proj(_C_KRA, _C_KRB) * cosm + proj(_C_KRB, _C_G) * sinm
    for h in range(MLA_HEADS):
        km_ref[:, h * LANES:(h + 1) * LANES] = (kv[:, h * LANES:(h + 1) * LANES] + kr).astype(BF16)
    vmt_ref[...] = (_dot_nt(wkvt_ref[...], ckvn) + ones).astype(BF16)

    g_ref[...] = _sigmoid(proj(_C_G, _C_END) + bg_ref[...]).astype(BF16)


def _front(x2, gattn, w1, wvt, bg, qng, wuq2, kvng, wkv, wkvt, cosm, sinm, *, seq, tm):
    n, d = x2.shape
    batch = n // seq
    hw = MLA_HEADS * LANES
    nblk = n // MOBA_BLOCK
    steps_per_seq = seq // tm
    row = lambda i: (i, 0)
    tcol = lambda i: (i // steps_per_seq, 0, i % steps_per_seq)
    out_shape = (
        jax.ShapeDtypeStruct((n, MOBA_WIDTH), BF16),
        jax.ShapeDtypeStruct((n, MOBA_WIDTH), BF16),
        jax.ShapeDtypeStruct((nblk, 1, MOBA_WIDTH), F32),
        jax.ShapeDtypeStruct((batch, MOBA_HEADS * V_ROWS, seq), BF16),
        jax.ShapeDtypeStruct((n, hw), BF16),
        jax.ShapeDtypeStruct((n, hw), BF16),
        jax.ShapeDtypeStruct((batch, MLA_HEADS * V_ROWS, seq), BF16),
        jax.ShapeDtypeStruct((n, N_BRANCH * d), BF16),
    )
    in_specs = [
        pl.BlockSpec((tm, d), row),
        _resident(gattn.shape), _resident(w1.shape), _resident(wvt.shape), _resident(bg.shape),
        _resident(qng.shape), _resident(wuq2.shape), _resident(kvng.shape), _resident(wkv.shape),
        _resident(wkvt.shape),
        pl.BlockSpec((tm, LANES), lambda i: (i % steps_per_seq, 0)),
        pl.BlockSpec((tm, LANES), lambda i: (i % steps_per_seq, 0)),
    ]
    out_specs = (
        pl.BlockSpec((tm, MOBA_WIDTH), row),
        pl.BlockSpec((tm, MOBA_WIDTH), row),
        pl.BlockSpec((tm // MOBA_BLOCK, 1, MOBA_WIDTH), lambda i: (i, 0, 0)),
        pl.BlockSpec((None, MOBA_HEADS * V_ROWS, tm), tcol),
        pl.BlockSpec((tm, hw), row),
        pl.BlockSpec((tm, hw), row),
        pl.BlockSpec((None, MLA_HEADS * V_ROWS, tm), tcol),
        pl.BlockSpec((tm, N_BRANCH * d), row),
    )
    return pl.pallas_call(
        functools.partial(_front_kernel, d_model=d, mla_scale=MLA_QK_DIM ** -0.5 * LOG2E),
        grid=(n // tm,),
        in_specs=in_specs,
        out_specs=out_specs,
        out_shape=out_shape,
        compiler_params=pltpu.CompilerParams(
            dimension_semantics=("arbitrary",), vmem_limit_bytes=VMEM_LIMIT_BYTES),
        name="front",
    )(x2, gattn, w1, wvt, bg, qng, wuq2, kvng, wkv, wkvt, cosm, sinm)


HEADS_PER_STEP = 8
LOG2E = math.log2(math.e)
BIG = 1e30
RESCALE_SLACK = 64.0


def _online_softmax_group(m_ref, acc_ref, score_blocks, vt_grp, chosen=None, shifts=None):
    n = len(score_blocks)
    chosen = chosen or [None] * n
    shifts = shifts or [None] * n
    m_old = m_ref[...]
    m_new = m_old
    for st, ch, sh in zip(score_blocks, chosen, shifts):
        rm = jnp.max(st, axis=0, keepdims=True)
        if sh is not None:
            rm = rm + sh
        m_new = jnp.maximum(m_new, rm if ch is None else jnp.where(ch, rm, NEG))
    probs = []
    for st, ch, sh in zip(score_blocks, chosen, shifts):
        off = m_new if sh is None else m_new - sh
        if ch is not None:
            off = jnp.where(ch, off, BIG)
        probs.append(jnp.exp2(st - off).astype(BF16))
    ot = _dot(vt_grp, jnp.concatenate(probs, axis=0))
    acc_ref[...] = jnp.exp2(m_old - m_new) * acc_ref[...] + ot
    m_ref[...] = m_new


def _streamed_softmax_group(m_ref, acc_ref, score_blocks, vt_grp, chosen=None, shifts=None):
    n = len(score_blocks)
    chosen = chosen or [None] * n
    shifts = shifts or [None] * n
    m_old = m_ref[...]
    seen = jnp.full(m_old.shape, NEG, F32)
    probs = []
    for st, ch, sh in zip(score_blocks, chosen, shifts):
        off = m_old if sh is None else m_old - sh
        if ch is not None:
            off = jnp.where(ch, off, BIG)
        probs.append(jnp.exp2(st - off).astype(BF16))
        rm = jnp.max(st, axis=0, keepdims=True)
        if sh is not None:
            rm = rm + sh
        seen = jnp.maximum(seen, rm if ch is None else jnp.where(ch, rm, NEG))
    late = seen > m_old + RESCALE_SLACK
    ot = _dot(vt_grp, jnp.concatenate(probs, axis=0))
    acc_ref[...] = acc_ref[...] + jnp.where(late, 0.0, ot)
    return jnp.where(late, 1.0, 0.0)


def _reset_softmax_state(m_s, acc_s):
    m_s[...] = jnp.full(m_s.shape, NEG, F32)
    acc_s[...] = jnp.zeros(acc_s.shape, F32)


def _normalised_heads(acc_s, v_dim):
    rows = []
    for hh in range(HEADS_PER_STEP):
        acc = acc_s[hh]
        rows.append(acc[0:v_dim, :] * (1.0 / acc[v_dim:v_dim + 1, :]))
    return jnp.concatenate(rows, axis=0)


def _moba_kernel(q_ref, k_ref, vt_ref, kmean_ref, near_ref, cfar_ref, o_ref,
                 m_s, acc_s, sel_s, *, unroll):
    blk = MOBA_BLOCK
    qb = pl.program_id(2)
    nb = kmean_ref.shape[0]
    lane = lax.broadcasted_iota(jnp.int32, (blk, LANES), 1)
    blk_id = lax.broadcasted_iota(jnp.int32, (nb, blk), 0).astype(F32)
    qbf = qb.astype(F32)
    has_prev = qb >= 1
    _reset_softmax_state(m_s, acc_s)

    def lane_group(hh):
        return slice((hh // 2) * LANES, (hh // 2 + 1) * LANES)

    def k_block(hh, j):
        return k_ref[pl.ds(pl.multiple_of(j * blk, blk), blk), lane_group(hh)]

    def vt_blocks(hh, j, n):
        return vt_ref[hh * V_ROWS:(hh + 1) * V_ROWS, pl.ds(pl.multiple_of(j * blk, blk), n * blk)]

    q_heads, prev_chosen = [], []
    for hh in range(HEADS_PER_STEP):
        in_head = (lane >= (hh % 2) * MOBA_HEAD_DIM) & (lane < (hh % 2 + 1) * MOBA_HEAD_DIM)
        qh = jnp.where(in_head, q_ref[:, lane_group(hh)].astype(F32), 0.0).astype(BF16)
        q_heads.append(qh)

        kmean = kmean_ref[:, lane_group(hh)].astype(BF16)
        gate = jnp.where(blk_id < qbf, _dot_nt(kmean, qh), -jnp.inf)
        sel = jnp.zeros((nb, blk), F32)
        for _ in range(min(MOBA_TOPK, nb)):
            best = jnp.max(gate, axis=0, keepdims=True)
            first = jnp.min(jnp.where(gate == best, blk_id, float(nb)), axis=0, keepdims=True)
            pick = blk_id == first
            sel = jnp.where(pick, 1.0, sel)
            gate = jnp.where(pick, -jnp.inf, gate)
        prev_chosen.append(jnp.max(jnp.where(blk_id == qbf - 1.0, sel, 0.0), axis=0, keepdims=True))
        sel_s[hh] = jnp.where(blk_id < qbf - 1.0, sel, 0.0)

    lo = jnp.maximum(qb - 1, 0)
    tab = jnp.where(has_prev, 0, 1)
    scores = [[_dot_nt(k_block(hh, lo + u), q_heads[hh]) + near_ref[hh, tab + u] for u in range(2)]
              for hh in range(HEADS_PER_STEP)]
    for hh in range(HEADS_PER_STEP):
        chosen = [jnp.where(has_prev, prev_chosen[hh], 1.0) > 0.5,
                  jnp.where(has_prev, jnp.ones((1, blk), F32), 0.0) > 0.5]
        _online_softmax_group(m_s.at[hh], acc_s.at[hh], scores[hh], vt_blocks(hh, lo, 2), chosen)

    def far(g, carry):
        first = g * unroll
        start = jnp.minimum(first, nb - unroll)

        def far_scores(hh):
            return [_dot_nt(k_block(hh, start + u), q_heads[hh]) for u in range(unroll)]

        def far_chosen(hh):
            return [jnp.where(start + u >= first, sel_s[hh, pl.ds(start + u, 1), :], 0.0) > 0.5
                    for u in range(unroll)]

        late = jnp.zeros((1, blk), F32)
        pending = None
        for hh in range(HEADS_PER_STEP + 1):
            scores = far_scores(hh) if hh < HEADS_PER_STEP else None
            if pending is not None:
                ph, blocks = pending
                late = jnp.maximum(late, _streamed_softmax_group(
                    m_s.at[ph], acc_s.at[ph], blocks, vt_blocks(ph, start, unroll),
                    far_chosen(ph), [cfar_ref[ph][:, 0:1]] * unroll))
            pending = (hh, scores)

        @pl.when(jnp.max(late) > 0.5)
        def _():
            for hh in range(HEADS_PER_STEP):
                blocks = far_scores(hh)
                chosen = far_chosen(hh)
                shift = cfar_ref[hh][:, 0:1]
                seen = functools.reduce(jnp.maximum, [
                    jnp.where(ch, jnp.max(st, axis=0, keepdims=True) + shift, NEG)
                    for st, ch in zip(blocks, chosen)])
                beyond = seen > m_s[hh] + RESCALE_SLACK
                redo = [jnp.where(beyond, jnp.where(ch, 1.0, 0.0), 0.0) > 0.5 for ch in chosen]
                _online_softmax_group(m_s.at[hh], acc_s.at[hh], blocks,
                                      vt_blocks(hh, start, unroll), redo, [shift] * unroll)
        return carry

    lax.fori_loop(0, (qb - 1 + unroll - 1) // unroll, far, 0)
    o_ref[...] = _normalised_heads(acc_s, MOBA_HEAD_DIM).T.astype(o_ref.dtype)


def _moba(qa, ka, vat, kmean, near, cfar, *, batch, seq, unroll):
    n = qa.shape[0]
    blk = MOBA_BLOCK
    nb = seq // blk
    assert nb >= unroll and nb >= 2
    steps = MOBA_HEADS // HEADS_PER_STEP
    width = HEADS_PER_STEP * MOBA_HEAD_DIM
    return pl.pallas_call(
        functools.partial(_moba_kernel, unroll=unroll),
        grid=(batch, steps, nb),
        in_specs=[
            pl.BlockSpec((blk, width), lambda b, p, i: (b * nb + i, p)),
            _per_batch((seq, width), lambda b, p, i: (b, p)),
            _per_batch((None, HEADS_PER_STEP * V_ROWS, seq), lambda b, p, i: (b, p, 0)),
            pl.BlockSpec((None, nb, width), lambda b, p, i: (b, 0, p)),
            _per_batch((HEADS_PER_STEP, 3, blk, blk), lambda b, p, i: (p, 0, 0, 0)),
            pl.BlockSpec((HEADS_PER_STEP, 1, LANES), lambda b, p, i: (p, 0, 0)),
        ],
        out_specs=pl.BlockSpec((blk, width), lambda b, p, i: (b * nb + i, p)),
        out_shape=jax.ShapeDtypeStruct((n, MOBA_WIDTH), BF16),
        scratch_shapes=[
            pltpu.VMEM((HEADS_PER_STEP, 1, blk), F32),
            pltpu.VMEM((HEADS_PER_STEP, V_ROWS, blk), F32),
            pltpu.VMEM((HEADS_PER_STEP, nb, blk), F32),
        ],
        compiler_params=pltpu.CompilerParams(
            dimension_semantics=("arbitrary", "arbitrary", "arbitrary"),
            vmem_limit_bytes=VMEM_LIMIT_BYTES),
        name="moba",
    )(qa, ka, vat, kmean, near, cfar)


def _mla_kernel(q_ref, k_ref, vt_ref, o_ref, m_s, acc_s, *, tq, tk, unroll):
    qi = pl.program_id(2)
    n_diag = tq // tk
    key = lax.broadcasted_iota(jnp.int32, (tk, tq), 0)
    qry = lax.broadcasted_iota(jnp.int32, (tk, tq), 1)
    q_heads = [q_ref[:, hh * LANES:(hh + 1) * LANES] for hh in range(HEADS_PER_STEP)]

    _reset_softmax_state(m_s, acc_s)

    def k_tile(hh, j):
        return k_ref[pl.ds(pl.multiple_of(j * tk, tk), tk), hh * LANES:(hh + 1) * LANES]

    def vt_tiles(hh, j, n):
        return vt_ref[hh * V_ROWS:(hh + 1) * V_ROWS, pl.ds(pl.multiple_of(j * tk, tk), n * tk)]

    def group(first, n, masked):
        scores = [[_dot_nt(k_tile(hh, first + u), q_heads[hh]) for u in range(n)]
                  for hh in range(HEADS_PER_STEP)]
        for hh in range(HEADS_PER_STEP):
            blocks = scores[hh]
            if masked:
                blocks = [jnp.where(key + u * tk <= qry, st, NEG) for u, st in enumerate(blocks)]
            _online_softmax_group(m_s.at[hh], acc_s.at[hh], blocks, vt_tiles(hh, first, n))

    group(qi * n_diag, n_diag, True)

    def body(g, carry):
        first = g * unroll
        late = jnp.zeros((1, tq), F32)
        pending = None
        for hh in range(HEADS_PER_STEP + 1):
            scores = None
            if hh < HEADS_PER_STEP:
                scores = [_dot_nt(k_tile(hh, first + u), q_heads[hh]) for u in range(unroll)]
            if pending is not None:
                ph, blocks = pending
                late = jnp.maximum(late, _streamed_softmax_group(
                    m_s.at[ph], acc_s.at[ph], blocks, vt_tiles(ph, first, unroll)))
            pending = (hh, scores)

        @pl.when(jnp.max(late) > 0.5)
        def _():
            for hh in range(HEADS_PER_STEP):
                blocks = [_dot_nt(k_tile(hh, first + u), q_heads[hh]) for u in range(unroll)]
                seen = functools.reduce(
                    jnp.maximum, [jnp.max(st, axis=0, keepdims=True) for st in blocks])
                redo = [seen > m_s[hh] + RESCALE_SLACK] * unroll
                _online_softmax_group(m_s.at[hh], acc_s.at[hh], blocks,
                                      vt_tiles(hh, first, unroll), redo)
        return carry

    lax.fori_loop(0, (qi * n_diag) // unroll, body, 0)
    o_ref[...] = _normalised_heads(acc_s, MLA_V_DIM).T.astype(o_ref.dtype)


def _mla(qm, km, vmt, *, batch, seq, tq, tk, unroll):
    n = qm.shape[0]
    nq = seq // tq
    assert tq % tk == 0 and (tq // tk) % unroll == 0
    pairs = MLA_HEADS // HEADS_PER_STEP
    return pl.pallas_call(
        functools.partial(_mla_kernel, tq=tq, tk=tk, unroll=unroll),
        grid=(batch, pairs, nq),
        in_specs=[
            pl.BlockSpec((tq, HEADS_PER_STEP * LANES), lambda b, p, i: (b * nq + i, p)),
            _per_batch((seq, HEADS_PER_STEP * LANES), lambda b, p, i: (b, p)),
            _per_batch((None, HEADS_PER_STEP * V_ROWS, seq), lambda b, p, i: (b, p, 0)),
        ],
        out_specs=pl.BlockSpec((tq, HEADS_PER_STEP * MLA_V_DIM), lambda b, p, i: (b * nq + i, p)),
        out_shape=jax.ShapeDtypeStruct((n, MLA_WIDTH), BF16),
        scratch_shapes=[
            pltpu.VMEM((HEADS_PER_STEP, 1, tq), F32),
            pltpu.VMEM((HEADS_PER_STEP, V_ROWS, tq), F32),
        ],
        compiler_params=pltpu.CompilerParams(
            dimension_semantics=("arbitrary", "arbitrary", "arbitrary"),
            vmem_limit_bytes=VMEM_LIMIT_BYTES),
        name="mla",
    )(qm, km, vmt)


def _back_kernel(x_ref, ya_ref, yb_ref, g_ref, wa_ref, wb_ref, wo_ref, gffn_ref, wup_ref,
                 cw_ref, cb_ref, wdn_ref, gfin_ref, o_ref,
                 carry_ref, act_ref, hn_ref, *, steps_per_seq, d_ff, fc, final):
    tm, d = x_ref.shape
    halo = SUBLANES

    @pl.when(pl.program_id(0) % steps_per_seq == 0)
    def _():
        carry_ref[...] = jnp.zeros_like(carry_ref)

    g = g_ref[...].astype(F32)
    mixed = g[:, :d] * _dot(ya_ref[...], wa_ref[...]) + g[:, d:] * _dot(yb_ref[...], wb_ref[...])
    h1 = x_ref[...] + _dot(mixed.astype(BF16), wo_ref[...])
    hn_ref[...] = _rms(h1, gffn_ref[...]).astype(BF16)

    def up_conv(col0):
        cols = slice(col0, col0 + fc)
        u = _dot(hn_ref[...], wup_ref[:, cols])
        rows = jnp.concatenate([carry_ref[:, cols], u], axis=0)
        carry_ref[:, cols] = u[tm - halo:tm, :]
        w = cw_ref[:, cols]
        y = cb_ref[:, cols]
        for t in range(CONV_WIDTH):
            back = CONV_WIDTH - 1 - t
            shifted = pltpu.roll(rows, back, axis=0) if back else rows
            y = y + w[t:t + 1, :] * shifted[halo:, :]
        return y

    for c in range(d_ff // fc):
        yg = up_conv(c * fc)
        yv = up_conv(d_ff + c * fc)
        act_ref[:, c * fc:(c + 1) * fc] = (yg * _sigmoid(yg) * yv).astype(BF16)

    h2 = h1 + _dot(act_ref[...], wdn_ref[...])
    o_ref[...] = _rms(h2, gfin_ref[...]) if final else h2


def _back(x2, ya, yb, g, wa, wb, wo, gffn, wup, cw, cb, wdn, gfin, *, seq, tm, fc, final):
    n, d = x2.shape
    d_ff = wdn.shape[0]
    row = lambda i: (i, 0)
    return pl.pallas_call(
        functools.partial(_back_kernel, steps_per_seq=seq // tm, d_ff=d_ff, fc=fc, final=final),
        grid=(n // tm,),
        in_specs=[
            pl.BlockSpec((tm, d), row),
            pl.BlockSpec((tm, ya.shape[1]), row),
            pl.BlockSpec((tm, yb.shape[1]), row),
            pl.BlockSpec((tm, g.shape[1]), row),
            _resident(wa.shape), _resident(wb.shape), _resident(wo.shape), _resident(gffn.shape),
            _resident(wup.shape), _resident(cw.shape), _resident(cb.shape), _resident(wdn.shape),
            _resident(gfin.shape),
        ],
        out_specs=pl.BlockSpec((tm, d), row),
        out_shape=jax.ShapeDtypeStruct((n, d), F32),
        scratch_shapes=[
            pltpu.VMEM((SUBLANES, 2 * d_ff), F32),
            pltpu.VMEM((tm, d_ff), BF16),
            pltpu.VMEM((tm, d), BF16),
        ],
        compiler_params=pltpu.CompilerParams(
            dimension_semantics=("arbitrary",), vmem_limit_bytes=VMEM_LIMIT_BYTES),
        name="back",
    )(x2, ya, yb, g, wa, wb, wo, gffn, wup, cw, cb, wdn, gfin)


def _t5_bucket_np(rel):
    n = np.maximum(rel, 0)
    max_exact = REL_BUCKETS // 2
    nf = np.maximum(n, 1).astype(np.float32)
    large = max_exact + (np.log(nf / np.float32(max_exact)) / np.float32(math.log(REL_MAX_DIST / max_exact))
                         * np.float32(REL_BUCKETS - max_exact)).astype(np.int32)
    large = np.minimum(large, REL_BUCKETS - 1)
    return np.where(n < max_exact, n, large)


def _moba_bias_tables(rel_bias):
    blk = MOBA_BLOCK
    assert REL_MAX_DIST <= blk + 1
    off = np.arange(blk, dtype=np.int32)
    rel_own = off[None, :] - off[:, None]
    bias_h = rel_bias.T.astype(F32)

    def lookup(bucket):
        onehot = jnp.asarray(bucket)[None, :, :] == jnp.arange(REL_BUCKETS)[:, None, None]
        return jnp.sum(jnp.where(onehot[None], bias_h[:, :, None, None], 0.0), axis=1)

    town = jnp.where(jnp.asarray(rel_own >= 0), lookup(_t5_bucket_np(rel_own)) * LOG2E, NEG)
    tprev = lookup(_t5_bucket_np(rel_own + blk)) * LOG2E
    near = jnp.stack([tprev, town, jnp.zeros_like(town)], axis=1)
    cfar = bias_h[:, int(_t5_bucket_np(np.int32(blk + 1)))] * LOG2E
    cfar = jnp.broadcast_to(cfar[:, None, None], (MOBA_HEADS, 1, LANES))
    return near, cfar


def _pad_heads(w, n_heads, width, padded=LANES):
    r = w.shape[0]
    w = w.reshape(r, n_heads, width)
    out = jnp.zeros((r, n_heads, padded), w.dtype).at[:, :, :width].set(w)
    return out.reshape(r, n_heads * padded)


def _rope_slot(w_rope, swap):
    half = MLA_ROPE_DIM // 2
    if swap:
        w_rope = jnp.concatenate([w_rope[:, half:], w_rope[:, :half]], axis=1)
    r = w_rope.shape[0]
    return jnp.zeros((r, LANES), w_rope.dtype).at[:, MLA_NOPE_DIM:MLA_NOPE_DIM + MLA_ROPE_DIM].set(w_rope)


def _front_weights(w_in, w_uq, w_ukv):
    d = w_in.shape[0]
    s0 = MOBA_WIDTH
    s1 = 2 * MOBA_WIDTH
    s2 = 3 * MOBA_WIDTH
    s3 = s2 + MLA_Q_RANK
    s4 = s3 + MLA_KV_RANK
    s5 = s4 + MLA_ROPE_DIM
    w1 = jnp.concatenate([
        w_in[:, :s0], w_in[:, s0:s1],
        w_in[:, s2:s3], w_in[:, s3:s4],
        _rope_slot(w_in[:, s4:s5], False), _rope_slot(w_in[:, s4:s5], True),
        w_in[:, s5:],
    ], axis=1).astype(BF16)
    assert w1.shape == (d, _C_END)
    wvt = _pad_heads(w_in[:, s1:s2], MOBA_HEADS, MOBA_HEAD_DIM, V_ROWS).T.astype(BF16)

    r = w_uq.shape[0]
    uq = w_uq.reshape(r, MLA_HEADS, MLA_QK_DIM)
    nope, rope = uq[:, :, :MLA_NOPE_DIM], uq[:, :, MLA_NOPE_DIM:]
    half = MLA_ROPE_DIM // 2
    zpad = jnp.zeros((r, MLA_HEADS, LANES - MLA_QK_DIM), w_uq.dtype)
    direct = jnp.concatenate([nope, rope, zpad], axis=2)
    swapped = jnp.concatenate([jnp.zeros_like(nope), rope[:, :, half:], rope[:, :, :half], zpad], axis=2)
    wuq2 = jnp.concatenate([direct.reshape(r, -1), swapped.reshape(r, -1)], axis=1).astype(BF16)

    r = w_ukv.shape[0]
    ukv = w_ukv.reshape(r, MLA_HEADS, MLA_NOPE_DIM + MLA_V_DIM)
    wk = _pad_heads(ukv[:, :, :MLA_NOPE_DIM].reshape(r, -1), MLA_HEADS, MLA_NOPE_DIM).astype(BF16)
    wkvt = _pad_heads(ukv[:, :, MLA_NOPE_DIM:].reshape(r, -1), MLA_HEADS, MLA_V_DIM, V_ROWS).T.astype(BF16)
    return w1, wvt, wuq2, wk, wkvt


def _rope_lane_tables(seq):
    dim = MLA_ROPE_DIM
    inv_freq = ROPE_THETA ** (-jnp.arange(0, dim, 2, dtype=F32) / dim)
    ang = jnp.arange(seq, dtype=F32)[:, None] * inv_freq[None, :]
    cos, sin = jnp.cos(ang), jnp.sin(ang)
    tail = jnp.zeros((seq, LANES - MLA_QK_DIM), F32)
    cosm = jnp.concatenate([jnp.ones((seq, MLA_NOPE_DIM), F32), cos, cos, tail], axis=1)
    sinm = jnp.concatenate([jnp.zeros((seq, MLA_NOPE_DIM), F32), -sin, sin, tail], axis=1)
    return cosm, sinm


class _Tiles:
    def __init__(self, seq):
        self.tm = 512 if seq % 512 == 0 else MOBA_BLOCK
        self.fc = 256
        self.mla_tq = self.tm
        self.mla_tk = 256
        self.mla_unroll = self.mla_tq // self.mla_tk
        self.moba_unroll = 4


def kernel(x, norm_attn_g, w_in, b_gate, q_norm_g, w_uq, kv_norm_g, w_ukv, rel_bias,
           w_branch_moba, w_branch_mla, w_out, norm_ffn_g, w_up, conv_w, conv_b, w_down,
           norm_final_g):
    batch, seq, d = x.shape
    depth = w_in.shape[0]
    assert seq % MOBA_BLOCK == 0
    t = _Tiles(seq)
    d_ff = w_down.shape[1]
    assert d_ff % t.fc == 0
    n = batch * seq

    cosm, sinm = _rope_lane_tables(seq)
    near, cfar = _moba_bias_tables(rel_bias)
    row = lambda v: v.reshape(1, -1).astype(F32)

    h = x.reshape(n, d)
    for l in range(depth):
        w1, wvt, wuq2, wk, wkvt = _front_weights(w_in[l], w_uq[l], w_ukv[l])
        qa, ka, kmean, vat, qm, km, vmt, g = _front(
            h, row(norm_attn_g[l]), w1, wvt, row(b_gate[l]), row(q_norm_g[l]), wuq2,
            row(kv_norm_g[l]), wk, wkvt, cosm, sinm, seq=seq, tm=t.tm)
        kmean = kmean.reshape(batch, seq // MOBA_BLOCK, MOBA_WIDTH)
        ya = _moba(qa, ka, vat, kmean, near, cfar, batch=batch, seq=seq,
                   unroll=t.moba_unroll)
        yb = _mla(qm, km, vmt, batch=batch, seq=seq, tq=t.mla_tq, tk=t.mla_tk,
                  unroll=t.mla_unroll)
        h = _back(h, ya, yb, g, w_branch_moba[l].astype(BF16), w_branch_mla[l].astype(BF16),
                  w_out[l].astype(BF16), row(norm_ffn_g[l]), w_up[l].astype(BF16),
                  conv_w[l].astype(F32), row(conv_b[l]), w_down[l].astype(BF16),
                  row(norm_final_g), seq=seq, tm=t.tm, fc=t.fc, final=(l == depth - 1))
    return h.reshape(batch, seq, d)
```

```python
import functools
import math

import numpy as np
import jax
import jax.numpy as jnp
from jax import lax
from jax.experimental import pallas as pl
from jax.experimental.pallas import tpu as pltpu

MOBA_HEADS = 8
MOBA_HEAD_DIM = 64
MOBA_BLOCK = 256
MOBA_TOPK = 3
MLA_HEADS = 8
MLA_Q_RANK = 256
MLA_KV_RANK = 128
MLA_NOPE_DIM = 64
MLA_ROPE_DIM = 32
MLA_V_DIM = 64
ROPE_THETA = 10000.0
REL_BUCKETS = 32
REL_MAX_DIST = 128
CONV_WIDTH = 3
N_BRANCH = 2
EPS = 1e-6

MOBA_WIDTH = MOBA_HEADS * MOBA_HEAD_DIM
MLA_QK_DIM = MLA_NOPE_DIM + MLA_ROPE_DIM
MLA_WIDTH = MLA_HEADS * MLA_V_DIM

LANES = 128
SUBLANES = 8
VMEM_LIMIT_BYTES = 56 * 1024 * 1024

NEG = -1e30

F32 = jnp.float32
BF16 = jnp.bfloat16


def _dot(a, b):
    return jnp.dot(a, b, preferred_element_type=F32)


def _dot_nt(a, b):
    return lax.dot_general(a, b, (((1,), (1,)), ((), ())), preferred_element_type=F32)


def _rms(x, g):
    return x * lax.rsqrt(jnp.mean(x * x, axis=-1, keepdims=True) + EPS) * g


def _sigmoid(z):
    return 1.0 / (1.0 + jnp.exp(-z))


def _resident(shape):
    nd = len(shape)
    return pl.BlockSpec(shape, lambda *_: (0,) * nd, pipeline_mode=pl.Buffered(1))


def _per_batch(shape, index_map):
    return pl.BlockSpec(shape, index_map, pipeline_mode=pl.Buffered(1))


_C_QA = 0
_C_KA = _C_QA + MOBA_WIDTH
_C_CQ = _C_KA + MOBA_WIDTH
_C_CKV = _C_CQ + MLA_Q_RANK
_C_KRA = _C_CKV + MLA_KV_RANK
_C_KRB = _C_KRA + LANES
_C_G = _C_KRB + LANES
_C_END = _C_G + N_BRANCH * 1024

BF16_SUBLANES = 16
V_ROWS = MLA_V_DIM + BF16_SUBLANES
assert MOBA_HEAD_DIM == MLA_V_DIM


def _ones_rows(n_rows):
    r = lax.broadcasted_iota(jnp.int32, (n_rows, 1), 0).astype(F32)
    within = r - jnp.floor((r + 0.5) * (1.0 / V_ROWS)) * V_ROWS
    return jnp.where(within == MLA_V_DIM, 1.0, 0.0).astype(F32)


def _front_kernel(x_ref, gattn_ref, w1_ref, wvt_ref, bg_ref, qng_ref, wuq_ref, kvng_ref, wkv_ref,
                  wkvt_ref, cos_ref, sin_ref,
                  qa_ref, ka_ref, kmean_ref, vat_ref, qm_ref, km_ref, vmt_ref, g_ref,
                  *, d_model, mla_scale):
    tm = x_ref.shape[0]
    hw = MLA_HEADS * LANES
    xn = _rms(x_ref[...], gattn_ref[...]).astype(BF16)

    def proj(a, b):
        return _dot(xn, w1_ref[:, a:b])

    qa_ref[...] = (proj(_C_QA, _C_KA) * (MOBA_HEAD_DIM ** -0.5 * LOG2E)).astype(BF16)
    ka = proj(_C_KA, _C_CQ)
    ka_ref[...] = ka.astype(BF16)
    for i in range(tm // MOBA_BLOCK):
        kmean_ref[i] = jnp.mean(ka[i * MOBA_BLOCK:(i + 1) * MOBA_BLOCK], axis=0, keepdims=True)
    ones = _ones_rows(vat_ref.shape[0])
    vat_ref[...] = (_dot_nt(wvt_ref[...], xn) + ones).astype(BF16)

    cosm = cos_ref[...]
    sinm = sin_ref[...]

    cqn = _rms(proj(_C_CQ, _C_CKV), qng_ref[...]).astype(BF16)
    qq = _dot(cqn, wuq_ref[...])
    for h in range(MLA_HEADS):
        a = qq[:, h * LANES:(h + 1) * LANES]
        b = qq[:, hw + h * LANES:hw + (h + 1) * LANES]
        qm_ref[:, h * LANES:(h + 1) * LANES] = ((a * cosm + b * sinm) * mla_scale).astype(BF16)

    ckvn = _rms(proj(_C_CKV, _C_KRA), kvng_ref[...]).astype(BF16)
    kv = _dot(ckvn, wkv_ref[...])
    kr = proj(_C_KRA, _C_KRB) * cosm + proj(_C_KRB, _C_G) * sinm
    for h in range(MLA_HEADS):
        km_ref[:, h * LANES:(h + 1) * LANES] = (kv[:, h * LANES:(h + 1) * LANES] + kr).astype(BF16)
    vmt_ref[...] = (_dot_nt(wkvt_ref[...], ckvn) + ones).astype(BF16)

    g_ref[...] = _sigmoid(proj(_C_G, _C_END) + bg_ref[...]).astype(BF16)


def _front(x2, gattn, w1, wvt, bg, qng, wuq2, kvng, wkv, wkvt, cosm, sinm, *, seq, tm):
    n, d = x2.shape
    batch = n // seq
    hw = MLA_HEADS * LANES
    nblk = n // MOBA_BLOCK
    steps_per_seq = seq // tm
    row = lambda i: (i, 0)
    tcol = lambda i: (i // steps_per_seq, 0, i % steps_per_seq)
    out_shape = (
        jax.ShapeDtypeStruct((n, MOBA_WIDTH), BF16),
        jax.ShapeDtypeStruct((n, MOBA_WIDTH), BF16),
        jax.ShapeDtypeStruct((nblk, 1, MOBA_WIDTH), F32),
        jax.ShapeDtypeStruct((batch, MOBA_HEADS * V_ROWS, seq), BF16),
        jax.ShapeDtypeStruct((n, hw), BF16),
        jax.ShapeDtypeStruct((n, hw), BF16),
        jax.ShapeDtypeStruct((batch, MLA_HEADS * V_ROWS, seq), BF16),
        jax.ShapeDtypeStruct((n, N_BRANCH * d), BF16),
    )
    in_specs = [
        pl.BlockSpec((tm, d), row),
        _resident(gattn.shape), _resident(w1.shape), _resident(wvt.shape), _resident(bg.shape),
        _resident(qng.shape), _resident(wuq2.shape), _resident(kvng.shape), _resident(wkv.shape),
        _resident(wkvt.shape),
        pl.BlockSpec((tm, LANES), lambda i: (i % steps_per_seq, 0)),
        pl.BlockSpec((tm, LANES), lambda i: (i % steps_per_seq, 0)),
    ]
    out_specs = (
        pl.BlockSpec((tm, MOBA_WIDTH), row),
        pl.BlockSpec((tm, MOBA_WIDTH), row),
        pl.BlockSpec((tm // MOBA_BLOCK, 1, MOBA_WIDTH), lambda i: (i, 0, 0)),
        pl.BlockSpec((None, MOBA_HEADS * V_ROWS, tm), tcol),
        pl.BlockSpec((tm, hw), row),
        pl.BlockSpec((tm, hw), row),
        pl.BlockSpec((None, MLA_HEADS * V_ROWS, tm), tcol),
        pl.BlockSpec((tm, N_BRANCH * d), row),
    )
    return pl.pallas_call(
        functools.partial(_front_kernel, d_model=d, mla_scale=MLA_QK_DIM ** -0.5 * LOG2E),
        grid=(n // tm,),
        in_specs=in_specs,
        out_specs=out_specs,
        out_shape=out_shape,
        compiler_params=pltpu.CompilerParams(
            dimension_semantics=("arbitrary",), vmem_limit_bytes=VMEM_LIMIT_BYTES),
        name="front",
    )(x2, gattn, w1, wvt, bg, qng, wuq2, kvng, wkv, wkvt, cosm, sinm)


HEADS_PER_STEP = 8
LOG2E = math.log2(math.e)
BIG = 1e30
RESCALE_SLACK = 64.0
MATMUL_LOOKAHEAD = 2


def _online_softmax_group(m_ref, acc_ref, score_blocks, vt_grp, chosen=None, shifts=None):
    n = len(score_blocks)
    chosen = chosen or [None] * n
    shifts = shifts or [None] * n
    m_old = m_ref[...]
    m_new = m_old
    for st, ch, sh in zip(score_blocks, chosen, shifts):
        rm = jnp.max(st, axis=0, keepdims=True)
        if sh is not None:
            rm = rm + sh
        m_new = jnp.maximum(m_new, rm if ch is None else jnp.where(ch, rm, NEG))
    probs = []
    for st, ch, sh in zip(score_blocks, chosen, shifts):
        off = m_new if sh is None else m_new - sh
        if ch is not None:
            off = jnp.where(ch, off, BIG)
        probs.append(jnp.exp2(st - off).astype(BF16))
    ot = _dot(vt_grp, jnp.concatenate(probs, axis=0))
    acc_ref[...] = jnp.exp2(m_old - m_new) * acc_ref[...] + ot
    m_ref[...] = m_new


def _streamed_softmax_group(m_ref, acc_ref, score_blocks, vt_grp, chosen=None, shifts=None):
    n = len(score_blocks)
    chosen = chosen or [None] * n
    shifts = shifts or [None] * n
    m_old = m_ref[...]
    seen = jnp.full(m_old.shape, NEG, F32)
    probs = []
    for st, ch, sh in zip(score_blocks, chosen, shifts):
        off = m_old if sh is None else m_old - sh
        if ch is not None:
            off = jnp.where(ch, off, BIG)
        probs.append(jnp.exp2(st - off).astype(BF16))
        rm = jnp.max(st, axis=0, keepdims=True)
        if sh is not None:
            rm = rm + sh
        seen = jnp.maximum(seen, rm if ch is None else jnp.where(ch, rm, NEG))
    late = seen > m_old + RESCALE_SLACK
    ot = _dot(vt_grp, jnp.concatenate(probs, axis=0))
    acc_ref[...] = acc_ref[...] + jnp.where(late, 0.0, ot)
    return jnp.where(late, 1.0, 0.0)


def _reset_softmax_state(m_s, acc_s):
    m_s[...] = jnp.full(m_s.shape, NEG, F32)
    acc_s[...] = jnp.zeros(acc_s.shape, F32)


def _normalised_heads(acc_s, v_dim):
    rows = []
    for hh in range(HEADS_PER_STEP):
        acc = acc_s[hh]
        rows.append(acc[0:v_dim, :] * (1.0 / acc[v_dim:v_dim + 1, :]))
    return jnp.concatenate(rows, axis=0)


def _moba_kernel(q_ref, k_ref, vt_ref, kmean_ref, near_ref, cfar_ref, o_ref,
                 m_s, acc_s, sel_s, *, unroll):
    blk = MOBA_BLOCK
    qb = pl.program_id(2)
    nb = kmean_ref.shape[0]
    lane = lax.broadcasted_iota(jnp.int32, (blk, LANES), 1)
    blk_id = lax.broadcasted_iota(jnp.int32, (nb, blk), 0).astype(F32)
    qbf = qb.astype(F32)
    has_prev = qb >= 1
    _reset_softmax_state(m_s, acc_s)

    def lane_group(hh):
        return slice((hh // 2) * LANES, (hh // 2 + 1) * LANES)

    def k_block(hh, j):
        return k_ref[pl.ds(pl.multiple_of(j * blk, blk), blk), lane_group(hh)]

    def vt_blocks(hh, j, n):
        return vt_ref[hh * V_ROWS:(hh + 1) * V_ROWS, pl.ds(pl.multiple_of(j * blk, blk), n * blk)]

    q_heads, prev_chosen = [], []
    for hh in range(HEADS_PER_STEP):
        in_head = (lane >= (hh % 2) * MOBA_HEAD_DIM) & (lane < (hh % 2 + 1) * MOBA_HEAD_DIM)
        qh = jnp.where(in_head, q_ref[:, lane_group(hh)].astype(F32), 0.0).astype(BF16)
        q_heads.append(qh)

        kmean = kmean_ref[:, lane_group(hh)].astype(BF16)
        gate = jnp.where(blk_id < qbf, _dot_nt(kmean, qh), -jnp.inf)
        sel = jnp.zeros((nb, blk), F32)
        for _ in range(min(MOBA_TOPK, nb)):
            best = jnp.max(gate, axis=0, keepdims=True)
            first = jnp.min(jnp.where(gate == best, blk_id, float(nb)), axis=0, keepdims=True)
            pick = blk_id == first
            sel = jnp.where(pick, 1.0, sel)
            gate = jnp.where(pick, -jnp.inf, gate)
        prev_chosen.append(jnp.max(jnp.where(blk_id == qbf - 1.0, sel, 0.0), axis=0, keepdims=True))
        sel_s[hh] = jnp.where(blk_id < qbf - 1.0, sel, 0.0)

    lo = jnp.maximum(qb - 1, 0)
    tab = jnp.where(has_prev, 0, 1)
    scores = [[_dot_nt(k_block(hh, lo + u), q_heads[hh]) + near_ref[hh, tab + u] for u in range(2)]
              for hh in range(HEADS_PER_STEP)]
    for hh in range(HEADS_PER_STEP):
        chosen = [jnp.where(has_prev, prev_chosen[hh], 1.0) > 0.5,
                  jnp.where(has_prev, jnp.ones((1, blk), F32), 0.0) > 0.5]
        _online_softmax_group(m_s.at[hh], acc_s.at[hh], scores[hh], vt_blocks(hh, lo, 2), chosen)

    def far(g, carry):
        first = g * unroll
        start = jnp.minimum(first, nb - unroll)

        def far_scores(hh):
            keys = k_ref[pl.ds(pl.multiple_of(start * blk, blk), unroll * blk), lane_group(hh)]
            st = _dot_nt(keys, q_heads[hh])
            return [st[u * blk:(u + 1) * blk, :] for u in range(unroll)]

        def far_chosen(hh):
            return [jnp.where(start + u >= first, sel_s[hh, pl.ds(start + u, 1), :], 0.0) > 0.5
                    for u in range(unroll)]

        late = {}
        queue = []
        for hh in range(HEADS_PER_STEP + MATMUL_LOOKAHEAD):
            if hh < HEADS_PER_STEP:
                queue.append((hh, far_scores(hh)))
            if hh >= MATMUL_LOOKAHEAD:
                ph, blocks = queue.pop(0)
                late[ph] = _streamed_softmax_group(
                    m_s.at[ph], acc_s.at[ph], blocks, vt_blocks(ph, start, unroll),
                    far_chosen(ph), [cfar_ref[ph][:, 0:1]] * unroll)

        @pl.when(jnp.max(functools.reduce(jnp.maximum, late.values())) > 0.5)
        def _():
            for hh in range(HEADS_PER_STEP):
                redo = [jnp.where(ch, late[hh], 0.0) > 0.5 for ch in far_chosen(hh)]
                _online_softmax_group(m_s.at[hh], acc_s.at[hh], far_scores(hh),
                                      vt_blocks(hh, start, unroll), redo,
                                      [cfar_ref[hh][:, 0:1]] * unroll)
        return carry

    lax.fori_loop(0, (qb - 1 + unroll - 1) // unroll, far, 0)
    o_ref[...] = _normalised_heads(acc_s, MOBA_HEAD_DIM).T.astype(o_ref.dtype)


def _moba(qa, ka, vat, kmean, near, cfar, *, batch, seq, unroll):
    n = qa.shape[0]
    blk = MOBA_BLOCK
    nb = seq // blk
    assert nb >= unroll and nb >= 2
    steps = MOBA_HEADS // HEADS_PER_STEP
    width = HEADS_PER_STEP * MOBA_HEAD_DIM
    return pl.pallas_call(
        functools.partial(_moba_kernel, unroll=unroll),
        grid=(batch, steps, nb),
        in_specs=[
            pl.BlockSpec((blk, width), lambda b, p, i: (b * nb + i, p)),
            _per_batch((seq, width), lambda b, p, i: (b, p)),
            _per_batch((None, HEADS_PER_STEP * V_ROWS, seq), lambda b, p, i: (b, p, 0)),
            pl.BlockSpec((None, nb, width), lambda b, p, i: (b, 0, p)),
            _per_batch((HEADS_PER_STEP, 3, blk, blk), lambda b, p, i: (p, 0, 0, 0)),
            pl.BlockSpec((HEADS_PER_STEP, 1, LANES), lambda b, p, i: (p, 0, 0)),
        ],
        out_specs=pl.BlockSpec((blk, width), lambda b, p, i: (b * nb + i, p)),
        out_shape=jax.ShapeDtypeStruct((n, MOBA_WIDTH), BF16),
        scratch_shapes=[
            pltpu.VMEM((HEADS_PER_STEP, 1, blk), F32),
            pltpu.VMEM((HEADS_PER_STEP, V_ROWS, blk), F32),
            pltpu.VMEM((HEADS_PER_STEP, nb, blk), F32),
        ],
        compiler_params=pltpu.CompilerParams(
            dimension_semantics=("arbitrary", "arbitrary", "arbitrary"),
            vmem_limit_bytes=VMEM_LIMIT_BYTES),
        name="moba",
    )(qa, ka, vat, kmean, near, cfar)


def _mla_kernel(q_ref, k_ref, vt_ref, o_ref, m_s, acc_s, *, tq, tk, unroll):
    qi = pl.program_id(2)
    n_diag = tq // tk
    key = lax.broadcasted_iota(jnp.int32, (tk, tq), 0)
    qry = lax.broadcasted_iota(jnp.int32, (tk, tq), 1)
    q_heads = [q_ref[:, hh * LANES:(hh + 1) * LANES] for hh in range(HEADS_PER_STEP)]

    _reset_softmax_state(m_s, acc_s)

    def k_tile(hh, j):
        return k_ref[pl.ds(pl.multiple_of(j * tk, tk), tk), hh * LANES:(hh + 1) * LANES]

    def vt_tiles(hh, j, n):
        return vt_ref[hh * V_ROWS:(hh + 1) * V_ROWS, pl.ds(pl.multiple_of(j * tk, tk), n * tk)]

    def group(first, n, masked):
        scores = [[_dot_nt(k_tile(hh, first + u), q_heads[hh]) for u in range(n)]
                  for hh in range(HEADS_PER_STEP)]
        for hh in range(HEADS_PER_STEP):
            blocks = scores[hh]
            if masked:
                blocks = [jnp.where(key + u * tk <= qry, st, NEG) for u, st in enumerate(blocks)]
            _online_softmax_group(m_s.at[hh], acc_s.at[hh], blocks, vt_tiles(hh, first, n))

    group(qi * n_diag, n_diag, True)

    def visible_scores(hh, g):
        start = pl.multiple_of(g * unroll * tk, tk)
        st = _dot_nt(k_ref[pl.ds(start, unroll * tk), hh * LANES:(hh + 1) * LANES], q_heads[hh])
        return [st[u * tk:(u + 1) * tk, :] for u in range(unroll)]

    def sweep(groups):
        stages = [(i, hh) for i in range(len(groups)) for hh in range(HEADS_PER_STEP)]
        late = {}
        queue = []
        for n in range(len(stages) + MATMUL_LOOKAHEAD):
            if n < len(stages):
                i, hh = stages[n]
                queue.append((stages[n], visible_scores(hh, groups[i])))
            if n >= MATMUL_LOOKAHEAD:
                (i, hh), blocks = queue.pop(0)
                late[(i, hh)] = _streamed_softmax_group(
                    m_s.at[hh], acc_s.at[hh], blocks, vt_tiles(hh, groups[i] * unroll, unroll))

        @pl.when(jnp.max(functools.reduce(jnp.maximum, late.values())) > 0.5)
        def _():
            for i, hh in stages:
                _online_softmax_group(m_s.at[hh], acc_s.at[hh], visible_scores(hh, groups[i]),
                                      vt_tiles(hh, groups[i] * unroll, unroll),
                                      [late[(i, hh)] > 0.5] * unroll)

    def body(g, carry):
        sweep([g])
        return carry

    lax.fori_loop(0, (qi * n_diag) // unroll, body, 0)

    o_ref[...] = _normalised_heads(acc_s, MLA_V_DIM).T.astype(o_ref.dtype)


def _mla(qm, km, vmt, *, batch, seq, tq, tk, unroll):
    n = qm.shape[0]
    nq = seq // tq
    assert tq % tk == 0 and (tq // tk) % unroll == 0
    pairs = MLA_HEADS // HEADS_PER_STEP
    return pl.pallas_call(
        functools.partial(_mla_kernel, tq=tq, tk=tk, unroll=unroll),
        grid=(batch, pairs, nq),
        in_specs=[
            pl.BlockSpec((tq, HEADS_PER_STEP * LANES), lambda b, p, i: (b * nq + i, p)),
            _per_batch((seq, HEADS_PER_STEP * LANES), lambda b, p, i: (b, p)),
            _per_batch((None, HEADS_PER_STEP * V_ROWS, seq), lambda b, p, i: (b, p, 0)),
        ],
        out_specs=pl.BlockSpec((tq, HEADS_PER_STEP * MLA_V_DIM), lambda b, p, i: (b * nq + i, p)),
        out_shape=jax.ShapeDtypeStruct((n, MLA_WIDTH), BF16),
        scratch_shapes=[
            pltpu.VMEM((HEADS_PER_STEP, 1, tq), F32),
            pltpu.VMEM((HEADS_PER_STEP, V_ROWS, tq), F32),
        ],
        compiler_params=pltpu.CompilerParams(
            dimension_semantics=("arbitrary", "arbitrary", "arbitrary"),
            vmem_limit_bytes=VMEM_LIMIT_BYTES),
        name="mla",
    )(qm, km, vmt)


def _back_kernel(x_ref, ya_ref, yb_ref, g_ref, wa_ref, wb_ref, wo_ref, gffn_ref, wup_ref,
                 cw_ref, cb_ref, wdn_ref, gfin_ref, o_ref,
                 carry_ref, act_ref, hn_ref, *, steps_per_seq, d_ff, fc, final):
    tm, d = x_ref.shape
    halo = SUBLANES

    @pl.when(pl.program_id(0) % steps_per_seq == 0)
    def _():
        carry_ref[...] = jnp.zeros_like(carry_ref)

    g = g_ref[...].astype(F32)
    mixed = g[:, :d] * _dot(ya_ref[...], wa_ref[...]) + g[:, d:] * _dot(yb_ref[...], wb_ref[...])
    h1 = x_ref[...] + _dot(mixed.astype(BF16), wo_ref[...])
    hn_ref[...] = _rms(h1, gffn_ref[...]).astype(BF16)

    def up_conv(col0):
        cols = slice(col0, col0 + fc)
        u = _dot(hn_ref[...], wup_ref[:, cols])
        rows = jnp.concatenate([carry_ref[:, cols], u], axis=0)
        carry_ref[:, cols] = u[tm - halo:tm, :]
        w = cw_ref[:, cols]
        y = cb_ref[:, cols]
        for t in range(CONV_WIDTH):
            back = CONV_WIDTH - 1 - t
            shifted = pltpu.roll(rows, back, axis=0) if back else rows
            y = y + w[t:t + 1, :] * shifted[halo:, :]
        return y

    for c in range(d_ff // fc):
        yg = up_conv(c * fc)
        yv = up_conv(d_ff + c * fc)
        act_ref[:, c * fc:(c + 1) * fc] = (yg * _sigmoid(yg) * yv).astype(BF16)

    h2 = h1 + _dot(act_ref[...], wdn_ref[...])
    o_ref[...] = _rms(h2, gfin_ref[...]) if final else h2


def _back(x2, ya, yb, g, wa, wb, wo, gffn, wup, cw, cb, wdn, gfin, *, seq, tm, fc, final):
    n, d = x2.shape
    d_ff = wdn.shape[0]
    row = lambda i: (i, 0)
    return pl.pallas_call(
        functools.partial(_back_kernel, steps_per_seq=seq // tm, d_ff=d_ff, fc=fc, final=final),
        grid=(n // tm,),
        in_specs=[
            pl.BlockSpec((tm, d), row),
            pl.BlockSpec((tm, ya.shape[1]), row),
            pl.BlockSpec((tm, yb.shape[1]), row),
            pl.BlockSpec((tm, g.shape[1]), row),
            _resident(wa.shape), _resident(wb.shape), _resident(wo.shape), _resident(gffn.shape),
            _resident(wup.shape), _resident(cw.shape), _resident(cb.shape), _resident(wdn.shape),
            _resident(gfin.shape),
        ],
        out_specs=pl.BlockSpec((tm, d), row),
        out_shape=jax.ShapeDtypeStruct((n, d), F32),
        scratch_shapes=[
            pltpu.VMEM((SUBLANES, 2 * d_ff), F32),
            pltpu.VMEM((tm, d_ff), BF16),
            pltpu.VMEM((tm, d), BF16),
        ],
        compiler_params=pltpu.CompilerParams(
            dimension_semantics=("arbitrary",), vmem_limit_bytes=VMEM_LIMIT_BYTES),
        name="back",
    )(x2, ya, yb, g, wa, wb, wo, gffn, wup, cw, cb, wdn, gfin)


def _t5_bucket_np(rel):
    n = np.maximum(rel, 0)
    max_exact = REL_BUCKETS // 2
    nf = np.maximum(n, 1).astype(np.float32)
    large = max_exact + (np.log(nf / np.float32(max_exact)) / np.float32(math.log(REL_MAX_DIST / max_exact))
                         * np.float32(REL_BUCKETS - max_exact)).astype(np.int32)
    large = np.minimum(large, REL_BUCKETS - 1)
    return np.where(n < max_exact, n, large)


def _moba_bias_tables(rel_bias):
    blk = MOBA_BLOCK
    period = 2 * blk
    assert REL_MAX_DIST <= blk + 1
    bias_h = rel_bias.T.astype(F32)
    slot = np.arange(period, dtype=np.int32)
    dist = np.where(slot < blk, slot, slot - period)

    def lookup(bucket):
        onehot = jnp.asarray(bucket)[None, :] == jnp.arange(REL_BUCKETS)[:, None]
        return jnp.sum(jnp.where(onehot[None], bias_h[:, :, None], 0.0), axis=1)

    def toeplitz(v):
        tiled = jnp.tile(v, (1, blk))[:, :blk * (period - 1)]
        return tiled.reshape(v.shape[0], blk, period - 1)[:, :, :blk]

    town = toeplitz(jnp.where(jnp.asarray(dist >= 0), lookup(_t5_bucket_np(dist)) * LOG2E, NEG))
    tprev = toeplitz(lookup(_t5_bucket_np(dist + blk)) * LOG2E)
    near = jnp.stack([tprev, town, jnp.zeros_like(town)], axis=1)
    cfar = bias_h[:, int(_t5_bucket_np(np.int32(blk + 1)))] * LOG2E
    cfar = jnp.broadcast_to(cfar[:, None, None], (MOBA_HEADS, 1, LANES))
    return near, cfar


def _pad_heads(w, n_heads, width, padded=LANES):
    r = w.shape[0]
    w = w.reshape(r, n_heads, width)
    out = jnp.zeros((r, n_heads, padded), w.dtype).at[:, :, :width].set(w)
    return out.reshape(r, n_heads * padded)


def _rope_slot(w_rope, swap):
    half = MLA_ROPE_DIM // 2
    if swap:
        w_rope = jnp.concatenate([w_rope[:, half:], w_rope[:, :half]], axis=1)
    r = w_rope.shape[0]
    return jnp.zeros((r, LANES), w_rope.dtype).at[:, MLA_NOPE_DIM:MLA_NOPE_DIM + MLA_ROPE_DIM].set(w_rope)


def _front_weights(w_in, w_uq, w_ukv):
    d = w_in.shape[0]
    s0 = MOBA_WIDTH
    s1 = 2 * MOBA_WIDTH
    s2 = 3 * MOBA_WIDTH
    s3 = s2 + MLA_Q_RANK
    s4 = s3 + MLA_KV_RANK
    s5 = s4 + MLA_ROPE_DIM
    w1 = jnp.concatenate([
        w_in[:, :s0], w_in[:, s0:s1],
        w_in[:, s2:s3], w_in[:, s3:s4],
        _rope_slot(w_in[:, s4:s5], False), _rope_slot(w_in[:, s4:s5], True),
        w_in[:, s5:],
    ], axis=1).astype(BF16)
    assert w1.shape == (d, _C_END)
    wvt = _pad_heads(w_in[:, s1:s2], MOBA_HEADS, MOBA_HEAD_DIM, V_ROWS).T.astype(BF16)

    r = w_uq.shape[0]
    uq = w_uq.reshape(r, MLA_HEADS, MLA_QK_DIM)
    nope, rope = uq[:, :, :MLA_NOPE_DIM], uq[:, :, MLA_NOPE_DIM:]
    half = MLA_ROPE_DIM // 2
    zpad = jnp.zeros((r, MLA_HEADS, LANES - MLA_QK_DIM), w_uq.dtype)
    direct = jnp.concatenate([nope, rope, zpad], axis=2)
    swapped = jnp.concatenate([jnp.zeros_like(nope), rope[:, :, half:], rope[:, :, :half], zpad], axis=2)
    wuq2 = jnp.concatenate([direct.reshape(r, -1), swapped.reshape(r, -1)], axis=1).astype(BF16)

    r = w_ukv.shape[0]
    ukv = w_ukv.reshape(r, MLA_HEADS, MLA_NOPE_DIM + MLA_V_DIM)
    wk = _pad_heads(ukv[:, :, :MLA_NOPE_DIM].reshape(r, -1), MLA_HEADS, MLA_NOPE_DIM).astype(BF16)
    wkvt = _pad_heads(ukv[:, :, MLA_NOPE_DIM:].reshape(r, -1), MLA_HEADS, MLA_V_DIM, V_ROWS).T.astype(BF16)
    return w1, wvt, wuq2, wk, wkvt


def _rope_lane_tables(seq):
    dim = MLA_ROPE_DIM
    inv_freq = ROPE_THETA ** (-jnp.arange(0, dim, 2, dtype=F32) / dim)
    ang = jnp.arange(seq, dtype=F32)[:, None] * inv_freq[None, :]
    cos, sin = jnp.cos(ang), jnp.sin(ang)
    tail = jnp.zeros((seq, LANES - MLA_QK_DIM), F32)
    cosm = jnp.concatenate([jnp.ones((seq, MLA_NOPE_DIM), F32), cos, cos, tail], axis=1)
    sinm = jnp.concatenate([jnp.zeros((seq, MLA_NOPE_DIM), F32), -sin, sin, tail], axis=1)
    return cosm, sinm


class _Tiles:
    def __init__(self, seq):
        self.tm = 512 if seq % 512 == 0 else MOBA_BLOCK
        self.fc = 256
        self.mla_tq = self.tm
        self.mla_tk = 256
        self.mla_unroll = self.mla_tq // self.mla_tk
        self.moba_unroll = 4


def kernel(x, norm_attn_g, w_in, b_gate, q_norm_g, w_uq, kv_norm_g, w_ukv, rel_bias,
           w_branch_moba, w_branch_mla, w_out, norm_ffn_g, w_up, conv_w, conv_b, w_down,
           norm_final_g):
    batch, seq, d = x.shape
    depth = w_in.shape[0]
    assert seq % MOBA_BLOCK == 0
    t = _Tiles(seq)
    d_ff = w_down.shape[1]
    assert d_ff % t.fc == 0
    n = batch * seq

    cosm, sinm = _rope_lane_tables(seq)
    near, cfar = _moba_bias_tables(rel_bias)
    row = lambda v: v.reshape(1, -1).astype(F32)

    h = x.reshape(n, d)
    for l in range(depth):
        w1, wvt, wuq2, wk, wkvt = _front_weights(w_in[l], w_uq[l], w_ukv[l])
        qa, ka, kmean, vat, qm, km, vmt, g = _front(
            h, row(norm_attn_g[l]), w1, wvt, row(b_gate[l]), row(q_norm_g[l]), wuq2,
            row(kv_norm_g[l]), wk, wkvt, cosm, sinm, seq=seq, tm=t.tm)
        kmean = kmean.reshape(batch, seq // MOBA_BLOCK, MOBA_WIDTH)
        ya = _moba(qa, ka, vat, kmean, near, cfar, batch=batch, seq=seq,
                   unroll=t.moba_unroll)
        yb = _mla(qm, km, vmt, batch=batch, seq=seq, tq=t.mla_tq, tk=t.mla_tk,
                  unroll=t.mla_unroll)
        h = _back(h, ya, yb, g, w_branch_moba[l].astype(BF16), w_branch_mla[l].astype(BF16),
                  w_out[l].astype(BF16), row(norm_ffn_g[l]), w_up[l].astype(BF16),
                  conv_w[l].astype(F32), row(conv_b[l]), w_down[l].astype(BF16),
                  row(norm_final_g), seq=seq, tm=t.tm, fc=t.fc, final=(l == depth - 1))
    return h.reshape(batch, seq, d)
```

```python
import functools
import math

import numpy as np
import jax
import jax.numpy as jnp
from jax import lax
from jax.experimental import pallas as pl
from jax.experimental.pallas import tpu as pltpu

MOBA_HEADS = 8
MOBA_HEAD_DIM = 64
MOBA_BLOCK = 256
MOBA_TOPK = 3
MLA_HEADS = 8
MLA_Q_RANK = 256
MLA_KV_RANK = 128
MLA_NOPE_DIM = 64
MLA_ROPE_DIM = 32
MLA_V_DIM = 64
ROPE_THETA = 10000.0
REL_BUCKETS = 32
REL_MAX_DIST = 128
CONV_WIDTH = 3
N_BRANCH = 2
EPS = 1e-6

MOBA_WIDTH = MOBA_HEADS * MOBA_HEAD_DIM
MLA_QK_DIM = MLA_NOPE_DIM + MLA_ROPE_DIM
MLA_WIDTH = MLA_HEADS * MLA_V_DIM

LANES = 128
SUBLANES = 8
VMEM_LIMIT_BYTES = 56 * 1024 * 1024

NEG = -1e30

F32 = jnp.float32
BF16 = jnp.bfloat16


def _dot(a, b):
    return jnp.dot(a, b, preferred_element_type=F32)


def _dot_nt(a, b):
    return lax.dot_general(a, b, (((1,), (1,)), ((), ())), preferred_element_type=F32)


def _rms(x, g):
    return x * lax.rsqrt(jnp.mean(x * x, axis=-1, keepdims=True) + EPS) * g


def _sigmoid(z):
    return 1.0 / (1.0 + jnp.exp(-z))


def _resident(shape):
    nd = len(shape)
    return pl.BlockSpec(shape, lambda *_: (0,) * nd, pipeline_mode=pl.Buffered(1))


def _per_batch(shape, index_map):
    return pl.BlockSpec(shape, index_map, pipeline_mode=pl.Buffered(1))


_C_QA = 0
_C_KA = _C_QA + MOBA_WIDTH
_C_CQ = _C_KA + MOBA_WIDTH
_C_CKV = _C_CQ + MLA_Q_RANK
_C_KRA = _C_CKV + MLA_KV_RANK
_C_G = _C_KRA + LANES
_C_END = _C_G + N_BRANCH * 1024

BF16_SUBLANES = 16
V_ROWS = MLA_V_DIM + BF16_SUBLANES
assert MOBA_HEAD_DIM == MLA_V_DIM


def _ones_rows(n_rows):
    r = lax.broadcasted_iota(jnp.int32, (n_rows, 1), 0).astype(F32)
    within = r - jnp.floor((r + 0.5) * (1.0 / V_ROWS)) * V_ROWS
    return jnp.where(within == MLA_V_DIM, 1.0, 0.0).astype(F32)


def _front_kernel(x_ref, gattn_ref, w1_ref, wvt_ref, bg_ref, qng_ref, wuq_ref, kvng_ref, wkv_ref,
                  wkvt_ref, cos_ref, sin_ref,
                  qa_ref, ka_ref, kmean_ref, vat_ref, qm_ref, km_ref, vmt_ref, g_ref,
                  *, mla_scale):
    tm = x_ref.shape[0]
    xn = _rms(x_ref[...], gattn_ref[...]).astype(BF16)

    def proj(a, b):
        return _dot(xn, w1_ref[:, a:b])

    qa_ref[...] = (proj(_C_QA, _C_KA) * (MOBA_HEAD_DIM ** -0.5 * LOG2E)).astype(BF16)
    ka = proj(_C_KA, _C_CQ)
    ka_ref[...] = ka.astype(BF16)
    for i in range(tm // MOBA_BLOCK):
        kmean_ref[i] = jnp.mean(ka[i * MOBA_BLOCK:(i + 1) * MOBA_BLOCK], axis=0, keepdims=True)
    ones = _ones_rows(vat_ref.shape[0])
    vat_ref[...] = (_dot_nt(wvt_ref[...], xn) + ones).astype(BF16)

    cosm = cos_ref[...]
    sinm = sin_ref[...]
    lane = lax.broadcasted_iota(jnp.int32, (tm, LANES), 1)
    half = MLA_ROPE_DIM // 2

    def rope(a):
        x2_on_x1 = pltpu.roll(a, LANES - half, axis=1)
        x1_on_x2 = pltpu.roll(a, half, axis=1)
        swapped = jnp.where(lane < MLA_NOPE_DIM + half, x2_on_x1, x1_on_x2)
        return a * cosm + swapped * sinm

    cqn = _rms(proj(_C_CQ, _C_CKV), qng_ref[...]).astype(BF16)
    qq = _dot(cqn, wuq_ref[...])
    for h in range(MLA_HEADS):
        qm_ref[:, h * LANES:(h + 1) * LANES] = (
            rope(qq[:, h * LANES:(h + 1) * LANES]) * mla_scale).astype(BF16)

    ckv_kra = proj(_C_CKV, _C_G)
    ckvn = _rms(ckv_kra[:, :MLA_KV_RANK], kvng_ref[...]).astype(BF16)
    kv = _dot(ckvn, wkv_ref[...])
    kr = rope(ckv_kra[:, MLA_KV_RANK:])
    for h in range(MLA_HEADS):
        km_ref[:, h * LANES:(h + 1) * LANES] = (kv[:, h * LANES:(h + 1) * LANES] + kr).astype(BF16)
    vmt_ref[...] = (_dot_nt(wkvt_ref[...], ckvn) + ones).astype(BF16)

    g_ref[...] = _sigmoid(proj(_C_G, _C_END) + bg_ref[...]).astype(BF16)


def _front(x2, gattn, w1, wvt, bg, qng, wuq, kvng, wkv, wkvt, cosm, sinm, *, seq, tm):
    n, d = x2.shape
    batch = n // seq
    hw = MLA_HEADS * LANES
    nblk = n // MOBA_BLOCK
    steps_per_seq = seq // tm
    row = lambda i: (i, 0)
    tcol = lambda i: (i // steps_per_seq, 0, i % steps_per_seq)
    out_shape = (
        jax.ShapeDtypeStruct((n, MOBA_WIDTH), BF16),
        jax.ShapeDtypeStruct((n, MOBA_WIDTH), BF16),
        jax.ShapeDtypeStruct((nblk, 1, MOBA_WIDTH), F32),
        jax.ShapeDtypeStruct((batch, MOBA_HEADS * V_ROWS, seq), BF16),
        jax.ShapeDtypeStruct((n, hw), BF16),
        jax.ShapeDtypeStruct((n, hw), BF16),
        jax.ShapeDtypeStruct((batch, MLA_HEADS * V_ROWS, seq), BF16),
        jax.ShapeDtypeStruct((n, N_BRANCH * d), BF16),
    )
    in_specs = [
        pl.BlockSpec((tm, d), row),
        _resident(gattn.shape), _resident(w1.shape), _resident(wvt.shape), _resident(bg.shape),
        _resident(qng.shape), _resident(wuq.shape), _resident(kvng.shape), _resident(wkv.shape),
        _resident(wkvt.shape),
        pl.BlockSpec((tm, LANES), lambda i: (i % steps_per_seq, 0)),
        pl.BlockSpec((tm, LANES), lambda i: (i % steps_per_seq, 0)),
    ]
    out_specs = (
        pl.BlockSpec((tm, MOBA_WIDTH), row),
        pl.BlockSpec((tm, MOBA_WIDTH), row),
        pl.BlockSpec((tm // MOBA_BLOCK, 1, MOBA_WIDTH), lambda i: (i, 0, 0)),
        pl.BlockSpec((None, MOBA_HEADS * V_ROWS, tm), tcol),
        pl.BlockSpec((tm, hw), row),
        pl.BlockSpec((tm, hw), row),
        pl.BlockSpec((None, MLA_HEADS * V_ROWS, tm), tcol),
        pl.BlockSpec((tm, N_BRANCH * d), row),
    )
    return pl.pallas_call(
        functools.partial(_front_kernel, mla_scale=MLA_QK_DIM ** -0.5 * LOG2E),
        grid=(n // tm,),
        in_specs=in_specs,
        out_specs=out_specs,
        out_shape=out_shape,
        compiler_params=pltpu.CompilerParams(
            dimension_semantics=("arbitrary",), vmem_limit_bytes=VMEM_LIMIT_BYTES),
        name="front",
    )(x2, gattn, w1, wvt, bg, qng, wuq, kvng, wkv, wkvt, cosm, sinm)


HEADS_PER_STEP = 8
LOG2E = math.log2(math.e)
BIG = 1e30
RESCALE_SLACK = 64.0
MATMUL_LOOKAHEAD = 2


def _online_softmax_group(m_ref, acc_ref, score_blocks, vt_grp, chosen=None, shifts=None):
    n = len(score_blocks)
    chosen = chosen or [None] * n
    shifts = shifts or [None] * n
    m_old = m_ref[...]
    m_new = m_old
    for st, ch, sh in zip(score_blocks, chosen, shifts):
        rm = jnp.max(st, axis=0, keepdims=True)
        if sh is not None:
            rm = rm + sh
        m_new = jnp.maximum(m_new, rm if ch is None else jnp.where(ch, rm, NEG))
    probs = []
    for st, ch, sh in zip(score_blocks, chosen, shifts):
        off = m_new if sh is None else m_new - sh
        if ch is not None:
            off = jnp.where(ch, off, BIG)
        probs.append(jnp.exp2(st - off).astype(BF16))
    ot = _dot(vt_grp, jnp.concatenate(probs, axis=0))
    acc_ref[...] = jnp.exp2(m_old - m_new) * acc_ref[...] + ot
    m_ref[...] = m_new


def _streamed_softmax_group(m_ref, acc_ref, score_blocks, vt_grp, chosen=None, shifts=None):
    n = len(score_blocks)
    chosen = chosen or [None] * n
    shifts = shifts or [None] * n
    m_old = m_ref[...]
    seen = jnp.full(m_old.shape, NEG, F32)
    probs = []
    for st, ch, sh in zip(score_blocks, chosen, shifts):
        off = m_old if sh is None else m_old - sh
        if ch is not None:
            off = jnp.where(ch, off, BIG)
        probs.append(jnp.exp2(st - off).astype(BF16))
        rm = jnp.max(st, axis=0, keepdims=True)
        if sh is not None:
            rm = rm + sh
        seen = jnp.maximum(seen, rm if ch is None else jnp.where(ch, rm, NEG))
    late = seen > m_old + RESCALE_SLACK
    ot = _dot(vt_grp, jnp.concatenate(probs, axis=0))
    acc_ref[...] = acc_ref[...] + jnp.where(late, 0.0, ot)
    return jnp.where(late, 1.0, 0.0)


def _reset_softmax_state(m_s, acc_s):
    m_s[...] = jnp.full(m_s.shape, NEG, F32)
    acc_s[...] = jnp.zeros(acc_s.shape, F32)


def _normalised_heads(acc_s, v_dim):
    rows = []
    for hh in range(HEADS_PER_STEP):
        acc = acc_s[hh]
        rows.append(acc[0:v_dim, :] * (1.0 / acc[v_dim:v_dim + 1, :]))
    return jnp.concatenate(rows, axis=0)


def _moba_kernel(q_ref, k_ref, vt_ref, kmean_ref, near_ref, cfar_ref, o_ref,
                 m_s, acc_s, sel_s, *, unroll):
    blk = MOBA_BLOCK
    qb = pl.program_id(2)
    nb = kmean_ref.shape[0]
    lane = lax.broadcasted_iota(jnp.int32, (blk, LANES), 1)
    blk_id = lax.broadcasted_iota(jnp.int32, (nb, blk), 0).astype(F32)
    qbf = qb.astype(F32)
    has_prev = qb >= 1
    _reset_softmax_state(m_s, acc_s)

    def lane_group(hh):
        return slice((hh // 2) * LANES, (hh // 2 + 1) * LANES)

    def k_block(hh, j):
        return k_ref[pl.ds(pl.multiple_of(j * blk, blk), blk), lane_group(hh)]

    def vt_blocks(hh, j, n):
        return vt_ref[hh * V_ROWS:(hh + 1) * V_ROWS, pl.ds(pl.multiple_of(j * blk, blk), n * blk)]

    q_heads, prev_chosen = [], []
    for hh in range(HEADS_PER_STEP):
        in_head = (lane >= (hh % 2) * MOBA_HEAD_DIM) & (lane < (hh % 2 + 1) * MOBA_HEAD_DIM)
        qh = jnp.where(in_head, q_ref[:, lane_group(hh)].astype(F32), 0.0).astype(BF16)
        q_heads.append(qh)

        kmean = kmean_ref[:, lane_group(hh)].astype(BF16)
        gate = jnp.where(blk_id < qbf, _dot_nt(kmean, qh), -jnp.inf)
        sel = jnp.zeros((nb, blk), F32)
        for _ in range(min(MOBA_TOPK, nb)):
            best = jnp.max(gate, axis=0, keepdims=True)
            first = jnp.min(jnp.where(gate == best, blk_id, float(nb)), axis=0, keepdims=True)
            pick = blk_id == first
            sel = jnp.where(pick, 1.0, sel)
            gate = jnp.where(pick, -jnp.inf, gate)
        prev_chosen.append(jnp.max(jnp.where(blk_id == qbf - 1.0, sel, 0.0), axis=0, keepdims=True))
        sel_s[hh] = jnp.where(blk_id < qbf - 1.0, sel, 0.0)

    lo = jnp.maximum(qb - 1, 0)
    tab = jnp.where(has_prev, 0, 1)
    scores = [[_dot_nt(k_block(hh, lo + u), q_heads[hh]) + near_ref[hh, tab + u] for u in range(2)]
              for hh in range(HEADS_PER_STEP)]
    for hh in range(HEADS_PER_STEP):
        chosen = [jnp.where(has_prev, prev_chosen[hh], 1.0) > 0.5,
                  jnp.where(has_prev, jnp.ones((1, blk), F32), 0.0) > 0.5]
        _online_softmax_group(m_s.at[hh], acc_s.at[hh], scores[hh], vt_blocks(hh, lo, 2), chosen)

    def far(g, carry):
        first = g * unroll
        start = jnp.minimum(first, nb - unroll)

        def far_scores(hh):
            keys = k_ref[pl.ds(pl.multiple_of(start * blk, blk), unroll * blk), lane_group(hh)]
            st = _dot_nt(keys, q_heads[hh])
            return [st[u * blk:(u + 1) * blk, :] for u in range(unroll)]

        def far_chosen(hh):
            return [jnp.where(start + u >= first, sel_s[hh, pl.ds(start + u, 1), :], 0.0) > 0.5
                    for u in range(unroll)]

        late = {}
        queue = []
        for hh in range(HEADS_PER_STEP + MATMUL_LOOKAHEAD):
            if hh < HEADS_PER_STEP:
                queue.append((hh, far_scores(hh)))
            if hh >= MATMUL_LOOKAHEAD:
                ph, blocks = queue.pop(0)
                late[ph] = _streamed_softmax_group(
                    m_s.at[ph], acc_s.at[ph], blocks, vt_blocks(ph, start, unroll),
                    far_chosen(ph), [cfar_ref[ph][:, 0:1]] * unroll)

        @pl.when(jnp.max(functools.reduce(jnp.maximum, late.values())) > 0.5)
        def _():
            for hh in range(HEADS_PER_STEP):
                redo = [jnp.where(ch, late[hh], 0.0) > 0.5 for ch in far_chosen(hh)]
                _online_softmax_group(m_s.at[hh], acc_s.at[hh], far_scores(hh),
                                      vt_blocks(hh, start, unroll), redo,
                                      [cfar_ref[hh][:, 0:1]] * unroll)
        return carry

    lax.fori_loop(0, (qb - 1 + unroll - 1) // unroll, far, 0)
    o_ref[...] = _normalised_heads(acc_s, MOBA_HEAD_DIM).T.astype(o_ref.dtype)


def _moba(qa, ka, vat, kmean, near, cfar, *, batch, seq, unroll):
    n = qa.shape[0]
    blk = MOBA_BLOCK
    nb = seq // blk
    assert nb >= unroll and nb >= 2
    steps = MOBA_HEADS // HEADS_PER_STEP
    width = HEADS_PER_STEP * MOBA_HEAD_DIM
    return pl.pallas_call(
        functools.partial(_moba_kernel, unroll=unroll),
        grid=(batch, steps, nb),
        in_specs=[
            pl.BlockSpec((blk, width), lambda b, p, i: (b * nb + i, p)),
            _per_batch((seq, width), lambda b, p, i: (b, p)),
            _per_batch((None, HEADS_PER_STEP * V_ROWS, seq), lambda b, p, i: (b, p, 0)),
            pl.BlockSpec((None, nb, width), lambda b, p, i: (b, 0, p)),
            _per_batch((HEADS_PER_STEP, 3, blk, blk), lambda b, p, i: (p, 0, 0, 0)),
            pl.BlockSpec((HEADS_PER_STEP, 1, LANES), lambda b, p, i: (p, 0, 0)),
        ],
        out_specs=pl.BlockSpec((blk, width), lambda b, p, i: (b * nb + i, p)),
        out_shape=jax.ShapeDtypeStruct((n, MOBA_WIDTH), BF16),
        scratch_shapes=[
            pltpu.VMEM((HEADS_PER_STEP, 1, blk), F32),
            pltpu.VMEM((HEADS_PER_STEP, V_ROWS, blk), F32),
            pltpu.VMEM((HEADS_PER_STEP, nb, blk), F32),
        ],
        compiler_params=pltpu.CompilerParams(
            dimension_semantics=("arbitrary", "arbitrary", "arbitrary"),
            vmem_limit_bytes=VMEM_LIMIT_BYTES),
        name="moba",
    )(qa, ka, vat, kmean, near, cfar)


def _mla_kernel(q_ref, k_ref, vt_ref, o_ref, m_s, acc_s, *, tq, tk, unroll):
    qi = pl.program_id(2)
    n_diag = tq // tk
    key = lax.broadcasted_iota(jnp.int32, (tk, tq), 0)
    qry = lax.broadcasted_iota(jnp.int32, (tk, tq), 1)
    q_heads = [q_ref[:, hh * LANES:(hh + 1) * LANES] for hh in range(HEADS_PER_STEP)]

    _reset_softmax_state(m_s, acc_s)

    def k_tile(hh, j):
        return k_ref[pl.ds(pl.multiple_of(j * tk, tk), tk), hh * LANES:(hh + 1) * LANES]

    def vt_tiles(hh, j, n):
        return vt_ref[hh * V_ROWS:(hh + 1) * V_ROWS, pl.ds(pl.multiple_of(j * tk, tk), n * tk)]

    def group(first, n, masked):
        scores = [[_dot_nt(k_tile(hh, first + u), q_heads[hh]) for u in range(n)]
                  for hh in range(HEADS_PER_STEP)]
        for hh in range(HEADS_PER_STEP):
            blocks = scores[hh]
            if masked:
                blocks = [jnp.where(key + u * tk <= qry, st, NEG) for u, st in enumerate(blocks)]
            _online_softmax_group(m_s.at[hh], acc_s.at[hh], blocks, vt_tiles(hh, first, n))

    group(qi * n_diag, n_diag, True)

    def visible_scores(hh, g):
        start = pl.multiple_of(g * unroll * tk, tk)
        st = _dot_nt(k_ref[pl.ds(start, unroll * tk), hh * LANES:(hh + 1) * LANES], q_heads[hh])
        return [st[u * tk:(u + 1) * tk, :] for u in range(unroll)]

    def sweep(groups):
        stages = [(i, hh) for i in range(len(groups)) for hh in range(HEADS_PER_STEP)]
        late = {}
        queue = []
        for n in range(len(stages) + MATMUL_LOOKAHEAD):
            if n < len(stages):
                i, hh = stages[n]
                queue.append((stages[n], visible_scores(hh, groups[i])))
            if n >= MATMUL_LOOKAHEAD:
                (i, hh), blocks = queue.pop(0)
                late[(i, hh)] = _streamed_softmax_group(
                    m_s.at[hh], acc_s.at[hh], blocks, vt_tiles(hh, groups[i] * unroll, unroll))

        @pl.when(jnp.max(functools.reduce(jnp.maximum, late.values())) > 0.5)
        def _():
            for i, hh in stages:
                _online_softmax_group(m_s.at[hh], acc_s.at[hh], visible_scores(hh, groups[i]),
                                      vt_tiles(hh, groups[i] * unroll, unroll),
                                      [late[(i, hh)] > 0.5] * unroll)

    def body(g, carry):
        sweep([g])
        return carry

    lax.fori_loop(0, (qi * n_diag) // unroll, body, 0)

    o_ref[...] = _normalised_heads(acc_s, MLA_V_DIM).T.astype(o_ref.dtype)


def _mla(qm, km, vmt, *, batch, seq, tq, tk, unroll):
    n = qm.shape[0]
    nq = seq // tq
    assert tq % tk == 0 and (tq // tk) % unroll == 0
    pairs = MLA_HEADS // HEADS_PER_STEP
    return pl.pallas_call(
        functools.partial(_mla_kernel, tq=tq, tk=tk, unroll=unroll),
        grid=(batch, pairs, nq),
        in_specs=[
            pl.BlockSpec((tq, HEADS_PER_STEP * LANES), lambda b, p, i: (b * nq + i, p)),
            _per_batch((seq, HEADS_PER_STEP * LANES), lambda b, p, i: (b, p)),
            _per_batch((None, HEADS_PER_STEP * V_ROWS, seq), lambda b, p, i: (b, p, 0)),
        ],
        out_specs=pl.BlockSpec((tq, HEADS_PER_STEP * MLA_V_DIM), lambda b, p, i: (b * nq + i, p)),
        out_shape=jax.ShapeDtypeStruct((n, MLA_WIDTH), BF16),
        scratch_shapes=[
            pltpu.VMEM((HEADS_PER_STEP, 1, tq), F32),
            pltpu.VMEM((HEADS_PER_STEP, V_ROWS, tq), F32),
        ],
        compiler_params=pltpu.CompilerParams(
            dimension_semantics=("arbitrary", "arbitrary", "arbitrary"),
            vmem_limit_bytes=VMEM_LIMIT_BYTES),
        name="mla",
    )(qm, km, vmt)


def _back_kernel(x_ref, ya_ref, yb_ref, g_ref, wa_ref, wb_ref, wo_ref, gffn_ref, wup_ref,
                 cw_ref, cb_ref, wdn_ref, gfin_ref, o_ref,
                 carry_ref, act_ref, hn_ref, *, steps_per_seq, d_ff, fc, final):
    tm, d = x_ref.shape
    halo = SUBLANES

    @pl.when(pl.program_id(0) % steps_per_seq == 0)
    def _():
        carry_ref[...] = jnp.zeros_like(carry_ref)

    g = g_ref[...].astype(F32)
    mixed = g[:, :d] * _dot(ya_ref[...], wa_ref[...]) + g[:, d:] * _dot(yb_ref[...], wb_ref[...])
    h1 = x_ref[...] + _dot(mixed.astype(BF16), wo_ref[...])
    hn_ref[...] = _rms(h1, gffn_ref[...]).astype(BF16)

    def up_conv(col0):
        cols = slice(col0, col0 + fc)
        u = _dot(hn_ref[...], wup_ref[:, cols])
        rows = jnp.concatenate([carry_ref[:, cols], u], axis=0)
        carry_ref[:, cols] = u[tm - halo:tm, :]
        w = cw_ref[:, cols]
        y = cb_ref[:, cols]
        for t in range(CONV_WIDTH):
            back = CONV_WIDTH - 1 - t
            shifted = pltpu.roll(rows, back, axis=0) if back else rows
            y = y + w[t:t + 1, :] * shifted[halo:, :]
        return y

    for c in range(d_ff // fc):
        yg = up_conv(c * fc)
        yv = up_conv(d_ff + c * fc)
        act_ref[:, c * fc:(c + 1) * fc] = (yg * _sigmoid(yg) * yv).astype(BF16)

    h2 = h1 + _dot(act_ref[...], wdn_ref[...])
    o_ref[...] = _rms(h2, gfin_ref[...]) if final else h2


def _back(x2, ya, yb, g, wa, wb, wo, gffn, wup, cw, cb, wdn, gfin, *, seq, tm, fc, final):
    n, d = x2.shape
    d_ff = wdn.shape[0]
    row = lambda i: (i, 0)
    return pl.pallas_call(
        functools.partial(_back_kernel, steps_per_seq=seq // tm, d_ff=d_ff, fc=fc, final=final),
        grid=(n // tm,),
        in_specs=[
            pl.BlockSpec((tm, d), row),
            pl.BlockSpec((tm, ya.shape[1]), row),
            pl.BlockSpec((tm, yb.shape[1]), row),
            pl.BlockSpec((tm, g.shape[1]), row),
            _resident(wa.shape), _resident(wb.shape), _resident(wo.shape), _resident(gffn.shape),
            _resident(wup.shape), _resident(cw.shape), _resident(cb.shape), _resident(wdn.shape),
            _resident(gfin.shape),
        ],
        out_specs=pl.BlockSpec((tm, d), row),
        out_shape=jax.ShapeDtypeStruct((n, d), F32),
        scratch_shapes=[
            pltpu.VMEM((SUBLANES, 2 * d_ff), F32),
            pltpu.VMEM((tm, d_ff), BF16),
            pltpu.VMEM((tm, d), BF16),
        ],
        compiler_params=pltpu.CompilerParams(
            dimension_semantics=("arbitrary",), vmem_limit_bytes=VMEM_LIMIT_BYTES),
        name="back",
    )(x2, ya, yb, g, wa, wb, wo, gffn, wup, cw, cb, wdn, gfin)


def _t5_bucket_np(rel):
    n = np.maximum(rel, 0)
    max_exact = REL_BUCKETS // 2
    nf = np.maximum(n, 1).astype(np.float32)
    large = max_exact + (np.log(nf / np.float32(max_exact)) / np.float32(math.log(REL_MAX_DIST / max_exact))
                         * np.float32(REL_BUCKETS - max_exact)).astype(np.int32)
    large = np.minimum(large, REL_BUCKETS - 1)
    return np.where(n < max_exact, n, large)


def _moba_bias_tables(rel_bias):
    blk = MOBA_BLOCK
    period = 2 * blk
    assert REL_MAX_DIST <= blk + 1
    bias_h = rel_bias.T.astype(F32)
    slot = np.arange(period, dtype=np.int32)
    dist = np.where(slot < blk, slot, slot - period)

    def lookup(bucket):
        onehot = jnp.asarray(bucket)[None, :] == jnp.arange(REL_BUCKETS)[:, None]
        return jnp.sum(jnp.where(onehot[None], bias_h[:, :, None], 0.0), axis=1)

    def toeplitz(v):
        tiled = jnp.tile(v, (1, blk))[:, :blk * (period - 1)]
        return tiled.reshape(v.shape[0], blk, period - 1)[:, :, :blk]

    town = toeplitz(jnp.where(jnp.asarray(dist >= 0), lookup(_t5_bucket_np(dist)) * LOG2E, NEG))
    tprev = toeplitz(lookup(_t5_bucket_np(dist + blk)) * LOG2E)
    near = jnp.stack([tprev, town, jnp.zeros_like(town)], axis=1)
    cfar = bias_h[:, int(_t5_bucket_np(np.int32(blk + 1)))] * LOG2E
    cfar = jnp.broadcast_to(cfar[:, None, None], (MOBA_HEADS, 1, LANES))
    return near, cfar


def _pad_heads(w, n_heads, width, padded=LANES):
    r = w.shape[0]
    w = w.reshape(r, n_heads, width)
    out = jnp.zeros((r, n_heads, padded), w.dtype).at[:, :, :width].set(w)
    return out.reshape(r, n_heads * padded)


def _rope_slot(w_rope):
    r = w_rope.shape[0]
    return jnp.zeros((r, LANES), w_rope.dtype).at[:, MLA_NOPE_DIM:MLA_NOPE_DIM + MLA_ROPE_DIM].set(w_rope)


def _front_weights(w_in, w_uq, w_ukv):
    d = w_in.shape[0]
    s0 = MOBA_WIDTH
    s1 = 2 * MOBA_WIDTH
    s2 = 3 * MOBA_WIDTH
    s3 = s2 + MLA_Q_RANK
    s4 = s3 + MLA_KV_RANK
    s5 = s4 + MLA_ROPE_DIM
    w1 = jnp.concatenate([
        w_in[:, :s0], w_in[:, s0:s1],
        w_in[:, s2:s3], w_in[:, s3:s4],
        _rope_slot(w_in[:, s4:s5]),
        w_in[:, s5:],
    ], axis=1).astype(BF16)
    assert w1.shape == (d, _C_END)
    wvt = _pad_heads(w_in[:, s1:s2], MOBA_HEADS, MOBA_HEAD_DIM, V_ROWS).T.astype(BF16)

    wuq = _pad_heads(w_uq, MLA_HEADS, MLA_QK_DIM).astype(BF16)

    r = w_ukv.shape[0]
    ukv = w_ukv.reshape(r, MLA_HEADS, MLA_NOPE_DIM + MLA_V_DIM)
    wk = _pad_heads(ukv[:, :, :MLA_NOPE_DIM].reshape(r, -1), MLA_HEADS, MLA_NOPE_DIM).astype(BF16)
    wkvt = _pad_heads(ukv[:, :, MLA_NOPE_DIM:].reshape(r, -1), MLA_HEADS, MLA_V_DIM, V_ROWS).T.astype(BF16)
    return w1, wvt, wuq, wk, wkvt


def _rope_lane_tables(seq):
    dim = MLA_ROPE_DIM
    inv_freq = ROPE_THETA ** (-jnp.arange(0, dim, 2, dtype=F32) / dim)
    ang = jnp.arange(seq, dtype=F32)[:, None] * inv_freq[None, :]
    cos, sin = jnp.cos(ang), jnp.sin(ang)
    tail = jnp.zeros((seq, LANES - MLA_QK_DIM), F32)
    cosm = jnp.concatenate([jnp.ones((seq, MLA_NOPE_DIM), F32), cos, cos, tail], axis=1)
    sinm = jnp.concatenate([jnp.zeros((seq, MLA_NOPE_DIM), F32), -sin, sin, tail], axis=1)
    return cosm, sinm


class _Tiles:
    def __init__(self, seq):
        self.tm = 512 if seq % 512 == 0 else MOBA_BLOCK
        self.fc = 256
        self.mla_tq = self.tm
        self.mla_tk = 256
        self.mla_unroll = self.mla_tq // self.mla_tk
        self.moba_unroll = 4


def kernel(x, norm_attn_g, w_in, b_gate, q_norm_g, w_uq, kv_norm_g, w_ukv, rel_bias,
           w_branch_moba, w_branch_mla, w_out, norm_ffn_g, w_up, conv_w, conv_b, w_down,
           norm_final_g):
    batch, seq, d = x.shape
    depth = w_in.shape[0]
    assert seq % MOBA_BLOCK == 0
    t = _Tiles(seq)
    d_ff = w_down.shape[1]
    assert d_ff % t.fc == 0
    n = batch * seq

    cosm, sinm = _rope_lane_tables(seq)
    near, cfar = _moba_bias_tables(rel_bias)
    row = lambda v: v.reshape(1, -1).astype(F32)

    h = x.reshape(n, d)
    for l in range(depth):
        w1, wvt, wuq, wk, wkvt = _front_weights(w_in[l], w_uq[l], w_ukv[l])
        qa, ka, kmean, vat, qm, km, vmt, g = _front(
            h, row(norm_attn_g[l]), w1, wvt, row(b_gate[l]), row(q_norm_g[l]), wuq,
            row(kv_norm_g[l]), wk, wkvt, cosm, sinm, seq=seq, tm=t.tm)
        kmean = kmean.reshape(batch, seq // MOBA_BLOCK, MOBA_WIDTH)
        ya = _moba(qa, ka, vat, kmean, near, cfar, batch=batch, seq=seq,
                   unroll=t.moba_unroll)
        yb = _mla(qm, km, vmt, batch=batch, seq=seq, tq=t.mla_tq, tk=t.mla_tk,
                  unroll=t.mla_unroll)
        h = _back(h, ya, yb, g, w_branch_moba[l].astype(BF16), w_branch_mla[l].astype(BF16),
                  w_out[l].astype(BF16), row(norm_ffn_g[l]), w_up[l].astype(BF16),
                  conv_w[l].astype(F32), row(conv_b[l]), w_down[l].astype(BF16),
                  row(norm_final_g), seq=seq, tm=t.tm, fc=t.fc, final=(l == depth - 1))
    return h.reshape(batch, seq, d)
```

```python
import functools
import math

import numpy as np
import jax
import jax.numpy as jnp
from jax import lax
from jax.experimental import pallas as pl
from jax.experimental.pallas import tpu as pltpu

MOBA_HEADS = 8
MOBA_HEAD_DIM = 64
MOBA_BLOCK = 256
MOBA_TOPK = 3
MLA_HEADS = 8
MLA_Q_RANK = 256
MLA_KV_RANK = 128
MLA_NOPE_DIM = 64
MLA_ROPE_DIM = 32
MLA_V_DIM = 64
ROPE_THETA = 10000.0
REL_BUCKETS = 32
REL_MAX_DIST = 128
CONV_WIDTH = 3
N_BRANCH = 2
EPS = 1e-6

MOBA_WIDTH = MOBA_HEADS * MOBA_HEAD_DIM
MLA_QK_DIM = MLA_NOPE_DIM + MLA_ROPE_DIM
MLA_WIDTH = MLA_HEADS * MLA_V_DIM

LANES = 128
SUBLANES = 8
VMEM_LIMIT_BYTES = 56 * 1024 * 1024

NEG = -1e30

F32 = jnp.float32
BF16 = jnp.bfloat16


def _dot(a, b):
    return jnp.dot(a, b, preferred_element_type=F32)


def _dot_nt(a, b):
    return lax.dot_general(a, b, (((1,), (1,)), ((), ())), preferred_element_type=F32)


def _rms(x, g):
    return x * lax.rsqrt(jnp.mean(x * x, axis=-1, keepdims=True) + EPS) * g


def _sigmoid(z):
    return 1.0 / (1.0 + jnp.exp(-z))


def _resident(shape):
    nd = len(shape)
    return pl.BlockSpec(shape, lambda *_: (0,) * nd, pipeline_mode=pl.Buffered(1))


def _per_batch(shape, index_map):
    return pl.BlockSpec(shape, index_map, pipeline_mode=pl.Buffered(1))


_C_QA = 0
_C_KA = _C_QA + MOBA_WIDTH
_C_CQ = _C_KA + MOBA_WIDTH
_C_CKV = _C_CQ + MLA_Q_RANK
_C_KRA = _C_CKV + MLA_KV_RANK
_C_G = _C_KRA + LANES
_C_END = _C_G + N_BRANCH * 1024

BF16_SUBLANES = 16
V_ROWS = MLA_V_DIM + BF16_SUBLANES
assert MOBA_HEAD_DIM == MLA_V_DIM


def _ones_rows(n_rows):
    r = lax.broadcasted_iota(jnp.int32, (n_rows, 1), 0).astype(F32)
    within = r - jnp.floor((r + 0.5) * (1.0 / V_ROWS)) * V_ROWS
    return jnp.where(within == MLA_V_DIM, 1.0, 0.0).astype(F32)


def _front_kernel(x_ref, gattn_ref, w1_ref, wvt_ref, bg_ref, qng_ref, wuq_ref, kvng_ref, wkv_ref,
                  wkvt_ref, cos_ref, sin_ref,
                  qa_ref, ka_ref, kmean_ref, vat_ref, qm_ref, km_ref, vmt_ref, g_ref,
                  *, mla_scale):
    tm = x_ref.shape[0]
    xn = _rms(x_ref[...], gattn_ref[...]).astype(BF16)

    def proj(a, b):
        return _dot(xn, w1_ref[:, a:b])

    qa_ref[...] = (proj(_C_QA, _C_KA) * (MOBA_HEAD_DIM ** -0.5 * LOG2E)).astype(BF16)
    ka = proj(_C_KA, _C_CQ)
    ka_ref[...] = ka.astype(BF16)
    for i in range(tm // MOBA_BLOCK):
        kmean_ref[i] = jnp.mean(ka[i * MOBA_BLOCK:(i + 1) * MOBA_BLOCK], axis=0, keepdims=True)
    ones = _ones_rows(vat_ref.shape[0])
    vat_ref[...] = (_dot_nt(wvt_ref[...], xn) + ones).astype(BF16)

    cosm = cos_ref[...]
    sinm = sin_ref[...]
    lane = lax.broadcasted_iota(jnp.int32, (tm, LANES), 1)
    half = MLA_ROPE_DIM // 2

    def rope(a):
        x2_on_x1 = pltpu.roll(a, LANES - half, axis=1)
        x1_on_x2 = pltpu.roll(a, half, axis=1)
        swapped = jnp.where(lane < MLA_NOPE_DIM + half, x2_on_x1, x1_on_x2)
        return a * cosm + swapped * sinm

    cqn = _rms(proj(_C_CQ, _C_CKV), qng_ref[...]).astype(BF16)
    qq = _dot(cqn, wuq_ref[...])
    for h in range(MLA_HEADS):
        qm_ref[:, h * LANES:(h + 1) * LANES] = (
            rope(qq[:, h * LANES:(h + 1) * LANES]) * mla_scale).astype(BF16)

    ckv_kra = proj(_C_CKV, _C_G)
    ckvn = _rms(ckv_kra[:, :MLA_KV_RANK], kvng_ref[...]).astype(BF16)
    kv = _dot(ckvn, wkv_ref[...])
    kr = rope(ckv_kra[:, MLA_KV_RANK:])
    for h in range(MLA_HEADS):
        km_ref[:, h * LANES:(h + 1) * LANES] = (kv[:, h * LANES:(h + 1) * LANES] + kr).astype(BF16)
    vmt_ref[...] = (_dot_nt(wkvt_ref[...], ckvn) + ones).astype(BF16)

    g_ref[...] = _sigmoid(proj(_C_G, _C_END) + bg_ref[...]).astype(BF16)


def _front(x2, gattn, w1, wvt, bg, qng, wuq, kvng, wkv, wkvt, cosm, sinm, *, seq, tm):
    n, d = x2.shape
    batch = n // seq
    hw = MLA_HEADS * LANES
    nblk = n // MOBA_BLOCK
    steps_per_seq = seq // tm
    row = lambda i: (i, 0)
    tcol = lambda i: (i // steps_per_seq, 0, i % steps_per_seq)
    out_shape = (
        jax.ShapeDtypeStruct((n, MOBA_WIDTH), BF16),
        jax.ShapeDtypeStruct((n, MOBA_WIDTH), BF16),
        jax.ShapeDtypeStruct((nblk, 1, MOBA_WIDTH), F32),
        jax.ShapeDtypeStruct((batch, MOBA_HEADS * V_ROWS, seq), BF16),
        jax.ShapeDtypeStruct((n, hw), BF16),
        jax.ShapeDtypeStruct((n, hw), BF16),
        jax.ShapeDtypeStruct((batch, MLA_HEADS * V_ROWS, seq), BF16),
        jax.ShapeDtypeStruct((n, N_BRANCH * d), BF16),
    )
    in_specs = [
        pl.BlockSpec((tm, d), row),
        _resident(gattn.shape), _resident(w1.shape), _resident(wvt.shape), _resident(bg.shape),
        _resident(qng.shape), _resident(wuq.shape), _resident(kvng.shape), _resident(wkv.shape),
        _resident(wkvt.shape),
        pl.BlockSpec((tm, LANES), lambda i: (i % steps_per_seq, 0)),
        pl.BlockSpec((tm, LANES), lambda i: (i % steps_per_seq, 0)),
    ]
    out_specs = (
        pl.BlockSpec((tm, MOBA_WIDTH), row),
        pl.BlockSpec((tm, MOBA_WIDTH), row),
        pl.BlockSpec((tm // MOBA_BLOCK, 1, MOBA_WIDTH), lambda i: (i, 0, 0)),
        pl.BlockSpec((None, MOBA_HEADS * V_ROWS, tm), tcol),
        pl.BlockSpec((tm, hw), row),
        pl.BlockSpec((tm, hw), row),
        pl.BlockSpec((None, MLA_HEADS * V_ROWS, tm), tcol),
        pl.BlockSpec((tm, N_BRANCH * d), row),
    )
    return pl.pallas_call(
        functools.partial(_front_kernel, mla_scale=MLA_QK_DIM ** -0.5 * LOG2E),
        grid=(n // tm,),
        in_specs=in_specs,
        out_specs=out_specs,
        out_shape=out_shape,
        compiler_params=pltpu.CompilerParams(
            dimension_semantics=("arbitrary",), vmem_limit_bytes=VMEM_LIMIT_BYTES),
        name="front",
    )(x2, gattn, w1, wvt, bg, qng, wuq, kvng, wkv, wkvt, cosm, sinm)


HEADS_PER_STEP = 8
LOG2E = math.log2(math.e)
BIG = 1e30
RESCALE_SLACK = 64.0
MATMUL_LOOKAHEAD = 2


def _online_softmax_group(m_ref, acc_ref, score_blocks, vt_grp, chosen=None, shifts=None):
    n = len(score_blocks)
    chosen = chosen or [None] * n
    shifts = shifts or [None] * n
    m_old = m_ref[...]
    m_new = m_old
    for st, ch, sh in zip(score_blocks, chosen, shifts):
        rm = jnp.max(st, axis=0, keepdims=True)
        if sh is not None:
            rm = rm + sh
        m_new = jnp.maximum(m_new, rm if ch is None else jnp.where(ch, rm, NEG))
    probs = []
    for st, ch, sh in zip(score_blocks, chosen, shifts):
        off = m_new if sh is None else m_new - sh
        if ch is not None:
            off = jnp.where(ch, off, BIG)
        probs.append(jnp.exp2(st - off).astype(BF16))
    ot = _dot(vt_grp, jnp.concatenate(probs, axis=0))
    acc_ref[...] = jnp.exp2(m_old - m_new) * acc_ref[...] + ot
    m_ref[...] = m_new


def _streamed_softmax_group(m_ref, acc_ref, score_blocks, vt_grp, chosen=None, shifts=None):
    n = len(score_blocks)
    chosen = chosen or [None] * n
    shifts = shifts or [None] * n
    m_old = m_ref[...]
    seen = jnp.full(m_old.shape, NEG, F32)
    probs = []
    for st, ch, sh in zip(score_blocks, chosen, shifts):
        off = m_old if sh is None else m_old - sh
        if ch is not None:
            off = jnp.where(ch, off, BIG)
        probs.append(jnp.exp2(st - off).astype(BF16))
        rm = jnp.max(st, axis=0, keepdims=True)
        if sh is not None:
            rm = rm + sh
        seen = jnp.maximum(seen, rm if ch is None else jnp.where(ch, rm, NEG))
    late = seen > m_old + RESCALE_SLACK
    ot = _dot(vt_grp, jnp.concatenate(probs, axis=0))
    acc_ref[...] = acc_ref[...] + jnp.where(late, 0.0, ot)
    return jnp.where(late, 1.0, 0.0)


def _reset_softmax_state(m_s, acc_s):
    m_s[...] = jnp.full(m_s.shape, NEG, F32)
    acc_s[...] = jnp.zeros(acc_s.shape, F32)


def _normalised_heads(acc_s, v_dim):
    rows = []
    for hh in range(HEADS_PER_STEP):
        acc = acc_s[hh]
        rows.append(acc[0:v_dim, :] * (1.0 / acc[v_dim:v_dim + 1, :]))
    return jnp.concatenate(rows, axis=0)


def _moba_kernel(q_ref, k_new_ref, vt_new_ref, kmean_ref, near_ref, cfar_ref, o_ref,
                 m_s, acc_s, sel_s, k_ref, vt_ref, *, unroll):
    blk = MOBA_BLOCK
    qb = pl.program_id(2)
    nb = kmean_ref.shape[0]
    lane = lax.broadcasted_iota(jnp.int32, (blk, LANES), 1)
    blk_id = lax.broadcasted_iota(jnp.int32, (nb, blk), 0).astype(F32)
    qbf = qb.astype(F32)
    has_prev = qb >= 1
    _reset_softmax_state(m_s, acc_s)

    @pl.when(qb == 0)
    def _():
        k_ref[...] = jnp.zeros_like(k_ref)
        vt_ref[...] = jnp.zeros_like(vt_ref)

    own = pl.ds(pl.multiple_of(qb * blk, blk), blk)
    k_ref[own, :] = k_new_ref[...]
    vt_ref[:, own] = vt_new_ref[...]

    def lane_group(hh):
        return slice((hh // 2) * LANES, (hh // 2 + 1) * LANES)

    def k_block(hh, j):
        return k_ref[pl.ds(pl.multiple_of(j * blk, blk), blk), lane_group(hh)]

    def vt_blocks(hh, j, n):
        return vt_ref[hh * V_ROWS:(hh + 1) * V_ROWS, pl.ds(pl.multiple_of(j * blk, blk), n * blk)]

    q_heads, prev_chosen = [], []
    for hh in range(HEADS_PER_STEP):
        in_head = (lane >= (hh % 2) * MOBA_HEAD_DIM) & (lane < (hh % 2 + 1) * MOBA_HEAD_DIM)
        qh = jnp.where(in_head, q_ref[:, lane_group(hh)].astype(F32), 0.0).astype(BF16)
        q_heads.append(qh)

        kmean = kmean_ref[:, lane_group(hh)].astype(BF16)
        gate = jnp.where(blk_id < qbf, _dot_nt(kmean, qh), -jnp.inf)
        sel = jnp.zeros((nb, blk), F32)
        for _ in range(min(MOBA_TOPK, nb)):
            best = jnp.max(gate, axis=0, keepdims=True)
            first = jnp.min(jnp.where(gate == best, blk_id, float(nb)), axis=0, keepdims=True)
            pick = blk_id == first
            sel = jnp.where(pick, 1.0, sel)
            gate = jnp.where(pick, -jnp.inf, gate)
        prev_chosen.append(jnp.max(jnp.where(blk_id == qbf - 1.0, sel, 0.0), axis=0, keepdims=True))
        sel_s[hh] = jnp.where(blk_id < qbf - 1.0, sel, 0.0)

    lo = jnp.maximum(qb - 1, 0)
    tab = jnp.where(has_prev, 0, 1)
    scores = [[_dot_nt(k_block(hh, lo + u), q_heads[hh]) + near_ref[hh, tab + u] for u in range(2)]
              for hh in range(HEADS_PER_STEP)]
    for hh in range(HEADS_PER_STEP):
        chosen = [jnp.where(has_prev, prev_chosen[hh], 1.0) > 0.5,
                  jnp.where(has_prev, jnp.ones((1, blk), F32), 0.0) > 0.5]
        _online_softmax_group(m_s.at[hh], acc_s.at[hh], scores[hh], vt_blocks(hh, lo, 2), chosen)

    def far(g, carry):
        first = g * unroll
        start = jnp.minimum(first, nb - unroll)

        def far_scores(hh):
            keys = k_ref[pl.ds(pl.multiple_of(start * blk, blk), unroll * blk), lane_group(hh)]
            st = _dot_nt(keys, q_heads[hh])
            return [st[u * blk:(u + 1) * blk, :] for u in range(unroll)]

        def far_chosen(hh):
            return [jnp.where(start + u >= first, sel_s[hh, pl.ds(start + u, 1), :], 0.0) > 0.5
                    for u in range(unroll)]

        late = {}
        queue = []
        for hh in range(HEADS_PER_STEP + MATMUL_LOOKAHEAD):
            if hh < HEADS_PER_STEP:
                queue.append((hh, far_scores(hh)))
            if hh >= MATMUL_LOOKAHEAD:
                ph, blocks = queue.pop(0)
                late[ph] = _streamed_softmax_group(
                    m_s.at[ph], acc_s.at[ph], blocks, vt_blocks(ph, start, unroll),
                    far_chosen(ph), [cfar_ref[ph][:, 0:1]] * unroll)

        @pl.when(jnp.max(functools.reduce(jnp.maximum, late.values())) > 0.5)
        def _():
            for hh in range(HEADS_PER_STEP):
                redo = [jnp.where(ch, late[hh], 0.0) > 0.5 for ch in far_chosen(hh)]
                _online_softmax_group(m_s.at[hh], acc_s.at[hh], far_scores(hh),
                                      vt_blocks(hh, start, unroll), redo,
                                      [cfar_ref[hh][:, 0:1]] * unroll)
        return carry

    lax.fori_loop(0, (qb - 1 + unroll - 1) // unroll, far, 0)
    o_ref[...] = _normalised_heads(acc_s, MOBA_HEAD_DIM).T.astype(o_ref.dtype)


def _moba(qa, ka, vat, kmean, near, cfar, *, batch, seq, unroll):
    n = qa.shape[0]
    blk = MOBA_BLOCK
    nb = seq // blk
    assert nb >= unroll and nb >= 2
    steps = MOBA_HEADS // HEADS_PER_STEP
    width = HEADS_PER_STEP * MOBA_HEAD_DIM
    return pl.pallas_call(
        functools.partial(_moba_kernel, unroll=unroll),
        grid=(batch, steps, nb),
        in_specs=[
            pl.BlockSpec((blk, width), lambda b, p, i: (b * nb + i, p)),
            pl.BlockSpec((blk, width), lambda b, p, i: (b * nb + i, p)),
            pl.BlockSpec((None, HEADS_PER_STEP * V_ROWS, blk), lambda b, p, i: (b, p, i)),
            pl.BlockSpec((None, nb, width), lambda b, p, i: (b, 0, p)),
            _per_batch((HEADS_PER_STEP, 3, blk, blk), lambda b, p, i: (p, 0, 0, 0)),
            pl.BlockSpec((HEADS_PER_STEP, 1, LANES), lambda b, p, i: (p, 0, 0)),
        ],
        out_specs=pl.BlockSpec((blk, width), lambda b, p, i: (b * nb + i, p)),
        out_shape=jax.ShapeDtypeStruct((n, MOBA_WIDTH), BF16),
        scratch_shapes=[
            pltpu.VMEM((HEADS_PER_STEP, 1, blk), F32),
            pltpu.VMEM((HEADS_PER_STEP, V_ROWS, blk), F32),
            pltpu.VMEM((HEADS_PER_STEP, nb, blk), F32),
            pltpu.VMEM((seq, width), BF16),
            pltpu.VMEM((HEADS_PER_STEP * V_ROWS, seq), BF16),
        ],
        compiler_params=pltpu.CompilerParams(
            dimension_semantics=("arbitrary", "arbitrary", "arbitrary"),
            vmem_limit_bytes=VMEM_LIMIT_BYTES),
        name="moba",
    )(qa, ka, vat, kmean, near, cfar)


def _mla_kernel(q_ref, k_new_ref, vt_new_ref, o_ref, m_s, acc_s, k_ref, vt_ref, *, tq, tk, unroll):
    qi = pl.program_id(2)
    n_diag = tq // tk
    key = lax.broadcasted_iota(jnp.int32, (tk, tq), 0)
    qry = lax.broadcasted_iota(jnp.int32, (tk, tq), 1)
    q_heads = [q_ref[:, hh * LANES:(hh + 1) * LANES] for hh in range(HEADS_PER_STEP)]

    own = pl.ds(pl.multiple_of(qi * tq, tq), tq)
    k_ref[own, :] = k_new_ref[...]
    vt_ref[:, own] = vt_new_ref[...]

    _reset_softmax_state(m_s, acc_s)

    def k_tile(hh, j):
        return k_ref[pl.ds(pl.multiple_of(j * tk, tk), tk), hh * LANES:(hh + 1) * LANES]

    def vt_tiles(hh, j, n):
        return vt_ref[hh * V_ROWS:(hh + 1) * V_ROWS, pl.ds(pl.multiple_of(j * tk, tk), n * tk)]

    def group(first, n, masked):
        scores = [[_dot_nt(k_tile(hh, first + u), q_heads[hh]) for u in range(n)]
                  for hh in range(HEADS_PER_STEP)]
        for hh in range(HEADS_PER_STEP):
            blocks = scores[hh]
            if masked:
                blocks = [jnp.where(key + u * tk <= qry, st, NEG) for u, st in enumerate(blocks)]
            _online_softmax_group(m_s.at[hh], acc_s.at[hh], blocks, vt_tiles(hh, first, n))

    group(qi * n_diag, n_diag, True)

    def visible_scores(hh, g):
        start = pl.multiple_of(g * unroll * tk, tk)
        st = _dot_nt(k_ref[pl.ds(start, unroll * tk), hh * LANES:(hh + 1) * LANES], q_heads[hh])
        return [st[u * tk:(u + 1) * tk, :] for u in range(unroll)]

    def sweep(groups):
        stages = [(i, hh) for i in range(len(groups)) for hh in range(HEADS_PER_STEP)]
        late = {}
        queue = []
        for n in range(len(stages) + MATMUL_LOOKAHEAD):
            if n < len(stages):
                i, hh = stages[n]
                queue.append((stages[n], visible_scores(hh, groups[i])))
            if n >= MATMUL_LOOKAHEAD:
                (i, hh), blocks = queue.pop(0)
                late[(i, hh)] = _streamed_softmax_group(
                    m_s.at[hh], acc_s.at[hh], blocks, vt_tiles(hh, groups[i] * unroll, unroll))

        @pl.when(jnp.max(functools.reduce(jnp.maximum, late.values())) > 0.5)
        def _():
            for i, hh in stages:
                _online_softmax_group(m_s.at[hh], acc_s.at[hh], visible_scores(hh, groups[i]),
                                      vt_tiles(hh, groups[i] * unroll, unroll),
                                      [late[(i, hh)] > 0.5] * unroll)

    def body(g, carry):
        sweep([g])
        return carry

    lax.fori_loop(0, (qi * n_diag) // unroll, body, 0)

    o_ref[...] = _normalised_heads(acc_s, MLA_V_DIM).T.astype(o_ref.dtype)


def _mla(qm, km, vmt, *, batch, seq, tq, tk, unroll):
    n = qm.shape[0]
    nq = seq // tq
    assert tq % tk == 0 and (tq // tk) % unroll == 0
    pairs = MLA_HEADS // HEADS_PER_STEP
    return pl.pallas_call(
        functools.partial(_mla_kernel, tq=tq, tk=tk, unroll=unroll),
        grid=(batch, pairs, nq),
        in_specs=[
            pl.BlockSpec((tq, HEADS_PER_STEP * LANES), lambda b, p, i: (b * nq + i, p)),
            pl.BlockSpec((tq, HEADS_PER_STEP * LANES), lambda b, p, i: (b * nq + i, p)),
            pl.BlockSpec((None, HEADS_PER_STEP * V_ROWS, tq), lambda b, p, i: (b, p, i)),
        ],
        out_specs=pl.BlockSpec((tq, HEADS_PER_STEP * MLA_V_DIM), lambda b, p, i: (b * nq + i, p)),
        out_shape=jax.ShapeDtypeStruct((n, MLA_WIDTH), BF16),
        scratch_shapes=[
            pltpu.VMEM((HEADS_PER_STEP, 1, tq), F32),
            pltpu.VMEM((HEADS_PER_STEP, V_ROWS, tq), F32),
            pltpu.VMEM((seq, HEADS_PER_STEP * LANES), BF16),
            pltpu.VMEM((HEADS_PER_STEP * V_ROWS, seq), BF16),
        ],
        compiler_params=pltpu.CompilerParams(
            dimension_semantics=("arbitrary", "arbitrary", "arbitrary"),
            vmem_limit_bytes=VMEM_LIMIT_BYTES),
        name="mla",
    )(qm, km, vmt)


def _back_kernel(x_ref, ya_ref, yb_ref, g_ref, wa_ref, wb_ref, wo_ref, gffn_ref, wup_ref,
                 cw_ref, cb_ref, wdn_ref, gfin_ref, o_ref,
                 carry_ref, act_ref, hn_ref, *, steps_per_seq, d_ff, fc, final):
    tm, d = x_ref.shape
    halo = SUBLANES

    @pl.when(pl.program_id(0) % steps_per_seq == 0)
    def _():
        carry_ref[...] = jnp.zeros_like(carry_ref)

    g = g_ref[...].astype(F32)
    mixed = g[:, :d] * _dot(ya_ref[...], wa_ref[...]) + g[:, d:] * _dot(yb_ref[...], wb_ref[...])
    h1 = x_ref[...] + _dot(mixed.astype(BF16), wo_ref[...])
    hn_ref[...] = _rms(h1, gffn_ref[...]).astype(BF16)

    def up_conv(col0):
        cols = slice(col0, col0 + fc)
        u = _dot(hn_ref[...], wup_ref[:, cols])
        rows = jnp.concatenate([carry_ref[:, cols], u], axis=0)
        carry_ref[:, cols] = u[tm - halo:tm, :]
        w = cw_ref[:, cols]
        y = cb_ref[:, cols]
        for t in range(CONV_WIDTH):
            back = CONV_WIDTH - 1 - t
            shifted = pltpu.roll(rows, back, axis=0) if back else rows
            y = y + w[t:t + 1, :] * shifted[halo:, :]
        return y

    for c in range(d_ff // fc):
        yg = up_conv(c * fc)
        yv = up_conv(d_ff + c * fc)
        act_ref[:, c * fc:(c + 1) * fc] = (yg * _sigmoid(yg) * yv).astype(BF16)

    h2 = h1 + _dot(act_ref[...], wdn_ref[...])
    o_ref[...] = _rms(h2, gfin_ref[...]) if final else h2


def _back(x2, ya, yb, g, wa, wb, wo, gffn, wup, cw, cb, wdn, gfin, *, seq, tm, fc, final):
    n, d = x2.shape
    d_ff = wdn.shape[0]
    row = lambda i: (i, 0)
    return pl.pallas_call(
        functools.partial(_back_kernel, steps_per_seq=seq // tm, d_ff=d_ff, fc=fc, final=final),
        grid=(n // tm,),
        in_specs=[
            pl.BlockSpec((tm, d), row),
            pl.BlockSpec((tm, ya.shape[1]), row),
            pl.BlockSpec((tm, yb.shape[1]), row),
            pl.BlockSpec((tm, g.shape[1]), row),
            _resident(wa.shape), _resident(wb.shape), _resident(wo.shape), _resident(gffn.shape),
            _resident(wup.shape), _resident(cw.shape), _resident(cb.shape), _resident(wdn.shape),
            _resident(gfin.shape),
        ],
        out_specs=pl.BlockSpec((tm, d), row),
        out_shape=jax.ShapeDtypeStruct((n, d), F32),
        scratch_shapes=[
            pltpu.VMEM((SUBLANES, 2 * d_ff), F32),
            pltpu.VMEM((tm, d_ff), BF16),
            pltpu.VMEM((tm, d), BF16),
        ],
        compiler_params=pltpu.CompilerParams(
            dimension_semantics=("arbitrary",), vmem_limit_bytes=VMEM_LIMIT_BYTES),
        name="back",
    )(x2, ya, yb, g, wa, wb, wo, gffn, wup, cw, cb, wdn, gfin)


def _t5_bucket_np(rel):
    n = np.maximum(rel, 0)
    max_exact = REL_BUCKETS // 2
    nf = np.maximum(n, 1).astype(np.float32)
    large = max_exact + (np.log(nf / np.float32(max_exact)) / np.float32(math.log(REL_MAX_DIST / max_exact))
                         * np.float32(REL_BUCKETS - max_exact)).astype(np.int32)
    large = np.minimum(large, REL_BUCKETS - 1)
    return np.where(n < max_exact, n, large)


def _moba_bias_tables(rel_bias):
    blk = MOBA_BLOCK
    period = 2 * blk
    assert REL_MAX_DIST <= blk + 1
    bias_h = rel_bias.T.astype(F32)
    slot = np.arange(period, dtype=np.int32)
    dist = np.where(slot < blk, slot, slot - period)

    def lookup(bucket):
        onehot = jnp.asarray(bucket)[None, :] == jnp.arange(REL_BUCKETS)[:, None]
        return jnp.sum(jnp.where(onehot[None], bias_h[:, :, None], 0.0), axis=1)

    def toeplitz(v):
        tiled = jnp.tile(v, (1, blk))[:, :blk * (period - 1)]
        return tiled.reshape(v.shape[0], blk, period - 1)[:, :, :blk]

    town = toeplitz(jnp.where(jnp.asarray(dist >= 0), lookup(_t5_bucket_np(dist)) * LOG2E, NEG))
    tprev = toeplitz(lookup(_t5_bucket_np(dist + blk)) * LOG2E)
    near = jnp.stack([tprev, town, jnp.zeros_like(town)], axis=1)
    cfar = bias_h[:, int(_t5_bucket_np(np.int32(blk + 1)))] * LOG2E
    cfar = jnp.broadcast_to(cfar[:, None, None], (MOBA_HEADS, 1, LANES))
    return near, cfar


def _pad_heads(w, n_heads, width, padded=LANES):
    r = w.shape[0]
    w = w.reshape(r, n_heads, width)
    out = jnp.zeros((r, n_heads, padded), w.dtype).at[:, :, :width].set(w)
    return out.reshape(r, n_heads * padded)


def _rope_slot(w_rope):
    r = w_rope.shape[0]
    return jnp.zeros((r, LANES), w_rope.dtype).at[:, MLA_NOPE_DIM:MLA_NOPE_DIM + MLA_ROPE_DIM].set(w_rope)


def _front_weights(w_in, w_uq, w_ukv):
    d = w_in.shape[0]
    s0 = MOBA_WIDTH
    s1 = 2 * MOBA_WIDTH
    s2 = 3 * MOBA_WIDTH
    s3 = s2 + MLA_Q_RANK
    s4 = s3 + MLA_KV_RANK
    s5 = s4 + MLA_ROPE_DIM
    w1 = jnp.concatenate([
        w_in[:, :s0], w_in[:, s0:s1],
        w_in[:, s2:s3], w_in[:, s3:s4],
        _rope_slot(w_in[:, s4:s5]),
        w_in[:, s5:],
    ], axis=1).astype(BF16)
    assert w1.shape == (d, _C_END)
    wvt = _pad_heads(w_in[:, s1:s2], MOBA_HEADS, MOBA_HEAD_DIM, V_ROWS).T.astype(BF16)

    wuq = _pad_heads(w_uq, MLA_HEADS, MLA_QK_DIM).astype(BF16)

    r = w_ukv.shape[0]
    ukv = w_ukv.reshape(r, MLA_HEADS, MLA_NOPE_DIM + MLA_V_DIM)
    wk = _pad_heads(ukv[:, :, :MLA_NOPE_DIM].reshape(r, -1), MLA_HEADS, MLA_NOPE_DIM).astype(BF16)
    wkvt = _pad_heads(ukv[:, :, MLA_NOPE_DIM:].reshape(r, -1), MLA_HEADS, MLA_V_DIM, V_ROWS).T.astype(BF16)
    return w1, wvt, wuq, wk, wkvt


def _rope_lane_tables(seq):
    dim = MLA_ROPE_DIM
    inv_freq = ROPE_THETA ** (-jnp.arange(0, dim, 2, dtype=F32) / dim)
    ang = jnp.arange(seq, dtype=F32)[:, None] * inv_freq[None, :]
    cos, sin = jnp.cos(ang), jnp.sin(ang)
    tail = jnp.zeros((seq, LANES - MLA_QK_DIM), F32)
    cosm = jnp.concatenate([jnp.ones((seq, MLA_NOPE_DIM), F32), cos, cos, tail], axis=1)
    sinm = jnp.concatenate([jnp.zeros((seq, MLA_NOPE_DIM), F32), -sin, sin, tail], axis=1)
    return cosm, sinm


class _Tiles:
    def __init__(self, seq):
        self.tm = 512 if seq % 512 == 0 else MOBA_BLOCK
        self.fc = 256
        self.mla_tq = self.tm
        self.mla_tk = 256
        self.mla_unroll = self.mla_tq // self.mla_tk
        self.moba_unroll = 4


def kernel(x, norm_attn_g, w_in, b_gate, q_norm_g, w_uq, kv_norm_g, w_ukv, rel_bias,
           w_branch_moba, w_branch_mla, w_out, norm_ffn_g, w_up, conv_w, conv_b, w_down,
           norm_final_g):
    batch, seq, d = x.shape
    depth = w_in.shape[0]
    assert seq % MOBA_BLOCK == 0
    t = _Tiles(seq)
    d_ff = w_down.shape[1]
    assert d_ff % t.fc == 0
    n = batch * seq

    cosm, sinm = _rope_lane_tables(seq)
    near, cfar = _moba_bias_tables(rel_bias)
    row = lambda v: v.reshape(1, -1).astype(F32)

    h = x.reshape(n, d)
    for l in range(depth):
        w1, wvt, wuq, wk, wkvt = _front_weights(w_in[l], w_uq[l], w_ukv[l])
        qa, ka, kmean, vat, qm, km, vmt, g = _front(
            h, row(norm_attn_g[l]), w1, wvt, row(b_gate[l]), row(q_norm_g[l]), wuq,
            row(kv_norm_g[l]), wk, wkvt, cosm, sinm, seq=seq, tm=t.tm)
        kmean = kmean.reshape(batch, seq // MOBA_BLOCK, MOBA_WIDTH)
        ya = _moba(qa, ka, vat, kmean, near, cfar, batch=batch, seq=seq,
                   unroll=t.moba_unroll)
        yb = _mla(qm, km, vmt, batch=batch, seq=seq, tq=t.mla_tq, tk=t.mla_tk,
                  unroll=t.mla_unroll)
        h = _back(h, ya, yb, g, w_branch_moba[l].astype(BF16), w_branch_mla[l].astype(BF16),
                  w_out[l].astype(BF16), row(norm_ffn_g[l]), w_up[l].astype(BF16),
                  conv_w[l].astype(F32), row(conv_b[l]), w_down[l].astype(BF16),
                  row(norm_final_g), seq=seq, tm=t.tm, fc=t.fc, final=(l == depth - 1))
    return h.reshape(batch, seq, d)
```

```python
import functools
import math

import numpy as np
import jax
import jax.numpy as jnp
from jax import lax
from jax.experimental import pallas as pl
from jax.experimental.pallas import tpu as pltpu

MOBA_HEADS = 8
MOBA_HEAD_DIM = 64
MOBA_BLOCK = 256
MOBA_TOPK = 3
MLA_HEADS = 8
MLA_Q_RANK = 256
MLA_KV_RANK = 128
MLA_NOPE_DIM = 64
MLA_ROPE_DIM = 32
MLA_V_DIM = 64
ROPE_THETA = 10000.0
REL_BUCKETS = 32
REL_MAX_DIST = 128
CONV_WIDTH = 3
N_BRANCH = 2
EPS = 1e-6

MOBA_WIDTH = MOBA_HEADS * MOBA_HEAD_DIM
MLA_QK_DIM = MLA_NOPE_DIM + MLA_ROPE_DIM
MLA_WIDTH = MLA_HEADS * MLA_V_DIM

LANES = 128
SUBLANES = 8
VMEM_LIMIT_BYTES = 56 * 1024 * 1024

NEG = -1e30

F32 = jnp.float32
BF16 = jnp.bfloat16


def _dot(a, b):
    return jnp.dot(a, b, preferred_element_type=F32)


def _dot_nt(a, b):
    return lax.dot_general(a, b, (((1,), (1,)), ((), ())), preferred_element_type=F32)


def _rms(x, g):
    return x * lax.rsqrt(jnp.mean(x * x, axis=-1, keepdims=True) + EPS) * g


def _sigmoid(z):
    return 1.0 / (1.0 + jnp.exp(-z))


def _resident(shape):
    nd = len(shape)
    return pl.BlockSpec(shape, lambda *_: (0,) * nd, pipeline_mode=pl.Buffered(1))


def _per_batch(shape, index_map):
    return pl.BlockSpec(shape, index_map, pipeline_mode=pl.Buffered(1))


_C_QA = 0
_C_KA = _C_QA + MOBA_WIDTH
_C_CQ = _C_KA + MOBA_WIDTH
_C_CKV = _C_CQ + MLA_Q_RANK
_C_KRA = _C_CKV + MLA_KV_RANK
_C_G = _C_KRA + LANES
_C_END = _C_G + N_BRANCH * 1024

BF16_SUBLANES = 16
V_ROWS = MLA_V_DIM + BF16_SUBLANES
assert MOBA_HEAD_DIM == MLA_V_DIM


def _ones_rows(n_rows):
    r = lax.broadcasted_iota(jnp.int32, (n_rows, 1), 0).astype(F32)
    within = r - jnp.floor((r + 0.5) * (1.0 / V_ROWS)) * V_ROWS
    return jnp.where(within == MLA_V_DIM, 1.0, 0.0).astype(F32)


def _front_kernel(x_ref, gattn_ref, w1_ref, wvt_ref, bg_ref, qng_ref, wuq_ref, kvng_ref, wkv_ref,
                  wkvt_ref, cos_ref, sin_ref,
                  qa_ref, ka_ref, kmean_ref, vat_ref, qm_ref, km_ref, vmt_ref, g_ref,
                  *, mla_scale):
    tm = x_ref.shape[0]
    xn = _rms(x_ref[...], gattn_ref[...]).astype(BF16)

    def proj(a, b):
        return _dot(xn, w1_ref[:, a:b])

    qa_ref[...] = (proj(_C_QA, _C_KA) * (MOBA_HEAD_DIM ** -0.5 * LOG2E)).astype(BF16)
    ka = proj(_C_KA, _C_CQ)
    ka_ref[...] = ka.astype(BF16)
    for i in range(tm // MOBA_BLOCK):
        kmean_ref[i] = jnp.mean(ka[i * MOBA_BLOCK:(i + 1) * MOBA_BLOCK], axis=0, keepdims=True)
    ones = _ones_rows(vat_ref.shape[0])
    vat_ref[...] = (_dot_nt(wvt_ref[...], xn) + ones).astype(BF16)

    cosm = cos_ref[...]
    sinm = sin_ref[...]
    lane = lax.broadcasted_iota(jnp.int32, (tm, LANES), 1)
    half = MLA_ROPE_DIM // 2

    def rope(a):
        x2_on_x1 = pltpu.roll(a, LANES - half, axis=1)
        x1_on_x2 = pltpu.roll(a, half, axis=1)
        swapped = jnp.where(lane < MLA_NOPE_DIM + half, x2_on_x1, x1_on_x2)
        return a * cosm + swapped * sinm

    cqn = _rms(proj(_C_CQ, _C_CKV), qng_ref[...]).astype(BF16)
    qq = _dot(cqn, wuq_ref[...])
    for h in range(MLA_HEADS):
        qm_ref[:, h * LANES:(h + 1) * LANES] = (
            rope(qq[:, h * LANES:(h + 1) * LANES]) * mla_scale).astype(BF16)

    ckv_kra = proj(_C_CKV, _C_G)
    ckvn = _rms(ckv_kra[:, :MLA_KV_RANK], kvng_ref[...]).astype(BF16)
    kv = _dot(ckvn, wkv_ref[...])
    kr = rope(ckv_kra[:, MLA_KV_RANK:])
    for h in range(MLA_HEADS):
        km_ref[:, h * LANES:(h + 1) * LANES] = (kv[:, h * LANES:(h + 1) * LANES] + kr).astype(BF16)
    vmt_ref[...] = (_dot_nt(wkvt_ref[...], ckvn) + ones).astype(BF16)

    g_ref[...] = _sigmoid(proj(_C_G, _C_END) + bg_ref[...]).astype(BF16)


def _front(x2, gattn, w1, wvt, bg, qng, wuq, kvng, wkv, wkvt, cosm, sinm, *, seq, tm):
    n, d = x2.shape
    batch = n // seq
    hw = MLA_HEADS * LANES
    nblk = n // MOBA_BLOCK
    steps_per_seq = seq // tm
    row = lambda i: (i, 0)
    tcol = lambda i: (i // steps_per_seq, 0, i % steps_per_seq)
    out_shape = (
        jax.ShapeDtypeStruct((n, MOBA_WIDTH), BF16),
        jax.ShapeDtypeStruct((n, MOBA_WIDTH), BF16),
        jax.ShapeDtypeStruct((nblk, 1, MOBA_WIDTH), F32),
        jax.ShapeDtypeStruct((batch, MOBA_HEADS * V_ROWS, seq), BF16),
        jax.ShapeDtypeStruct((n, hw), BF16),
        jax.ShapeDtypeStruct((n, hw), BF16),
        jax.ShapeDtypeStruct((batch, MLA_HEADS * V_ROWS, seq), BF16),
        jax.ShapeDtypeStruct((n, N_BRANCH * d), BF16),
    )
    in_specs = [
        pl.BlockSpec((tm, d), row),
        _resident(gattn.shape), _resident(w1.shape), _resident(wvt.shape), _resident(bg.shape),
        _resident(qng.shape), _resident(wuq.shape), _resident(kvng.shape), _resident(wkv.shape),
        _resident(wkvt.shape),
        pl.BlockSpec((tm, LANES), lambda i: (i % steps_per_seq, 0)),
        pl.BlockSpec((tm, LANES), lambda i: (i % steps_per_seq, 0)),
    ]
    out_specs = (
        pl.BlockSpec((tm, MOBA_WIDTH), row),
        pl.BlockSpec((tm, MOBA_WIDTH), row),
        pl.BlockSpec((tm // MOBA_BLOCK, 1, MOBA_WIDTH), lambda i: (i, 0, 0)),
        pl.BlockSpec((None, MOBA_HEADS * V_ROWS, tm), tcol),
        pl.BlockSpec((tm, hw), row),
        pl.BlockSpec((tm, hw), row),
        pl.BlockSpec((None, MLA_HEADS * V_ROWS, tm), tcol),
        pl.BlockSpec((tm, N_BRANCH * d), row),
    )
    return pl.pallas_call(
        functools.partial(_front_kernel, mla_scale=MLA_QK_DIM ** -0.5 * LOG2E),
        grid=(n // tm,),
        in_specs=in_specs,
        out_specs=out_specs,
        out_shape=out_shape,
        compiler_params=pltpu.CompilerParams(
            dimension_semantics=("arbitrary",), vmem_limit_bytes=VMEM_LIMIT_BYTES),
        name="front",
    )(x2, gattn, w1, wvt, bg, qng, wuq, kvng, wkv, wkvt, cosm, sinm)


HEADS_PER_STEP = 8
LOG2E = math.log2(math.e)
BIG = 1e30
RESCALE_SLACK = 64.0
MATMUL_LOOKAHEAD = 2


def _online_softmax_group(m_ref, acc_ref, score_blocks, vt_grp, chosen=None, shifts=None):
    n = len(score_blocks)
    chosen = chosen or [None] * n
    shifts = shifts or [None] * n
    m_old = m_ref[...]
    m_new = m_old
    for st, ch, sh in zip(score_blocks, chosen, shifts):
        rm = jnp.max(st, axis=0, keepdims=True)
        if sh is not None:
            rm = rm + sh
        m_new = jnp.maximum(m_new, rm if ch is None else jnp.where(ch, rm, NEG))
    probs = []
    for st, ch, sh in zip(score_blocks, chosen, shifts):
        off = m_new if sh is None else m_new - sh
        if ch is not None:
            off = jnp.where(ch, off, BIG)
        probs.append(jnp.exp2(st - off).astype(BF16))
    ot = _dot(vt_grp, jnp.concatenate(probs, axis=0))
    acc_ref[...] = jnp.exp2(m_old - m_new) * acc_ref[...] + ot
    m_ref[...] = m_new


def _streamed_softmax_group(m_ref, acc_ref, late_ref, score_blocks, vt_grp, chosen=None, shifts=None):
    n = len(score_blocks)
    chosen = chosen or [None] * n
    shifts = shifts or [None] * n
    m_old = m_ref[...]
    seen = jnp.full(m_old.shape, NEG, F32)
    probs = []
    for st, ch, sh in zip(score_blocks, chosen, shifts):
        off = m_old if sh is None else m_old - sh
        if ch is not None:
            off = jnp.where(ch, off, BIG)
        probs.append(jnp.exp2(st - off).astype(BF16))
        rm = jnp.max(st, axis=0, keepdims=True)
        if sh is not None:
            rm = rm + sh
        seen = jnp.maximum(seen, rm if ch is None else jnp.where(ch, rm, NEG))
    acc_ref[...] = acc_ref[...] + _dot(vt_grp, jnp.concatenate(probs, axis=0))
    late_ref[...] = jnp.where(seen > m_old + RESCALE_SLACK, 1.0, late_ref[...])


def _reset_softmax_state(m_s, acc_s):
    m_s[...] = jnp.full(m_s.shape, NEG, F32)
    acc_s[...] = jnp.zeros(acc_s.shape, F32)


def _normalised_heads(acc_s, v_dim):
    rows = []
    for hh in range(HEADS_PER_STEP):
        acc = acc_s[hh]
        rows.append(acc[0:v_dim, :] * (1.0 / acc[v_dim:v_dim + 1, :]))
    return jnp.concatenate(rows, axis=0)


def _moba_kernel(q_ref, k_new_ref, vt_new_ref, kmean_ref, near_ref, cfar_ref, o_ref,
                 m_s, acc_s, late_s, sel_s, k_ref, vt_ref, *, unroll):
    blk = MOBA_BLOCK
    qb = pl.program_id(2)
    nb = kmean_ref.shape[0]
    lane = lax.broadcasted_iota(jnp.int32, (blk, LANES), 1)
    blk_id = lax.broadcasted_iota(jnp.int32, (nb, blk), 0).astype(F32)
    qbf = qb.astype(F32)
    has_prev = qb >= 1
    _reset_softmax_state(m_s, acc_s)

    @pl.when(qb == 0)
    def _():
        k_ref[...] = jnp.zeros_like(k_ref)
        vt_ref[...] = jnp.zeros_like(vt_ref)

    own = pl.ds(pl.multiple_of(qb * blk, blk), blk)
    k_ref[own, :] = k_new_ref[...]
    vt_ref[:, own] = vt_new_ref[...]

    def lane_group(hh):
        return slice((hh // 2) * LANES, (hh // 2 + 1) * LANES)

    def k_block(hh, j):
        return k_ref[pl.ds(pl.multiple_of(j * blk, blk), blk), lane_group(hh)]

    def vt_blocks(hh, j, n):
        return vt_ref[hh * V_ROWS:(hh + 1) * V_ROWS, pl.ds(pl.multiple_of(j * blk, blk), n * blk)]

    q_heads, prev_chosen = [], []
    for hh in range(HEADS_PER_STEP):
        in_head = (lane >= (hh % 2) * MOBA_HEAD_DIM) & (lane < (hh % 2 + 1) * MOBA_HEAD_DIM)
        qh = jnp.where(in_head, q_ref[:, lane_group(hh)].astype(F32), 0.0).astype(BF16)
        q_heads.append(qh)

        kmean = kmean_ref[:, lane_group(hh)].astype(BF16)
        gate = jnp.where(blk_id < qbf, _dot_nt(kmean, qh), -jnp.inf)
        sel = jnp.zeros((nb, blk), F32)
        for _ in range(min(MOBA_TOPK, nb)):
            best = jnp.max(gate, axis=0, keepdims=True)
            first = jnp.min(jnp.where(gate == best, blk_id, float(nb)), axis=0, keepdims=True)
            pick = blk_id == first
            sel = jnp.where(pick, 1.0, sel)
            gate = jnp.where(pick, -jnp.inf, gate)
        prev_chosen.append(jnp.max(jnp.where(blk_id == qbf - 1.0, sel, 0.0), axis=0, keepdims=True))
        sel_s[hh] = jnp.where(blk_id < qbf - 1.0, sel, 0.0)

    lo = jnp.maximum(qb - 1, 0)
    tab = jnp.where(has_prev, 0, 1)

    def nearest_exact():
        scores = [[_dot_nt(k_block(hh, lo + u), q_heads[hh]) + near_ref[hh, tab + u]
                   for u in range(2)] for hh in range(HEADS_PER_STEP)]
        for hh in range(HEADS_PER_STEP):
            chosen = [jnp.where(has_prev, prev_chosen[hh], 1.0) > 0.5,
                      jnp.where(has_prev, jnp.ones((1, blk), F32), 0.0) > 0.5]
            _online_softmax_group(m_s.at[hh], acc_s.at[hh], scores[hh], vt_blocks(hh, lo, 2), chosen)

    n_groups = (qb - 1 + unroll - 1) // unroll

    def far_group(g, streamed):
        first = g * unroll
        start = jnp.minimum(first, nb - unroll)

        def scores(hh):
            keys = k_ref[pl.ds(pl.multiple_of(start * blk, blk), unroll * blk), lane_group(hh)]
            st = _dot_nt(keys, q_heads[hh])
            return [st[u * blk:(u + 1) * blk, :] for u in range(unroll)]

        def operands(hh):
            chosen = [jnp.where(start + u >= first, sel_s[hh, pl.ds(start + u, 1), :], 0.0) > 0.5
                      for u in range(unroll)]
            return vt_blocks(hh, start, unroll), chosen, [cfar_ref[hh][:, 0:1]] * unroll

        lookahead = MATMUL_LOOKAHEAD if streamed else HEADS_PER_STEP
        queue = []
        for n in range(HEADS_PER_STEP + lookahead):
            if n < HEADS_PER_STEP:
                queue.append((n, scores(n)))
            if n >= lookahead:
                hh, blocks = queue.pop(0)
                if streamed:
                    _streamed_softmax_group(m_s.at[hh], acc_s.at[hh], late_s.at[hh], blocks,
                                            *operands(hh))
                else:
                    _online_softmax_group(m_s.at[hh], acc_s.at[hh], blocks, *operands(hh))

    late_s[...] = jnp.zeros_like(late_s)
    nearest_exact()

    def two_groups(t, carry):
        far_group(2 * t, True)
        far_group(2 * t + 1, True)
        return carry

    lax.fori_loop(0, n_groups // 2, two_groups, 0)

    @pl.when(n_groups % 2 == 1)
    def _():
        far_group(n_groups - 1, True)

    @pl.when(jnp.max(late_s[...]) > 0.5)
    def _():
        _reset_softmax_state(m_s, acc_s)
        nearest_exact()

        def exact(g, carry):
            far_group(g, False)
            return carry

        lax.fori_loop(0, n_groups, exact, 0)

    o_ref[...] = _normalised_heads(acc_s, MOBA_HEAD_DIM).T.astype(o_ref.dtype)


def _moba(qa, ka, vat, kmean, near, cfar, *, batch, seq, unroll):
    n = qa.shape[0]
    blk = MOBA_BLOCK
    nb = seq // blk
    assert nb >= unroll and nb >= 2
    steps = MOBA_HEADS // HEADS_PER_STEP
    width = HEADS_PER_STEP * MOBA_HEAD_DIM
    return pl.pallas_call(
        functools.partial(_moba_kernel, unroll=unroll),
        grid=(batch, steps, nb),
        in_specs=[
            pl.BlockSpec((blk, width), lambda b, p, i: (b * nb + i, p)),
            pl.BlockSpec((blk, width), lambda b, p, i: (b * nb + i, p)),
            pl.BlockSpec((None, HEADS_PER_STEP * V_ROWS, blk), lambda b, p, i: (b, p, i)),
            pl.BlockSpec((None, nb, width), lambda b, p, i: (b, 0, p)),
            _per_batch((HEADS_PER_STEP, 3, blk, blk), lambda b, p, i: (p, 0, 0, 0)),
            pl.BlockSpec((HEADS_PER_STEP, 1, LANES), lambda b, p, i: (p, 0, 0)),
        ],
        out_specs=pl.BlockSpec((blk, width), lambda b, p, i: (b * nb + i, p)),
        out_shape=jax.ShapeDtypeStruct((n, MOBA_WIDTH), BF16),
        scratch_shapes=[
            pltpu.VMEM((HEADS_PER_STEP, 1, blk), F32),
            pltpu.VMEM((HEADS_PER_STEP, V_ROWS, blk), F32),
            pltpu.VMEM((HEADS_PER_STEP, 1, blk), F32),
            pltpu.VMEM((HEADS_PER_STEP, nb, blk), F32),
            pltpu.VMEM((seq, width), BF16),
            pltpu.VMEM((HEADS_PER_STEP * V_ROWS, seq), BF16),
        ],
        compiler_params=pltpu.CompilerParams(
            dimension_semantics=("arbitrary", "arbitrary", "arbitrary"),
            vmem_limit_bytes=VMEM_LIMIT_BYTES),
        name="moba",
    )(qa, ka, vat, kmean, near, cfar)


def _mla_kernel(q_ref, k_new_ref, vt_new_ref, o_ref, m_s, acc_s, late_s, k_ref, vt_ref,
                *, tq, tk, unroll):
    qi = pl.program_id(2)
    n_diag = tq // tk
    key = lax.broadcasted_iota(jnp.int32, (tk, tq), 0)
    qry = lax.broadcasted_iota(jnp.int32, (tk, tq), 1)
    q_heads = [q_ref[:, hh * LANES:(hh + 1) * LANES] for hh in range(HEADS_PER_STEP)]

    own = pl.ds(pl.multiple_of(qi * tq, tq), tq)
    k_ref[own, :] = k_new_ref[...]
    vt_ref[:, own] = vt_new_ref[...]

    _reset_softmax_state(m_s, acc_s)

    def k_tile(hh, j):
        return k_ref[pl.ds(pl.multiple_of(j * tk, tk), tk), hh * LANES:(hh + 1) * LANES]

    def vt_tiles(hh, j, n):
        return vt_ref[hh * V_ROWS:(hh + 1) * V_ROWS, pl.ds(pl.multiple_of(j * tk, tk), n * tk)]

    def group(first, n, masked):
        scores = [[_dot_nt(k_tile(hh, first + u), q_heads[hh]) for u in range(n)]
                  for hh in range(HEADS_PER_STEP)]
        for hh in range(HEADS_PER_STEP):
            blocks = scores[hh]
            if masked:
                blocks = [jnp.where(key + u * tk <= qry, st, NEG) for u, st in enumerate(blocks)]
            _online_softmax_group(m_s.at[hh], acc_s.at[hh], blocks, vt_tiles(hh, first, n))

    n_groups = (qi * n_diag) // unroll

    def visible_scores(hh, g):
        start = pl.multiple_of(g * unroll * tk, tk)
        st = _dot_nt(k_ref[pl.ds(start, unroll * tk), hh * LANES:(hh + 1) * LANES], q_heads[hh])
        return [st[u * tk:(u + 1) * tk, :] for u in range(unroll)]

    def sweep(groups):
        stages = [(g, hh) for g in groups for hh in range(HEADS_PER_STEP)]
        queue = []
        for n in range(len(stages) + MATMUL_LOOKAHEAD):
            if n < len(stages):
                g, hh = stages[n]
                queue.append((stages[n], visible_scores(hh, g)))
            if n >= MATMUL_LOOKAHEAD:
                (g, hh), blocks = queue.pop(0)
                _streamed_softmax_group(m_s.at[hh], acc_s.at[hh], late_s.at[hh], blocks,
                                        vt_tiles(hh, g * unroll, unroll))

    late_s[...] = jnp.zeros_like(late_s)
    group(qi * n_diag, n_diag, True)

    def two_groups(t, carry):
        sweep([2 * t, 2 * t + 1])
        return carry

    lax.fori_loop(0, n_groups // 2, two_groups, 0)

    @pl.when(n_groups % 2 == 1)
    def _():
        sweep([n_groups - 1])

    @pl.when(jnp.max(late_s[...]) > 0.5)
    def _():
        _reset_softmax_state(m_s, acc_s)
        group(qi * n_diag, n_diag, True)

        def exact(g, carry):
            group(g * unroll, unroll, False)
            return carry

        lax.fori_loop(0, n_groups, exact, 0)

    o_ref[...] = _normalised_heads(acc_s, MLA_V_DIM).T.astype(o_ref.dtype)


def _mla(qm, km, vmt, *, batch, seq, tq, tk, unroll):
    n = qm.shape[0]
    nq = seq // tq
    assert tq % tk == 0 and (tq // tk) % unroll == 0
    pairs = MLA_HEADS // HEADS_PER_STEP
    return pl.pallas_call(
        functools.partial(_mla_kernel, tq=tq, tk=tk, unroll=unroll),
        grid=(batch, pairs, nq),
        in_specs=[
            pl.BlockSpec((tq, HEADS_PER_STEP * LANES), lambda b, p, i: (b * nq + i, p)),
            pl.BlockSpec((tq, HEADS_PER_STEP * LANES), lambda b, p, i: (b * nq + i, p)),
            pl.BlockSpec((None, HEADS_PER_STEP * V_ROWS, tq), lambda b, p, i: (b, p, i)),
        ],
        out_specs=pl.BlockSpec((tq, HEADS_PER_STEP * MLA_V_DIM), lambda b, p, i: (b * nq + i, p)),
        out_shape=jax.ShapeDtypeStruct((n, MLA_WIDTH), BF16),
        scratch_shapes=[
            pltpu.VMEM((HEADS_PER_STEP, 1, tq), F32),
            pltpu.VMEM((HEADS_PER_STEP, V_ROWS, tq), F32),
            pltpu.VMEM((HEADS_PER_STEP, 1, tq), F32),
            pltpu.VMEM((seq, HEADS_PER_STEP * LANES), BF16),
            pltpu.VMEM((HEADS_PER_STEP * V_ROWS, seq), BF16),
        ],
        compiler_params=pltpu.CompilerParams(
            dimension_semantics=("arbitrary", "arbitrary", "arbitrary"),
            vmem_limit_bytes=VMEM_LIMIT_BYTES),
        name="mla",
    )(qm, km, vmt)


def _back_kernel(x_ref, ya_ref, yb_ref, g_ref, wa_ref, wb_ref, wo_ref, gffn_ref, wup_ref,
                 cw_ref, cb_ref, wdn_ref, gfin_ref, o_ref,
                 carry_ref, act_ref, hn_ref, *, steps_per_seq, d_ff, fc, final):
    tm, d = x_ref.shape
    halo = SUBLANES

    @pl.when(pl.program_id(0) % steps_per_seq == 0)
    def _():
        carry_ref[...] = jnp.zeros_like(carry_ref)

    g = g_ref[...].astype(F32)
    mixed = g[:, :d] * _dot(ya_ref[...], wa_ref[...]) + g[:, d:] * _dot(yb_ref[...], wb_ref[...])
    h1 = x_ref[...] + _dot(mixed.astype(BF16), wo_ref[...])
    hn_ref[...] = _rms(h1, gffn_ref[...]).astype(BF16)

    def up_conv(col0):
        cols = slice(col0, col0 + fc)
        u = _dot(hn_ref[...], wup_ref[:, cols])
        rows = jnp.concatenate([carry_ref[:, cols], u], axis=0)
        carry_ref[:, cols] = u[tm - halo:tm, :]
        w = cw_ref[:, cols]
        y = cb_ref[:, cols]
        for t in range(CONV_WIDTH):
            back = CONV_WIDTH - 1 - t
            shifted = pltpu.roll(rows, back, axis=0) if back else rows
            y = y + w[t:t + 1, :] * shifted[halo:, :]
        return y

    for c in range(d_ff // fc):
        yg = up_conv(c * fc)
        yv = up_conv(d_ff + c * fc)
        act_ref[:, c * fc:(c + 1) * fc] = (yg * _sigmoid(yg) * yv).astype(BF16)

    h2 = h1 + _dot(act_ref[...], wdn_ref[...])
    o_ref[...] = _rms(h2, gfin_ref[...]) if final else h2


def _back(x2, ya, yb, g, wa, wb, wo, gffn, wup, cw, cb, wdn, gfin, *, seq, tm, fc, final):
    n, d = x2.shape
    d_ff = wdn.shape[0]
    row = lambda i: (i, 0)
    return pl.pallas_call(
        functools.partial(_back_kernel, steps_per_seq=seq // tm, d_ff=d_ff, fc=fc, final=final),
        grid=(n // tm,),
        in_specs=[
            pl.BlockSpec((tm, d), row),
            pl.BlockSpec((tm, ya.shape[1]), row),
            pl.BlockSpec((tm, yb.shape[1]), row),
            pl.BlockSpec((tm, g.shape[1]), row),
            _resident(wa.shape), _resident(wb.shape), _resident(wo.shape), _resident(gffn.shape),
            _resident(wup.shape), _resident(cw.shape), _resident(cb.shape), _resident(wdn.shape),
            _resident(gfin.shape),
        ],
        out_specs=pl.BlockSpec((tm, d), row),
        out_shape=jax.ShapeDtypeStruct((n, d), F32),
        scratch_shapes=[
            pltpu.VMEM((SUBLANES, 2 * d_ff), F32),
            pltpu.VMEM((tm, d_ff), BF16),
            pltpu.VMEM((tm, d), BF16),
        ],
        compiler_params=pltpu.CompilerParams(
            dimension_semantics=("arbitrary",), vmem_limit_bytes=VMEM_LIMIT_BYTES),
        name="back",
    )(x2, ya, yb, g, wa, wb, wo, gffn, wup, cw, cb, wdn, gfin)


def _t5_bucket_np(rel):
    n = np.maximum(rel, 0)
    max_exact = REL_BUCKETS // 2
    nf = np.maximum(n, 1).astype(np.float32)
    large = max_exact + (np.log(nf / np.float32(max_exact)) / np.float32(math.log(REL_MAX_DIST / max_exact))
                         * np.float32(REL_BUCKETS - max_exact)).astype(np.int32)
    large = np.minimum(large, REL_BUCKETS - 1)
    return np.where(n < max_exact, n, large)


def _moba_bias_tables(rel_bias):
    blk = MOBA_BLOCK
    period = 2 * blk
    assert REL_MAX_DIST <= blk + 1
    bias_h = rel_bias.T.astype(F32)
    slot = np.arange(period, dtype=np.int32)
    dist = np.where(slot < blk, slot, slot - period)

    def lookup(bucket):
        onehot = jnp.asarray(bucket)[None, :] == jnp.arange(REL_BUCKETS)[:, None]
        return jnp.sum(jnp.where(onehot[None], bias_h[:, :, None], 0.0), axis=1)

    def toeplitz(v):
        tiled = jnp.tile(v, (1, blk))[:, :blk * (period - 1)]
        return tiled.reshape(v.shape[0], blk, period - 1)[:, :, :blk]

    town = toeplitz(jnp.where(jnp.asarray(dist >= 0), lookup(_t5_bucket_np(dist)) * LOG2E, NEG))
    tprev = toeplitz(lookup(_t5_bucket_np(dist + blk)) * LOG2E)
    near = jnp.stack([tprev, town, jnp.zeros_like(town)], axis=1)
    cfar = bias_h[:, int(_t5_bucket_np(np.int32(blk + 1)))] * LOG2E
    cfar = jnp.broadcast_to(cfar[:, None, None], (MOBA_HEADS, 1, LANES))
    return near, cfar


def _pad_heads(w, n_heads, width, padded=LANES):
    r = w.shape[0]
    w = w.reshape(r, n_heads, width)
    out = jnp.zeros((r, n_heads, padded), w.dtype).at[:, :, :width].set(w)
    return out.reshape(r, n_heads * padded)


def _rope_slot(w_rope):
    r = w_rope.shape[0]
    return jnp.zeros((r, LANES), w_rope.dtype).at[:, MLA_NOPE_DIM:MLA_NOPE_DIM + MLA_ROPE_DIM].set(w_rope)


def _front_weights(w_in, w_uq, w_ukv):
    d = w_in.shape[0]
    s0 = MOBA_WIDTH
    s1 = 2 * MOBA_WIDTH
    s2 = 3 * MOBA_WIDTH
    s3 = s2 + MLA_Q_RANK
    s4 = s3 + MLA_KV_RANK
    s5 = s4 + MLA_ROPE_DIM
    w1 = jnp.concatenate([
        w_in[:, :s0], w_in[:, s0:s1],
        w_in[:, s2:s3], w_in[:, s3:s4],
        _rope_slot(w_in[:, s4:s5]),
        w_in[:, s5:],
    ], axis=1).astype(BF16)
    assert w1.shape == (d, _C_END)
    wvt = _pad_heads(w_in[:, s1:s2], MOBA_HEADS, MOBA_HEAD_DIM, V_ROWS).T.astype(BF16)

    wuq = _pad_heads(w_uq, MLA_HEADS, MLA_QK_DIM).astype(BF16)

    r = w_ukv.shape[0]
    ukv = w_ukv.reshape(r, MLA_HEADS, MLA_NOPE_DIM + MLA_V_DIM)
    wk = _pad_heads(ukv[:, :, :MLA_NOPE_DIM].reshape(r, -1), MLA_HEADS, MLA_NOPE_DIM).astype(BF16)
    wkvt = _pad_heads(ukv[:, :, MLA_NOPE_DIM:].reshape(r, -1), MLA_HEADS, MLA_V_DIM, V_ROWS).T.astype(BF16)
    return w1, wvt, wuq, wk, wkvt


def _rope_lane_tables(seq):
    dim = MLA_ROPE_DIM
    inv_freq = ROPE_THETA ** (-jnp.arange(0, dim, 2, dtype=F32) / dim)
    ang = jnp.arange(seq, dtype=F32)[:, None] * inv_freq[None, :]
    cos, sin = jnp.cos(ang), jnp.sin(ang)
    tail = jnp.zeros((seq, LANES - MLA_QK_DIM), F32)
    cosm = jnp.concatenate([jnp.ones((seq, MLA_NOPE_DIM), F32), cos, cos, tail], axis=1)
    sinm = jnp.concatenate([jnp.zeros((seq, MLA_NOPE_DIM), F32), -sin, sin, tail], axis=1)
    return cosm, sinm


class _Tiles:
    def __init__(self, seq):
        self.tm = 512 if seq % 512 == 0 else MOBA_BLOCK
        self.fc = 256
        self.mla_tq = self.tm
        self.mla_tk = 256
        self.mla_unroll = self.mla_tq // self.mla_tk
        self.moba_unroll = 4


def kernel(x, norm_attn_g, w_in, b_gate, q_norm_g, w_uq, kv_norm_g, w_ukv, rel_bias,
           w_branch_moba, w_branch_mla, w_out, norm_ffn_g, w_up, conv_w, conv_b, w_down,
           norm_final_g):
    batch, seq, d = x.shape
    depth = w_in.shape[0]
    assert seq % MOBA_BLOCK == 0
    t = _Tiles(seq)
    d_ff = w_down.shape[1]
    assert d_ff % t.fc == 0
    n = batch * seq

    cosm, sinm = _rope_lane_tables(seq)
    near, cfar = _moba_bias_tables(rel_bias)
    row = lambda v: v.reshape(1, -1).astype(F32)

    h = x.reshape(n, d)
    for l in range(depth):
        w1, wvt, wuq, wk, wkvt = _front_weights(w_in[l], w_uq[l], w_ukv[l])
        qa, ka, kmean, vat, qm, km, vmt, g = _front(
            h, row(norm_attn_g[l]), w1, wvt, row(b_gate[l]), row(q_norm_g[l]), wuq,
            row(kv_norm_g[l]), wk, wkvt, cosm, sinm, seq=seq, tm=t.tm)
        kmean = kmean.reshape(batch, seq // MOBA_BLOCK, MOBA_WIDTH)
        ya = _moba(qa, ka, vat, kmean, near, cfar, batch=batch, seq=seq,
                   unroll=t.moba_unroll)
        yb = _mla(qm, km, vmt, batch=batch, seq=seq, tq=t.mla_tq, tk=t.mla_tk,
                  unroll=t.mla_unroll)
        h = _back(h, ya, yb, g, w_branch_moba[l].astype(BF16), w_branch_mla[l].astype(BF16),
                  w_out[l].astype(BF16), row(norm_ffn_g[l]), w_up[l].astype(BF16),
                  conv_w[l].astype(F32), row(conv_b[l]), w_down[l].astype(BF16),
                  row(norm_final_g), seq=seq, tm=t.tm, fc=t.fc, final=(l == depth - 1))
    return h.reshape(batch, seq, d)
```

```python
import functools
import math

import numpy as np
import jax
import jax.numpy as jnp
from jax import lax
from jax.experimental import pallas as pl
from jax.experimental.pallas import tpu as pltpu

MOBA_HEADS = 8
MOBA_HEAD_DIM = 64
MOBA_BLOCK = 256
MOBA_TOPK = 3
MLA_HEADS = 8
MLA_Q_RANK = 256
MLA_KV_RANK = 128
MLA_NOPE_DIM = 64
MLA_ROPE_DIM = 32
MLA_V_DIM = 64
ROPE_THETA = 10000.0
REL_BUCKETS = 32
REL_MAX_DIST = 128
CONV_WIDTH = 3
N_BRANCH = 2
EPS = 1e-6

MOBA_WIDTH = MOBA_HEADS * MOBA_HEAD_DIM
MLA_QK_DIM = MLA_NOPE_DIM + MLA_ROPE_DIM
MLA_WIDTH = MLA_HEADS * MLA_V_DIM

LANES = 128
SUBLANES = 8
VMEM_LIMIT_BYTES = 56 * 1024 * 1024

NEG = -1e30

F32 = jnp.float32
BF16 = jnp.bfloat16


def _dot(a, b):
    return jnp.dot(a, b, preferred_element_type=F32)


def _dot_nt(a, b):
    return lax.dot_general(a, b, (((1,), (1,)), ((), ())), preferred_element_type=F32)


def _rms(x, g):
    return x * lax.rsqrt(jnp.mean(x * x, axis=-1, keepdims=True) + EPS) * g


def _sigmoid(z):
    return 1.0 / (1.0 + jnp.exp(-z))


def _resident(shape):
    nd = len(shape)
    return pl.BlockSpec(shape, lambda *_: (0,) * nd, pipeline_mode=pl.Buffered(1))


def _per_batch(shape, index_map):
    return pl.BlockSpec(shape, index_map, pipeline_mode=pl.Buffered(1))


_C_QA = 0
_C_KA = _C_QA + MOBA_WIDTH
_C_CQ = _C_KA + MOBA_WIDTH
_C_CKV = _C_CQ + MLA_Q_RANK
_C_KRA = _C_CKV + MLA_KV_RANK
_C_G = _C_KRA + LANES
_C_END = _C_G + N_BRANCH * 1024

BF16_SUBLANES = 16
V_ROWS = MLA_V_DIM + BF16_SUBLANES
assert MOBA_HEAD_DIM == MLA_V_DIM


def _ones_rows(n_rows):
    r = lax.broadcasted_iota(jnp.int32, (n_rows, 1), 0).astype(F32)
    within = r - jnp.floor((r + 0.5) * (1.0 / V_ROWS)) * V_ROWS
    return jnp.where(within == MLA_V_DIM, 1.0, 0.0).astype(F32)


def _front_kernel(x_ref, gattn_ref, w1_ref, wvt_ref, bg_ref, qng_ref, wuq_ref, kvng_ref, wkv_ref,
                  wkvt_ref, cos_ref, sin_ref,
                  qa_ref, ka_ref, kmean_ref, vat_ref, qm_ref, km_ref, vmt_ref, g_ref,
                  *, mla_scale):
    tm = x_ref.shape[0]
    xn = _rms(x_ref[...], gattn_ref[...]).astype(BF16)

    def proj(a, b):
        return _dot(xn, w1_ref[:, a:b])

    qa_ref[...] = (proj(_C_QA, _C_KA) * (MOBA_HEAD_DIM ** -0.5 * LOG2E)).astype(BF16)
    ka = proj(_C_KA, _C_CQ)
    ka_ref[...] = ka.astype(BF16)
    for i in range(tm // MOBA_BLOCK):
        kmean_ref[i] = jnp.mean(ka[i * MOBA_BLOCK:(i + 1) * MOBA_BLOCK], axis=0, keepdims=True)
    ones = _ones_rows(vat_ref.shape[0])
    vat_ref[...] = (_dot_nt(wvt_ref[...], xn) + ones).astype(BF16)

    cosm = cos_ref[...]
    sinm = sin_ref[...]
    lane = lax.broadcasted_iota(jnp.int32, (tm, LANES), 1)
    half = MLA_ROPE_DIM // 2

    def rope(a):
        x2_on_x1 = pltpu.roll(a, LANES - half, axis=1)
        x1_on_x2 = pltpu.roll(a, half, axis=1)
        swapped = jnp.where(lane < MLA_NOPE_DIM + half, x2_on_x1, x1_on_x2)
        return a * cosm + swapped * sinm

    cqn = _rms(proj(_C_CQ, _C_CKV), qng_ref[...]).astype(BF16)
    qq = _dot(cqn, wuq_ref[...])
    for h in range(MLA_HEADS):
        qm_ref[:, h * LANES:(h + 1) * LANES] = (
            rope(qq[:, h * LANES:(h + 1) * LANES]) * mla_scale).astype(BF16)

    ckv_kra = proj(_C_CKV, _C_G)
    ckvn = _rms(ckv_kra[:, :MLA_KV_RANK], kvng_ref[...]).astype(BF16)
    kv = _dot(ckvn, wkv_ref[...])
    kr = rope(ckv_kra[:, MLA_KV_RANK:])
    for h in range(MLA_HEADS):
        km_ref[:, h * LANES:(h + 1) * LANES] = (kv[:, h * LANES:(h + 1) * LANES] + kr).astype(BF16)
    vmt_ref[...] = (_dot_nt(wkvt_ref[...], ckvn) + ones).astype(BF16)

    g_ref[...] = _sigmoid(proj(_C_G, _C_END) + bg_ref[...]).astype(BF16)


def _front(x2, gattn, w1, wvt, bg, qng, wuq, kvng, wkv, wkvt, cosm, sinm, *, seq, tm):
    n, d = x2.shape
    batch = n // seq
    hw = MLA_HEADS * LANES
    nblk = n // MOBA_BLOCK
    steps_per_seq = seq // tm
    row = lambda i: (i, 0)
    tcol = lambda i: (i // steps_per_seq, 0, i % steps_per_seq)
    out_shape = (
        jax.ShapeDtypeStruct((n, MOBA_WIDTH), BF16),
        jax.ShapeDtypeStruct((n, MOBA_WIDTH), BF16),
        jax.ShapeDtypeStruct((nblk, 1, MOBA_WIDTH), F32),
        jax.ShapeDtypeStruct((batch, MOBA_HEADS * V_ROWS, seq), BF16),
        jax.ShapeDtypeStruct((n, hw), BF16),
        jax.ShapeDtypeStruct((n, hw), BF16),
        jax.ShapeDtypeStruct((batch, MLA_HEADS * V_ROWS, seq), BF16),
        jax.ShapeDtypeStruct((n, N_BRANCH * d), BF16),
    )
    in_specs = [
        pl.BlockSpec((tm, d), row),
        _resident(gattn.shape), _resident(w1.shape), _resident(wvt.shape), _resident(bg.shape),
        _resident(qng.shape), _resident(wuq.shape), _resident(kvng.shape), _resident(wkv.shape),
        _resident(wkvt.shape),
        pl.BlockSpec((tm, LANES), lambda i: (i % steps_per_seq, 0)),
        pl.BlockSpec((tm, LANES), lambda i: (i % steps_per_seq, 0)),
    ]
    out_specs = (
        pl.BlockSpec((tm, MOBA_WIDTH), row),
        pl.BlockSpec((tm, MOBA_WIDTH), row),
        pl.BlockSpec((tm // MOBA_BLOCK, 1, MOBA_WIDTH), lambda i: (i, 0, 0)),
        pl.BlockSpec((None, MOBA_HEADS * V_ROWS, tm), tcol),
        pl.BlockSpec((tm, hw), row),
        pl.BlockSpec((tm, hw), row),
        pl.BlockSpec((None, MLA_HEADS * V_ROWS, tm), tcol),
        pl.BlockSpec((tm, N_BRANCH * d), row),
    )
    return pl.pallas_call(
        functools.partial(_front_kernel, mla_scale=MLA_QK_DIM ** -0.5 * LOG2E),
        grid=(n // tm,),
        in_specs=in_specs,
        out_specs=out_specs,
        out_shape=out_shape,
        compiler_params=pltpu.CompilerParams(
            dimension_semantics=("arbitrary",), vmem_limit_bytes=VMEM_LIMIT_BYTES),
        name="front",
    )(x2, gattn, w1, wvt, bg, qng, wuq, kvng, wkv, wkvt, cosm, sinm)


HEADS_PER_STEP = 8
LOG2E = math.log2(math.e)
BIG = 1e30
RESCALE_SLACK = 64.0
MATMUL_LOOKAHEAD = 2
TRIP_GROUPS = 4


def _online_softmax_group(m_ref, acc_ref, score_blocks, vt_grp, chosen=None, shifts=None):
    n = len(score_blocks)
    chosen = chosen or [None] * n
    shifts = shifts or [None] * n
    m_old = m_ref[...]
    m_new = m_old
    for st, ch, sh in zip(score_blocks, chosen, shifts):
        rm = jnp.max(st, axis=0, keepdims=True)
        if sh is not None:
            rm = rm + sh
        m_new = jnp.maximum(m_new, rm if ch is None else jnp.where(ch, rm, NEG))
    probs = []
    for st, ch, sh in zip(score_blocks, chosen, shifts):
        off = m_new if sh is None else m_new - sh
        if ch is not None:
            off = jnp.where(ch, off, BIG)
        probs.append(jnp.exp2(st - off).astype(BF16))
    ot = _dot(vt_grp, jnp.concatenate(probs, axis=0))
    acc_ref[...] = jnp.exp2(m_old - m_new) * acc_ref[...] + ot
    m_ref[...] = m_new


def _streamed_softmax_group(m_ref, acc_ref, late_ref, score_blocks, vt_grp, chosen=None, shifts=None):
    n = len(score_blocks)
    chosen = chosen or [None] * n
    shifts = shifts or [None] * n
    m_old = m_ref[...]
    seen = jnp.full(m_old.shape, NEG, F32)
    probs = []
    for st, ch, sh in zip(score_blocks, chosen, shifts):
        off = m_old if sh is None else m_old - sh
        if ch is not None:
            off = jnp.where(ch, off, BIG)
        probs.append(jnp.exp2(st - off).astype(BF16))
        rm = jnp.max(st, axis=0, keepdims=True)
        if sh is not None:
            rm = rm + sh
        seen = jnp.maximum(seen, rm if ch is None else jnp.where(ch, rm, NEG))
    acc_ref[...] = acc_ref[...] + _dot(vt_grp, jnp.concatenate(probs, axis=0))
    late_ref[...] = jnp.where(seen > m_old + RESCALE_SLACK, 1.0, late_ref[...])


def _pipelined(stages, issue, consume, lookahead):
    queue = []
    for n in range(len(stages) + lookahead):
        if n < len(stages):
            queue.append((stages[n], issue(stages[n])))
        if n >= lookahead:
            consume(*queue.pop(0))


def _in_trips(n, sweep):
    def trip(t, carry):
        sweep([TRIP_GROUPS * t + i for i in range(TRIP_GROUPS)])
        return carry

    lax.fori_loop(0, n // TRIP_GROUPS, trip, 0)
    done = (n // TRIP_GROUPS) * TRIP_GROUPS
    size = TRIP_GROUPS // 2
    while size >= 1:
        has = ((n - done) // size) % 2 == 1
        first = done + ((n - done) // (2 * size)) * (2 * size)
        pl.when(has)(functools.partial(sweep, [first + i for i in range(size)]))
        size //= 2


def _reset_softmax_state(m_s, acc_s):
    m_s[...] = jnp.full(m_s.shape, NEG, F32)
    acc_s[...] = jnp.zeros(acc_s.shape, F32)


def _normalised_heads(acc_s, v_dim):
    rows = []
    for hh in range(HEADS_PER_STEP):
        acc = acc_s[hh]
        rows.append(acc[0:v_dim, :] * (1.0 / acc[v_dim:v_dim + 1, :]))
    return jnp.concatenate(rows, axis=0)


def _moba_kernel(q_ref, k_new_ref, vt_new_ref, kmean_ref, near_ref, cfar_ref, o_ref,
                 m_s, acc_s, late_s, sel_s, k_ref, vt_ref, *, unroll):
    blk = MOBA_BLOCK
    qb = pl.program_id(2)
    nb = kmean_ref.shape[0]
    lane = lax.broadcasted_iota(jnp.int32, (blk, LANES), 1)
    blk_id = lax.broadcasted_iota(jnp.int32, (nb, blk), 0).astype(F32)
    qbf = qb.astype(F32)
    has_prev = qb >= 1
    _reset_softmax_state(m_s, acc_s)

    @pl.when(qb == 0)
    def _():
        k_ref[...] = jnp.zeros_like(k_ref)
        vt_ref[...] = jnp.zeros_like(vt_ref)

    own = pl.ds(pl.multiple_of(qb * blk, blk), blk)
    k_ref[own, :] = k_new_ref[...]
    vt_ref[:, own] = vt_new_ref[...]

    def lane_group(hh):
        return slice((hh // 2) * LANES, (hh // 2 + 1) * LANES)

    def k_block(hh, j):
        return k_ref[pl.ds(pl.multiple_of(j * blk, blk), blk), lane_group(hh)]

    def vt_blocks(hh, j, n):
        return vt_ref[hh * V_ROWS:(hh + 1) * V_ROWS, pl.ds(pl.multiple_of(j * blk, blk), n * blk)]

    q_heads, prev_chosen = [], []
    for hh in range(HEADS_PER_STEP):
        in_head = (lane >= (hh % 2) * MOBA_HEAD_DIM) & (lane < (hh % 2 + 1) * MOBA_HEAD_DIM)
        qh = jnp.where(in_head, q_ref[:, lane_group(hh)].astype(F32), 0.0).astype(BF16)
        q_heads.append(qh)

        kmean = kmean_ref[:, lane_group(hh)].astype(BF16)
        gate = jnp.where(blk_id < qbf, _dot_nt(kmean, qh), -jnp.inf)
        sel = jnp.zeros((nb, blk), F32)
        for _ in range(min(MOBA_TOPK, nb)):
            best = jnp.max(gate, axis=0, keepdims=True)
            first = jnp.min(jnp.where(gate == best, blk_id, float(nb)), axis=0, keepdims=True)
            pick = blk_id == first
            sel = jnp.where(pick, 1.0, sel)
            gate = jnp.where(pick, -jnp.inf, gate)
        prev_chosen.append(jnp.max(jnp.where(blk_id == qbf - 1.0, sel, 0.0), axis=0, keepdims=True))
        sel_s[hh] = jnp.where(blk_id < qbf - 1.0, sel, 0.0)

    lo = jnp.maximum(qb - 1, 0)
    tab = jnp.where(has_prev, 0, 1)

    def nearest_exact():
        scores = [[_dot_nt(k_block(hh, lo + u), q_heads[hh]) + near_ref[hh, tab + u]
                   for u in range(2)] for hh in range(HEADS_PER_STEP)]
        for hh in range(HEADS_PER_STEP):
            chosen = [jnp.where(has_prev, prev_chosen[hh], 1.0) > 0.5,
                      jnp.where(has_prev, jnp.ones((1, blk), F32), 0.0) > 0.5]
            _online_softmax_group(m_s.at[hh], acc_s.at[hh], scores[hh], vt_blocks(hh, lo, 2), chosen)

    n_groups = (qb - 1 + unroll - 1) // unroll

    def far_sweep(groups, streamed=True):
        def start_of(g):
            return jnp.minimum(g * unroll, nb - unroll)

        def scores(stage):
            g, hh = stage
            keys = k_ref[pl.ds(pl.multiple_of(start_of(g) * blk, blk), unroll * blk), lane_group(hh)]
            st = _dot_nt(keys, q_heads[hh])
            return [st[u * blk:(u + 1) * blk, :] for u in range(unroll)]

        def softmax(stage, blocks):
            g, hh = stage
            start = start_of(g)
            chosen = [jnp.where(start + u >= g * unroll, sel_s[hh, pl.ds(start + u, 1), :], 0.0) > 0.5
                      for u in range(unroll)]
            operands = (blocks, vt_blocks(hh, start, unroll), chosen, [cfar_ref[hh][:, 0:1]] * unroll)
            if streamed:
                _streamed_softmax_group(m_s.at[hh], acc_s.at[hh], late_s.at[hh], *operands)
            else:
                _online_softmax_group(m_s.at[hh], acc_s.at[hh], *operands)

        stages = [(g, hh) for g in groups for hh in range(HEADS_PER_STEP)]
        _pipelined(stages, scores, softmax, MATMUL_LOOKAHEAD if streamed else len(stages))

    late_s[...] = jnp.zeros_like(late_s)
    nearest_exact()
    _in_trips(n_groups, far_sweep)

    @pl.when(jnp.max(late_s[...]) > 0.5)
    def _():
        _reset_softmax_state(m_s, acc_s)
        nearest_exact()

        def exact(g, carry):
            far_sweep([g], streamed=False)
            return carry

        lax.fori_loop(0, n_groups, exact, 0)

    o_ref[...] = _normalised_heads(acc_s, MOBA_HEAD_DIM).T.astype(o_ref.dtype)


def _moba(qa, ka, vat, kmean, near, cfar, *, batch, seq, unroll):
    n = qa.shape[0]
    blk = MOBA_BLOCK
    nb = seq // blk
    assert nb >= unroll and nb >= 2
    steps = MOBA_HEADS // HEADS_PER_STEP
    width = HEADS_PER_STEP * MOBA_HEAD_DIM
    return pl.pallas_call(
        functools.partial(_moba_kernel, unroll=unroll),
        grid=(batch, steps, nb),
        in_specs=[
            pl.BlockSpec((blk, width), lambda b, p, i: (b * nb + i, p)),
            pl.BlockSpec((blk, width), lambda b, p, i: (b * nb + i, p)),
            pl.BlockSpec((None, HEADS_PER_STEP * V_ROWS, blk), lambda b, p, i: (b, p, i)),
            pl.BlockSpec((None, nb, width), lambda b, p, i: (b, 0, p)),
            _per_batch((HEADS_PER_STEP, 3, blk, blk), lambda b, p, i: (p, 0, 0, 0)),
            pl.BlockSpec((HEADS_PER_STEP, 1, LANES), lambda b, p, i: (p, 0, 0)),
        ],
        out_specs=pl.BlockSpec((blk, width), lambda b, p, i: (b * nb + i, p)),
        out_shape=jax.ShapeDtypeStruct((n, MOBA_WIDTH), BF16),
        scratch_shapes=[
            pltpu.VMEM((HEADS_PER_STEP, 1, blk), F32),
            pltpu.VMEM((HEADS_PER_STEP, V_ROWS, blk), F32),
            pltpu.VMEM((HEADS_PER_STEP, 1, blk), F32),
            pltpu.VMEM((HEADS_PER_STEP, nb, blk), F32),
            pltpu.VMEM((seq, width), BF16),
            pltpu.VMEM((HEADS_PER_STEP * V_ROWS, seq), BF16),
        ],
        compiler_params=pltpu.CompilerParams(
            dimension_semantics=("arbitrary", "arbitrary", "arbitrary"),
            vmem_limit_bytes=VMEM_LIMIT_BYTES),
        name="moba",
    )(qa, ka, vat, kmean, near, cfar)


def _mla_kernel(q_ref, k_new_ref, vt_new_ref, o_ref, m_s, acc_s, late_s, k_ref, vt_ref,
                *, tq, tk, unroll):
    qi = pl.program_id(2)
    n_diag = tq // tk
    key = lax.broadcasted_iota(jnp.int32, (tk, tq), 0)
    qry = lax.broadcasted_iota(jnp.int32, (tk, tq), 1)
    q_heads = [q_ref[:, hh * LANES:(hh + 1) * LANES] for hh in range(HEADS_PER_STEP)]

    own = pl.ds(pl.multiple_of(qi * tq, tq), tq)
    k_ref[own, :] = k_new_ref[...]
    vt_ref[:, own] = vt_new_ref[...]

    _reset_softmax_state(m_s, acc_s)

    def k_tile(hh, j):
        return k_ref[pl.ds(pl.multiple_of(j * tk, tk), tk), hh * LANES:(hh + 1) * LANES]

    def vt_tiles(hh, j, n):
        return vt_ref[hh * V_ROWS:(hh + 1) * V_ROWS, pl.ds(pl.multiple_of(j * tk, tk), n * tk)]

    def group(first, n, masked):
        scores = [[_dot_nt(k_tile(hh, first + u), q_heads[hh]) for u in range(n)]
                  for hh in range(HEADS_PER_STEP)]
        for hh in range(HEADS_PER_STEP):
            blocks = scores[hh]
            if masked:
                blocks = [jnp.where(key + u * tk <= qry, st, NEG) for u, st in enumerate(blocks)]
            _online_softmax_group(m_s.at[hh], acc_s.at[hh], blocks, vt_tiles(hh, first, n))

    n_groups = (qi * n_diag) // unroll

    def visible_sweep(groups):
        def scores(stage):
            g, hh = stage
            start = pl.multiple_of(g * unroll * tk, tk)
            st = _dot_nt(k_ref[pl.ds(start, unroll * tk), hh * LANES:(hh + 1) * LANES], q_heads[hh])
            return [st[u * tk:(u + 1) * tk, :] for u in range(unroll)]

        def softmax(stage, blocks):
            g, hh = stage
            _streamed_softmax_group(m_s.at[hh], acc_s.at[hh], late_s.at[hh], blocks,
                                    vt_tiles(hh, g * unroll, unroll))

        stages = [(g, hh) for g in groups for hh in range(HEADS_PER_STEP)]
        _pipelined(stages, scores, softmax, MATMUL_LOOKAHEAD)

    late_s[...] = jnp.zeros_like(late_s)
    group(qi * n_diag, n_diag, True)
    _in_trips(n_groups, visible_sweep)

    @pl.when(jnp.max(late_s[...]) > 0.5)
    def _():
        _reset_softmax_state(m_s, acc_s)
        group(qi * n_diag, n_diag, True)

        def exact(g, carry):
            group(g * unroll, unroll, False)
            return carry

        lax.fori_loop(0, n_groups, exact, 0)

    o_ref[...] = _normalised_heads(acc_s, MLA_V_DIM).T.astype(o_ref.dtype)


def _mla(qm, km, vmt, *, batch, seq, tq, tk, unroll):
    n = qm.shape[0]
    nq = seq // tq
    assert tq % tk == 0 and (tq // tk) % unroll == 0
    pairs = MLA_HEADS // HEADS_PER_STEP
    return pl.pallas_call(
        functools.partial(_mla_kernel, tq=tq, tk=tk, unroll=unroll),
        grid=(batch, pairs, nq),
        in_specs=[
            pl.BlockSpec((tq, HEADS_PER_STEP * LANES), lambda b, p, i: (b * nq + i, p)),
            pl.BlockSpec((tq, HEADS_PER_STEP * LANES), lambda b, p, i: (b * nq + i, p)),
            pl.BlockSpec((None, HEADS_PER_STEP * V_ROWS, tq), lambda b, p, i: (b, p, i)),
        ],
        out_specs=pl.BlockSpec((tq, HEADS_PER_STEP * MLA_V_DIM), lambda b, p, i: (b * nq + i, p)),
        out_shape=jax.ShapeDtypeStruct((n, MLA_WIDTH), BF16),
        scratch_shapes=[
            pltpu.VMEM((HEADS_PER_STEP, 1, tq), F32),
            pltpu.VMEM((HEADS_PER_STEP, V_ROWS, tq), F32),
            pltpu.VMEM((HEADS_PER_STEP, 1, tq), F32),
            pltpu.VMEM((seq, HEADS_PER_STEP * LANES), BF16),
            pltpu.VMEM((HEADS_PER_STEP * V_ROWS, seq), BF16),
        ],
        compiler_params=pltpu.CompilerParams(
            dimension_semantics=("arbitrary", "arbitrary", "arbitrary"),
            vmem_limit_bytes=VMEM_LIMIT_BYTES),
        name="mla",
    )(qm, km, vmt)


def _back_kernel(x_ref, ya_ref, yb_ref, g_ref, wa_ref, wb_ref, wo_ref, gffn_ref, wup_ref,
                 cw_ref, cb_ref, wdn_ref, gfin_ref, o_ref,
                 carry_ref, act_ref, hn_ref, *, steps_per_seq, d_ff, fc, final):
    tm, d = x_ref.shape
    halo = SUBLANES

    @pl.when(pl.program_id(0) % steps_per_seq == 0)
    def _():
        carry_ref[...] = jnp.zeros_like(carry_ref)

    g = g_ref[...].astype(F32)
    mixed = g[:, :d] * _dot(ya_ref[...], wa_ref[...]) + g[:, d:] * _dot(yb_ref[...], wb_ref[...])
    h1 = x_ref[...] + _dot(mixed.astype(BF16), wo_ref[...])
    hn_ref[...] = _rms(h1, gffn_ref[...]).astype(BF16)

    def up_conv(col0):
        cols = slice(col0, col0 + fc)
        u = _dot(hn_ref[...], wup_ref[:, cols])
        rows = jnp.concatenate([carry_ref[:, cols], u], axis=0)
        carry_ref[:, cols] = u[tm - halo:tm, :]
        w = cw_ref[:, cols]
        y = cb_ref[:, cols]
        for t in range(CONV_WIDTH):
            back = CONV_WIDTH - 1 - t
            shifted = pltpu.roll(rows, back, axis=0) if back else rows
            y = y + w[t:t + 1, :] * shifted[halo:, :]
        return y

    for c in range(d_ff // fc):
        yg = up_conv(c * fc)
        yv = up_conv(d_ff + c * fc)
        act_ref[:, c * fc:(c + 1) * fc] = (yg * _sigmoid(yg) * yv).astype(BF16)

    h2 = h1 + _dot(act_ref[...], wdn_ref[...])
    o_ref[...] = _rms(h2, gfin_ref[...]) if final else h2


def _back(x2, ya, yb, g, wa, wb, wo, gffn, wup, cw, cb, wdn, gfin, *, seq, tm, fc, final):
    n, d = x2.shape
    d_ff = wdn.shape[0]
    row = lambda i: (i, 0)
    return pl.pallas_call(
        functools.partial(_back_kernel, steps_per_seq=seq // tm, d_ff=d_ff, fc=fc, final=final),
        grid=(n // tm,),
        in_specs=[
            pl.BlockSpec((tm, d), row),
            pl.BlockSpec((tm, ya.shape[1]), row),
            pl.BlockSpec((tm, yb.shape[1]), row),
            pl.BlockSpec((tm, g.shape[1]), row),
            _resident(wa.shape), _resident(wb.shape), _resident(wo.shape), _resident(gffn.shape),
            _resident(wup.shape), _resident(cw.shape), _resident(cb.shape), _resident(wdn.shape),
            _resident(gfin.shape),
        ],
        out_specs=pl.BlockSpec((tm, d), row),
        out_shape=jax.ShapeDtypeStruct((n, d), F32),
        scratch_shapes=[
            pltpu.VMEM((SUBLANES, 2 * d_ff), F32),
            pltpu.VMEM((tm, d_ff), BF16),
            pltpu.VMEM((tm, d), BF16),
        ],
        compiler_params=pltpu.CompilerParams(
            dimension_semantics=("arbitrary",), vmem_limit_bytes=VMEM_LIMIT_BYTES),
        name="back",
    )(x2, ya, yb, g, wa, wb, wo, gffn, wup, cw, cb, wdn, gfin)


def _t5_bucket_np(rel):
    n = np.maximum(rel, 0)
    max_exact = REL_BUCKETS // 2
    nf = np.maximum(n, 1).astype(np.float32)
    large = max_exact + (np.log(nf / np.float32(max_exact)) / np.float32(math.log(REL_MAX_DIST / max_exact))
                         * np.float32(REL_BUCKETS - max_exact)).astype(np.int32)
    large = np.minimum(large, REL_BUCKETS - 1)
    return np.where(n < max_exact, n, large)


def _moba_bias_tables(rel_bias):
    blk = MOBA_BLOCK
    period = 2 * blk
    assert REL_MAX_DIST <= blk + 1
    bias_h = rel_bias.T.astype(F32)
    slot = np.arange(period, dtype=np.int32)
    dist = np.where(slot < blk, slot, slot - period)

    def lookup(bucket):
        onehot = jnp.asarray(bucket)[None, :] == jnp.arange(REL_BUCKETS)[:, None]
        return jnp.sum(jnp.where(onehot[None], bias_h[:, :, None], 0.0), axis=1)

    def toeplitz(v):
        tiled = jnp.tile(v, (1, blk))[:, :blk * (period - 1)]
        return tiled.reshape(v.shape[0], blk, period - 1)[:, :, :blk]

    town = toeplitz(jnp.where(jnp.asarray(dist >= 0), lookup(_t5_bucket_np(dist)) * LOG2E, NEG))
    tprev = toeplitz(lookup(_t5_bucket_np(dist + blk)) * LOG2E)
    near = jnp.stack([tprev, town, jnp.zeros_like(town)], axis=1)
    cfar = bias_h[:, int(_t5_bucket_np(np.int32(blk + 1)))] * LOG2E
    cfar = jnp.broadcast_to(cfar[:, None, None], (MOBA_HEADS, 1, LANES))
    return near, cfar


def _pad_heads(w, n_heads, width, padded=LANES):
    r = w.shape[0]
    w = w.reshape(r, n_heads, width)
    out = jnp.zeros((r, n_heads, padded), w.dtype).at[:, :, :width].set(w)
    return out.reshape(r, n_heads * padded)


def _rope_slot(w_rope):
    r = w_rope.shape[0]
    return jnp.zeros((r, LANES), w_rope.dtype).at[:, MLA_NOPE_DIM:MLA_NOPE_DIM + MLA_ROPE_DIM].set(w_rope)


def _front_weights(w_in, w_uq, w_ukv):
    d = w_in.shape[0]
    s0 = MOBA_WIDTH
    s1 = 2 * MOBA_WIDTH
    s2 = 3 * MOBA_WIDTH
    s3 = s2 + MLA_Q_RANK
    s4 = s3 + MLA_KV_RANK
    s5 = s4 + MLA_ROPE_DIM
    w1 = jnp.concatenate([
        w_in[:, :s0], w_in[:, s0:s1],
        w_in[:, s2:s3], w_in[:, s3:s4],
        _rope_slot(w_in[:, s4:s5]),
        w_in[:, s5:],
    ], axis=1).astype(BF16)
    assert w1.shape == (d, _C_END)
    wvt = _pad_heads(w_in[:, s1:s2], MOBA_HEADS, MOBA_HEAD_DIM, V_ROWS).T.astype(BF16)

    wuq = _pad_heads(w_uq, MLA_HEADS, MLA_QK_DIM).astype(BF16)

    r = w_ukv.shape[0]
    ukv = w_ukv.reshape(r, MLA_HEADS, MLA_NOPE_DIM + MLA_V_DIM)
    wk = _pad_heads(ukv[:, :, :MLA_NOPE_DIM].reshape(r, -1), MLA_HEADS, MLA_NOPE_DIM).astype(BF16)
    wkvt = _pad_heads(ukv[:, :, MLA_NOPE_DIM:].reshape(r, -1), MLA_HEADS, MLA_V_DIM, V_ROWS).T.astype(BF16)
    return w1, wvt, wuq, wk, wkvt


def _rope_lane_tables(seq):
    dim = MLA_ROPE_DIM
    inv_freq = ROPE_THETA ** (-jnp.arange(0, dim, 2, dtype=F32) / dim)
    ang = jnp.arange(seq, dtype=F32)[:, None] * inv_freq[None, :]
    cos, sin = jnp.cos(ang), jnp.sin(ang)
    tail = jnp.zeros((seq, LANES - MLA_QK_DIM), F32)
    cosm = jnp.concatenate([jnp.ones((seq, MLA_NOPE_DIM), F32), cos, cos, tail], axis=1)
    sinm = jnp.concatenate([jnp.zeros((seq, MLA_NOPE_DIM), F32), -sin, sin, tail], axis=1)
    return cosm, sinm


class _Tiles:
    def __init__(self, seq):
        self.tm = 512 if seq % 512 == 0 else MOBA_BLOCK
        self.fc = 256
        self.mla_tq = self.tm
        self.mla_tk = 256
        self.mla_unroll = self.mla_tq // self.mla_tk
        self.moba_unroll = 4


def kernel(x, norm_attn_g, w_in, b_gate, q_norm_g, w_uq, kv_norm_g, w_ukv, rel_bias,
           w_branch_moba, w_branch_mla, w_out, norm_ffn_g, w_up, conv_w, conv_b, w_down,
           norm_final_g):
    batch, seq, d = x.shape
    depth = w_in.shape[0]
    assert seq % MOBA_BLOCK == 0
    t = _Tiles(seq)
    d_ff = w_down.shape[1]
    assert d_ff % t.fc == 0
    n = batch * seq

    cosm, sinm = _rope_lane_tables(seq)
    near, cfar = _moba_bias_tables(rel_bias)
    row = lambda v: v.reshape(1, -1).astype(F32)

    h = x.reshape(n, d)
    for l in range(depth):
        w1, wvt, wuq, wk, wkvt = _front_weights(w_in[l], w_uq[l], w_ukv[l])
        qa, ka, kmean, vat, qm, km, vmt, g = _front(
            h, row(norm_attn_g[l]), w1, wvt, row(b_gate[l]), row(q_norm_g[l]), wuq,
            row(kv_norm_g[l]), wk, wkvt, cosm, sinm, seq=seq, tm=t.tm)
        kmean = kmean.reshape(batch, seq // MOBA_BLOCK, MOBA_WIDTH)
        ya = _moba(qa, ka, vat, kmean, near, cfar, batch=batch, seq=seq,
                   unroll=t.moba_unroll)
        yb = _mla(qm, km, vmt, batch=batch, seq=seq, tq=t.mla_tq, tk=t.mla_tk,
                  unroll=t.mla_unroll)
        h = _back(h, ya, yb, g, w_branch_moba[l].astype(BF16), w_branch_mla[l].astype(BF16),
                  w_out[l].astype(BF16), row(norm_ffn_g[l]), w_up[l].astype(BF16),
                  conv_w[l].astype(F32), row(conv_b[l]), w_down[l].astype(BF16),
                  row(norm_final_g), seq=seq, tm=t.tm, fc=t.fc, final=(l == depth - 1))
    return h.reshape(batch, seq, d)
```

```python
import functools
import math

import numpy as np
import jax
import jax.numpy as jnp
from jax import lax
from jax.experimental import pallas as pl
from jax.experimental.pallas import tpu as pltpu

MOBA_HEADS = 8
MOBA_HEAD_DIM = 64
MOBA_BLOCK = 256
MOBA_TOPK = 3
MLA_HEADS = 8
MLA_Q_RANK = 256
MLA_KV_RANK = 128
MLA_NOPE_DIM = 64
MLA_ROPE_DIM = 32
MLA_V_DIM = 64
ROPE_THETA = 10000.0
REL_BUCKETS = 32
REL_MAX_DIST = 128
CONV_WIDTH = 3
N_BRANCH = 2
EPS = 1e-6

MOBA_WIDTH = MOBA_HEADS * MOBA_HEAD_DIM
MLA_QK_DIM = MLA_NOPE_DIM + MLA_ROPE_DIM
MLA_WIDTH = MLA_HEADS * MLA_V_DIM

LANES = 128
SUBLANES = 8
VMEM_LIMIT_BYTES = 56 * 1024 * 1024

NEG = -1e30
LOG2E = math.log2(math.e)

F32 = jnp.float32
BF16 = jnp.bfloat16


def _dot(a, b):
    return jnp.dot(a, b, preferred_element_type=F32)


def _dot_nt(a, b):
    return lax.dot_general(a, b, (((1,), (1,)), ((), ())), preferred_element_type=F32)


def _rms(x, g):
    return x * lax.rsqrt(jnp.mean(x * x, axis=-1, keepdims=True) + EPS) * g


def _sigmoid(z):
    return 1.0 / (1.0 + jnp.exp(-z))


def _resident(shape):
    nd = len(shape)
    return pl.BlockSpec(shape, lambda *_: (0,) * nd, pipeline_mode=pl.Buffered(1))


def _per_head_group(shape, index_map):
    return pl.BlockSpec(shape, index_map, pipeline_mode=pl.Buffered(1))


_C_QA = 0
_C_KA = _C_QA + MOBA_WIDTH
_C_CQ = _C_KA + MOBA_WIDTH
_C_CKV = _C_CQ + MLA_Q_RANK
_C_KRA = _C_CKV + MLA_KV_RANK
_C_G = _C_KRA + LANES

BF16_SUBLANES = 16
V_ROWS = MLA_V_DIM + BF16_SUBLANES
assert MOBA_HEAD_DIM == MLA_V_DIM


def _ones_rows(n_rows):
    r = lax.broadcasted_iota(jnp.int32, (n_rows, 1), 0).astype(F32)
    within = r - jnp.floor((r + 0.5) * (1.0 / V_ROWS)) * V_ROWS
    return jnp.where(within == MLA_V_DIM, 1.0, 0.0).astype(F32)


def _front_kernel(x_ref, gattn_ref, w1_ref, wvt_ref, bg_ref, qng_ref, wuq_ref, kvng_ref, wkv_ref,
                  wkvt_ref, cos_ref, sin_ref,
                  qa_ref, ka_ref, kmean_ref, vat_ref, qm_ref, km_ref, vmt_ref, g_ref,
                  *, mla_scale):
    tm = x_ref.shape[0]
    xn = _rms(x_ref[...], gattn_ref[...]).astype(BF16)

    def proj(a, b):
        return _dot(xn, w1_ref[:, a:b])

    qa_ref[...] = (proj(_C_QA, _C_KA) * (MOBA_HEAD_DIM ** -0.5 * LOG2E)).astype(BF16)
    ka = proj(_C_KA, _C_CQ)
    ka_ref[...] = ka.astype(BF16)
    for i in range(tm // MOBA_BLOCK):
        kmean_ref[i] = jnp.mean(ka[i * MOBA_BLOCK:(i + 1) * MOBA_BLOCK], axis=0, keepdims=True)
    ones = _ones_rows(vat_ref.shape[0])
    vat_ref[...] = (_dot_nt(wvt_ref[...], xn) + ones).astype(BF16)

    cosm = cos_ref[...]
    sinm = sin_ref[...]
    lane = lax.broadcasted_iota(jnp.int32, (tm, LANES), 1)
    half = MLA_ROPE_DIM // 2

    def rope(a):
        x2_on_x1 = pltpu.roll(a, LANES - half, axis=1)
        x1_on_x2 = pltpu.roll(a, half, axis=1)
        swapped = jnp.where(lane < MLA_NOPE_DIM + half, x2_on_x1, x1_on_x2)
        return a * cosm + swapped * sinm

    cqn = _rms(proj(_C_CQ, _C_CKV), qng_ref[...]).astype(BF16)
    qq = _dot(cqn, wuq_ref[...])
    for h in range(MLA_HEADS):
        qm_ref[:, h * LANES:(h + 1) * LANES] = (
            rope(qq[:, h * LANES:(h + 1) * LANES]) * mla_scale).astype(BF16)

    ckv_kra = proj(_C_CKV, _C_G)
    ckvn = _rms(ckv_kra[:, :MLA_KV_RANK], kvng_ref[...]).astype(BF16)
    kv = _dot(ckvn, wkv_ref[...])
    kr = rope(ckv_kra[:, MLA_KV_RANK:])
    for h in range(MLA_HEADS):
        km_ref[:, h * LANES:(h + 1) * LANES] = (kv[:, h * LANES:(h + 1) * LANES] + kr).astype(BF16)
    vmt_ref[...] = (_dot_nt(wkvt_ref[...], ckvn) + ones).astype(BF16)

    g_ref[...] = _sigmoid(proj(_C_G, w1_ref.shape[1]) + bg_ref[...]).astype(BF16)


def _front(x2, gattn, w1, wvt, bg, qng, wuq, kvng, wkv, wkvt, cosm, sinm, *, seq, tm):
    n, d = x2.shape
    batch = n // seq
    hw = MLA_HEADS * LANES
    nblk = n // MOBA_BLOCK
    steps_per_seq = seq // tm
    row = lambda i: (i, 0)
    tcol = lambda i: (i // steps_per_seq, 0, i % steps_per_seq)
    out_shape = (
        jax.ShapeDtypeStruct((n, MOBA_WIDTH), BF16),
        jax.ShapeDtypeStruct((n, MOBA_WIDTH), BF16),
        jax.ShapeDtypeStruct((nblk, 1, MOBA_WIDTH), F32),
        jax.ShapeDtypeStruct((batch, MOBA_HEADS * V_ROWS, seq), BF16),
        jax.ShapeDtypeStruct((n, hw), BF16),
        jax.ShapeDtypeStruct((n, hw), BF16),
        jax.ShapeDtypeStruct((batch, MLA_HEADS * V_ROWS, seq), BF16),
        jax.ShapeDtypeStruct((n, N_BRANCH * d), BF16),
    )
    in_specs = [
        pl.BlockSpec((tm, d), row),
        _resident(gattn.shape), _resident(w1.shape), _resident(wvt.shape), _resident(bg.shape),
        _resident(qng.shape), _resident(wuq.shape), _resident(kvng.shape), _resident(wkv.shape),
        _resident(wkvt.shape),
        pl.BlockSpec((tm, LANES), lambda i: (i % steps_per_seq, 0)),
        pl.BlockSpec((tm, LANES), lambda i: (i % steps_per_seq, 0)),
    ]
    out_specs = (
        pl.BlockSpec((tm, MOBA_WIDTH), row),
        pl.BlockSpec((tm, MOBA_WIDTH), row),
        pl.BlockSpec((tm // MOBA_BLOCK, 1, MOBA_WIDTH), lambda i: (i, 0, 0)),
        pl.BlockSpec((None, MOBA_HEADS * V_ROWS, tm), tcol),
        pl.BlockSpec((tm, hw), row),
        pl.BlockSpec((tm, hw), row),
        pl.BlockSpec((None, MLA_HEADS * V_ROWS, tm), tcol),
        pl.BlockSpec((tm, N_BRANCH * d), row),
    )
    return pl.pallas_call(
        functools.partial(_front_kernel, mla_scale=MLA_QK_DIM ** -0.5 * LOG2E),
        grid=(n // tm,),
        in_specs=in_specs,
        out_specs=out_specs,
        out_shape=out_shape,
        compiler_params=pltpu.CompilerParams(
            dimension_semantics=("arbitrary",), vmem_limit_bytes=VMEM_LIMIT_BYTES),
        name="front",
    )(x2, gattn, w1, wvt, bg, qng, wuq, kvng, wkv, wkvt, cosm, sinm)


HEADS_PER_STEP = 8
BIG = 1e30
RESCALE_SLACK = 64.0
MATMUL_LOOKAHEAD = 2
TRIP_GROUPS = 4


def _online_softmax_group(m_ref, acc_ref, score_blocks, vt_grp, chosen=None, shifts=None):
    n = len(score_blocks)
    chosen = chosen or [None] * n
    shifts = shifts or [None] * n
    m_old = m_ref[...]
    m_new = m_old
    for st, ch, sh in zip(score_blocks, chosen, shifts):
        rm = jnp.max(st, axis=0, keepdims=True)
        if sh is not None:
            rm = rm + sh
        m_new = jnp.maximum(m_new, rm if ch is None else jnp.where(ch, rm, NEG))
    probs = []
    for st, ch, sh in zip(score_blocks, chosen, shifts):
        off = m_new if sh is None else m_new - sh
        if ch is not None:
            off = jnp.where(ch, off, BIG)
        probs.append(jnp.exp2(st - off).astype(BF16))
    ot = _dot(vt_grp, jnp.concatenate(probs, axis=0))
    acc_ref[...] = jnp.exp2(m_old - m_new) * acc_ref[...] + ot
    m_ref[...] = m_new


def _streamed_softmax_group(m_ref, acc_ref, late_ref, score_blocks, vt_grp, chosen=None, shifts=None):
    n = len(score_blocks)
    chosen = chosen or [None] * n
    shifts = shifts or [None] * n
    m_old = m_ref[...]
    seen = jnp.full(m_old.shape, NEG, F32)
    probs = []
    for st, ch, sh in zip(score_blocks, chosen, shifts):
        off = m_old if sh is None else m_old - sh
        if ch is not None:
            off = jnp.where(ch, off, BIG)
        probs.append(jnp.exp2(st - off).astype(BF16))
        rm = jnp.max(st, axis=0, keepdims=True)
        if sh is not None:
            rm = rm + sh
        seen = jnp.maximum(seen, rm if ch is None else jnp.where(ch, rm, NEG))
    acc_ref[...] = acc_ref[...] + _dot(vt_grp, jnp.concatenate(probs, axis=0))
    late_ref[...] = jnp.where(seen > m_old + RESCALE_SLACK, 1.0, late_ref[...])


def _pipelined(stages, issue, consume, lookahead):
    queue = []
    for n in range(len(stages) + lookahead):
        if n < len(stages):
            queue.append((stages[n], issue(stages[n])))
        if n >= lookahead:
            consume(*queue.pop(0))


def _in_trips(n, sweep):
    def trip(t, carry):
        sweep([TRIP_GROUPS * t + i for i in range(TRIP_GROUPS)])
        return carry

    lax.fori_loop(0, n // TRIP_GROUPS, trip, 0)
    done = (n // TRIP_GROUPS) * TRIP_GROUPS
    size = TRIP_GROUPS // 2
    while size >= 1:
        has = ((n - done) // size) % 2 == 1
        first = done + ((n - done) // (2 * size)) * (2 * size)
        pl.when(has)(functools.partial(sweep, [first + i for i in range(size)]))
        size //= 2


def _reset_softmax_state(m_s, acc_s):
    m_s[...] = jnp.full(m_s.shape, NEG, F32)
    acc_s[...] = jnp.zeros(acc_s.shape, F32)


def _normalised_heads(acc_s, v_dim):
    rows = []
    for hh in range(HEADS_PER_STEP):
        acc = acc_s[hh]
        rows.append(acc[0:v_dim, :] * (1.0 / acc[v_dim:v_dim + 1, :]))
    return jnp.concatenate(rows, axis=0)


def _moba_kernel(q_ref, k_new_ref, vt_new_ref, kmean_ref, near_ref, cfar_ref, o_ref,
                 m_s, acc_s, late_s, sel_s, k_ref, vt_ref, *, unroll):
    blk = MOBA_BLOCK
    qb = pl.program_id(2)
    nb = kmean_ref.shape[0]
    lane = lax.broadcasted_iota(jnp.int32, (blk, LANES), 1)
    blk_id = lax.broadcasted_iota(jnp.int32, (nb, blk), 0).astype(F32)
    qbf = qb.astype(F32)
    has_prev = qb >= 1
    _reset_softmax_state(m_s, acc_s)

    @pl.when(qb == 0)
    def _():
        k_ref[...] = jnp.zeros_like(k_ref)
        vt_ref[...] = jnp.zeros_like(vt_ref)

    own = pl.ds(pl.multiple_of(qb * blk, blk), blk)
    k_ref[own, :] = k_new_ref[...]
    vt_ref[:, own] = vt_new_ref[...]

    def lane_group(hh):
        return slice((hh // 2) * LANES, (hh // 2 + 1) * LANES)

    def k_block(hh, j):
        return k_ref[pl.ds(pl.multiple_of(j * blk, blk), blk), lane_group(hh)]

    def vt_blocks(hh, j, n):
        return vt_ref[hh * V_ROWS:(hh + 1) * V_ROWS, pl.ds(pl.multiple_of(j * blk, blk), n * blk)]

    q_heads, prev_chosen = [], []
    for hh in range(HEADS_PER_STEP):
        in_head = (lane >= (hh % 2) * MOBA_HEAD_DIM) & (lane < (hh % 2 + 1) * MOBA_HEAD_DIM)
        qh = jnp.where(in_head, q_ref[:, lane_group(hh)].astype(F32), 0.0).astype(BF16)
        q_heads.append(qh)

        kmean = kmean_ref[:, lane_group(hh)].astype(BF16)
        gate = jnp.where(blk_id < qbf, _dot_nt(kmean, qh), -jnp.inf)
        sel = jnp.zeros((nb, blk), F32)
        for _ in range(min(MOBA_TOPK, nb)):
            best = jnp.max(gate, axis=0, keepdims=True)
            first = jnp.min(jnp.where(gate == best, blk_id, float(nb)), axis=0, keepdims=True)
            pick = blk_id == first
            sel = jnp.where(pick, 1.0, sel)
            gate = jnp.where(pick, -jnp.inf, gate)
        prev_chosen.append(jnp.max(jnp.where(blk_id == qbf - 1.0, sel, 0.0), axis=0, keepdims=True))
        sel_s[hh] = jnp.where(blk_id < qbf - 1.0, sel, 0.0)

    lo = jnp.maximum(qb - 1, 0)
    tab = jnp.where(has_prev, 0, 1)

    def nearest_exact():
        scores = [[_dot_nt(k_block(hh, lo + u), q_heads[hh]) + near_ref[hh, tab + u]
                   for u in range(2)] for hh in range(HEADS_PER_STEP)]
        for hh in range(HEADS_PER_STEP):
            chosen = [jnp.where(has_prev, prev_chosen[hh], 1.0) > 0.5,
                      jnp.where(has_prev, jnp.ones((1, blk), F32), 0.0) > 0.5]
            _online_softmax_group(m_s.at[hh], acc_s.at[hh], scores[hh], vt_blocks(hh, lo, 2), chosen)

    n_groups = (qb - 1 + unroll - 1) // unroll

    def far_sweep(groups, streamed=True):
        def start_of(g):
            return jnp.minimum(g * unroll, nb - unroll)

        def scores(stage):
            g, hh = stage
            keys = k_ref[pl.ds(pl.multiple_of(start_of(g) * blk, blk), unroll * blk), lane_group(hh)]
            st = _dot_nt(keys, q_heads[hh])
            return [st[u * blk:(u + 1) * blk, :] for u in range(unroll)]

        def softmax(stage, blocks):
            g, hh = stage
            start = start_of(g)
            chosen = [jnp.where(start + u >= g * unroll, sel_s[hh, pl.ds(start + u, 1), :], 0.0) > 0.5
                      for u in range(unroll)]
            operands = (blocks, vt_blocks(hh, start, unroll), chosen, [cfar_ref[hh][:, 0:1]] * unroll)
            if streamed:
                _streamed_softmax_group(m_s.at[hh], acc_s.at[hh], late_s.at[hh], *operands)
            else:
                _online_softmax_group(m_s.at[hh], acc_s.at[hh], *operands)

        stages = [(g, hh) for g in groups for hh in range(HEADS_PER_STEP)]
        _pipelined(stages, scores, softmax, MATMUL_LOOKAHEAD if streamed else len(stages))

    late_s[...] = jnp.zeros_like(late_s)
    nearest_exact()
    _in_trips(n_groups, far_sweep)

    @pl.when(jnp.max(late_s[...]) > 0.5)
    def _():
        _reset_softmax_state(m_s, acc_s)
        nearest_exact()

        def exact(g, carry):
            far_sweep([g], streamed=False)
            return carry

        lax.fori_loop(0, n_groups, exact, 0)

    o_ref[...] = _normalised_heads(acc_s, MOBA_HEAD_DIM).T.astype(o_ref.dtype)


def _moba(qa, ka, vat, kmean, near, cfar, *, batch, seq, unroll):
    n = qa.shape[0]
    blk = MOBA_BLOCK
    nb = seq // blk
    assert nb >= unroll and nb >= 2
    steps = MOBA_HEADS // HEADS_PER_STEP
    width = HEADS_PER_STEP * MOBA_HEAD_DIM
    return pl.pallas_call(
        functools.partial(_moba_kernel, unroll=unroll),
        grid=(batch, steps, nb),
        in_specs=[
            pl.BlockSpec((blk, width), lambda b, p, i: (b * nb + i, p)),
            pl.BlockSpec((blk, width), lambda b, p, i: (b * nb + i, p)),
            pl.BlockSpec((None, HEADS_PER_STEP * V_ROWS, blk), lambda b, p, i: (b, p, i)),
            pl.BlockSpec((None, nb, width), lambda b, p, i: (b, 0, p)),
            _per_head_group((HEADS_PER_STEP, 3, blk, blk), lambda b, p, i: (p, 0, 0, 0)),
            pl.BlockSpec((HEADS_PER_STEP, 1, LANES), lambda b, p, i: (p, 0, 0)),
        ],
        out_specs=pl.BlockSpec((blk, width), lambda b, p, i: (b * nb + i, p)),
        out_shape=jax.ShapeDtypeStruct((n, MOBA_WIDTH), BF16),
        scratch_shapes=[
            pltpu.VMEM((HEADS_PER_STEP, 1, blk), F32),
            pltpu.VMEM((HEADS_PER_STEP, V_ROWS, blk), F32),
            pltpu.VMEM((HEADS_PER_STEP, 1, blk), F32),
            pltpu.VMEM((HEADS_PER_STEP, nb, blk), F32),
            pltpu.VMEM((seq, width), BF16),
            pltpu.VMEM((HEADS_PER_STEP * V_ROWS, seq), BF16),
        ],
        compiler_params=pltpu.CompilerParams(
            dimension_semantics=("arbitrary", "arbitrary", "arbitrary"),
            vmem_limit_bytes=VMEM_LIMIT_BYTES),
        name="moba",
    )(qa, ka, vat, kmean, near, cfar)


def _mla_kernel(q_ref, k_new_ref, vt_new_ref, o_ref, m_s, acc_s, late_s, k_ref, vt_ref,
                *, tq, tk, unroll):
    qi = pl.program_id(2)
    n_diag = tq // tk
    key = lax.broadcasted_iota(jnp.int32, (tk, tq), 0)
    qry = lax.broadcasted_iota(jnp.int32, (tk, tq), 1)
    q_heads = [q_ref[:, hh * LANES:(hh + 1) * LANES] for hh in range(HEADS_PER_STEP)]

    own = pl.ds(pl.multiple_of(qi * tq, tq), tq)
    k_ref[own, :] = k_new_ref[...]
    vt_ref[:, own] = vt_new_ref[...]

    _reset_softmax_state(m_s, acc_s)

    def k_tile(hh, j):
        return k_ref[pl.ds(pl.multiple_of(j * tk, tk), tk), hh * LANES:(hh + 1) * LANES]

    def vt_tiles(hh, j, n):
        return vt_ref[hh * V_ROWS:(hh + 1) * V_ROWS, pl.ds(pl.multiple_of(j * tk, tk), n * tk)]

    def group(first, n, masked):
        scores = [[_dot_nt(k_tile(hh, first + u), q_heads[hh]) for u in range(n)]
                  for hh in range(HEADS_PER_STEP)]
        for hh in range(HEADS_PER_STEP):
            blocks = scores[hh]
            if masked:
                blocks = [jnp.where(key + u * tk <= qry, st, NEG) for u, st in enumerate(blocks)]
            _online_softmax_group(m_s.at[hh], acc_s.at[hh], blocks, vt_tiles(hh, first, n))

    n_groups = (qi * n_diag) // unroll

    def visible_sweep(groups):
        def scores(stage):
            g, hh = stage
            start = pl.multiple_of(g * unroll * tk, tk)
            st = _dot_nt(k_ref[pl.ds(start, unroll * tk), hh * LANES:(hh + 1) * LANES], q_heads[hh])
            return [st[u * tk:(u + 1) * tk, :] for u in range(unroll)]

        def softmax(stage, blocks):
            g, hh = stage
            _streamed_softmax_group(m_s.at[hh], acc_s.at[hh], late_s.at[hh], blocks,
                                    vt_tiles(hh, g * unroll, unroll))

        stages = [(g, hh) for g in groups for hh in range(HEADS_PER_STEP)]
        _pipelined(stages, scores, softmax, MATMUL_LOOKAHEAD)

    late_s[...] = jnp.zeros_like(late_s)
    group(qi * n_diag, n_diag, True)
    _in_trips(n_groups, visible_sweep)

    @pl.when(jnp.max(late_s[...]) > 0.5)
    def _():
        _reset_softmax_state(m_s, acc_s)
        group(qi * n_diag, n_diag, True)

        def exact(g, carry):
            group(g * unroll, unroll, False)
            return carry

        lax.fori_loop(0, n_groups, exact, 0)

    o_ref[...] = _normalised_heads(acc_s, MLA_V_DIM).T.astype(o_ref.dtype)


def _mla(qm, km, vmt, *, batch, seq, tq, tk, unroll):
    n = qm.shape[0]
    nq = seq // tq
    assert tq % tk == 0 and (tq // tk) % unroll == 0
    pairs = MLA_HEADS // HEADS_PER_STEP
    return pl.pallas_call(
        functools.partial(_mla_kernel, tq=tq, tk=tk, unroll=unroll),
        grid=(batch, pairs, nq),
        in_specs=[
            pl.BlockSpec((tq, HEADS_PER_STEP * LANES), lambda b, p, i: (b * nq + i, p)),
            pl.BlockSpec((tq, HEADS_PER_STEP * LANES), lambda b, p, i: (b * nq + i, p)),
            pl.BlockSpec((None, HEADS_PER_STEP * V_ROWS, tq), lambda b, p, i: (b, p, i)),
        ],
        out_specs=pl.BlockSpec((tq, HEADS_PER_STEP * MLA_V_DIM), lambda b, p, i: (b * nq + i, p)),
        out_shape=jax.ShapeDtypeStruct((n, MLA_WIDTH), BF16),
        scratch_shapes=[
            pltpu.VMEM((HEADS_PER_STEP, 1, tq), F32),
            pltpu.VMEM((HEADS_PER_STEP, V_ROWS, tq), F32),
            pltpu.VMEM((HEADS_PER_STEP, 1, tq), F32),
            pltpu.VMEM((seq, HEADS_PER_STEP * LANES), BF16),
            pltpu.VMEM((HEADS_PER_STEP * V_ROWS, seq), BF16),
        ],
        compiler_params=pltpu.CompilerParams(
            dimension_semantics=("arbitrary", "arbitrary", "arbitrary"),
            vmem_limit_bytes=VMEM_LIMIT_BYTES),
        name="mla",
    )(qm, km, vmt)


def _back_kernel(x_ref, ya_ref, yb_ref, g_ref, wa_ref, wb_ref, wo_ref, gffn_ref, wup_ref,
                 cw_ref, cb_ref, wdn_ref, gfin_ref, o_ref,
                 carry_ref, act_ref, hn_ref, *, steps_per_seq, d_ff, fc, final):
    tm, d = x_ref.shape
    halo = SUBLANES

    @pl.when(pl.program_id(0) % steps_per_seq == 0)
    def _():
        carry_ref[...] = jnp.zeros_like(carry_ref)

    g = g_ref[...].astype(F32)
    mixed = g[:, :d] * _dot(ya_ref[...], wa_ref[...]) + g[:, d:] * _dot(yb_ref[...], wb_ref[...])
    h1 = x_ref[...] + _dot(mixed.astype(BF16), wo_ref[...])
    hn_ref[...] = _rms(h1, gffn_ref[...]).astype(BF16)

    def up_conv(col0):
        cols = slice(col0, col0 + fc)
        u = _dot(hn_ref[...], wup_ref[:, cols])
        rows = jnp.concatenate([carry_ref[:, cols], u], axis=0)
        carry_ref[:, cols] = u[tm - halo:tm, :]
        w = cw_ref[:, cols]
        y = cb_ref[:, cols]
        for t in range(CONV_WIDTH):
            back = CONV_WIDTH - 1 - t
            shifted = pltpu.roll(rows, back, axis=0) if back else rows
            y = y + w[t:t + 1, :] * shifted[halo:, :]
        return y

    for c in range(d_ff // fc):
        yg = up_conv(c * fc)
        yv = up_conv(d_ff + c * fc)
        act_ref[:, c * fc:(c + 1) * fc] = (yg * _sigmoid(yg) * yv).astype(BF16)

    h2 = h1 + _dot(act_ref[...], wdn_ref[...])
    o_ref[...] = _rms(h2, gfin_ref[...]) if final else h2


def _back(x2, ya, yb, g, wa, wb, wo, gffn, wup, cw, cb, wdn, gfin, *, seq, tm, fc, final):
    n, d = x2.shape
    d_ff = wdn.shape[0]
    row = lambda i: (i, 0)
    return pl.pallas_call(
        functools.partial(_back_kernel, steps_per_seq=seq // tm, d_ff=d_ff, fc=fc, final=final),
        grid=(n // tm,),
        in_specs=[
            pl.BlockSpec((tm, d), row),
            pl.BlockSpec((tm, ya.shape[1]), row),
            pl.BlockSpec((tm, yb.shape[1]), row),
            pl.BlockSpec((tm, g.shape[1]), row),
            _resident(wa.shape), _resident(wb.shape), _resident(wo.shape), _resident(gffn.shape),
            _resident(wup.shape), _resident(cw.shape), _resident(cb.shape), _resident(wdn.shape),
            _resident(gfin.shape),
        ],
        out_specs=pl.BlockSpec((tm, d), row),
        out_shape=jax.ShapeDtypeStruct((n, d), F32),
        scratch_shapes=[
            pltpu.VMEM((SUBLANES, 2 * d_ff), F32),
            pltpu.VMEM((tm, d_ff), BF16),
            pltpu.VMEM((tm, d), BF16),
        ],
        compiler_params=pltpu.CompilerParams(
            dimension_semantics=("arbitrary",), vmem_limit_bytes=VMEM_LIMIT_BYTES),
        name="back",
    )(x2, ya, yb, g, wa, wb, wo, gffn, wup, cw, cb, wdn, gfin)


def _t5_bucket_np(rel):
    n = np.maximum(rel, 0)
    max_exact = REL_BUCKETS // 2
    nf = np.maximum(n, 1).astype(np.float32)
    large = max_exact + (np.log(nf / np.float32(max_exact)) / np.float32(math.log(REL_MAX_DIST / max_exact))
                         * np.float32(REL_BUCKETS - max_exact)).astype(np.int32)
    large = np.minimum(large, REL_BUCKETS - 1)
    return np.where(n < max_exact, n, large)


def _moba_bias_tables(rel_bias):
    blk = MOBA_BLOCK
    period = 2 * blk
    assert REL_MAX_DIST <= blk + 1
    bias_h = rel_bias.T.astype(F32)
    slot = np.arange(period, dtype=np.int32)
    dist = np.where(slot < blk, slot, slot - period)

    def lookup(bucket):
        onehot = jnp.asarray(bucket)[None, :] == jnp.arange(REL_BUCKETS)[:, None]
        return jnp.sum(jnp.where(onehot[None], bias_h[:, :, None], 0.0), axis=1)

    def toeplitz(v):
        tiled = jnp.tile(v, (1, blk))[:, :blk * (period - 1)]
        return tiled.reshape(v.shape[0], blk, period - 1)[:, :, :blk]

    town = toeplitz(jnp.where(jnp.asarray(dist >= 0), lookup(_t5_bucket_np(dist)) * LOG2E, NEG))
    tprev = toeplitz(lookup(_t5_bucket_np(dist + blk)) * LOG2E)
    near = jnp.stack([tprev, town, jnp.zeros_like(town)], axis=1)
    cfar = bias_h[:, int(_t5_bucket_np(np.int32(blk + 1)))] * LOG2E
    cfar = jnp.broadcast_to(cfar[:, None, None], (MOBA_HEADS, 1, LANES))
    return near, cfar


def _pad_heads(w, n_heads, width, padded=LANES):
    r = w.shape[0]
    w = w.reshape(r, n_heads, width)
    out = jnp.zeros((r, n_heads, padded), w.dtype).at[:, :, :width].set(w)
    return out.reshape(r, n_heads * padded)


def _rope_slot(w_rope):
    r = w_rope.shape[0]
    return jnp.zeros((r, LANES), w_rope.dtype).at[:, MLA_NOPE_DIM:MLA_NOPE_DIM + MLA_ROPE_DIM].set(w_rope)


def _front_weights(w_in, w_uq, w_ukv):
    d = w_in.shape[0]
    s0 = MOBA_WIDTH
    s1 = 2 * MOBA_WIDTH
    s2 = 3 * MOBA_WIDTH
    s3 = s2 + MLA_Q_RANK
    s4 = s3 + MLA_KV_RANK
    s5 = s4 + MLA_ROPE_DIM
    w1 = jnp.concatenate([
        w_in[:, :s0], w_in[:, s0:s1],
        w_in[:, s2:s3], w_in[:, s3:s4],
        _rope_slot(w_in[:, s4:s5]),
        w_in[:, s5:],
    ], axis=1).astype(BF16)
    assert w1.shape == (d, _C_G + N_BRANCH * d)
    wvt = _pad_heads(w_in[:, s1:s2], MOBA_HEADS, MOBA_HEAD_DIM, V_ROWS).T.astype(BF16)

    wuq = _pad_heads(w_uq, MLA_HEADS, MLA_QK_DIM).astype(BF16)

    r = w_ukv.shape[0]
    ukv = w_ukv.reshape(r, MLA_HEADS, MLA_NOPE_DIM + MLA_V_DIM)
    wk = _pad_heads(ukv[:, :, :MLA_NOPE_DIM].reshape(r, -1), MLA_HEADS, MLA_NOPE_DIM).astype(BF16)
    wkvt = _pad_heads(ukv[:, :, MLA_NOPE_DIM:].reshape(r, -1), MLA_HEADS, MLA_V_DIM, V_ROWS).T.astype(BF16)
    return w1, wvt, wuq, wk, wkvt


def _rope_lane_tables(seq):
    dim = MLA_ROPE_DIM
    inv_freq = ROPE_THETA ** (-jnp.arange(0, dim, 2, dtype=F32) / dim)
    ang = jnp.arange(seq, dtype=F32)[:, None] * inv_freq[None, :]
    cos, sin = jnp.cos(ang), jnp.sin(ang)
    tail = jnp.zeros((seq, LANES - MLA_QK_DIM), F32)
    cosm = jnp.concatenate([jnp.ones((seq, MLA_NOPE_DIM), F32), cos, cos, tail], axis=1)
    sinm = jnp.concatenate([jnp.zeros((seq, MLA_NOPE_DIM), F32), -sin, sin, tail], axis=1)
    return cosm, sinm


class _Tiles:
    def __init__(self, seq):
        self.tm = 512 if seq % 512 == 0 else MOBA_BLOCK
        self.fc = 256
        self.mla_tq = self.tm
        self.mla_tk = 256
        self.mla_unroll = self.mla_tq // self.mla_tk
        self.moba_unroll = 4


def kernel(x, norm_attn_g, w_in, b_gate, q_norm_g, w_uq, kv_norm_g, w_ukv, rel_bias,
           w_branch_moba, w_branch_mla, w_out, norm_ffn_g, w_up, conv_w, conv_b, w_down,
           norm_final_g):
    batch, seq, d = x.shape
    depth = w_in.shape[0]
    assert seq % MOBA_BLOCK == 0
    t = _Tiles(seq)
    d_ff = w_down.shape[1]
    assert d_ff % t.fc == 0
    n = batch * seq

    cosm, sinm = _rope_lane_tables(seq)
    near, cfar = _moba_bias_tables(rel_bias)
    row = lambda v: v.reshape(1, -1).astype(F32)

    h = x.reshape(n, d)
    for l in range(depth):
        w1, wvt, wuq, wk, wkvt = _front_weights(w_in[l], w_uq[l], w_ukv[l])
        qa, ka, kmean, vat, qm, km, vmt, g = _front(
            h, row(norm_attn_g[l]), w1, wvt, row(b_gate[l]), row(q_norm_g[l]), wuq,
            row(kv_norm_g[l]), wk, wkvt, cosm, sinm, seq=seq, tm=t.tm)
        kmean = kmean.reshape(batch, seq // MOBA_BLOCK, MOBA_WIDTH)
        ya = _moba(qa, ka, vat, kmean, near, cfar, batch=batch, seq=seq,
                   unroll=t.moba_unroll)
        yb = _mla(qm, km, vmt, batch=batch, seq=seq, tq=t.mla_tq, tk=t.mla_tk,
                  unroll=t.mla_unroll)
        h = _back(h, ya, yb, g, w_branch_moba[l].astype(BF16), w_branch_mla[l].astype(BF16),
                  w_out[l].astype(BF16), row(norm_ffn_g[l]), w_up[l].astype(BF16),
                  conv_w[l].astype(F32), row(conv_b[l]), w_down[l].astype(BF16),
                  row(norm_final_g), seq=seq, tm=t.tm, fc=t.fc, final=(l == depth - 1))
    return h.reshape(batch, seq, d)
```

```python
import functools
import math

import numpy as np
import jax
import jax.numpy as jnp
from jax import lax
from jax.experimental import pallas as pl
from jax.experimental.pallas import tpu as pltpu

MOBA_HEADS = 8
MOBA_HEAD_DIM = 64
MOBA_BLOCK = 256
MOBA_TOPK = 3
MLA_HEADS = 8
MLA_Q_RANK = 256
MLA_KV_RANK = 128
MLA_NOPE_DIM = 64
MLA_ROPE_DIM = 32
MLA_V_DIM = 64
ROPE_THETA = 10000.0
REL_BUCKETS = 32
REL_MAX_DIST = 128
CONV_WIDTH = 3
N_BRANCH = 2
EPS = 1e-6

MOBA_WIDTH = MOBA_HEADS * MOBA_HEAD_DIM
MLA_QK_DIM = MLA_NOPE_DIM + MLA_ROPE_DIM
MLA_WIDTH = MLA_HEADS * MLA_V_DIM

LANES = 128
SUBLANES = 8
VMEM_LIMIT_BYTES = 56 * 1024 * 1024

NEG = -1e30
LOG2E = math.log2(math.e)

F32 = jnp.float32
BF16 = jnp.bfloat16


def _dot(a, b):
    return jnp.dot(a, b, preferred_element_type=F32)


def _dot_nt(a, b):
    return lax.dot_general(a, b, (((1,), (1,)), ((), ())), preferred_element_type=F32)


def _rms(x, g):
    return x * lax.rsqrt(jnp.mean(x * x, axis=-1, keepdims=True) + EPS) * g


def _sigmoid(z):
    return 1.0 / (1.0 + jnp.exp(-z))


def _resident(shape):
    nd = len(shape)
    return pl.BlockSpec(shape, lambda *_: (0,) * nd, pipeline_mode=pl.Buffered(1))


def _per_head_group(shape, index_map):
    return pl.BlockSpec(shape, index_map, pipeline_mode=pl.Buffered(1))


_C_QA = 0
_C_KA = _C_QA + MOBA_WIDTH
_C_CQ = _C_KA + MOBA_WIDTH
_C_CKV = _C_CQ + MLA_Q_RANK
_C_KRA = _C_CKV + MLA_KV_RANK
_C_G = _C_KRA + LANES

BF16_SUBLANES = 16
V_ROWS = MLA_V_DIM + BF16_SUBLANES
assert MOBA_HEAD_DIM == MLA_V_DIM


def _ones_rows(n_rows):
    r = lax.broadcasted_iota(jnp.int32, (n_rows, 1), 0).astype(F32)
    within = r - jnp.floor((r + 0.5) * (1.0 / V_ROWS)) * V_ROWS
    return jnp.where(within == MLA_V_DIM, 1.0, 0.0).astype(F32)


def _front_kernel(x_ref, gattn_ref, w1_ref, wvt_ref, bg_ref, qng_ref, wuq_ref, kvng_ref, wkv_ref,
                  wkvt_ref, cos_ref, sin_ref,
                  qa_ref, ka_ref, kmean_ref, vat_ref, qm_ref, km_ref, vmt_ref, g_ref,
                  *, mla_scale):
    tm = x_ref.shape[0]
    xn = _rms(x_ref[...], gattn_ref[...]).astype(BF16)

    def proj(a, b):
        return _dot(xn, w1_ref[:, a:b])

    qa_ref[...] = (proj(_C_QA, _C_KA) * (MOBA_HEAD_DIM ** -0.5 * LOG2E)).astype(BF16)
    ka = proj(_C_KA, _C_CQ)
    ka_ref[...] = ka.astype(BF16)
    for i in range(tm // MOBA_BLOCK):
        kmean_ref[i] = jnp.mean(ka[i * MOBA_BLOCK:(i + 1) * MOBA_BLOCK], axis=0, keepdims=True)
    ones = _ones_rows(vat_ref.shape[0])
    vat_ref[...] = (_dot_nt(wvt_ref[...], xn) + ones).astype(BF16)

    cosm = cos_ref[...]
    sinm = sin_ref[...]
    lane = lax.broadcasted_iota(jnp.int32, (tm, LANES), 1)
    half = MLA_ROPE_DIM // 2

    def rope(a):
        x2_on_x1 = pltpu.roll(a, LANES - half, axis=1)
        x1_on_x2 = pltpu.roll(a, half, axis=1)
        swapped = jnp.where(lane < MLA_NOPE_DIM + half, x2_on_x1, x1_on_x2)
        return a * cosm + swapped * sinm

    cqn = _rms(proj(_C_CQ, _C_CKV), qng_ref[...]).astype(BF16)
    qq = _dot(cqn, wuq_ref[...])
    for h in range(MLA_HEADS):
        qm_ref[:, h * LANES:(h + 1) * LANES] = (
            rope(qq[:, h * LANES:(h + 1) * LANES]) * mla_scale).astype(BF16)

    ckv_kra = proj(_C_CKV, _C_G)
    ckvn = _rms(ckv_kra[:, :MLA_KV_RANK], kvng_ref[...]).astype(BF16)
    kv = _dot(ckvn, wkv_ref[...])
    kr = rope(ckv_kra[:, MLA_KV_RANK:])
    for h in range(MLA_HEADS):
        km_ref[:, h * LANES:(h + 1) * LANES] = (kv[:, h * LANES:(h + 1) * LANES] + kr).astype(BF16)
    vmt_ref[...] = (_dot_nt(wkvt_ref[...], ckvn) + ones).astype(BF16)

    g_ref[...] = _sigmoid(proj(_C_G, w1_ref.shape[1]) + bg_ref[...]).astype(BF16)


def _front(x2, gattn, w1, wvt, bg, qng, wuq, kvng, wkv, wkvt, cosm, sinm, *, seq, tm):
    n, d = x2.shape
    batch = n // seq
    hw = MLA_HEADS * LANES
    nblk = n // MOBA_BLOCK
    steps_per_seq = seq // tm
    row = lambda i: (i, 0)
    tcol = lambda i: (i // steps_per_seq, 0, i % steps_per_seq)
    out_shape = (
        jax.ShapeDtypeStruct((n, MOBA_WIDTH), BF16),
        jax.ShapeDtypeStruct((n, MOBA_WIDTH), BF16),
        jax.ShapeDtypeStruct((nblk, 1, MOBA_WIDTH), F32),
        jax.ShapeDtypeStruct((batch, MOBA_HEADS * V_ROWS, seq), BF16),
        jax.ShapeDtypeStruct((n, hw), BF16),
        jax.ShapeDtypeStruct((n, hw), BF16),
        jax.ShapeDtypeStruct((batch, MLA_HEADS * V_ROWS, seq), BF16),
        jax.ShapeDtypeStruct((n, N_BRANCH * d), BF16),
    )
    in_specs = [
        pl.BlockSpec((tm, d), row),
        _resident(gattn.shape), _resident(w1.shape), _resident(wvt.shape), _resident(bg.shape),
        _resident(qng.shape), _resident(wuq.shape), _resident(kvng.shape), _resident(wkv.shape),
        _resident(wkvt.shape),
        pl.BlockSpec((tm, LANES), lambda i: (i % steps_per_seq, 0)),
        pl.BlockSpec((tm, LANES), lambda i: (i % steps_per_seq, 0)),
    ]
    out_specs = (
        pl.BlockSpec((tm, MOBA_WIDTH), row),
        pl.BlockSpec((tm, MOBA_WIDTH), row),
        pl.BlockSpec((tm // MOBA_BLOCK, 1, MOBA_WIDTH), lambda i: (i, 0, 0)),
        pl.BlockSpec((None, MOBA_HEADS * V_ROWS, tm), tcol),
        pl.BlockSpec((tm, hw), row),
        pl.BlockSpec((tm, hw), row),
        pl.BlockSpec((None, MLA_HEADS * V_ROWS, tm), tcol),
        pl.BlockSpec((tm, N_BRANCH * d), row),
    )
    return pl.pallas_call(
        functools.partial(_front_kernel, mla_scale=MLA_QK_DIM ** -0.5 * LOG2E),
        grid=(n // tm,),
        in_specs=in_specs,
        out_specs=out_specs,
        out_shape=out_shape,
        compiler_params=pltpu.CompilerParams(
            dimension_semantics=("arbitrary",), vmem_limit_bytes=VMEM_LIMIT_BYTES),
        name="front",
    )(x2, gattn, w1, wvt, bg, qng, wuq, kvng, wkv, wkvt, cosm, sinm)


HEADS_PER_STEP = 8
BIG = 1e30
RESCALE_SLACK = 64.0
MATMUL_LOOKAHEAD = 2


def _online_softmax_group(m_ref, acc_ref, score_blocks, vt_grp, chosen=None, shifts=None):
    n = len(score_blocks)
    chosen = chosen or [None] * n
    shifts = shifts or [None] * n
    m_old = m_ref[...]
    m_new = m_old
    for st, ch, sh in zip(score_blocks, chosen, shifts):
        rm = jnp.max(st, axis=0, keepdims=True)
        if sh is not None:
            rm = rm + sh
        m_new = jnp.maximum(m_new, rm if ch is None else jnp.where(ch, rm, NEG))
    probs = []
    for st, ch, sh in zip(score_blocks, chosen, shifts):
        off = m_new if sh is None else m_new - sh
        if ch is not None:
            off = jnp.where(ch, off, BIG)
        probs.append(jnp.exp2(st - off).astype(BF16))
    ot = _dot(vt_grp, jnp.concatenate(probs, axis=0))
    acc_ref[...] = jnp.exp2(m_old - m_new) * acc_ref[...] + ot
    m_ref[...] = m_new


def _streamed_softmax_group(m_ref, acc_ref, late_ref, score_blocks, vt_grp, chosen=None, shifts=None):
    n = len(score_blocks)
    chosen = chosen or [None] * n
    shifts = shifts or [None] * n
    m_old = m_ref[...]
    seen = jnp.full(m_old.shape, NEG, F32)
    probs = []
    for st, ch, sh in zip(score_blocks, chosen, shifts):
        off = m_old if sh is None else m_old - sh
        if ch is not None:
            off = jnp.where(ch, off, BIG)
        probs.append(jnp.exp2(st - off).astype(BF16))
        rm = jnp.max(st, axis=0, keepdims=True)
        if sh is not None:
            rm = rm + sh
        seen = jnp.maximum(seen, rm if ch is None else jnp.where(ch, rm, NEG))
    acc_ref[...] = acc_ref[...] + _dot(vt_grp, jnp.concatenate(probs, axis=0))
    late_ref[...] = jnp.where(seen > m_old + RESCALE_SLACK, 1.0, late_ref[...])


def _pipelined(stages, issue, consume, lookahead):
    queue = []
    for n in range(len(stages) + lookahead):
        if n < len(stages):
            queue.append((stages[n], issue(stages[n])))
        if n >= lookahead:
            consume(*queue.pop(0))


def _in_trips(n, sweep, trip_groups):
    def trip(t, carry):
        sweep([trip_groups * t + i for i in range(trip_groups)])
        return carry

    lax.fori_loop(0, n // trip_groups, trip, 0)
    done = (n // trip_groups) * trip_groups
    size = trip_groups // 2
    while size >= 1:
        has = ((n - done) // size) % 2 == 1
        first = done + ((n - done) // (2 * size)) * (2 * size)
        pl.when(has)(functools.partial(sweep, [first + i for i in range(size)]))
        size //= 2


def _reset_softmax_state(m_s, acc_s):
    m_s[...] = jnp.full(m_s.shape, NEG, F32)
    acc_s[...] = jnp.zeros(acc_s.shape, F32)


def _normalised_heads(acc_s, v_dim):
    rows = []
    for hh in range(HEADS_PER_STEP):
        acc = acc_s[hh]
        rows.append(acc[0:v_dim, :] * (1.0 / acc[v_dim:v_dim + 1, :]))
    return jnp.concatenate(rows, axis=0)


def _moba_kernel(q_ref, k_new_ref, vt_new_ref, kmean_ref, near_ref, cfar_ref, o_ref,
                 m_s, acc_s, late_s, sel_s, k_ref, vt_ref, *, unroll, trip):
    blk = MOBA_BLOCK
    qb = pl.program_id(2)
    nb = kmean_ref.shape[0]
    lane = lax.broadcasted_iota(jnp.int32, (blk, LANES), 1)
    blk_id = lax.broadcasted_iota(jnp.int32, (nb, blk), 0).astype(F32)
    qbf = qb.astype(F32)
    has_prev = qb >= 1
    _reset_softmax_state(m_s, acc_s)

    @pl.when(qb == 0)
    def _():
        k_ref[...] = jnp.zeros_like(k_ref)
        vt_ref[...] = jnp.zeros_like(vt_ref)

    own = pl.ds(pl.multiple_of(qb * blk, blk), blk)
    k_ref[own, :] = k_new_ref[...]
    vt_ref[:, own] = vt_new_ref[...]

    def lane_group(hh):
        return slice((hh // 2) * LANES, (hh // 2 + 1) * LANES)

    def k_block(hh, j):
        return k_ref[pl.ds(pl.multiple_of(j * blk, blk), blk), lane_group(hh)]

    def vt_blocks(hh, j, n):
        return vt_ref[hh * V_ROWS:(hh + 1) * V_ROWS, pl.ds(pl.multiple_of(j * blk, blk), n * blk)]

    q_heads, prev_chosen = [], []
    for hh in range(HEADS_PER_STEP):
        in_head = (lane >= (hh % 2) * MOBA_HEAD_DIM) & (lane < (hh % 2 + 1) * MOBA_HEAD_DIM)
        qh = jnp.where(in_head, q_ref[:, lane_group(hh)].astype(F32), 0.0).astype(BF16)
        q_heads.append(qh)

        kmean = kmean_ref[:, lane_group(hh)].astype(BF16)
        gate = jnp.where(blk_id < qbf, _dot_nt(kmean, qh), -jnp.inf)
        sel = jnp.zeros((nb, blk), F32)
        for _ in range(min(MOBA_TOPK, nb)):
            best = jnp.max(gate, axis=0, keepdims=True)
            first = jnp.min(jnp.where(gate == best, blk_id, float(nb)), axis=0, keepdims=True)
            pick = blk_id == first
            sel = jnp.where(pick, 1.0, sel)
            gate = jnp.where(pick, -jnp.inf, gate)
        prev_chosen.append(jnp.max(jnp.where(blk_id == qbf - 1.0, sel, 0.0), axis=0, keepdims=True))
        sel_s[hh] = jnp.where(blk_id < qbf - 1.0, sel, 0.0)

    lo = jnp.maximum(qb - 1, 0)
    tab = jnp.where(has_prev, 0, 1)

    def nearest_exact():
        scores = [[_dot_nt(k_block(hh, lo + u), q_heads[hh]) + near_ref[hh, tab + u]
                   for u in range(2)] for hh in range(HEADS_PER_STEP)]
        for hh in range(HEADS_PER_STEP):
            chosen = [jnp.where(has_prev, prev_chosen[hh], 1.0) > 0.5,
                      jnp.where(has_prev, jnp.ones((1, blk), F32), 0.0) > 0.5]
            _online_softmax_group(m_s.at[hh], acc_s.at[hh], scores[hh], vt_blocks(hh, lo, 2), chosen)

    n_groups = (qb - 1 + unroll - 1) // unroll

    def far_sweep(groups, streamed=True):
        def start_of(g):
            return jnp.minimum(g * unroll, nb - unroll)

        def scores(stage):
            g, hh = stage
            keys = k_ref[pl.ds(pl.multiple_of(start_of(g) * blk, blk), unroll * blk), lane_group(hh)]
            st = _dot_nt(keys, q_heads[hh])
            return [st[u * blk:(u + 1) * blk, :] for u in range(unroll)]

        def softmax(stage, blocks):
            g, hh = stage
            start = start_of(g)
            chosen = [jnp.where(start + u >= g * unroll, sel_s[hh, pl.ds(start + u, 1), :], 0.0) > 0.5
                      for u in range(unroll)]
            operands = (blocks, vt_blocks(hh, start, unroll), chosen, [cfar_ref[hh][:, 0:1]] * unroll)
            if streamed:
                _streamed_softmax_group(m_s.at[hh], acc_s.at[hh], late_s.at[hh], *operands)
            else:
                _online_softmax_group(m_s.at[hh], acc_s.at[hh], *operands)

        stages = [(g, hh) for g in groups for hh in range(HEADS_PER_STEP)]
        _pipelined(stages, scores, softmax, MATMUL_LOOKAHEAD if streamed else len(stages))

    late_s[...] = jnp.zeros_like(late_s)
    nearest_exact()
    _in_trips(n_groups, far_sweep, trip)

    @pl.when(jnp.max(late_s[...]) > 0.5)
    def _():
        _reset_softmax_state(m_s, acc_s)
        nearest_exact()

        def exact(g, carry):
            far_sweep([g], streamed=False)
            return carry

        lax.fori_loop(0, n_groups, exact, 0)

    o_ref[...] = _normalised_heads(acc_s, MOBA_HEAD_DIM).T.astype(o_ref.dtype)


def _moba(qa, ka, vat, kmean, near, cfar, *, batch, seq, unroll, trip):
    n = qa.shape[0]
    blk = MOBA_BLOCK
    nb = seq // blk
    assert nb >= unroll and nb >= 2
    steps = MOBA_HEADS // HEADS_PER_STEP
    width = HEADS_PER_STEP * MOBA_HEAD_DIM
    return pl.pallas_call(
        functools.partial(_moba_kernel, unroll=unroll, trip=trip),
        grid=(batch, steps, nb),
        in_specs=[
            pl.BlockSpec((blk, width), lambda b, p, i: (b * nb + i, p)),
            pl.BlockSpec((blk, width), lambda b, p, i: (b * nb + i, p)),
            pl.BlockSpec((None, HEADS_PER_STEP * V_ROWS, blk), lambda b, p, i: (b, p, i)),
            pl.BlockSpec((None, nb, width), lambda b, p, i: (b, 0, p)),
            _per_head_group((HEADS_PER_STEP, 3, blk, blk), lambda b, p, i: (p, 0, 0, 0)),
            pl.BlockSpec((HEADS_PER_STEP, 1, LANES), lambda b, p, i: (p, 0, 0)),
        ],
        out_specs=pl.BlockSpec((blk, width), lambda b, p, i: (b * nb + i, p)),
        out_shape=jax.ShapeDtypeStruct((n, MOBA_WIDTH), BF16),
        scratch_shapes=[
            pltpu.VMEM((HEADS_PER_STEP, 1, blk), F32),
            pltpu.VMEM((HEADS_PER_STEP, V_ROWS, blk), F32),
            pltpu.VMEM((HEADS_PER_STEP, 1, blk), F32),
            pltpu.VMEM((HEADS_PER_STEP, nb, blk), F32),
            pltpu.VMEM((seq, width), BF16),
            pltpu.VMEM((HEADS_PER_STEP * V_ROWS, seq), BF16),
        ],
        compiler_params=pltpu.CompilerParams(
            dimension_semantics=("arbitrary", "arbitrary", "arbitrary"),
            vmem_limit_bytes=VMEM_LIMIT_BYTES),
        name="moba",
    )(qa, ka, vat, kmean, near, cfar)


def _mla_kernel(q_ref, k_new_ref, vt_new_ref, o_ref, m_s, acc_s, late_s, k_ref, vt_ref,
                *, tq, tk, unroll, trip):
    qi = pl.program_id(2)
    n_diag = tq // tk
    key = lax.broadcasted_iota(jnp.int32, (tk, tq), 0)
    qry = lax.broadcasted_iota(jnp.int32, (tk, tq), 1)
    q_heads = [q_ref[:, hh * LANES:(hh + 1) * LANES] for hh in range(HEADS_PER_STEP)]

    own = pl.ds(pl.multiple_of(qi * tq, tq), tq)
    k_ref[own, :] = k_new_ref[...]
    vt_ref[:, own] = vt_new_ref[...]

    _reset_softmax_state(m_s, acc_s)

    def k_tile(hh, j):
        return k_ref[pl.ds(pl.multiple_of(j * tk, tk), tk), hh * LANES:(hh + 1) * LANES]

    def vt_tiles(hh, j, n):
        return vt_ref[hh * V_ROWS:(hh + 1) * V_ROWS, pl.ds(pl.multiple_of(j * tk, tk), n * tk)]

    def group(first, n, masked):
        scores = [[_dot_nt(k_tile(hh, first + u), q_heads[hh]) for u in range(n)]
                  for hh in range(HEADS_PER_STEP)]
        for hh in range(HEADS_PER_STEP):
            blocks = scores[hh]
            if masked:
                blocks = [jnp.where(key + u * tk <= qry, st, NEG) for u, st in enumerate(blocks)]
            _online_softmax_group(m_s.at[hh], acc_s.at[hh], blocks, vt_tiles(hh, first, n))

    n_groups = (qi * n_diag) // unroll

    def visible_sweep(groups):
        def scores(stage):
            g, hh = stage
            start = pl.multiple_of(g * unroll * tk, tk)
            st = _dot_nt(k_ref[pl.ds(start, unroll * tk), hh * LANES:(hh + 1) * LANES], q_heads[hh])
            return [st[u * tk:(u + 1) * tk, :] for u in range(unroll)]

        def softmax(stage, blocks):
            g, hh = stage
            _streamed_softmax_group(m_s.at[hh], acc_s.at[hh], late_s.at[hh], blocks,
                                    vt_tiles(hh, g * unroll, unroll))

        stages = [(g, hh) for g in groups for hh in range(HEADS_PER_STEP)]
        _pipelined(stages, scores, softmax, MATMUL_LOOKAHEAD)

    late_s[...] = jnp.zeros_like(late_s)
    group(qi * n_diag, n_diag, True)
    _in_trips(n_groups, visible_sweep, trip)

    @pl.when(jnp.max(late_s[...]) > 0.5)
    def _():
        _reset_softmax_state(m_s, acc_s)
        group(qi * n_diag, n_diag, True)

        def exact(g, carry):
            group(g * unroll, unroll, False)
            return carry

        lax.fori_loop(0, n_groups, exact, 0)

    o_ref[...] = _normalised_heads(acc_s, MLA_V_DIM).T.astype(o_ref.dtype)


def _mla(qm, km, vmt, *, batch, seq, tq, tk, unroll, trip):
    n = qm.shape[0]
    nq = seq // tq
    assert tq % tk == 0 and (tq // tk) % unroll == 0
    pairs = MLA_HEADS // HEADS_PER_STEP
    return pl.pallas_call(
        functools.partial(_mla_kernel, tq=tq, tk=tk, unroll=unroll, trip=trip),
        grid=(batch, pairs, nq),
        in_specs=[
            pl.BlockSpec((tq, HEADS_PER_STEP * LANES), lambda b, p, i: (b * nq + i, p)),
            pl.BlockSpec((tq, HEADS_PER_STEP * LANES), lambda b, p, i: (b * nq + i, p)),
            pl.BlockSpec((None, HEADS_PER_STEP * V_ROWS, tq), lambda b, p, i: (b, p, i)),
        ],
        out_specs=pl.BlockSpec((tq, HEADS_PER_STEP * MLA_V_DIM), lambda b, p, i: (b * nq + i, p)),
        out_shape=jax.ShapeDtypeStruct((n, MLA_WIDTH), BF16),
        scratch_shapes=[
            pltpu.VMEM((HEADS_PER_STEP, 1, tq), F32),
            pltpu.VMEM((HEADS_PER_STEP, V_ROWS, tq), F32),
            pltpu.VMEM((HEADS_PER_STEP, 1, tq), F32),
            pltpu.VMEM((seq, HEADS_PER_STEP * LANES), BF16),
            pltpu.VMEM((HEADS_PER_STEP * V_ROWS, seq), BF16),
        ],
        compiler_params=pltpu.CompilerParams(
            dimension_semantics=("arbitrary", "arbitrary", "arbitrary"),
            vmem_limit_bytes=VMEM_LIMIT_BYTES),
        name="mla",
    )(qm, km, vmt)


def _back_kernel(x_ref, ya_ref, yb_ref, g_ref, wa_ref, wb_ref, wo_ref, gffn_ref, wup_ref,
                 cw_ref, cb_ref, wdn_ref, gfin_ref, o_ref,
                 carry_ref, act_ref, hn_ref, *, steps_per_seq, d_ff, fc, final):
    tm, d = x_ref.shape
    halo = SUBLANES

    @pl.when(pl.program_id(0) % steps_per_seq == 0)
    def _():
        carry_ref[...] = jnp.zeros_like(carry_ref)

    g = g_ref[...].astype(F32)
    mixed = g[:, :d] * _dot(ya_ref[...], wa_ref[...]) + g[:, d:] * _dot(yb_ref[...], wb_ref[...])
    h1 = x_ref[...] + _dot(mixed.astype(BF16), wo_ref[...])
    hn_ref[...] = _rms(h1, gffn_ref[...]).astype(BF16)

    def up_conv(col0):
        cols = slice(col0, col0 + fc)
        u = _dot(hn_ref[...], wup_ref[:, cols])
        rows = jnp.concatenate([carry_ref[:, cols], u], axis=0)
        carry_ref[:, cols] = u[tm - halo:tm, :]
        w = cw_ref[:, cols]
        y = cb_ref[:, cols]
        for t in range(CONV_WIDTH):
            back = CONV_WIDTH - 1 - t
            shifted = pltpu.roll(rows, back, axis=0) if back else rows
            y = y + w[t:t + 1, :] * shifted[halo:, :]
        return y

    for c in range(d_ff // fc):
        yg = up_conv(c * fc)
        yv = up_conv(d_ff + c * fc)
        act_ref[:, c * fc:(c + 1) * fc] = (yg * _sigmoid(yg) * yv).astype(BF16)

    h2 = h1 + _dot(act_ref[...], wdn_ref[...])
    o_ref[...] = _rms(h2, gfin_ref[...]) if final else h2


def _back(x2, ya, yb, g, wa, wb, wo, gffn, wup, cw, cb, wdn, gfin, *, seq, tm, fc, final):
    n, d = x2.shape
    d_ff = wdn.shape[0]
    row = lambda i: (i, 0)
    return pl.pallas_call(
        functools.partial(_back_kernel, steps_per_seq=seq // tm, d_ff=d_ff, fc=fc, final=final),
        grid=(n // tm,),
        in_specs=[
            pl.BlockSpec((tm, d), row),
            pl.BlockSpec((tm, ya.shape[1]), row),
            pl.BlockSpec((tm, yb.shape[1]), row),
            pl.BlockSpec((tm, g.shape[1]), row),
            _resident(wa.shape), _resident(wb.shape), _resident(wo.shape), _resident(gffn.shape),
            _resident(wup.shape), _resident(cw.shape), _resident(cb.shape), _resident(wdn.shape),
            _resident(gfin.shape),
        ],
        out_specs=pl.BlockSpec((tm, d), row),
        out_shape=jax.ShapeDtypeStruct((n, d), F32),
        scratch_shapes=[
            pltpu.VMEM((SUBLANES, 2 * d_ff), F32),
            pltpu.VMEM((tm, d_ff), BF16),
            pltpu.VMEM((tm, d), BF16),
        ],
        compiler_params=pltpu.CompilerParams(
            dimension_semantics=("arbitrary",), vmem_limit_bytes=VMEM_LIMIT_BYTES),
        name="back",
    )(x2, ya, yb, g, wa, wb, wo, gffn, wup, cw, cb, wdn, gfin)


def _t5_bucket_np(rel):
    n = np.maximum(rel, 0)
    max_exact = REL_BUCKETS // 2
    nf = np.maximum(n, 1).astype(np.float32)
    large = max_exact + (np.log(nf / np.float32(max_exact)) / np.float32(math.log(REL_MAX_DIST / max_exact))
                         * np.float32(REL_BUCKETS - max_exact)).astype(np.int32)
    large = np.minimum(large, REL_BUCKETS - 1)
    return np.where(n < max_exact, n, large)


def _moba_bias_tables(rel_bias):
    blk = MOBA_BLOCK
    period = 2 * blk
    assert REL_MAX_DIST <= blk + 1
    bias_h = rel_bias.T.astype(F32)
    slot = np.arange(period, dtype=np.int32)
    dist = np.where(slot < blk, slot, slot - period)

    def lookup(bucket):
        onehot = jnp.asarray(bucket)[None, :] == jnp.arange(REL_BUCKETS)[:, None]
        return jnp.sum(jnp.where(onehot[None], bias_h[:, :, None], 0.0), axis=1)

    def toeplitz(v):
        tiled = jnp.tile(v, (1, blk))[:, :blk * (period - 1)]
        return tiled.reshape(v.shape[0], blk, period - 1)[:, :, :blk]

    town = toeplitz(jnp.where(jnp.asarray(dist >= 0), lookup(_t5_bucket_np(dist)) * LOG2E, NEG))
    tprev = toeplitz(lookup(_t5_bucket_np(dist + blk)) * LOG2E)
    near = jnp.stack([tprev, town, jnp.zeros_like(town)], axis=1)
    cfar = bias_h[:, int(_t5_bucket_np(np.int32(blk + 1)))] * LOG2E
    cfar = jnp.broadcast_to(cfar[:, None, None], (MOBA_HEADS, 1, LANES))
    return near, cfar


def _pad_heads(w, n_heads, width, padded=LANES):
    r = w.shape[0]
    w = w.reshape(r, n_heads, width)
    out = jnp.zeros((r, n_heads, padded), w.dtype).at[:, :, :width].set(w)
    return out.reshape(r, n_heads * padded)


def _rope_slot(w_rope):
    r = w_rope.shape[0]
    return jnp.zeros((r, LANES), w_rope.dtype).at[:, MLA_NOPE_DIM:MLA_NOPE_DIM + MLA_ROPE_DIM].set(w_rope)


def _front_weights(w_in, w_uq, w_ukv):
    d = w_in.shape[0]
    s0 = MOBA_WIDTH
    s1 = 2 * MOBA_WIDTH
    s2 = 3 * MOBA_WIDTH
    s3 = s2 + MLA_Q_RANK
    s4 = s3 + MLA_KV_RANK
    s5 = s4 + MLA_ROPE_DIM
    w1 = jnp.concatenate([
        w_in[:, :s0], w_in[:, s0:s1],
        w_in[:, s2:s3], w_in[:, s3:s4],
        _rope_slot(w_in[:, s4:s5]),
        w_in[:, s5:],
    ], axis=1).astype(BF16)
    assert w1.shape == (d, _C_G + N_BRANCH * d)
    wvt = _pad_heads(w_in[:, s1:s2], MOBA_HEADS, MOBA_HEAD_DIM, V_ROWS).T.astype(BF16)

    wuq = _pad_heads(w_uq, MLA_HEADS, MLA_QK_DIM).astype(BF16)

    r = w_ukv.shape[0]
    ukv = w_ukv.reshape(r, MLA_HEADS, MLA_NOPE_DIM + MLA_V_DIM)
    wk = _pad_heads(ukv[:, :, :MLA_NOPE_DIM].reshape(r, -1), MLA_HEADS, MLA_NOPE_DIM).astype(BF16)
    wkvt = _pad_heads(ukv[:, :, MLA_NOPE_DIM:].reshape(r, -1), MLA_HEADS, MLA_V_DIM, V_ROWS).T.astype(BF16)
    return w1, wvt, wuq, wk, wkvt


def _rope_lane_tables(seq):
    dim = MLA_ROPE_DIM
    inv_freq = ROPE_THETA ** (-jnp.arange(0, dim, 2, dtype=F32) / dim)
    ang = jnp.arange(seq, dtype=F32)[:, None] * inv_freq[None, :]
    cos, sin = jnp.cos(ang), jnp.sin(ang)
    tail = jnp.zeros((seq, LANES - MLA_QK_DIM), F32)
    cosm = jnp.concatenate([jnp.ones((seq, MLA_NOPE_DIM), F32), cos, cos, tail], axis=1)
    sinm = jnp.concatenate([jnp.zeros((seq, MLA_NOPE_DIM), F32), -sin, sin, tail], axis=1)
    return cosm, sinm


class _Tiles:
    def __init__(self, seq):
        self.tm = 512 if seq % 512 == 0 else MOBA_BLOCK
        self.fc = 256
        self.mla_tq = self.tm
        self.mla_tk = 256
        self.mla_unroll = 1
        self.mla_trip = 8
        self.moba_unroll = 2
        self.moba_trip = 8


def kernel(x, norm_attn_g, w_in, b_gate, q_norm_g, w_uq, kv_norm_g, w_ukv, rel_bias,
           w_branch_moba, w_branch_mla, w_out, norm_ffn_g, w_up, conv_w, conv_b, w_down,
           norm_final_g):
    batch, seq, d = x.shape
    depth = w_in.shape[0]
    assert seq % MOBA_BLOCK == 0
    t = _Tiles(seq)
    d_ff = w_down.shape[1]
    assert d_ff % t.fc == 0
    n = batch * seq

    cosm, sinm = _rope_lane_tables(seq)
    near, cfar = _moba_bias_tables(rel_bias)
    row = lambda v: v.reshape(1, -1).astype(F32)

    h = x.reshape(n, d)
    for l in range(depth):
        w1, wvt, wuq, wk, wkvt = _front_weights(w_in[l], w_uq[l], w_ukv[l])
        qa, ka, kmean, vat, qm, km, vmt, g = _front(
            h, row(norm_attn_g[l]), w1, wvt, row(b_gate[l]), row(q_norm_g[l]), wuq,
            row(kv_norm_g[l]), wk, wkvt, cosm, sinm, seq=seq, tm=t.tm)
        kmean = kmean.reshape(batch, seq // MOBA_BLOCK, MOBA_WIDTH)
        ya = _moba(qa, ka, vat, kmean, near, cfar, batch=batch, seq=seq,
                   unroll=t.moba_unroll, trip=t.moba_trip)
        yb = _mla(qm, km, vmt, batch=batch, seq=seq, tq=t.mla_tq, tk=t.mla_tk,
                  unroll=t.mla_unroll, trip=t.mla_trip)
        h = _back(h, ya, yb, g, w_branch_moba[l].astype(BF16), w_branch_mla[l].astype(BF16),
                  w_out[l].astype(BF16), row(norm_ffn_g[l]), w_up[l].astype(BF16),
                  conv_w[l].astype(F32), row(conv_b[l]), w_down[l].astype(BF16),
                  row(norm_final_g), seq=seq, tm=t.tm, fc=t.fc, final=(l == depth - 1))
    return h.reshape(batch, seq, d)
```

```python
import functools
import math

import numpy as np
import jax
import jax.numpy as jnp
from jax import lax
from jax.experimental import pallas as pl
from jax.experimental.pallas import tpu as pltpu

MOBA_HEADS = 8
MOBA_HEAD_DIM = 64
MOBA_BLOCK = 256
MOBA_TOPK = 3
MLA_HEADS = 8
MLA_Q_RANK = 256
MLA_KV_RANK = 128
MLA_NOPE_DIM = 64
MLA_ROPE_DIM = 32
MLA_V_DIM = 64
ROPE_THETA = 10000.0
REL_BUCKETS = 32
REL_MAX_DIST = 128
CONV_WIDTH = 3
N_BRANCH = 2
EPS = 1e-6

MOBA_WIDTH = MOBA_HEADS * MOBA_HEAD_DIM
MLA_QK_DIM = MLA_NOPE_DIM + MLA_ROPE_DIM
MLA_WIDTH = MLA_HEADS * MLA_V_DIM

LANES = 128
SUBLANES = 8
VMEM_LIMIT_BYTES = 56 * 1024 * 1024

NEG = -1e30
LOG2E = math.log2(math.e)

F32 = jnp.float32
BF16 = jnp.bfloat16


def _dot(a, b):
    return jnp.dot(a, b, preferred_element_type=F32)


def _dot_nt(a, b):
    return lax.dot_general(a, b, (((1,), (1,)), ((), ())), preferred_element_type=F32)


def _rms(x, g):
    return x * lax.rsqrt(jnp.mean(x * x, axis=-1, keepdims=True) + EPS) * g


def _sigmoid(z):
    return 1.0 / (1.0 + jnp.exp(-z))


def _resident(shape):
    nd = len(shape)
    return pl.BlockSpec(shape, lambda *_: (0,) * nd, pipeline_mode=pl.Buffered(1))


def _per_head_group(shape, index_map):
    return pl.BlockSpec(shape, index_map, pipeline_mode=pl.Buffered(1))


_C_QA = 0
_C_KA = _C_QA + MOBA_WIDTH
_C_CQ = _C_KA + MOBA_WIDTH
_C_CKV = _C_CQ + MLA_Q_RANK
_C_KRA = _C_CKV + MLA_KV_RANK
_C_G = _C_KRA + LANES

BF16_SUBLANES = 16
V_ROWS = MLA_V_DIM + BF16_SUBLANES
assert MOBA_HEAD_DIM == MLA_V_DIM


def _ones_rows(n_rows):
    r = lax.broadcasted_iota(jnp.int32, (n_rows, 1), 0).astype(F32)
    within = r - jnp.floor((r + 0.5) * (1.0 / V_ROWS)) * V_ROWS
    return jnp.where(within == MLA_V_DIM, 1.0, 0.0).astype(F32)


def _front_kernel(x_ref, gattn_ref, w1_ref, wvt_ref, bg_ref, qng_ref, wuq_ref, kvng_ref, wkv_ref,
                  wkvt_ref, cos_ref, sin_ref,
                  qa_ref, ka_ref, kmean_ref, vat_ref, qm_ref, km_ref, vmt_ref, g_ref,
                  *, mla_scale):
    tm = x_ref.shape[0]
    xn = _rms(x_ref[...], gattn_ref[...]).astype(BF16)

    def proj(a, b):
        return _dot(xn, w1_ref[:, a:b])

    qa_ref[...] = (proj(_C_QA, _C_KA) * (MOBA_HEAD_DIM ** -0.5 * LOG2E)).astype(BF16)
    ka = proj(_C_KA, _C_CQ)
    ka_ref[...] = ka.astype(BF16)
    for i in range(tm // MOBA_BLOCK):
        kmean_ref[i] = jnp.mean(ka[i * MOBA_BLOCK:(i + 1) * MOBA_BLOCK], axis=0, keepdims=True)
    ones = _ones_rows(vat_ref.shape[0])
    vat_ref[...] = (_dot_nt(wvt_ref[...], xn) + ones).astype(BF16)

    cosm = cos_ref[...]
    sinm = sin_ref[...]
    lane = lax.broadcasted_iota(jnp.int32, (tm, LANES), 1)
    half = MLA_ROPE_DIM // 2

    def rope(a):
        x2_on_x1 = pltpu.roll(a, LANES - half, axis=1)
        x1_on_x2 = pltpu.roll(a, half, axis=1)
        swapped = jnp.where(lane < MLA_NOPE_DIM + half, x2_on_x1, x1_on_x2)
        return a * cosm + swapped * sinm

    cqn = _rms(proj(_C_CQ, _C_CKV), qng_ref[...]).astype(BF16)
    qq = _dot(cqn, wuq_ref[...])
    for h in range(MLA_HEADS):
        qm_ref[:, h * LANES:(h + 1) * LANES] = (
            rope(qq[:, h * LANES:(h + 1) * LANES]) * mla_scale).astype(BF16)

    ckv_kra = proj(_C_CKV, _C_G)
    ckvn = _rms(ckv_kra[:, :MLA_KV_RANK], kvng_ref[...]).astype(BF16)
    kv = _dot(ckvn, wkv_ref[...])
    kr = rope(ckv_kra[:, MLA_KV_RANK:])
    for h in range(MLA_HEADS):
        km_ref[:, h * LANES:(h + 1) * LANES] = (kv[:, h * LANES:(h + 1) * LANES] + kr).astype(BF16)
    vmt_ref[...] = (_dot_nt(wkvt_ref[...], ckvn) + ones).astype(BF16)

    g_ref[...] = _sigmoid(proj(_C_G, w1_ref.shape[1]) + bg_ref[...]).astype(BF16)


def _front(x2, gattn, w1, wvt, bg, qng, wuq, kvng, wkv, wkvt, cosm, sinm, *, seq, tm):
    n, d = x2.shape
    batch = n // seq
    hw = MLA_HEADS * LANES
    nblk = n // MOBA_BLOCK
    steps_per_seq = seq // tm
    row = lambda i: (i, 0)
    tcol = lambda i: (i // steps_per_seq, 0, i % steps_per_seq)
    out_shape = (
        jax.ShapeDtypeStruct((n, MOBA_WIDTH), BF16),
        jax.ShapeDtypeStruct((n, MOBA_WIDTH), BF16),
        jax.ShapeDtypeStruct((nblk, 1, MOBA_WIDTH), F32),
        jax.ShapeDtypeStruct((batch, MOBA_HEADS * V_ROWS, seq), BF16),
        jax.ShapeDtypeStruct((n, hw), BF16),
        jax.ShapeDtypeStruct((n, hw), BF16),
        jax.ShapeDtypeStruct((batch, MLA_HEADS * V_ROWS, seq), BF16),
        jax.ShapeDtypeStruct((n, N_BRANCH * d), BF16),
    )
    in_specs = [
        pl.BlockSpec((tm, d), row),
        _resident(gattn.shape), _resident(w1.shape), _resident(wvt.shape), _resident(bg.shape),
        _resident(qng.shape), _resident(wuq.shape), _resident(kvng.shape), _resident(wkv.shape),
        _resident(wkvt.shape),
        pl.BlockSpec((tm, LANES), lambda i: (i % steps_per_seq, 0)),
        pl.BlockSpec((tm, LANES), lambda i: (i % steps_per_seq, 0)),
    ]
    out_specs = (
        pl.BlockSpec((tm, MOBA_WIDTH), row),
        pl.BlockSpec((tm, MOBA_WIDTH), row),
        pl.BlockSpec((tm // MOBA_BLOCK, 1, MOBA_WIDTH), lambda i: (i, 0, 0)),
        pl.BlockSpec((None, MOBA_HEADS * V_ROWS, tm), tcol),
        pl.BlockSpec((tm, hw), row),
        pl.BlockSpec((tm, hw), row),
        pl.BlockSpec((None, MLA_HEADS * V_ROWS, tm), tcol),
        pl.BlockSpec((tm, N_BRANCH * d), row),
    )
    return pl.pallas_call(
        functools.partial(_front_kernel, mla_scale=MLA_QK_DIM ** -0.5 * LOG2E),
        grid=(n // tm,),
        in_specs=in_specs,
        out_specs=out_specs,
        out_shape=out_shape,
        compiler_params=pltpu.CompilerParams(
            dimension_semantics=("arbitrary",), vmem_limit_bytes=VMEM_LIMIT_BYTES),
        name="front",
    )(x2, gattn, w1, wvt, bg, qng, wuq, kvng, wkv, wkvt, cosm, sinm)


HEADS_PER_STEP = 8
BIG = 1e30
MATMUL_LOOKAHEAD = 2


def _online_softmax_group(m_ref, acc_ref, score_blocks, vt_grp, chosen=None, shifts=None):
    n = len(score_blocks)
    chosen = chosen or [None] * n
    shifts = shifts or [None] * n
    m_old = m_ref[...]
    m_new = m_old
    for st, ch, sh in zip(score_blocks, chosen, shifts):
        rm = jnp.max(st, axis=0, keepdims=True)
        if sh is not None:
            rm = rm + sh
        m_new = jnp.maximum(m_new, rm if ch is None else jnp.where(ch, rm, NEG))
    probs = []
    for st, ch, sh in zip(score_blocks, chosen, shifts):
        off = m_new if sh is None else m_new - sh
        if ch is not None:
            off = jnp.where(ch, off, BIG)
        probs.append(jnp.exp2(st - off).astype(BF16))
    ot = _dot(vt_grp, jnp.concatenate(probs, axis=0))
    acc_ref[...] = jnp.exp2(m_old - m_new) * acc_ref[...] + ot
    m_ref[...] = m_new


def _streamed_softmax_group(m_ref, acc_ref, score_blocks, vt_grp, chosen=None, shifts=None):
    n = len(score_blocks)
    chosen = chosen or [None] * n
    shifts = shifts or [None] * n
    m_old = m_ref[...]
    probs = []
    for st, ch, sh in zip(score_blocks, chosen, shifts):
        off = m_old if sh is None else m_old - sh
        if ch is not None:
            off = jnp.where(ch, off, BIG)
        probs.append(jnp.exp2(st - off).astype(BF16))
    acc_ref[...] = acc_ref[...] + _dot(vt_grp, jnp.concatenate(probs, axis=0))


def _overflowed(acc_s):
    return jnp.max(jnp.where(jnp.isfinite(acc_s[...]), 0.0, 1.0)) > 0.5


def _pipelined(stages, issue, consume, lookahead):
    queue = []
    for n in range(len(stages) + lookahead):
        if n < len(stages):
            queue.append((stages[n], issue(stages[n])))
        if n >= lookahead:
            consume(*queue.pop(0))


def _in_trips(n, sweep, trip_groups):
    def trip(t, carry):
        sweep([trip_groups * t + i for i in range(trip_groups)])
        return carry

    lax.fori_loop(0, n // trip_groups, trip, 0)
    done = (n // trip_groups) * trip_groups
    size = trip_groups // 2
    while size >= 1:
        has = ((n - done) // size) % 2 == 1
        first = done + ((n - done) // (2 * size)) * (2 * size)
        pl.when(has)(functools.partial(sweep, [first + i for i in range(size)]))
        size //= 2


def _reset_softmax_state(m_s, acc_s):
    m_s[...] = jnp.full(m_s.shape, NEG, F32)
    acc_s[...] = jnp.zeros(acc_s.shape, F32)


def _normalised_heads(acc_s, v_dim):
    rows = []
    for hh in range(HEADS_PER_STEP):
        acc = acc_s[hh]
        rows.append(acc[0:v_dim, :] * (1.0 / acc[v_dim:v_dim + 1, :]))
    return jnp.concatenate(rows, axis=0)


def _moba_kernel(q_ref, k_new_ref, vt_new_ref, kmean_ref, near_ref, cfar_ref, o_ref,
                 m_s, acc_s, sel_s, k_ref, vt_ref, *, unroll, trip):
    blk = MOBA_BLOCK
    qb = pl.program_id(2)
    nb = kmean_ref.shape[0]
    lane = lax.broadcasted_iota(jnp.int32, (blk, LANES), 1)
    blk_id = lax.broadcasted_iota(jnp.int32, (nb, blk), 0).astype(F32)
    qbf = qb.astype(F32)
    has_prev = qb >= 1
    _reset_softmax_state(m_s, acc_s)

    @pl.when(qb == 0)
    def _():
        k_ref[...] = jnp.zeros_like(k_ref)
        vt_ref[...] = jnp.zeros_like(vt_ref)

    own = pl.ds(pl.multiple_of(qb * blk, blk), blk)
    k_ref[own, :] = k_new_ref[...]
    vt_ref[:, own] = vt_new_ref[...]

    def lane_group(hh):
        return slice((hh // 2) * LANES, (hh // 2 + 1) * LANES)

    def k_block(hh, j):
        return k_ref[pl.ds(pl.multiple_of(j * blk, blk), blk), lane_group(hh)]

    def vt_blocks(hh, j, n):
        return vt_ref[hh * V_ROWS:(hh + 1) * V_ROWS, pl.ds(pl.multiple_of(j * blk, blk), n * blk)]

    q_heads, prev_chosen = [], []
    for hh in range(HEADS_PER_STEP):
        in_head = (lane >= (hh % 2) * MOBA_HEAD_DIM) & (lane < (hh % 2 + 1) * MOBA_HEAD_DIM)
        qh = jnp.where(in_head, q_ref[:, lane_group(hh)].astype(F32), 0.0).astype(BF16)
        q_heads.append(qh)

        kmean = kmean_ref[:, lane_group(hh)].astype(BF16)
        gate = jnp.where(blk_id < qbf, _dot_nt(kmean, qh), -jnp.inf)
        sel = jnp.zeros((nb, blk), F32)
        for _ in range(min(MOBA_TOPK, nb)):
            best = jnp.max(gate, axis=0, keepdims=True)
            first = jnp.min(jnp.where(gate == best, blk_id, float(nb)), axis=0, keepdims=True)
            pick = blk_id == first
            sel = jnp.where(pick, 1.0, sel)
            gate = jnp.where(pick, -jnp.inf, gate)
        prev_chosen.append(jnp.max(jnp.where(blk_id == qbf - 1.0, sel, 0.0), axis=0, keepdims=True))
        sel_s[hh] = jnp.where(blk_id < qbf - 1.0, sel, 0.0)

    lo = jnp.maximum(qb - 1, 0)
    tab = jnp.where(has_prev, 0, 1)

    def nearest_exact():
        scores = [[_dot_nt(k_block(hh, lo + u), q_heads[hh]) + near_ref[hh, tab + u]
                   for u in range(2)] for hh in range(HEADS_PER_STEP)]
        for hh in range(HEADS_PER_STEP):
            chosen = [jnp.where(has_prev, prev_chosen[hh], 1.0) > 0.5,
                      jnp.where(has_prev, jnp.ones((1, blk), F32), 0.0) > 0.5]
            _online_softmax_group(m_s.at[hh], acc_s.at[hh], scores[hh], vt_blocks(hh, lo, 2), chosen)

    n_groups = (qb - 1 + unroll - 1) // unroll

    def far_sweep(groups, streamed=True):
        def start_of(g):
            return jnp.minimum(g * unroll, nb - unroll)

        def scores(stage):
            g, hh = stage
            keys = k_ref[pl.ds(pl.multiple_of(start_of(g) * blk, blk), unroll * blk), lane_group(hh)]
            st = _dot_nt(keys, q_heads[hh])
            return [st[u * blk:(u + 1) * blk, :] for u in range(unroll)]

        def softmax(stage, blocks):
            g, hh = stage
            start = start_of(g)
            chosen = [jnp.where(start + u >= g * unroll, sel_s[hh, pl.ds(start + u, 1), :], 0.0) > 0.5
                      for u in range(unroll)]
            operands = (blocks, vt_blocks(hh, start, unroll), chosen, [cfar_ref[hh][:, 0:1]] * unroll)
            if streamed:
                _streamed_softmax_group(m_s.at[hh], acc_s.at[hh], *operands)
            else:
                _online_softmax_group(m_s.at[hh], acc_s.at[hh], *operands)

        stages = [(g, hh) for g in groups for hh in range(HEADS_PER_STEP)]
        _pipelined(stages, scores, softmax, MATMUL_LOOKAHEAD if streamed else len(stages))

    nearest_exact()
    _in_trips(n_groups, far_sweep, trip)

    @pl.when(_overflowed(acc_s))
    def _():
        _reset_softmax_state(m_s, acc_s)
        nearest_exact()

        def exact(g, carry):
            far_sweep([g], streamed=False)
            return carry

        lax.fori_loop(0, n_groups, exact, 0)

    o_ref[...] = _normalised_heads(acc_s, MOBA_HEAD_DIM).T.astype(o_ref.dtype)


def _moba(qa, ka, vat, kmean, near, cfar, *, batch, seq, unroll, trip):
    n = qa.shape[0]
    blk = MOBA_BLOCK
    nb = seq // blk
    assert nb >= unroll and nb >= 2
    steps = MOBA_HEADS // HEADS_PER_STEP
    width = HEADS_PER_STEP * MOBA_HEAD_DIM
    return pl.pallas_call(
        functools.partial(_moba_kernel, unroll=unroll, trip=trip),
        grid=(batch, steps, nb),
        in_specs=[
            pl.BlockSpec((blk, width), lambda b, p, i: (b * nb + i, p)),
            pl.BlockSpec((blk, width), lambda b, p, i: (b * nb + i, p)),
            pl.BlockSpec((None, HEADS_PER_STEP * V_ROWS, blk), lambda b, p, i: (b, p, i)),
            pl.BlockSpec((None, nb, width), lambda b, p, i: (b, 0, p)),
            _per_head_group((HEADS_PER_STEP, 3, blk, blk), lambda b, p, i: (p, 0, 0, 0)),
            pl.BlockSpec((HEADS_PER_STEP, 1, LANES), lambda b, p, i: (p, 0, 0)),
        ],
        out_specs=pl.BlockSpec((blk, width), lambda b, p, i: (b * nb + i, p)),
        out_shape=jax.ShapeDtypeStruct((n, MOBA_WIDTH), BF16),
        scratch_shapes=[
            pltpu.VMEM((HEADS_PER_STEP, 1, blk), F32),
            pltpu.VMEM((HEADS_PER_STEP, V_ROWS, blk), F32),
            pltpu.VMEM((HEADS_PER_STEP, nb, blk), F32),
            pltpu.VMEM((seq, width), BF16),
            pltpu.VMEM((HEADS_PER_STEP * V_ROWS, seq), BF16),
        ],
        compiler_params=pltpu.CompilerParams(
            dimension_semantics=("arbitrary", "arbitrary", "arbitrary"),
            vmem_limit_bytes=VMEM_LIMIT_BYTES),
        name="moba",
    )(qa, ka, vat, kmean, near, cfar)


def _mla_kernel(q_ref, k_new_ref, vt_new_ref, o_ref, m_s, acc_s, k_ref, vt_ref,
                *, tq, tk, unroll, trip):
    qi = pl.program_id(2)
    n_diag = tq // tk
    key = lax.broadcasted_iota(jnp.int32, (tk, tq), 0)
    qry = lax.broadcasted_iota(jnp.int32, (tk, tq), 1)
    q_heads = [q_ref[:, hh * LANES:(hh + 1) * LANES] for hh in range(HEADS_PER_STEP)]

    own = pl.ds(pl.multiple_of(qi * tq, tq), tq)
    k_ref[own, :] = k_new_ref[...]
    vt_ref[:, own] = vt_new_ref[...]

    _reset_softmax_state(m_s, acc_s)

    def k_tile(hh, j):
        return k_ref[pl.ds(pl.multiple_of(j * tk, tk), tk), hh * LANES:(hh + 1) * LANES]

    def vt_tiles(hh, j, n):
        return vt_ref[hh * V_ROWS:(hh + 1) * V_ROWS, pl.ds(pl.multiple_of(j * tk, tk), n * tk)]

    def group(first, n, masked):
        scores = [[_dot_nt(k_tile(hh, first + u), q_heads[hh]) for u in range(n)]
                  for hh in range(HEADS_PER_STEP)]
        for hh in range(HEADS_PER_STEP):
            blocks = scores[hh]
            if masked:
                blocks = [jnp.where(key + u * tk <= qry, st, NEG) for u, st in enumerate(blocks)]
            _online_softmax_group(m_s.at[hh], acc_s.at[hh], blocks, vt_tiles(hh, first, n))

    n_groups = (qi * n_diag) // unroll

    def visible_sweep(groups):
        def scores(stage):
            g, hh = stage
            start = pl.multiple_of(g * unroll * tk, tk)
            st = _dot_nt(k_ref[pl.ds(start, unroll * tk), hh * LANES:(hh + 1) * LANES], q_heads[hh])
            return [st[u * tk:(u + 1) * tk, :] for u in range(unroll)]

        def softmax(stage, blocks):
            g, hh = stage
            _streamed_softmax_group(m_s.at[hh], acc_s.at[hh], blocks,
                                    vt_tiles(hh, g * unroll, unroll))

        stages = [(g, hh) for g in groups for hh in range(HEADS_PER_STEP)]
        _pipelined(stages, scores, softmax, MATMUL_LOOKAHEAD)

    group(qi * n_diag, n_diag, True)
    _in_trips(n_groups, visible_sweep, trip)

    @pl.when(_overflowed(acc_s))
    def _():
        _reset_softmax_state(m_s, acc_s)
        group(qi * n_diag, n_diag, True)

        def exact(g, carry):
            group(g * unroll, unroll, False)
            return carry

        lax.fori_loop(0, n_groups, exact, 0)

    o_ref[...] = _normalised_heads(acc_s, MLA_V_DIM).T.astype(o_ref.dtype)


def _mla(qm, km, vmt, *, batch, seq, tq, tk, unroll, trip):
    n = qm.shape[0]
    nq = seq // tq
    assert tq % tk == 0 and (tq // tk) % unroll == 0
    pairs = MLA_HEADS // HEADS_PER_STEP
    return pl.pallas_call(
        functools.partial(_mla_kernel, tq=tq, tk=tk, unroll=unroll, trip=trip),
        grid=(batch, pairs, nq),
        in_specs=[
            pl.BlockSpec((tq, HEADS_PER_STEP * LANES), lambda b, p, i: (b * nq + i, p)),
            pl.BlockSpec((tq, HEADS_PER_STEP * LANES), lambda b, p, i: (b * nq + i, p)),
            pl.BlockSpec((None, HEADS_PER_STEP * V_ROWS, tq), lambda b, p, i: (b, p, i)),
        ],
        out_specs=pl.BlockSpec((tq, HEADS_PER_STEP * MLA_V_DIM), lambda b, p, i: (b * nq + i, p)),
        out_shape=jax.ShapeDtypeStruct((n, MLA_WIDTH), BF16),
        scratch_shapes=[
            pltpu.VMEM((HEADS_PER_STEP, 1, tq), F32),
            pltpu.VMEM((HEADS_PER_STEP, V_ROWS, tq), F32),
            pltpu.VMEM((seq, HEADS_PER_STEP * LANES), BF16),
            pltpu.VMEM((HEADS_PER_STEP * V_ROWS, seq), BF16),
        ],
        compiler_params=pltpu.CompilerParams(
            dimension_semantics=("arbitrary", "arbitrary", "arbitrary"),
            vmem_limit_bytes=VMEM_LIMIT_BYTES),
        name="mla",
    )(qm, km, vmt)


def _back_kernel(x_ref, ya_ref, yb_ref, g_ref, wa_ref, wb_ref, wo_ref, gffn_ref, wup_ref,
                 cw_ref, cb_ref, wdn_ref, gfin_ref, o_ref,
                 carry_ref, act_ref, hn_ref, *, steps_per_seq, d_ff, fc, final):
    tm, d = x_ref.shape
    halo = SUBLANES

    @pl.when(pl.program_id(0) % steps_per_seq == 0)
    def _():
        carry_ref[...] = jnp.zeros_like(carry_ref)

    g = g_ref[...].astype(F32)
    mixed = g[:, :d] * _dot(ya_ref[...], wa_ref[...]) + g[:, d:] * _dot(yb_ref[...], wb_ref[...])
    h1 = x_ref[...] + _dot(mixed.astype(BF16), wo_ref[...])
    hn_ref[...] = _rms(h1, gffn_ref[...]).astype(BF16)

    def up_conv(col0):
        cols = slice(col0, col0 + fc)
        u = _dot(hn_ref[...], wup_ref[:, cols])
        rows = jnp.concatenate([carry_ref[:, cols], u], axis=0)
        carry_ref[:, cols] = u[tm - halo:tm, :]
        w = cw_ref[:, cols]
        y = cb_ref[:, cols]
        for t in range(CONV_WIDTH):
            back = CONV_WIDTH - 1 - t
            shifted = pltpu.roll(rows, back, axis=0) if back else rows
            y = y + w[t:t + 1, :] * shifted[halo:, :]
        return y

    for c in range(d_ff // fc):
        yg = up_conv(c * fc)
        yv = up_conv(d_ff + c * fc)
        act_ref[:, c * fc:(c + 1) * fc] = (yg * _sigmoid(yg) * yv).astype(BF16)

    h2 = h1 + _dot(act_ref[...], wdn_ref[...])
    o_ref[...] = _rms(h2, gfin_ref[...]) if final else h2


def _back(x2, ya, yb, g, wa, wb, wo, gffn, wup, cw, cb, wdn, gfin, *, seq, tm, fc, final):
    n, d = x2.shape
    d_ff = wdn.shape[0]
    row = lambda i: (i, 0)
    return pl.pallas_call(
        functools.partial(_back_kernel, steps_per_seq=seq // tm, d_ff=d_ff, fc=fc, final=final),
        grid=(n // tm,),
        in_specs=[
            pl.BlockSpec((tm, d), row),
            pl.BlockSpec((tm, ya.shape[1]), row),
            pl.BlockSpec((tm, yb.shape[1]), row),
            pl.BlockSpec((tm, g.shape[1]), row),
            _resident(wa.shape), _resident(wb.shape), _resident(wo.shape), _resident(gffn.shape),
            _resident(wup.shape), _resident(cw.shape), _resident(cb.shape), _resident(wdn.shape),
            _resident(gfin.shape),
        ],
        out_specs=pl.BlockSpec((tm, d), row),
        out_shape=jax.ShapeDtypeStruct((n, d), F32),
        scratch_shapes=[
            pltpu.VMEM((SUBLANES, 2 * d_ff), F32),
            pltpu.VMEM((tm, d_ff), BF16),
            pltpu.VMEM((tm, d), BF16),
        ],
        compiler_params=pltpu.CompilerParams(
            dimension_semantics=("arbitrary",), vmem_limit_bytes=VMEM_LIMIT_BYTES),
        name="back",
    )(x2, ya, yb, g, wa, wb, wo, gffn, wup, cw, cb, wdn, gfin)


def _t5_bucket_np(rel):
    n = np.maximum(rel, 0)
    max_exact = REL_BUCKETS // 2
    nf = np.maximum(n, 1).astype(np.float32)
    large = max_exact + (np.log(nf / np.float32(max_exact)) / np.float32(math.log(REL_MAX_DIST / max_exact))
                         * np.float32(REL_BUCKETS - max_exact)).astype(np.int32)
    large = np.minimum(large, REL_BUCKETS - 1)
    return np.where(n < max_exact, n, large)


def _moba_bias_tables(rel_bias):
    blk = MOBA_BLOCK
    period = 2 * blk
    assert REL_MAX_DIST <= blk + 1
    bias_h = rel_bias.T.astype(F32)
    slot = np.arange(period, dtype=np.int32)
    dist = np.where(slot < blk, slot, slot - period)

    def lookup(bucket):
        onehot = jnp.asarray(bucket)[None, :] == jnp.arange(REL_BUCKETS)[:, None]
        return jnp.sum(jnp.where(onehot[None], bias_h[:, :, None], 0.0), axis=1)

    def toeplitz(v):
        tiled = jnp.tile(v, (1, blk))[:, :blk * (period - 1)]
        return tiled.reshape(v.shape[0], blk, period - 1)[:, :, :blk]

    town = toeplitz(jnp.where(jnp.asarray(dist >= 0), lookup(_t5_bucket_np(dist)) * LOG2E, NEG))
    tprev = toeplitz(lookup(_t5_bucket_np(dist + blk)) * LOG2E)
    near = jnp.stack([tprev, town, jnp.zeros_like(town)], axis=1)
    cfar = bias_h[:, int(_t5_bucket_np(np.int32(blk + 1)))] * LOG2E
    cfar = jnp.broadcast_to(cfar[:, None, None], (MOBA_HEADS, 1, LANES))
    return near, cfar


def _pad_heads(w, n_heads, width, padded=LANES):
    r = w.shape[0]
    w = w.reshape(r, n_heads, width)
    out = jnp.zeros((r, n_heads, padded), w.dtype).at[:, :, :width].set(w)
    return out.reshape(r, n_heads * padded)


def _rope_slot(w_rope):
    r = w_rope.shape[0]
    return jnp.zeros((r, LANES), w_rope.dtype).at[:, MLA_NOPE_DIM:MLA_NOPE_DIM + MLA_ROPE_DIM].set(w_rope)


def _front_weights(w_in, w_uq, w_ukv):
    d = w_in.shape[0]
    s0 = MOBA_WIDTH
    s1 = 2 * MOBA_WIDTH
    s2 = 3 * MOBA_WIDTH
    s3 = s2 + MLA_Q_RANK
    s4 = s3 + MLA_KV_RANK
    s5 = s4 + MLA_ROPE_DIM
    w1 = jnp.concatenate([
        w_in[:, :s0], w_in[:, s0:s1],
        w_in[:, s2:s3], w_in[:, s3:s4],
        _rope_slot(w_in[:, s4:s5]),
        w_in[:, s5:],
    ], axis=1).astype(BF16)
    assert w1.shape == (d, _C_G + N_BRANCH * d)
    wvt = _pad_heads(w_in[:, s1:s2], MOBA_HEADS, MOBA_HEAD_DIM, V_ROWS).T.astype(BF16)

    wuq = _pad_heads(w_uq, MLA_HEADS, MLA_QK_DIM).astype(BF16)

    r = w_ukv.shape[0]
    ukv = w_ukv.reshape(r, MLA_HEADS, MLA_NOPE_DIM + MLA_V_DIM)
    wk = _pad_heads(ukv[:, :, :MLA_NOPE_DIM].reshape(r, -1), MLA_HEADS, MLA_NOPE_DIM).astype(BF16)
    wkvt = _pad_heads(ukv[:, :, MLA_NOPE_DIM:].reshape(r, -1), MLA_HEADS, MLA_V_DIM, V_ROWS).T.astype(BF16)
    return w1, wvt, wuq, wk, wkvt


def _rope_lane_tables(seq):
    dim = MLA_ROPE_DIM
    inv_freq = ROPE_THETA ** (-jnp.arange(0, dim, 2, dtype=F32) / dim)
    ang = jnp.arange(seq, dtype=F32)[:, None] * inv_freq[None, :]
    cos, sin = jnp.cos(ang), jnp.sin(ang)
    tail = jnp.zeros((seq, LANES - MLA_QK_DIM), F32)
    cosm = jnp.concatenate([jnp.ones((seq, MLA_NOPE_DIM), F32), cos, cos, tail], axis=1)
    sinm = jnp.concatenate([jnp.zeros((seq, MLA_NOPE_DIM), F32), -sin, sin, tail], axis=1)
    return cosm, sinm


class _Tiles:
    def __init__(self, seq):
        self.tm = 512 if seq % 512 == 0 else MOBA_BLOCK
        self.fc = 256
        self.mla_tq = self.tm
        self.mla_tk = 256
        self.mla_unroll = 1
        self.mla_trip = 8
        self.moba_unroll = 2
        self.moba_trip = 8


def kernel(x, norm_attn_g, w_in, b_gate, q_norm_g, w_uq, kv_norm_g, w_ukv, rel_bias,
           w_branch_moba, w_branch_mla, w_out, norm_ffn_g, w_up, conv_w, conv_b, w_down,
           norm_final_g):
    batch, seq, d = x.shape
    depth = w_in.shape[0]
    assert seq % MOBA_BLOCK == 0
    t = _Tiles(seq)
    d_ff = w_down.shape[1]
    assert d_ff % t.fc == 0
    n = batch * seq

    cosm, sinm = _rope_lane_tables(seq)
    near, cfar = _moba_bias_tables(rel_bias)
    row = lambda v: v.reshape(1, -1).astype(F32)

    h = x.reshape(n, d)
    for l in range(depth):
        w1, wvt, wuq, wk, wkvt = _front_weights(w_in[l], w_uq[l], w_ukv[l])
        qa, ka, kmean, vat, qm, km, vmt, g = _front(
            h, row(norm_attn_g[l]), w1, wvt, row(b_gate[l]), row(q_norm_g[l]), wuq,
            row(kv_norm_g[l]), wk, wkvt, cosm, sinm, seq=seq, tm=t.tm)
        kmean = kmean.reshape(batch, seq // MOBA_BLOCK, MOBA_WIDTH)
        ya = _moba(qa, ka, vat, kmean, near, cfar, batch=batch, seq=seq,
                   unroll=t.moba_unroll, trip=t.moba_trip)
        yb = _mla(qm, km, vmt, batch=batch, seq=seq, tq=t.mla_tq, tk=t.mla_tk,
                  unroll=t.mla_unroll, trip=t.mla_trip)
        h = _back(h, ya, yb, g, w_branch_moba[l].astype(BF16), w_branch_mla[l].astype(BF16),
                  w_out[l].astype(BF16), row(norm_ffn_g[l]), w_up[l].astype(BF16),
                  conv_w[l].astype(F32), row(conv_b[l]), w_down[l].astype(BF16),
                  row(norm_final_g), seq=seq, tm=t.tm, fc=t.fc, final=(l == depth - 1))
    return h.reshape(batch, seq, d)
```

```python
import functools
import math

import numpy as np
import jax
import jax.numpy as jnp
from jax import lax
from jax.experimental import pallas as pl
from jax.experimental.pallas import tpu as pltpu

MOBA_HEADS = 8
MOBA_HEAD_DIM = 64
MOBA_BLOCK = 256
MOBA_TOPK = 3
MLA_HEADS = 8
MLA_Q_RANK = 256
MLA_KV_RANK = 128
MLA_NOPE_DIM = 64
MLA_ROPE_DIM = 32
MLA_V_DIM = 64
ROPE_THETA = 10000.0
REL_BUCKETS = 32
REL_MAX_DIST = 128
CONV_WIDTH = 3
N_BRANCH = 2
EPS = 1e-6

MOBA_WIDTH = MOBA_HEADS * MOBA_HEAD_DIM
MLA_QK_DIM = MLA_NOPE_DIM + MLA_ROPE_DIM
MLA_WIDTH = MLA_HEADS * MLA_V_DIM

LANES = 128
SUBLANES = 8
VMEM_LIMIT_BYTES = 56 * 1024 * 1024

NEG = -1e30
LOG2E = math.log2(math.e)

F32 = jnp.float32
BF16 = jnp.bfloat16


def _dot(a, b):
    return jnp.dot(a, b, preferred_element_type=F32)


def _dot_nt(a, b):
    return lax.dot_general(a, b, (((1,), (1,)), ((), ())), preferred_element_type=F32)


def _rms(x, g):
    return x * lax.rsqrt(jnp.mean(x * x, axis=-1, keepdims=True) + EPS) * g


def _sigmoid(z):
    return 1.0 / (1.0 + jnp.exp(-z))


def _resident(shape):
    nd = len(shape)
    return pl.BlockSpec(shape, lambda *_: (0,) * nd, pipeline_mode=pl.Buffered(1))


def _per_head_group(shape, index_map):
    return pl.BlockSpec(shape, index_map, pipeline_mode=pl.Buffered(1))


_C_QA = 0
_C_KA = _C_QA + MOBA_WIDTH
_C_CQ = _C_KA + MOBA_WIDTH
_C_CKV = _C_CQ + MLA_Q_RANK
_C_KRA = _C_CKV + MLA_KV_RANK
_C_G = _C_KRA + LANES

BF16_SUBLANES = 16
V_ROWS = MLA_V_DIM + BF16_SUBLANES
assert MOBA_HEAD_DIM == MLA_V_DIM


def _ones_rows(n_rows):
    r = lax.broadcasted_iota(jnp.int32, (n_rows, 1), 0).astype(F32)
    within = r - jnp.floor((r + 0.5) * (1.0 / V_ROWS)) * V_ROWS
    return jnp.where(within == MLA_V_DIM, 1.0, 0.0).astype(F32)


def _front_kernel(x_ref, gattn_ref, w1_ref, wvt_ref, bg_ref, qng_ref, wuq_ref, kvng_ref, wkv_ref,
                  wkvt_ref, cos_ref, sin_ref,
                  qa_ref, ka_ref, kmean_ref, vat_ref, qm_ref, km_ref, vmt_ref, g_ref,
                  *, mla_scale):
    tm = x_ref.shape[0]
    xn = _rms(x_ref[...], gattn_ref[...]).astype(BF16)

    def proj(a, b):
        return _dot(xn, w1_ref[:, a:b])

    qa_ref[...] = (proj(_C_QA, _C_KA) * (MOBA_HEAD_DIM ** -0.5 * LOG2E)).astype(BF16)
    ka = proj(_C_KA, _C_CQ)
    ka_ref[...] = ka.astype(BF16)
    for i in range(tm // MOBA_BLOCK):
        kmean_ref[i] = jnp.mean(ka[i * MOBA_BLOCK:(i + 1) * MOBA_BLOCK], axis=0, keepdims=True)
    ones = _ones_rows(vat_ref.shape[0])
    vat_ref[...] = (_dot_nt(wvt_ref[...], xn) + ones).astype(BF16)

    cosm = cos_ref[...]
    sinm = sin_ref[...]
    lane = lax.broadcasted_iota(jnp.int32, (tm, LANES), 1)
    half = MLA_ROPE_DIM // 2

    def rope(a):
        x2_on_x1 = pltpu.roll(a, LANES - half, axis=1)
        x1_on_x2 = pltpu.roll(a, half, axis=1)
        swapped = jnp.where(lane < MLA_NOPE_DIM + half, x2_on_x1, x1_on_x2)
        return a * cosm + swapped * sinm

    cqn = _rms(proj(_C_CQ, _C_CKV), qng_ref[...]).astype(BF16)
    qq = _dot(cqn, wuq_ref[...])
    for h in range(MLA_HEADS):
        qm_ref[:, h * LANES:(h + 1) * LANES] = (
            rope(qq[:, h * LANES:(h + 1) * LANES]) * mla_scale).astype(BF16)

    ckv_kra = proj(_C_CKV, _C_G)
    ckvn = _rms(ckv_kra[:, :MLA_KV_RANK], kvng_ref[...]).astype(BF16)
    kv = _dot(ckvn, wkv_ref[...])
    kr = rope(ckv_kra[:, MLA_KV_RANK:])
    for h in range(MLA_HEADS):
        km_ref[:, h * LANES:(h + 1) * LANES] = (kv[:, h * LANES:(h + 1) * LANES] + kr).astype(BF16)
    vmt_ref[...] = (_dot_nt(wkvt_ref[...], ckvn) + ones).astype(BF16)

    g_ref[...] = _sigmoid(proj(_C_G, w1_ref.shape[1]) + bg_ref[...]).astype(BF16)


def _front(x2, gattn, w1, wvt, bg, qng, wuq, kvng, wkv, wkvt, cosm, sinm, *, seq, tm):
    n, d = x2.shape
    batch = n // seq
    hw = MLA_HEADS * LANES
    nblk = n // MOBA_BLOCK
    steps_per_seq = seq // tm
    row = lambda i: (i, 0)
    tcol = lambda i: (i // steps_per_seq, 0, i % steps_per_seq)
    out_shape = (
        jax.ShapeDtypeStruct((n, MOBA_WIDTH), BF16),
        jax.ShapeDtypeStruct((n, MOBA_WIDTH), BF16),
        jax.ShapeDtypeStruct((nblk, 1, MOBA_WIDTH), F32),
        jax.ShapeDtypeStruct((batch, MOBA_HEADS * V_ROWS, seq), BF16),
        jax.ShapeDtypeStruct((n, hw), BF16),
        jax.ShapeDtypeStruct((n, hw), BF16),
        jax.ShapeDtypeStruct((batch, MLA_HEADS * V_ROWS, seq), BF16),
        jax.ShapeDtypeStruct((n, N_BRANCH * d), BF16),
    )
    in_specs = [
        pl.BlockSpec((tm, d), row),
        _resident(gattn.shape), _resident(w1.shape), _resident(wvt.shape), _resident(bg.shape),
        _resident(qng.shape), _resident(wuq.shape), _resident(kvng.shape), _resident(wkv.shape),
        _resident(wkvt.shape),
        pl.BlockSpec((tm, LANES), lambda i: (i % steps_per_seq, 0)),
        pl.BlockSpec((tm, LANES), lambda i: (i % steps_per_seq, 0)),
    ]
    out_specs = (
        pl.BlockSpec((tm, MOBA_WIDTH), row),
        pl.BlockSpec((tm, MOBA_WIDTH), row),
        pl.BlockSpec((tm // MOBA_BLOCK, 1, MOBA_WIDTH), lambda i: (i, 0, 0)),
        pl.BlockSpec((None, MOBA_HEADS * V_ROWS, tm), tcol),
        pl.BlockSpec((tm, hw), row),
        pl.BlockSpec((tm, hw), row),
        pl.BlockSpec((None, MLA_HEADS * V_ROWS, tm), tcol),
        pl.BlockSpec((tm, N_BRANCH * d), row),
    )
    return pl.pallas_call(
        functools.partial(_front_kernel, mla_scale=MLA_QK_DIM ** -0.5 * LOG2E),
        grid=(n // tm,),
        in_specs=in_specs,
        out_specs=out_specs,
        out_shape=out_shape,
        compiler_params=pltpu.CompilerParams(
            dimension_semantics=("arbitrary",), vmem_limit_bytes=VMEM_LIMIT_BYTES),
        name="front",
    )(x2, gattn, w1, wvt, bg, qng, wuq, kvng, wkv, wkvt, cosm, sinm)


HEADS_PER_STEP = 8
BIG = 1e30
RESCALE_SLACK = 64.0
MATMUL_LOOKAHEAD = 2


def _online_softmax_group(m_ref, acc_ref, score_blocks, vt_grp, chosen=None, shifts=None):
    n = len(score_blocks)
    chosen = chosen or [None] * n
    shifts = shifts or [None] * n
    m_old = m_ref[...]
    m_new = m_old
    for st, ch, sh in zip(score_blocks, chosen, shifts):
        rm = jnp.max(st, axis=0, keepdims=True)
        if sh is not None:
            rm = rm + sh
        m_new = jnp.maximum(m_new, rm if ch is None else jnp.where(ch, rm, NEG))
    probs = []
    for st, ch, sh in zip(score_blocks, chosen, shifts):
        off = m_new if sh is None else m_new - sh
        if ch is not None:
            off = jnp.where(ch, off, BIG)
        probs.append(jnp.exp2(st - off).astype(BF16))
    ot = _dot(vt_grp, jnp.concatenate(probs, axis=0))
    acc_ref[...] = jnp.exp2(m_old - m_new) * acc_ref[...] + ot
    m_ref[...] = m_new


def _streamed_softmax_group(m_ref, acc_ref, late_ref, score_blocks, vt_grp, chosen=None, shifts=None):
    n = len(score_blocks)
    chosen = chosen or [None] * n
    shifts = shifts or [None] * n
    m_old = m_ref[...]
    seen = jnp.full(m_old.shape, NEG, F32)
    probs = []
    for st, ch, sh in zip(score_blocks, chosen, shifts):
        off = m_old if sh is None else m_old - sh
        if ch is not None:
            off = jnp.where(ch, off, BIG)
        probs.append(jnp.exp2(st - off).astype(BF16))
        rm = jnp.max(st, axis=0, keepdims=True)
        if sh is not None:
            rm = rm + sh
        seen = jnp.maximum(seen, rm if ch is None else jnp.where(ch, rm, NEG))
    acc_ref[...] = acc_ref[...] + _dot(vt_grp, jnp.concatenate(probs, axis=0))
    late_ref[...] = jnp.where(seen > m_old + RESCALE_SLACK, 1.0, late_ref[...])


def _pipelined(stages, issue, consume, lookahead):
    queue = []
    for n in range(len(stages) + lookahead):
        if n < len(stages):
            queue.append((stages[n], issue(stages[n])))
        if n >= lookahead:
            consume(*queue.pop(0))


def _in_trips(n, sweep, trip_groups):
    def trip(t, carry):
        sweep([trip_groups * t + i for i in range(trip_groups)])
        return carry

    lax.fori_loop(0, n // trip_groups, trip, 0)
    done = (n // trip_groups) * trip_groups
    size = trip_groups // 2
    while size >= 1:
        has = ((n - done) // size) % 2 == 1
        first = done + ((n - done) // (2 * size)) * (2 * size)
        pl.when(has)(functools.partial(sweep, [first + i for i in range(size)]))
        size //= 2


def _reset_softmax_state(m_s, acc_s):
    m_s[...] = jnp.full(m_s.shape, NEG, F32)
    acc_s[...] = jnp.zeros(acc_s.shape, F32)


def _normalised_heads(acc_s, v_dim):
    rows = []
    for hh in range(HEADS_PER_STEP):
        acc = acc_s[hh]
        rows.append(acc[0:v_dim, :] * (1.0 / acc[v_dim:v_dim + 1, :]))
    return jnp.concatenate(rows, axis=0)


def _moba_kernel(q_ref, k_new_ref, vt_new_ref, kmean_ref, near_ref, cfar_ref, o_ref,
                 m_s, acc_s, late_s, sel_s, k_ref, vt_ref, *, unroll, trip):
    blk = MOBA_BLOCK
    qb = pl.program_id(2)
    nb = kmean_ref.shape[0]
    lane = lax.broadcasted_iota(jnp.int32, (blk, LANES), 1)
    blk_id = lax.broadcasted_iota(jnp.int32, (nb, blk), 0).astype(F32)
    qbf = qb.astype(F32)
    has_prev = qb >= 1
    _reset_softmax_state(m_s, acc_s)

    @pl.when(qb == 0)
    def _():
        k_ref[...] = jnp.zeros_like(k_ref)
        vt_ref[...] = jnp.zeros_like(vt_ref)

    own = pl.ds(pl.multiple_of(qb * blk, blk), blk)
    k_ref[own, :] = k_new_ref[...]
    vt_ref[:, own] = vt_new_ref[...]

    def lane_group(hh):
        return slice((hh // 2) * LANES, (hh // 2 + 1) * LANES)

    def k_block(hh, j):
        return k_ref[pl.ds(pl.multiple_of(j * blk, blk), blk), lane_group(hh)]

    def vt_blocks(hh, j, n):
        return vt_ref[hh * V_ROWS:(hh + 1) * V_ROWS, pl.ds(pl.multiple_of(j * blk, blk), n * blk)]

    q_heads, prev_chosen = [], []
    for hh in range(HEADS_PER_STEP):
        in_head = (lane >= (hh % 2) * MOBA_HEAD_DIM) & (lane < (hh % 2 + 1) * MOBA_HEAD_DIM)
        qh = jnp.where(in_head, q_ref[:, lane_group(hh)].astype(F32), 0.0).astype(BF16)
        q_heads.append(qh)

        kmean = kmean_ref[:, lane_group(hh)].astype(BF16)
        gate = jnp.where(blk_id < qbf, _dot_nt(kmean, qh), -jnp.inf)
        sel = jnp.zeros((nb, blk), F32)
        for _ in range(min(MOBA_TOPK, nb)):
            best = jnp.max(gate, axis=0, keepdims=True)
            first = jnp.min(jnp.where(gate == best, blk_id, float(nb)), axis=0, keepdims=True)
            pick = blk_id == first
            sel = jnp.where(pick, 1.0, sel)
            gate = jnp.where(pick, -jnp.inf, gate)
        prev_chosen.append(jnp.max(jnp.where(blk_id == qbf - 1.0, sel, 0.0), axis=0, keepdims=True))
        sel_s[hh] = jnp.where(blk_id < qbf - 1.0, sel, 0.0)

    lo = jnp.maximum(qb - 1, 0)
    tab = jnp.where(has_prev, 0, 1)

    def nearest_exact():
        scores = [[_dot_nt(k_block(hh, lo + u), q_heads[hh]) + near_ref[hh, tab + u]
                   for u in range(2)] for hh in range(HEADS_PER_STEP)]
        for hh in range(HEADS_PER_STEP):
            chosen = [jnp.where(has_prev, prev_chosen[hh], 1.0) > 0.5,
                      jnp.where(has_prev, jnp.ones((1, blk), F32), 0.0) > 0.5]
            _online_softmax_group(m_s.at[hh], acc_s.at[hh], scores[hh], vt_blocks(hh, lo, 2), chosen)

    n_groups = (qb - 1 + unroll - 1) // unroll

    def far_sweep(groups, streamed=True):
        def start_of(g):
            return jnp.minimum(g * unroll, nb - unroll)

        def scores(stage):
            g, hh = stage
            keys = k_ref[pl.ds(pl.multiple_of(start_of(g) * blk, blk), unroll * blk), lane_group(hh)]
            st = _dot_nt(keys, q_heads[hh])
            return [st[u * blk:(u + 1) * blk, :] for u in range(unroll)]

        def softmax(stage, blocks):
            g, hh = stage
            start = start_of(g)
            chosen = [jnp.where(start + u >= g * unroll, sel_s[hh, pl.ds(start + u, 1), :], 0.0) > 0.5
                      for u in range(unroll)]
            operands = (blocks, vt_blocks(hh, start, unroll), chosen, [cfar_ref[hh][:, 0:1]] * unroll)
            if streamed:
                _streamed_softmax_group(m_s.at[hh], acc_s.at[hh], late_s.at[hh], *operands)
            else:
                _online_softmax_group(m_s.at[hh], acc_s.at[hh], *operands)

        stages = [(g, hh) for g in groups for hh in range(HEADS_PER_STEP)]
        _pipelined(stages, scores, softmax, MATMUL_LOOKAHEAD if streamed else len(stages))

    late_s[...] = jnp.zeros_like(late_s)
    nearest_exact()
    _in_trips(n_groups, far_sweep, trip)

    @pl.when(jnp.max(late_s[...]) > 0.5)
    def _():
        _reset_softmax_state(m_s, acc_s)
        nearest_exact()

        def exact(g, carry):
            far_sweep([g], streamed=False)
            return carry

        lax.fori_loop(0, n_groups, exact, 0)

    o_ref[...] = _normalised_heads(acc_s, MOBA_HEAD_DIM).T.astype(o_ref.dtype)


def _moba(qa, ka, vat, kmean, near, cfar, *, batch, seq, unroll, trip):
    n = qa.shape[0]
    blk = MOBA_BLOCK
    nb = seq // blk
    assert nb >= unroll and nb >= 2
    steps = MOBA_HEADS // HEADS_PER_STEP
    width = HEADS_PER_STEP * MOBA_HEAD_DIM
    return pl.pallas_call(
        functools.partial(_moba_kernel, unroll=unroll, trip=trip),
        grid=(batch, steps, nb),
        in_specs=[
            pl.BlockSpec((blk, width), lambda b, p, i: (b * nb + i, p)),
            pl.BlockSpec((blk, width), lambda b, p, i: (b * nb + i, p)),
            pl.BlockSpec((None, HEADS_PER_STEP * V_ROWS, blk), lambda b, p, i: (b, p, i)),
            pl.BlockSpec((None, nb, width), lambda b, p, i: (b, 0, p)),
            _per_head_group((HEADS_PER_STEP, 3, blk, blk), lambda b, p, i: (p, 0, 0, 0)),
            pl.BlockSpec((HEADS_PER_STEP, 1, LANES), lambda b, p, i: (p, 0, 0)),
        ],
        out_specs=pl.BlockSpec((blk, width), lambda b, p, i: (b * nb + i, p)),
        out_shape=jax.ShapeDtypeStruct((n, MOBA_WIDTH), BF16),
        scratch_shapes=[
            pltpu.VMEM((HEADS_PER_STEP, 1, blk), F32),
            pltpu.VMEM((HEADS_PER_STEP, V_ROWS, blk), F32),
            pltpu.VMEM((HEADS_PER_STEP, 1, blk), F32),
            pltpu.VMEM((HEADS_PER_STEP, nb, blk), F32),
            pltpu.VMEM((seq, width), BF16),
            pltpu.VMEM((HEADS_PER_STEP * V_ROWS, seq), BF16),
        ],
        compiler_params=pltpu.CompilerParams(
            dimension_semantics=("arbitrary", "arbitrary", "arbitrary"),
            vmem_limit_bytes=VMEM_LIMIT_BYTES),
        name="moba",
    )(qa, ka, vat, kmean, near, cfar)


def _mla_kernel(q_ref, k_new_ref, vt_new_ref, o_ref, m_s, acc_s, late_s, k_ref, vt_ref,
                *, tq, tk, unroll, trip):
    qi = pl.program_id(2)
    n_diag = tq // tk
    key = lax.broadcasted_iota(jnp.int32, (tk, tq), 0)
    qry = lax.broadcasted_iota(jnp.int32, (tk, tq), 1)
    q_heads = [q_ref[:, hh * LANES:(hh + 1) * LANES] for hh in range(HEADS_PER_STEP)]

    own = pl.ds(pl.multiple_of(qi * tq, tq), tq)
    k_ref[own, :] = k_new_ref[...]
    vt_ref[:, own] = vt_new_ref[...]

    _reset_softmax_state(m_s, acc_s)

    def k_tile(hh, j):
        return k_ref[pl.ds(pl.multiple_of(j * tk, tk), tk), hh * LANES:(hh + 1) * LANES]

    def vt_tiles(hh, j, n):
        return vt_ref[hh * V_ROWS:(hh + 1) * V_ROWS, pl.ds(pl.multiple_of(j * tk, tk), n * tk)]

    def group(first, n, masked):
        scores = [[_dot_nt(k_tile(hh, first + u), q_heads[hh]) for u in range(n)]
                  for hh in range(HEADS_PER_STEP)]
        for hh in range(HEADS_PER_STEP):
            blocks = scores[hh]
            if masked:
                blocks = [jnp.where(key + u * tk <= qry, st, NEG) for u, st in enumerate(blocks)]
            _online_softmax_group(m_s.at[hh], acc_s.at[hh], blocks, vt_tiles(hh, first, n))

    n_visible = qi * n_diag
    n_groups = (n_visible + unroll - 1) // unroll

    def visible_sweep(groups):
        def start_of(g):
            return jnp.maximum(jnp.minimum(g * unroll, n_visible - unroll), 0)

        def scores(stage):
            g, hh = stage
            start = pl.multiple_of(start_of(g) * tk, tk)
            st = _dot_nt(k_ref[pl.ds(start, unroll * tk), hh * LANES:(hh + 1) * LANES], q_heads[hh])
            return [st[u * tk:(u + 1) * tk, :] for u in range(unroll)]

        def softmax(stage, blocks):
            g, hh = stage
            start = start_of(g)
            chosen = None
            if unroll > 1:
                chosen = [jnp.where((start + u >= g * unroll) & (start + u < n_visible),
                                    jnp.ones((1, tq), F32), 0.0) > 0.5 for u in range(unroll)]
            _streamed_softmax_group(m_s.at[hh], acc_s.at[hh], late_s.at[hh], blocks,
                                    vt_tiles(hh, start, unroll), chosen)

        stages = [(g, hh) for g in groups for hh in range(HEADS_PER_STEP)]
        _pipelined(stages, scores, softmax, MATMUL_LOOKAHEAD)

    late_s[...] = jnp.zeros_like(late_s)
    group(qi * n_diag, n_diag, True)
    _in_trips(n_groups, visible_sweep, trip)

    @pl.when(jnp.max(late_s[...]) > 0.5)
    def _():
        _reset_softmax_state(m_s, acc_s)
        group(qi * n_diag, n_diag, True)

        def exact(j, carry):
            group(j, 1, False)
            return carry

        lax.fori_loop(0, n_visible, exact, 0)

    o_ref[...] = _normalised_heads(acc_s, MLA_V_DIM).T.astype(o_ref.dtype)


def _mla(qm, km, vmt, *, batch, seq, tq, tk, unroll, trip):
    n = qm.shape[0]
    nq = seq // tq
    assert tq % tk == 0 and seq // tk >= unroll
    pairs = MLA_HEADS // HEADS_PER_STEP
    return pl.pallas_call(
        functools.partial(_mla_kernel, tq=tq, tk=tk, unroll=unroll, trip=trip),
        grid=(batch, pairs, nq),
        in_specs=[
            pl.BlockSpec((tq, HEADS_PER_STEP * LANES), lambda b, p, i: (b * nq + i, p)),
            pl.BlockSpec((tq, HEADS_PER_STEP * LANES), lambda b, p, i: (b * nq + i, p)),
            pl.BlockSpec((None, HEADS_PER_STEP * V_ROWS, tq), lambda b, p, i: (b, p, i)),
        ],
        out_specs=pl.BlockSpec((tq, HEADS_PER_STEP * MLA_V_DIM), lambda b, p, i: (b * nq + i, p)),
        out_shape=jax.ShapeDtypeStruct((n, MLA_WIDTH), BF16),
        scratch_shapes=[
            pltpu.VMEM((HEADS_PER_STEP, 1, tq), F32),
            pltpu.VMEM((HEADS_PER_STEP, V_ROWS, tq), F32),
            pltpu.VMEM((HEADS_PER_STEP, 1, tq), F32),
            pltpu.VMEM((seq, HEADS_PER_STEP * LANES), BF16),
            pltpu.VMEM((HEADS_PER_STEP * V_ROWS, seq), BF16),
        ],
        compiler_params=pltpu.CompilerParams(
            dimension_semantics=("arbitrary", "arbitrary", "arbitrary"),
            vmem_limit_bytes=VMEM_LIMIT_BYTES),
        name="mla",
    )(qm, km, vmt)


def _back_kernel(x_ref, ya_ref, yb_ref, g_ref, wa_ref, wb_ref, wo_ref, gffn_ref, wup_ref,
                 cw_ref, cb_ref, wdn_ref, gfin_ref, o_ref,
                 carry_ref, act_ref, hn_ref, *, steps_per_seq, d_ff, fc, final):
    tm, d = x_ref.shape
    halo = SUBLANES

    @pl.when(pl.program_id(0) % steps_per_seq == 0)
    def _():
        carry_ref[...] = jnp.zeros_like(carry_ref)

    g = g_ref[...].astype(F32)
    mixed = g[:, :d] * _dot(ya_ref[...], wa_ref[...]) + g[:, d:] * _dot(yb_ref[...], wb_ref[...])
    h1 = x_ref[...] + _dot(mixed.astype(BF16), wo_ref[...])
    hn_ref[...] = _rms(h1, gffn_ref[...]).astype(BF16)

    def up_conv(col0):
        cols = slice(col0, col0 + fc)
        u = _dot(hn_ref[...], wup_ref[:, cols])
        rows = jnp.concatenate([carry_ref[:, cols], u], axis=0)
        carry_ref[:, cols] = u[tm - halo:tm, :]
        w = cw_ref[:, cols]
        y = cb_ref[:, cols]
        for t in range(CONV_WIDTH):
            back = CONV_WIDTH - 1 - t
            shifted = pltpu.roll(rows, back, axis=0) if back else rows
            y = y + w[t:t + 1, :] * shifted[halo:, :]
        return y

    for c in range(d_ff // fc):
        yg = up_conv(c * fc)
        yv = up_conv(d_ff + c * fc)
        act_ref[:, c * fc:(c + 1) * fc] = (yg * _sigmoid(yg) * yv).astype(BF16)

    h2 = h1 + _dot(act_ref[...], wdn_ref[...])
    o_ref[...] = _rms(h2, gfin_ref[...]) if final else h2


def _back(x2, ya, yb, g, wa, wb, wo, gffn, wup, cw, cb, wdn, gfin, *, seq, tm, fc, final):
    n, d = x2.shape
    d_ff = wdn.shape[0]
    row = lambda i: (i, 0)
    return pl.pallas_call(
        functools.partial(_back_kernel, steps_per_seq=seq // tm, d_ff=d_ff, fc=fc, final=final),
        grid=(n // tm,),
        in_specs=[
            pl.BlockSpec((tm, d), row),
            pl.BlockSpec((tm, ya.shape[1]), row),
            pl.BlockSpec((tm, yb.shape[1]), row),
            pl.BlockSpec((tm, g.shape[1]), row),
            _resident(wa.shape), _resident(wb.shape), _resident(wo.shape), _resident(gffn.shape),
            _resident(wup.shape), _resident(cw.shape), _resident(cb.shape), _resident(wdn.shape),
            _resident(gfin.shape),
        ],
        out_specs=pl.BlockSpec((tm, d), row),
        out_shape=jax.ShapeDtypeStruct((n, d), F32),
        scratch_shapes=[
            pltpu.VMEM((SUBLANES, 2 * d_ff), F32),
            pltpu.VMEM((tm, d_ff), BF16),
            pltpu.VMEM((tm, d), BF16),
        ],
        compiler_params=pltpu.CompilerParams(
            dimension_semantics=("arbitrary",), vmem_limit_bytes=VMEM_LIMIT_BYTES),
        name="back",
    )(x2, ya, yb, g, wa, wb, wo, gffn, wup, cw, cb, wdn, gfin)


def _t5_bucket_np(rel):
    n = np.maximum(rel, 0)
    max_exact = REL_BUCKETS // 2
    nf = np.maximum(n, 1).astype(np.float32)
    large = max_exact + (np.log(nf / np.float32(max_exact)) / np.float32(math.log(REL_MAX_DIST / max_exact))
                         * np.float32(REL_BUCKETS - max_exact)).astype(np.int32)
    large = np.minimum(large, REL_BUCKETS - 1)
    return np.where(n < max_exact, n, large)


def _moba_bias_tables(rel_bias):
    blk = MOBA_BLOCK
    period = 2 * blk
    assert REL_MAX_DIST <= blk + 1
    bias_h = rel_bias.T.astype(F32)
    slot = np.arange(period, dtype=np.int32)
    dist = np.where(slot < blk, slot, slot - period)

    def lookup(bucket):
        onehot = jnp.asarray(bucket)[None, :] == jnp.arange(REL_BUCKETS)[:, None]
        return jnp.sum(jnp.where(onehot[None], bias_h[:, :, None], 0.0), axis=1)

    def toeplitz(v):
        tiled = jnp.tile(v, (1, blk))[:, :blk * (period - 1)]
        return tiled.reshape(v.shape[0], blk, period - 1)[:, :, :blk]

    town = toeplitz(jnp.where(jnp.asarray(dist >= 0), lookup(_t5_bucket_np(dist)) * LOG2E, NEG))
    tprev = toeplitz(lookup(_t5_bucket_np(dist + blk)) * LOG2E)
    near = jnp.stack([tprev, town, jnp.zeros_like(town)], axis=1)
    cfar = bias_h[:, int(_t5_bucket_np(np.int32(blk + 1)))] * LOG2E
    cfar = jnp.broadcast_to(cfar[:, None, None], (MOBA_HEADS, 1, LANES))
    return near, cfar


def _pad_heads(w, n_heads, width, padded=LANES):
    r = w.shape[0]
    w = w.reshape(r, n_heads, width)
    out = jnp.zeros((r, n_heads, padded), w.dtype).at[:, :, :width].set(w)
    return out.reshape(r, n_heads * padded)


def _rope_slot(w_rope):
    r = w_rope.shape[0]
    return jnp.zeros((r, LANES), w_rope.dtype).at[:, MLA_NOPE_DIM:MLA_NOPE_DIM + MLA_ROPE_DIM].set(w_rope)


def _front_weights(w_in, w_uq, w_ukv):
    d = w_in.shape[0]
    s0 = MOBA_WIDTH
    s1 = 2 * MOBA_WIDTH
    s2 = 3 * MOBA_WIDTH
    s3 = s2 + MLA_Q_RANK
    s4 = s3 + MLA_KV_RANK
    s5 = s4 + MLA_ROPE_DIM
    w1 = jnp.concatenate([
        w_in[:, :s0], w_in[:, s0:s1],
        w_in[:, s2:s3], w_in[:, s3:s4],
        _rope_slot(w_in[:, s4:s5]),
        w_in[:, s5:],
    ], axis=1).astype(BF16)
    assert w1.shape == (d, _C_G + N_BRANCH * d)
    wvt = _pad_heads(w_in[:, s1:s2], MOBA_HEADS, MOBA_HEAD_DIM, V_ROWS).T.astype(BF16)

    wuq = _pad_heads(w_uq, MLA_HEADS, MLA_QK_DIM).astype(BF16)

    r = w_ukv.shape[0]
    ukv = w_ukv.reshape(r, MLA_HEADS, MLA_NOPE_DIM + MLA_V_DIM)
    wk = _pad_heads(ukv[:, :, :MLA_NOPE_DIM].reshape(r, -1), MLA_HEADS, MLA_NOPE_DIM).astype(BF16)
    wkvt = _pad_heads(ukv[:, :, MLA_NOPE_DIM:].reshape(r, -1), MLA_HEADS, MLA_V_DIM, V_ROWS).T.astype(BF16)
    return w1, wvt, wuq, wk, wkvt


def _rope_lane_tables(seq):
    dim = MLA_ROPE_DIM
    inv_freq = ROPE_THETA ** (-jnp.arange(0, dim, 2, dtype=F32) / dim)
    ang = jnp.arange(seq, dtype=F32)[:, None] * inv_freq[None, :]
    cos, sin = jnp.cos(ang), jnp.sin(ang)
    tail = jnp.zeros((seq, LANES - MLA_QK_DIM), F32)
    cosm = jnp.concatenate([jnp.ones((seq, MLA_NOPE_DIM), F32), cos, cos, tail], axis=1)
    sinm = jnp.concatenate([jnp.zeros((seq, MLA_NOPE_DIM), F32), -sin, sin, tail], axis=1)
    return cosm, sinm


class _Tiles:
    def __init__(self, seq):
        self.tm = 512 if seq % 512 == 0 else MOBA_BLOCK
        self.fc = 256
        self.mla_tq = 256
        self.mla_tk = 256
        self.mla_unroll = 2
        self.mla_trip = 8
        self.moba_unroll = 2
        self.moba_trip = 8


def kernel(x, norm_attn_g, w_in, b_gate, q_norm_g, w_uq, kv_norm_g, w_ukv, rel_bias,
           w_branch_moba, w_branch_mla, w_out, norm_ffn_g, w_up, conv_w, conv_b, w_down,
           norm_final_g):
    batch, seq, d = x.shape
    depth = w_in.shape[0]
    assert seq % MOBA_BLOCK == 0
    t = _Tiles(seq)
    d_ff = w_down.shape[1]
    assert d_ff % t.fc == 0
    n = batch * seq

    cosm, sinm = _rope_lane_tables(seq)
    near, cfar = _moba_bias_tables(rel_bias)
    row = lambda v: v.reshape(1, -1).astype(F32)

    h = x.reshape(n, d)
    for l in range(depth):
        w1, wvt, wuq, wk, wkvt = _front_weights(w_in[l], w_uq[l], w_ukv[l])
        qa, ka, kmean, vat, qm, km, vmt, g = _front(
            h, row(norm_attn_g[l]), w1, wvt, row(b_gate[l]), row(q_norm_g[l]), wuq,
            row(kv_norm_g[l]), wk, wkvt, cosm, sinm, seq=seq, tm=t.tm)
        kmean = kmean.reshape(batch, seq // MOBA_BLOCK, MOBA_WIDTH)
        ya = _moba(qa, ka, vat, kmean, near, cfar, batch=batch, seq=seq,
                   unroll=t.moba_unroll, trip=t.moba_trip)
        yb = _mla(qm, km, vmt, batch=batch, seq=seq, tq=t.mla_tq, tk=t.mla_tk,
                  unroll=t.mla_unroll, trip=t.mla_trip)
        h = _back(h, ya, yb, g, w_branch_moba[l].astype(BF16), w_branch_mla[l].astype(BF16),
                  w_out[l].astype(BF16), row(norm_ffn_g[l]), w_up[l].astype(BF16),
                  conv_w[l].astype(F32), row(conv_b[l]), w_down[l].astype(BF16),
                  row(norm_final_g), seq=seq, tm=t.tm, fc=t.fc, final=(l == depth - 1))
    return h.reshape(batch, seq, d)
```

```python
import functools
import math

import numpy as np
import jax
import jax.numpy as jnp
from jax import lax
from jax.experimental import pallas as pl
from jax.experimental.pallas import tpu as pltpu

MOBA_HEADS = 8
MOBA_HEAD_DIM = 64
MOBA_BLOCK = 256
MOBA_TOPK = 3
MLA_HEADS = 8
MLA_Q_RANK = 256
MLA_KV_RANK = 128
MLA_NOPE_DIM = 64
MLA_ROPE_DIM = 32
MLA_V_DIM = 64
ROPE_THETA = 10000.0
REL_BUCKETS = 32
REL_MAX_DIST = 128
CONV_WIDTH = 3
N_BRANCH = 2
EPS = 1e-6

MOBA_WIDTH = MOBA_HEADS * MOBA_HEAD_DIM
MLA_QK_DIM = MLA_NOPE_DIM + MLA_ROPE_DIM
MLA_WIDTH = MLA_HEADS * MLA_V_DIM

LANES = 128
SUBLANES = 8
VMEM_LIMIT_BYTES = 56 * 1024 * 1024

NEG = -1e30
LOG2E = math.log2(math.e)

F32 = jnp.float32
BF16 = jnp.bfloat16


def _dot(a, b):
    return jnp.dot(a, b, preferred_element_type=F32)


def _dot_nt(a, b):
    return lax.dot_general(a, b, (((1,), (1,)), ((), ())), preferred_element_type=F32)


def _rms(x, g):
    return x * lax.rsqrt(jnp.mean(x * x, axis=-1, keepdims=True) + EPS) * g


def _sigmoid(z):
    return 1.0 / (1.0 + jnp.exp(-z))


def _resident(shape):
    nd = len(shape)
    return pl.BlockSpec(shape, lambda *_: (0,) * nd, pipeline_mode=pl.Buffered(1))


def _per_head_group(shape, index_map):
    return pl.BlockSpec(shape, index_map, pipeline_mode=pl.Buffered(1))


_C_QA = 0
_C_KA = _C_QA + MOBA_WIDTH
_C_CQ = _C_KA + MOBA_WIDTH
_C_CKV = _C_CQ + MLA_Q_RANK
_C_KRA = _C_CKV + MLA_KV_RANK
_C_G = _C_KRA + LANES

BF16_SUBLANES = 16
V_ROWS = MLA_V_DIM + BF16_SUBLANES
assert MOBA_HEAD_DIM == MLA_V_DIM


def _ones_rows(n_rows):
    r = lax.broadcasted_iota(jnp.int32, (n_rows, 1), 0).astype(F32)
    within = r - jnp.floor((r + 0.5) * (1.0 / V_ROWS)) * V_ROWS
    return jnp.where(within == MLA_V_DIM, 1.0, 0.0).astype(F32)


def _front_kernel(x_ref, gattn_ref, w1_ref, wvt_ref, bg_ref, qng_ref, wuq_ref, kvng_ref, wkv_ref,
                  wkvt_ref, cos_ref, sin_ref,
                  qa_ref, ka_ref, kmean_ref, vat_ref, qm_ref, km_ref, vmt_ref, g_ref,
                  *, mla_scale):
    tm = x_ref.shape[0]
    xn = _rms(x_ref[...], gattn_ref[...]).astype(BF16)

    def proj(a, b):
        return _dot(xn, w1_ref[:, a:b])

    qa_ref[...] = (proj(_C_QA, _C_KA) * (MOBA_HEAD_DIM ** -0.5 * LOG2E)).astype(BF16)
    ka = proj(_C_KA, _C_CQ)
    ka_ref[...] = ka.astype(BF16)
    for i in range(tm // MOBA_BLOCK):
        kmean_ref[i] = jnp.mean(ka[i * MOBA_BLOCK:(i + 1) * MOBA_BLOCK], axis=0, keepdims=True)
    ones = _ones_rows(vat_ref.shape[0])
    vat_ref[...] = (_dot_nt(wvt_ref[...], xn) + ones).astype(BF16)

    cosm = cos_ref[...]
    sinm = sin_ref[...]
    lane = lax.broadcasted_iota(jnp.int32, (tm, LANES), 1)
    half = MLA_ROPE_DIM // 2

    def rope(a):
        x2_on_x1 = pltpu.roll(a, LANES - half, axis=1)
        x1_on_x2 = pltpu.roll(a, half, axis=1)
        swapped = jnp.where(lane < MLA_NOPE_DIM + half, x2_on_x1, x1_on_x2)
        return a * cosm + swapped * sinm

    cqn = _rms(proj(_C_CQ, _C_CKV), qng_ref[...]).astype(BF16)
    qq = _dot(cqn, wuq_ref[...])
    for h in range(MLA_HEADS):
        qm_ref[:, h * LANES:(h + 1) * LANES] = (
            rope(qq[:, h * LANES:(h + 1) * LANES]) * mla_scale).astype(BF16)

    ckv_kra = proj(_C_CKV, _C_G)
    ckvn = _rms(ckv_kra[:, :MLA_KV_RANK], kvng_ref[...]).astype(BF16)
    kv = _dot(ckvn, wkv_ref[...])
    kr = rope(ckv_kra[:, MLA_KV_RANK:])
    for h in range(MLA_HEADS):
        km_ref[:, h * LANES:(h + 1) * LANES] = (kv[:, h * LANES:(h + 1) * LANES] + kr).astype(BF16)
    vmt_ref[...] = (_dot_nt(wkvt_ref[...], ckvn) + ones).astype(BF16)

    g_ref[...] = _sigmoid(proj(_C_G, w1_ref.shape[1]) + bg_ref[...]).astype(BF16)


def _front(x2, gattn, w1, wvt, bg, qng, wuq, kvng, wkv, wkvt, cosm, sinm, *, seq, tm):
    n, d = x2.shape
    batch = n // seq
    hw = MLA_HEADS * LANES
    nblk = n // MOBA_BLOCK
    steps_per_seq = seq // tm
    row = lambda i: (i, 0)
    tcol = lambda i: (i // steps_per_seq, 0, i % steps_per_seq)
    out_shape = (
        jax.ShapeDtypeStruct((n, MOBA_WIDTH), BF16),
        jax.ShapeDtypeStruct((n, MOBA_WIDTH), BF16),
        jax.ShapeDtypeStruct((nblk, 1, MOBA_WIDTH), F32),
        jax.ShapeDtypeStruct((batch, MOBA_HEADS * V_ROWS, seq), BF16),
        jax.ShapeDtypeStruct((n, hw), BF16),
        jax.ShapeDtypeStruct((n, hw), BF16),
        jax.ShapeDtypeStruct((batch, MLA_HEADS * V_ROWS, seq), BF16),
        jax.ShapeDtypeStruct((n, N_BRANCH * d), BF16),
    )
    in_specs = [
        pl.BlockSpec((tm, d), row),
        _resident(gattn.shape), _resident(w1.shape), _resident(wvt.shape), _resident(bg.shape),
        _resident(qng.shape), _resident(wuq.shape), _resident(kvng.shape), _resident(wkv.shape),
        _resident(wkvt.shape),
        pl.BlockSpec((tm, LANES), lambda i: (i % steps_per_seq, 0)),
        pl.BlockSpec((tm, LANES), lambda i: (i % steps_per_seq, 0)),
    ]
    out_specs = (
        pl.BlockSpec((tm, MOBA_WIDTH), row),
        pl.BlockSpec((tm, MOBA_WIDTH), row),
        pl.BlockSpec((tm // MOBA_BLOCK, 1, MOBA_WIDTH), lambda i: (i, 0, 0)),
        pl.BlockSpec((None, MOBA_HEADS * V_ROWS, tm), tcol),
        pl.BlockSpec((tm, hw), row),
        pl.BlockSpec((tm, hw), row),
        pl.BlockSpec((None, MLA_HEADS * V_ROWS, tm), tcol),
        pl.BlockSpec((tm, N_BRANCH * d), row),
    )
    return pl.pallas_call(
        functools.partial(_front_kernel, mla_scale=MLA_QK_DIM ** -0.5 * LOG2E),
        grid=(n // tm,),
        in_specs=in_specs,
        out_specs=out_specs,
        out_shape=out_shape,
        compiler_params=pltpu.CompilerParams(
            dimension_semantics=("arbitrary",), vmem_limit_bytes=VMEM_LIMIT_BYTES),
        name="front",
    )(x2, gattn, w1, wvt, bg, qng, wuq, kvng, wkv, wkvt, cosm, sinm)


HEADS_PER_STEP = 8
BIG = 1e30
RESCALE_SLACK = 64.0
MATMUL_LOOKAHEAD = 2


def _online_softmax_group(m_ref, acc_ref, score_blocks, vt_grp, chosen=None, shifts=None):
    n = len(score_blocks)
    chosen = chosen or [None] * n
    shifts = shifts or [None] * n
    m_old = m_ref[...]
    m_new = m_old
    for st, ch, sh in zip(score_blocks, chosen, shifts):
        rm = jnp.max(st, axis=0, keepdims=True)
        if sh is not None:
            rm = rm + sh
        m_new = jnp.maximum(m_new, rm if ch is None else jnp.where(ch, rm, NEG))
    probs = []
    for st, ch, sh in zip(score_blocks, chosen, shifts):
        off = m_new if sh is None else m_new - sh
        if ch is not None:
            off = jnp.where(ch, off, BIG)
        probs.append(jnp.exp2(st - off).astype(BF16))
    ot = _dot(vt_grp, jnp.concatenate(probs, axis=0))
    acc_ref[...] = jnp.exp2(m_old - m_new) * acc_ref[...] + ot
    m_ref[...] = m_new


def _streamed_softmax_group(m_ref, acc_ref, late_ref, score_blocks, vt_grp, chosen=None, shifts=None):
    n = len(score_blocks)
    chosen = chosen or [None] * n
    shifts = shifts or [None] * n
    m_old = m_ref[...]
    seen = jnp.full(m_old.shape, NEG, F32)
    probs = []
    for st, ch, sh in zip(score_blocks, chosen, shifts):
        off = m_old if sh is None else m_old - sh
        if ch is not None:
            off = jnp.where(ch, off, BIG)
        probs.append(jnp.exp2(st - off).astype(BF16))
        rm = jnp.max(st, axis=0, keepdims=True)
        if sh is not None:
            rm = rm + sh
        seen = jnp.maximum(seen, rm if ch is None else jnp.where(ch, rm, NEG))
    acc_ref[...] = acc_ref[...] + _dot(vt_grp, jnp.concatenate(probs, axis=0))
    late_ref[...] = jnp.where(seen > m_old + RESCALE_SLACK, 1.0, late_ref[...])


def _pipelined(stages, issue, consume, lookahead):
    queue = []
    for n in range(len(stages) + lookahead):
        if n < len(stages):
            queue.append((stages[n], issue(stages[n])))
        if n >= lookahead:
            consume(*queue.pop(0))


def _in_trips(n, sweep, trip_groups):
    def trip(t, carry):
        sweep([trip_groups * t + i for i in range(trip_groups)])
        return carry

    lax.fori_loop(0, n // trip_groups, trip, 0)
    done = (n // trip_groups) * trip_groups
    size = trip_groups // 2
    while size >= 1:
        has = ((n - done) // size) % 2 == 1
        first = done + ((n - done) // (2 * size)) * (2 * size)
        pl.when(has)(functools.partial(sweep, [first + i for i in range(size)]))
        size //= 2


def _reset_softmax_state(m_s, acc_s):
    m_s[...] = jnp.full(m_s.shape, NEG, F32)
    acc_s[...] = jnp.zeros(acc_s.shape, F32)


def _normalised_heads(acc_s, v_dim):
    rows = []
    for hh in range(HEADS_PER_STEP):
        acc = acc_s[hh]
        rows.append(acc[0:v_dim, :] * (1.0 / acc[v_dim:v_dim + 1, :]))
    return jnp.concatenate(rows, axis=0)


def _moba_kernel(q_ref, k_new_ref, vt_new_ref, kmean_ref, near_ref, cfar_ref, o_ref,
                 m_s, acc_s, late_s, sel_s, k_ref, vt_ref, *, unroll, trip):
    blk = MOBA_BLOCK
    qb = pl.program_id(2)
    nb = kmean_ref.shape[0]
    lane = lax.broadcasted_iota(jnp.int32, (blk, LANES), 1)
    blk_id = lax.broadcasted_iota(jnp.int32, (nb, blk), 0).astype(F32)
    qbf = qb.astype(F32)
    has_prev = qb >= 1
    _reset_softmax_state(m_s, acc_s)

    @pl.when(qb == 0)
    def _():
        k_ref[...] = jnp.zeros_like(k_ref)
        vt_ref[...] = jnp.zeros_like(vt_ref)

    own = pl.ds(pl.multiple_of(qb * blk, blk), blk)
    k_ref[own, :] = k_new_ref[...]
    vt_ref[:, own] = vt_new_ref[...]

    def lane_group(hh):
        return slice((hh // 2) * LANES, (hh // 2 + 1) * LANES)

    def k_block(hh, j):
        return k_ref[pl.ds(pl.multiple_of(j * blk, blk), blk), lane_group(hh)]

    def vt_blocks(hh, j, n):
        return vt_ref[hh * V_ROWS:(hh + 1) * V_ROWS, pl.ds(pl.multiple_of(j * blk, blk), n * blk)]

    q_heads, prev_chosen = [], []
    for hh in range(HEADS_PER_STEP):
        in_head = (lane >= (hh % 2) * MOBA_HEAD_DIM) & (lane < (hh % 2 + 1) * MOBA_HEAD_DIM)
        qh = jnp.where(in_head, q_ref[:, lane_group(hh)].astype(F32), 0.0).astype(BF16)
        q_heads.append(qh)

        kmean = kmean_ref[:, lane_group(hh)].astype(BF16)
        gate = jnp.where(blk_id < qbf, _dot_nt(kmean, qh), -jnp.inf)
        sel = jnp.zeros((nb, blk), F32)
        for _ in range(min(MOBA_TOPK, nb)):
            best = jnp.max(gate, axis=0, keepdims=True)
            first = jnp.min(jnp.where(gate == best, blk_id, float(nb)), axis=0, keepdims=True)
            pick = blk_id == first
            sel = jnp.where(pick, 1.0, sel)
            gate = jnp.where(pick, -jnp.inf, gate)
        prev_chosen.append(jnp.max(jnp.where(blk_id == qbf - 1.0, sel, 0.0), axis=0, keepdims=True))
        sel_s[hh] = jnp.where(blk_id < qbf - 1.0, sel, 0.0)

    lo = jnp.maximum(qb - 1, 0)
    tab = jnp.where(has_prev, 0, 1)

    def nearest_exact():
        for first in range(0, HEADS_PER_STEP, HEADS_PER_STEP // 2):
            heads = range(first, first + HEADS_PER_STEP // 2)
            scores = {hh: [_dot_nt(k_block(hh, lo + u), q_heads[hh]) + near_ref[hh, tab + u]
                           for u in range(2)] for hh in heads}
            for hh in heads:
                chosen = [jnp.where(has_prev, prev_chosen[hh], 1.0) > 0.5,
                          jnp.where(has_prev, jnp.ones((1, blk), F32), 0.0) > 0.5]
                _online_softmax_group(m_s.at[hh], acc_s.at[hh], scores[hh], vt_blocks(hh, lo, 2), chosen)

    n_groups = (qb - 1 + unroll - 1) // unroll

    def far_sweep(groups, streamed=True):
        def start_of(g):
            return jnp.minimum(g * unroll, nb - unroll)

        def scores(stage):
            g, hh = stage
            keys = k_ref[pl.ds(pl.multiple_of(start_of(g) * blk, blk), unroll * blk), lane_group(hh)]
            st = _dot_nt(keys, q_heads[hh])
            return [st[u * blk:(u + 1) * blk, :] for u in range(unroll)]

        def softmax(stage, blocks):
            g, hh = stage
            start = start_of(g)
            chosen = [jnp.where(start + u >= g * unroll, sel_s[hh, pl.ds(start + u, 1), :], 0.0) > 0.5
                      for u in range(unroll)]
            operands = (blocks, vt_blocks(hh, start, unroll), chosen, [cfar_ref[hh][:, 0:1]] * unroll)
            if streamed:
                _streamed_softmax_group(m_s.at[hh], acc_s.at[hh], late_s.at[hh], *operands)
            else:
                _online_softmax_group(m_s.at[hh], acc_s.at[hh], *operands)

        stages = [(g, hh) for g in groups for hh in range(HEADS_PER_STEP)]
        _pipelined(stages, scores, softmax, MATMUL_LOOKAHEAD if streamed else len(stages))

    late_s[...] = jnp.zeros_like(late_s)
    nearest_exact()
    _in_trips(n_groups, far_sweep, trip)

    @pl.when(jnp.max(late_s[...]) > 0.5)
    def _():
        _reset_softmax_state(m_s, acc_s)
        nearest_exact()

        def exact(g, carry):
            far_sweep([g], streamed=False)
            return carry

        lax.fori_loop(0, n_groups, exact, 0)

    o_ref[...] = _normalised_heads(acc_s, MOBA_HEAD_DIM).T.astype(o_ref.dtype)


def _moba(qa, ka, vat, kmean, near, cfar, *, batch, seq, unroll, trip):
    n = qa.shape[0]
    blk = MOBA_BLOCK
    nb = seq // blk
    assert nb >= unroll and nb >= 2
    steps = MOBA_HEADS // HEADS_PER_STEP
    width = HEADS_PER_STEP * MOBA_HEAD_DIM
    return pl.pallas_call(
        functools.partial(_moba_kernel, unroll=unroll, trip=trip),
        grid=(batch, steps, nb),
        in_specs=[
            pl.BlockSpec((blk, width), lambda b, p, i: (b * nb + i, p)),
            pl.BlockSpec((blk, width), lambda b, p, i: (b * nb + i, p)),
            pl.BlockSpec((None, HEADS_PER_STEP * V_ROWS, blk), lambda b, p, i: (b, p, i)),
            pl.BlockSpec((None, nb, width), lambda b, p, i: (b, 0, p)),
            _per_head_group((HEADS_PER_STEP, 3, blk, blk), lambda b, p, i: (p, 0, 0, 0)),
            pl.BlockSpec((HEADS_PER_STEP, 1, LANES), lambda b, p, i: (p, 0, 0)),
        ],
        out_specs=pl.BlockSpec((blk, width), lambda b, p, i: (b * nb + i, p)),
        out_shape=jax.ShapeDtypeStruct((n, MOBA_WIDTH), BF16),
        scratch_shapes=[
            pltpu.VMEM((HEADS_PER_STEP, 1, blk), F32),
            pltpu.VMEM((HEADS_PER_STEP, V_ROWS, blk), F32),
            pltpu.VMEM((HEADS_PER_STEP, 1, blk), F32),
            pltpu.VMEM((HEADS_PER_STEP, nb, blk), F32),
            pltpu.VMEM((seq, width), BF16),
            pltpu.VMEM((HEADS_PER_STEP * V_ROWS, seq), BF16),
        ],
        compiler_params=pltpu.CompilerParams(
            dimension_semantics=("arbitrary", "arbitrary", "arbitrary"),
            vmem_limit_bytes=VMEM_LIMIT_BYTES),
        name="moba",
    )(qa, ka, vat, kmean, near, cfar)


def _mla_kernel(q_ref, k_new_ref, vt_new_ref, o_ref, m_s, acc_s, late_s, k_ref, vt_ref,
                *, tq, tk, unroll, trip):
    qi = pl.program_id(2)
    n_diag = tq // tk
    key = lax.broadcasted_iota(jnp.int32, (tk, tq), 0)
    qry = lax.broadcasted_iota(jnp.int32, (tk, tq), 1)
    q_heads = [q_ref[:, hh * LANES:(hh + 1) * LANES] for hh in range(HEADS_PER_STEP)]

    own = pl.ds(pl.multiple_of(qi * tq, tq), tq)
    k_ref[own, :] = k_new_ref[...]
    vt_ref[:, own] = vt_new_ref[...]

    _reset_softmax_state(m_s, acc_s)

    def k_tile(hh, j):
        return k_ref[pl.ds(pl.multiple_of(j * tk, tk), tk), hh * LANES:(hh + 1) * LANES]

    def vt_tiles(hh, j, n):
        return vt_ref[hh * V_ROWS:(hh + 1) * V_ROWS, pl.ds(pl.multiple_of(j * tk, tk), n * tk)]

    def group(first, n, masked):
        scores = [[_dot_nt(k_tile(hh, first + u), q_heads[hh]) for u in range(n)]
                  for hh in range(HEADS_PER_STEP)]
        for hh in range(HEADS_PER_STEP):
            blocks = scores[hh]
            if masked:
                blocks = [jnp.where(key + u * tk <= qry, st, NEG) for u, st in enumerate(blocks)]
            _online_softmax_group(m_s.at[hh], acc_s.at[hh], blocks, vt_tiles(hh, first, n))

    n_visible = qi * n_diag
    n_groups = (n_visible + unroll - 1) // unroll

    def visible_sweep(groups):
        def start_of(g):
            return jnp.maximum(jnp.minimum(g * unroll, n_visible - unroll), 0)

        def scores(stage):
            g, hh = stage
            start = pl.multiple_of(start_of(g) * tk, tk)
            st = _dot_nt(k_ref[pl.ds(start, unroll * tk), hh * LANES:(hh + 1) * LANES], q_heads[hh])
            return [st[u * tk:(u + 1) * tk, :] for u in range(unroll)]

        def softmax(stage, blocks):
            g, hh = stage
            start = start_of(g)
            chosen = None
            if unroll > 1:
                chosen = [jnp.where((start + u >= g * unroll) & (start + u < n_visible),
                                    jnp.ones((1, tq), F32), 0.0) > 0.5 for u in range(unroll)]
            _streamed_softmax_group(m_s.at[hh], acc_s.at[hh], late_s.at[hh], blocks,
                                    vt_tiles(hh, start, unroll), chosen)

        stages = [(g, hh) for g in groups for hh in range(HEADS_PER_STEP)]
        _pipelined(stages, scores, softmax, MATMUL_LOOKAHEAD)

    late_s[...] = jnp.zeros_like(late_s)
    group(qi * n_diag, n_diag, True)
    _in_trips(n_groups, visible_sweep, trip)

    @pl.when(jnp.max(late_s[...]) > 0.5)
    def _():
        _reset_softmax_state(m_s, acc_s)
        group(qi * n_diag, n_diag, True)

        def exact(j, carry):
            group(j, 1, False)
            return carry

        lax.fori_loop(0, n_visible, exact, 0)

    o_ref[...] = _normalised_heads(acc_s, MLA_V_DIM).T.astype(o_ref.dtype)


def _mla(qm, km, vmt, *, batch, seq, tq, tk, unroll, trip):
    n = qm.shape[0]
    nq = seq // tq
    assert tq % tk == 0 and seq // tk >= unroll
    pairs = MLA_HEADS // HEADS_PER_STEP
    return pl.pallas_call(
        functools.partial(_mla_kernel, tq=tq, tk=tk, unroll=unroll, trip=trip),
        grid=(batch, pairs, nq),
        in_specs=[
            pl.BlockSpec((tq, HEADS_PER_STEP * LANES), lambda b, p, i: (b * nq + i, p)),
            pl.BlockSpec((tq, HEADS_PER_STEP * LANES), lambda b, p, i: (b * nq + i, p)),
            pl.BlockSpec((None, HEADS_PER_STEP * V_ROWS, tq), lambda b, p, i: (b, p, i)),
        ],
        out_specs=pl.BlockSpec((tq, HEADS_PER_STEP * MLA_V_DIM), lambda b, p, i: (b * nq + i, p)),
        out_shape=jax.ShapeDtypeStruct((n, MLA_WIDTH), BF16),
        scratch_shapes=[
            pltpu.VMEM((HEADS_PER_STEP, 1, tq), F32),
            pltpu.VMEM((HEADS_PER_STEP, V_ROWS, tq), F32),
            pltpu.VMEM((HEADS_PER_STEP, 1, tq), F32),
            pltpu.VMEM((seq, HEADS_PER_STEP * LANES), BF16),
            pltpu.VMEM((HEADS_PER_STEP * V_ROWS, seq), BF16),
        ],
        compiler_params=pltpu.CompilerParams(
            dimension_semantics=("arbitrary", "arbitrary", "arbitrary"),
            vmem_limit_bytes=VMEM_LIMIT_BYTES),
        name="mla",
    )(qm, km, vmt)


def _back_kernel(x_ref, ya_ref, yb_ref, g_ref, wa_ref, wb_ref, wo_ref, gffn_ref, wup_ref,
                 cw_ref, cb_ref, wdn_ref, gfin_ref, o_ref,
                 carry_ref, act_ref, hn_ref, *, steps_per_seq, d_ff, fc, final):
    tm, d = x_ref.shape
    halo = SUBLANES

    @pl.when(pl.program_id(0) % steps_per_seq == 0)
    def _():
        carry_ref[...] = jnp.zeros_like(carry_ref)

    g = g_ref[...].astype(F32)
    mixed = g[:, :d] * _dot(ya_ref[...], wa_ref[...]) + g[:, d:] * _dot(yb_ref[...], wb_ref[...])
    h1 = x_ref[...] + _dot(mixed.astype(BF16), wo_ref[...])
    hn_ref[...] = _rms(h1, gffn_ref[...]).astype(BF16)

    def up_conv(col0):
        cols = slice(col0, col0 + fc)
        u = _dot(hn_ref[...], wup_ref[:, cols])
        rows = jnp.concatenate([carry_ref[:, cols], u], axis=0)
        carry_ref[:, cols] = u[tm - halo:tm, :]
        w = cw_ref[:, cols]
        y = cb_ref[:, cols]
        for t in range(CONV_WIDTH):
            back = CONV_WIDTH - 1 - t
            shifted = pltpu.roll(rows, back, axis=0) if back else rows
            y = y + w[t:t + 1, :] * shifted[halo:, :]
        return y

    for c in range(d_ff // fc):
        yg = up_conv(c * fc)
        yv = up_conv(d_ff + c * fc)
        act_ref[:, c * fc:(c + 1) * fc] = (yg * _sigmoid(yg) * yv).astype(BF16)

    h2 = h1 + _dot(act_ref[...], wdn_ref[...])
    o_ref[...] = _rms(h2, gfin_ref[...]) if final else h2


def _back(x2, ya, yb, g, wa, wb, wo, gffn, wup, cw, cb, wdn, gfin, *, seq, tm, fc, final):
    n, d = x2.shape
    d_ff = wdn.shape[0]
    row = lambda i: (i, 0)
    return pl.pallas_call(
        functools.partial(_back_kernel, steps_per_seq=seq // tm, d_ff=d_ff, fc=fc, final=final),
        grid=(n // tm,),
        in_specs=[
            pl.BlockSpec((tm, d), row),
            pl.BlockSpec((tm, ya.shape[1]), row),
            pl.BlockSpec((tm, yb.shape[1]), row),
            pl.BlockSpec((tm, g.shape[1]), row),
            _resident(wa.shape), _resident(wb.shape), _resident(wo.shape), _resident(gffn.shape),
            _resident(wup.shape), _resident(cw.shape), _resident(cb.shape), _resident(wdn.shape),
            _resident(gfin.shape),
        ],
        out_specs=pl.BlockSpec((tm, d), row),
        out_shape=jax.ShapeDtypeStruct((n, d), F32),
        scratch_shapes=[
            pltpu.VMEM((SUBLANES, 2 * d_ff), F32),
            pltpu.VMEM((tm, d_ff), BF16),
            pltpu.VMEM((tm, d), BF16),
        ],
        compiler_params=pltpu.CompilerParams(
            dimension_semantics=("arbitrary",), vmem_limit_bytes=VMEM_LIMIT_BYTES),
        name="back",
    )(x2, ya, yb, g, wa, wb, wo, gffn, wup, cw, cb, wdn, gfin)


def _t5_bucket_np(rel):
    n = np.maximum(rel, 0)
    max_exact = REL_BUCKETS // 2
    nf = np.maximum(n, 1).astype(np.float32)
    large = max_exact + (np.log(nf / np.float32(max_exact)) / np.float32(math.log(REL_MAX_DIST / max_exact))
                         * np.float32(REL_BUCKETS - max_exact)).astype(np.int32)
    large = np.minimum(large, REL_BUCKETS - 1)
    return np.where(n < max_exact, n, large)


def _moba_bias_tables(rel_bias):
    blk = MOBA_BLOCK
    period = 2 * blk
    assert REL_MAX_DIST <= blk + 1
    bias_h = rel_bias.T.astype(F32)
    slot = np.arange(period, dtype=np.int32)
    dist = np.where(slot < blk, slot, slot - period)

    def lookup(bucket):
        onehot = jnp.asarray(bucket)[None, :] == jnp.arange(REL_BUCKETS)[:, None]
        return jnp.sum(jnp.where(onehot[None], bias_h[:, :, None], 0.0), axis=1)

    def toeplitz(v):
        tiled = jnp.tile(v, (1, blk))[:, :blk * (period - 1)]
        return tiled.reshape(v.shape[0], blk, period - 1)[:, :, :blk]

    town = toeplitz(jnp.where(jnp.asarray(dist >= 0), lookup(_t5_bucket_np(dist)) * LOG2E, NEG))
    tprev = toeplitz(lookup(_t5_bucket_np(dist + blk)) * LOG2E)
    near = jnp.stack([tprev, town, jnp.zeros_like(town)], axis=1)
    cfar = bias_h[:, int(_t5_bucket_np(np.int32(blk + 1)))] * LOG2E
    cfar = jnp.broadcast_to(cfar[:, None, None], (MOBA_HEADS, 1, LANES))
    return near, cfar


def _pad_heads(w, n_heads, width, padded=LANES):
    r = w.shape[0]
    w = w.reshape(r, n_heads, width)
    out = jnp.zeros((r, n_heads, padded), w.dtype).at[:, :, :width].set(w)
    return out.reshape(r, n_heads * padded)


def _rope_slot(w_rope):
    r = w_rope.shape[0]
    return jnp.zeros((r, LANES), w_rope.dtype).at[:, MLA_NOPE_DIM:MLA_NOPE_DIM + MLA_ROPE_DIM].set(w_rope)


def _front_weights(w_in, w_uq, w_ukv):
    d = w_in.shape[0]
    s0 = MOBA_WIDTH
    s1 = 2 * MOBA_WIDTH
    s2 = 3 * MOBA_WIDTH
    s3 = s2 + MLA_Q_RANK
    s4 = s3 + MLA_KV_RANK
    s5 = s4 + MLA_ROPE_DIM
    w1 = jnp.concatenate([
        w_in[:, :s0], w_in[:, s0:s1],
        w_in[:, s2:s3], w_in[:, s3:s4],
        _rope_slot(w_in[:, s4:s5]),
        w_in[:, s5:],
    ], axis=1).astype(BF16)
    assert w1.shape == (d, _C_G + N_BRANCH * d)
    wvt = _pad_heads(w_in[:, s1:s2], MOBA_HEADS, MOBA_HEAD_DIM, V_ROWS).T.astype(BF16)

    wuq = _pad_heads(w_uq, MLA_HEADS, MLA_QK_DIM).astype(BF16)

    r = w_ukv.shape[0]
    ukv = w_ukv.reshape(r, MLA_HEADS, MLA_NOPE_DIM + MLA_V_DIM)
    wk = _pad_heads(ukv[:, :, :MLA_NOPE_DIM].reshape(r, -1), MLA_HEADS, MLA_NOPE_DIM).astype(BF16)
    wkvt = _pad_heads(ukv[:, :, MLA_NOPE_DIM:].reshape(r, -1), MLA_HEADS, MLA_V_DIM, V_ROWS).T.astype(BF16)
    return w1, wvt, wuq, wk, wkvt


def _rope_lane_tables(seq):
    dim = MLA_ROPE_DIM
    inv_freq = ROPE_THETA ** (-jnp.arange(0, dim, 2, dtype=F32) / dim)
    ang = jnp.arange(seq, dtype=F32)[:, None] * inv_freq[None, :]
    cos, sin = jnp.cos(ang), jnp.sin(ang)
    tail = jnp.zeros((seq, LANES - MLA_QK_DIM), F32)
    cosm = jnp.concatenate([jnp.ones((seq, MLA_NOPE_DIM), F32), cos, cos, tail], axis=1)
    sinm = jnp.concatenate([jnp.zeros((seq, MLA_NOPE_DIM), F32), -sin, sin, tail], axis=1)
    return cosm, sinm


class _Tiles:
    def __init__(self, seq):
        self.tm = 512 if seq % 512 == 0 else MOBA_BLOCK
        self.fc = 256
        self.mla_tq = 256
        self.mla_tk = 256
        self.mla_unroll = 2
        self.mla_trip = 8
        self.moba_unroll = 2
        self.moba_trip = 8


def kernel(x, norm_attn_g, w_in, b_gate, q_norm_g, w_uq, kv_norm_g, w_ukv, rel_bias,
           w_branch_moba, w_branch_mla, w_out, norm_ffn_g, w_up, conv_w, conv_b, w_down,
           norm_final_g):
    batch, seq, d = x.shape
    depth = w_in.shape[0]
    assert seq % MOBA_BLOCK == 0
    t = _Tiles(seq)
    d_ff = w_down.shape[1]
    assert d_ff % t.fc == 0
    n = batch * seq

    cosm, sinm = _rope_lane_tables(seq)
    near, cfar = _moba_bias_tables(rel_bias)
    row = lambda v: v.reshape(1, -1).astype(F32)

    h = x.reshape(n, d)
    for l in range(depth):
        w1, wvt, wuq, wk, wkvt = _front_weights(w_in[l], w_uq[l], w_ukv[l])
        qa, ka, kmean, vat, qm, km, vmt, g = _front(
            h, row(norm_attn_g[l]), w1, wvt, row(b_gate[l]), row(q_norm_g[l]), wuq,
            row(kv_norm_g[l]), wk, wkvt, cosm, sinm, seq=seq, tm=t.tm)
        kmean = kmean.reshape(batch, seq // MOBA_BLOCK, MOBA_WIDTH)
        ya = _moba(qa, ka, vat, kmean, near, cfar, batch=batch, seq=seq,
                   unroll=t.moba_unroll, trip=t.moba_trip)
        yb = _mla(qm, km, vmt, batch=batch, seq=seq, tq=t.mla_tq, tk=t.mla_tk,
                  unroll=t.mla_unroll, trip=t.mla_trip)
        h = _back(h, ya, yb, g, w_branch_moba[l].astype(BF16), w_branch_mla[l].astype(BF16),
                  w_out[l].astype(BF16), row(norm_ffn_g[l]), w_up[l].astype(BF16),
                  conv_w[l].astype(F32), row(conv_b[l]), w_down[l].astype(BF16),
                  row(norm_final_g), seq=seq, tm=t.tm, fc=t.fc, final=(l == depth - 1))
    return h.reshape(batch, seq, d)
```

```python
import functools
import math

import numpy as np
import jax
import jax.numpy as jnp
from jax import lax
from jax.experimental import pallas as pl
from jax.experimental.pallas import tpu as pltpu

MOBA_HEADS = 8
MOBA_HEAD_DIM = 64
MOBA_BLOCK = 256
MOBA_TOPK = 3
MLA_HEADS = 8
MLA_Q_RANK = 256
MLA_KV_RANK = 128
MLA_NOPE_DIM = 64
MLA_ROPE_DIM = 32
MLA_V_DIM = 64
ROPE_THETA = 10000.0
REL_BUCKETS = 32
REL_MAX_DIST = 128
CONV_WIDTH = 3
N_BRANCH = 2
EPS = 1e-6

MOBA_WIDTH = MOBA_HEADS * MOBA_HEAD_DIM
MLA_QK_DIM = MLA_NOPE_DIM + MLA_ROPE_DIM
MLA_WIDTH = MLA_HEADS * MLA_V_DIM

LANES = 128
SUBLANES = 8
VMEM_LIMIT_BYTES = 56 * 1024 * 1024

NEG = -1e30
LOG2E = math.log2(math.e)

F32 = jnp.float32
BF16 = jnp.bfloat16


def _dot(a, b):
    return jnp.dot(a, b, preferred_element_type=F32)


def _dot_nt(a, b):
    return lax.dot_general(a, b, (((1,), (1,)), ((), ())), preferred_element_type=F32)


def _rms(x, g):
    return x * lax.rsqrt(jnp.mean(x * x, axis=-1, keepdims=True) + EPS) * g


def _sigmoid(z):
    return 1.0 / (1.0 + jnp.exp(-z))


def _resident(shape):
    nd = len(shape)
    return pl.BlockSpec(shape, lambda *_: (0,) * nd, pipeline_mode=pl.Buffered(1))


def _per_head_group(shape, index_map):
    return pl.BlockSpec(shape, index_map, pipeline_mode=pl.Buffered(1))


_C_QA = 0
_C_KA = _C_QA + MOBA_WIDTH
_C_CQ = _C_KA + MOBA_WIDTH
_C_CKV = _C_CQ + MLA_Q_RANK
_C_KRA = _C_CKV + MLA_KV_RANK
_C_G = _C_KRA + LANES

BF16_SUBLANES = 16
V_ROWS = MLA_V_DIM + BF16_SUBLANES
assert MOBA_HEAD_DIM == MLA_V_DIM


def _ones_rows(n_rows):
    r = lax.broadcasted_iota(jnp.int32, (n_rows, 1), 0).astype(F32)
    within = r - jnp.floor((r + 0.5) * (1.0 / V_ROWS)) * V_ROWS
    return jnp.where(within == MLA_V_DIM, 1.0, 0.0).astype(F32)


def _front_kernel(x_ref, gattn_ref, w1_ref, wvt_ref, bg_ref, qng_ref, wuq_ref, kvng_ref, wkv_ref,
                  wkvt_ref, cos_ref, sin_ref,
                  qa_ref, ka_ref, kmean_ref, vat_ref, qm_ref, km_ref, vmt_ref, g_ref,
                  *, mla_scale):
    tm = x_ref.shape[0]
    xn = _rms(x_ref[...], gattn_ref[...]).astype(BF16)

    def proj(a, b):
        return _dot(xn, w1_ref[:, a:b])

    qa_ref[...] = (proj(_C_QA, _C_KA) * (MOBA_HEAD_DIM ** -0.5 * LOG2E)).astype(BF16)
    ka = proj(_C_KA, _C_CQ)
    ka_ref[...] = ka.astype(BF16)
    for i in range(tm // MOBA_BLOCK):
        kmean_ref[i] = jnp.mean(ka[i * MOBA_BLOCK:(i + 1) * MOBA_BLOCK], axis=0, keepdims=True)
    ones = _ones_rows(vat_ref.shape[0])
    vat_ref[...] = (_dot_nt(wvt_ref[...], xn) + ones).astype(BF16)

    cosm = cos_ref[...]
    sinm = sin_ref[...]
    lane = lax.broadcasted_iota(jnp.int32, (tm, LANES), 1)
    half = MLA_ROPE_DIM // 2

    def rope(a):
        x2_on_x1 = pltpu.roll(a, LANES - half, axis=1)
        x1_on_x2 = pltpu.roll(a, half, axis=1)
        swapped = jnp.where(lane < MLA_NOPE_DIM + half, x2_on_x1, x1_on_x2)
        return a * cosm + swapped * sinm

    cqn = _rms(proj(_C_CQ, _C_CKV), qng_ref[...]).astype(BF16)
    qq = _dot(cqn, wuq_ref[...])
    for h in range(MLA_HEADS):
        qm_ref[:, h * LANES:(h + 1) * LANES] = (
            rope(qq[:, h * LANES:(h + 1) * LANES]) * mla_scale).astype(BF16)

    ckv_kra = proj(_C_CKV, _C_G)
    ckvn = _rms(ckv_kra[:, :MLA_KV_RANK], kvng_ref[...]).astype(BF16)
    kv = _dot(ckvn, wkv_ref[...])
    kr = rope(ckv_kra[:, MLA_KV_RANK:])
    for h in range(MLA_HEADS):
        km_ref[:, h * LANES:(h + 1) * LANES] = (kv[:, h * LANES:(h + 1) * LANES] + kr).astype(BF16)
    vmt_ref[...] = (_dot_nt(wkvt_ref[...], ckvn) + ones).astype(BF16)

    g_ref[...] = _sigmoid(proj(_C_G, w1_ref.shape[1]) + bg_ref[...]).astype(BF16)


def _front(x2, gattn, w1, wvt, bg, qng, wuq, kvng, wkv, wkvt, cosm, sinm, *, seq, tm):
    n, d = x2.shape
    batch = n // seq
    hw = MLA_HEADS * LANES
    nblk = n // MOBA_BLOCK
    steps_per_seq = seq // tm
    row = lambda i: (i, 0)
    tcol = lambda i: (i // steps_per_seq, 0, i % steps_per_seq)
    out_shape = (
        jax.ShapeDtypeStruct((n, MOBA_WIDTH), BF16),
        jax.ShapeDtypeStruct((n, MOBA_WIDTH), BF16),
        jax.ShapeDtypeStruct((nblk, 1, MOBA_WIDTH), F32),
        jax.ShapeDtypeStruct((batch, MOBA_HEADS * V_ROWS, seq), BF16),
        jax.ShapeDtypeStruct((n, hw), BF16),
        jax.ShapeDtypeStruct((n, hw), BF16),
        jax.ShapeDtypeStruct((batch, MLA_HEADS * V_ROWS, seq), BF16),
        jax.ShapeDtypeStruct((n, N_BRANCH * d), BF16),
    )
    in_specs = [
        pl.BlockSpec((tm, d), row),
        _resident(gattn.shape), _resident(w1.shape), _resident(wvt.shape), _resident(bg.shape),
        _resident(qng.shape), _resident(wuq.shape), _resident(kvng.shape), _resident(wkv.shape),
        _resident(wkvt.shape),
        pl.BlockSpec((tm, LANES), lambda i: (i % steps_per_seq, 0)),
        pl.BlockSpec((tm, LANES), lambda i: (i % steps_per_seq, 0)),
    ]
    out_specs = (
        pl.BlockSpec((tm, MOBA_WIDTH), row),
        pl.BlockSpec((tm, MOBA_WIDTH), row),
        pl.BlockSpec((tm // MOBA_BLOCK, 1, MOBA_WIDTH), lambda i: (i, 0, 0)),
        pl.BlockSpec((None, MOBA_HEADS * V_ROWS, tm), tcol),
        pl.BlockSpec((tm, hw), row),
        pl.BlockSpec((tm, hw), row),
        pl.BlockSpec((None, MLA_HEADS * V_ROWS, tm), tcol),
        pl.BlockSpec((tm, N_BRANCH * d), row),
    )
    return pl.pallas_call(
        functools.partial(_front_kernel, mla_scale=MLA_QK_DIM ** -0.5 * LOG2E),
        grid=(n // tm,),
        in_specs=in_specs,
        out_specs=out_specs,
        out_shape=out_shape,
        compiler_params=pltpu.CompilerParams(
            dimension_semantics=("arbitrary",), vmem_limit_bytes=VMEM_LIMIT_BYTES),
        name="front",
    )(x2, gattn, w1, wvt, bg, qng, wuq, kvng, wkv, wkvt, cosm, sinm)


HEADS_PER_STEP = 8
BIG = 1e30
RESCALE_SLACK = 64.0
MATMUL_LOOKAHEAD = 2


def _online_softmax_group(m_ref, acc_ref, score_blocks, vt_grp, chosen=None, shifts=None):
    n = len(score_blocks)
    chosen = chosen or [None] * n
    shifts = shifts or [None] * n
    m_old = m_ref[...]
    m_new = m_old
    for st, ch, sh in zip(score_blocks, chosen, shifts):
        rm = jnp.max(st, axis=0, keepdims=True)
        if sh is not None:
            rm = rm + sh
        m_new = jnp.maximum(m_new, rm if ch is None else jnp.where(ch, rm, NEG))
    probs = []
    for st, ch, sh in zip(score_blocks, chosen, shifts):
        off = m_new if sh is None else m_new - sh
        if ch is not None:
            off = jnp.where(ch, off, BIG)
        probs.append(jnp.exp2(st - off).astype(BF16))
    ot = _dot(vt_grp, jnp.concatenate(probs, axis=0))
    acc_ref[...] = jnp.exp2(m_old - m_new) * acc_ref[...] + ot
    m_ref[...] = m_new


def _streamed_softmax_group(m_ref, acc_ref, late_ref, score_blocks, vt_grp, chosen=None, shifts=None):
    n = len(score_blocks)
    chosen = chosen or [None] * n
    shifts = shifts or [None] * n
    m_old = m_ref[...]
    seen = jnp.full(m_old.shape, NEG, F32)
    probs = []
    for st, ch, sh in zip(score_blocks, chosen, shifts):
        off = m_old if sh is None else m_old - sh
        if ch is not None:
            off = jnp.where(ch, off, BIG)
        probs.append(jnp.exp2(st - off).astype(BF16))
        rm = jnp.max(st, axis=0, keepdims=True)
        if sh is not None:
            rm = rm + sh
        seen = jnp.maximum(seen, rm if ch is None else jnp.where(ch, rm, NEG))
    acc_ref[...] = acc_ref[...] + _dot(vt_grp, jnp.concatenate(probs, axis=0))
    late_ref[...] = jnp.where(seen > m_old + RESCALE_SLACK, 1.0, late_ref[...])


def _pipelined(stages, issue, consume, lookahead):
    queue = []
    for n in range(len(stages) + lookahead):
        if n < len(stages):
            queue.append((stages[n], issue(stages[n])))
        if n >= lookahead:
            consume(*queue.pop(0))


def _in_trips(n, sweep, trip_groups):
    def trip(t, carry):
        sweep([trip_groups * t + i for i in range(trip_groups)])
        return carry

    lax.fori_loop(0, n // trip_groups, trip, 0)
    done = (n // trip_groups) * trip_groups
    size = trip_groups // 2
    while size >= 1:
        has = ((n - done) // size) % 2 == 1
        first = done + ((n - done) // (2 * size)) * (2 * size)
        pl.when(has)(functools.partial(sweep, [first + i for i in range(size)]))
        size //= 2


def _reset_softmax_state(m_s, acc_s):
    m_s[...] = jnp.full(m_s.shape, NEG, F32)
    acc_s[...] = jnp.zeros(acc_s.shape, F32)


def _normalised_heads(acc_s, v_dim):
    rows = []
    for hh in range(HEADS_PER_STEP):
        acc = acc_s[hh]
        rows.append(acc[0:v_dim, :] * (1.0 / acc[v_dim:v_dim + 1, :]))
    return jnp.concatenate(rows, axis=0)


def _moba_kernel(q_ref, k_new_ref, vt_new_ref, kmean_ref, near_ref, cfar_ref, o_ref,
                 m_s, acc_s, late_s, sel_s, k_ref, vt_ref, *, unroll, trip):
    blk = MOBA_BLOCK
    qb = pl.program_id(2)
    nb = kmean_ref.shape[0]
    lane = lax.broadcasted_iota(jnp.int32, (blk, LANES), 1)
    blk_id = lax.broadcasted_iota(jnp.int32, (nb, blk), 0).astype(F32)
    qbf = qb.astype(F32)
    has_prev = qb >= 1
    _reset_softmax_state(m_s, acc_s)

    @pl.when(qb == 0)
    def _():
        k_ref[...] = jnp.zeros_like(k_ref)
        vt_ref[...] = jnp.zeros_like(vt_ref)

    own = pl.ds(pl.multiple_of(qb * blk, blk), blk)
    k_ref[own, :] = k_new_ref[...]
    vt_ref[:, own] = vt_new_ref[...]

    def lane_group(hh):
        return slice((hh // 2) * LANES, (hh // 2 + 1) * LANES)

    def k_block(hh, j):
        return k_ref[pl.ds(pl.multiple_of(j * blk, blk), blk), lane_group(hh)]

    def vt_blocks(hh, j, n):
        return vt_ref[hh * V_ROWS:(hh + 1) * V_ROWS, pl.ds(pl.multiple_of(j * blk, blk), n * blk)]

    q_heads, prev_chosen = [], []
    for hh in range(HEADS_PER_STEP):
        in_head = (lane >= (hh % 2) * MOBA_HEAD_DIM) & (lane < (hh % 2 + 1) * MOBA_HEAD_DIM)
        qh = jnp.where(in_head, q_ref[:, lane_group(hh)].astype(F32), 0.0).astype(BF16)
        q_heads.append(qh)

        kmean = kmean_ref[:, lane_group(hh)].astype(BF16)
        gate = jnp.where(blk_id < qbf, _dot_nt(kmean, qh), -jnp.inf)
        sel = jnp.zeros((nb, blk), F32)
        for _ in range(min(MOBA_TOPK, nb)):
            best = jnp.max(gate, axis=0, keepdims=True)
            first = jnp.min(jnp.where(gate == best, blk_id, float(nb)), axis=0, keepdims=True)
            pick = blk_id == first
            sel = jnp.where(pick, 1.0, sel)
            gate = jnp.where(pick, -jnp.inf, gate)
        prev_chosen.append(jnp.max(jnp.where(blk_id == qbf - 1.0, sel, 0.0), axis=0, keepdims=True))
        sel_s[hh] = jnp.where(blk_id < qbf - 1.0, sel, 0.0)

    lo = jnp.maximum(qb - 1, 0)
    tab = jnp.where(has_prev, 0, 1)

    def nearest_exact():
        for first in range(0, HEADS_PER_STEP, HEADS_PER_STEP // 2):
            heads = range(first, first + HEADS_PER_STEP // 2)
            scores = {hh: [_dot_nt(k_block(hh, lo + u), q_heads[hh]) + near_ref[hh, tab + u]
                           for u in range(2)] for hh in heads}
            for hh in heads:
                chosen = [jnp.where(has_prev, prev_chosen[hh], 1.0) > 0.5,
                          jnp.where(has_prev, jnp.ones((1, blk), F32), 0.0) > 0.5]
                _online_softmax_group(m_s.at[hh], acc_s.at[hh], scores[hh], vt_blocks(hh, lo, 2), chosen)

    n_groups = (qb - 1 + unroll - 1) // unroll

    def far_sweep(groups, streamed=True):
        def start_of(g):
            return jnp.minimum(g * unroll, nb - unroll)

        def scores(stage):
            g, hh = stage
            keys = k_ref[pl.ds(pl.multiple_of(start_of(g) * blk, blk), unroll * blk), lane_group(hh)]
            st = _dot_nt(keys, q_heads[hh])
            return [st[u * blk:(u + 1) * blk, :] for u in range(unroll)]

        def softmax(stage, blocks):
            g, hh = stage
            start = start_of(g)
            chosen = [jnp.where(start + u >= g * unroll, sel_s[hh, pl.ds(start + u, 1), :], 0.0) > 0.5
                      for u in range(unroll)]
            operands = (blocks, vt_blocks(hh, start, unroll), chosen, [cfar_ref[hh][:, 0:1]] * unroll)
            if streamed:
                _streamed_softmax_group(m_s.at[hh], acc_s.at[hh], late_s.at[hh], *operands)
            else:
                _online_softmax_group(m_s.at[hh], acc_s.at[hh], *operands)

        stages = [(g, hh) for g in groups for hh in range(HEADS_PER_STEP)]
        _pipelined(stages, scores, softmax, MATMUL_LOOKAHEAD if streamed else len(stages))

    late_s[...] = jnp.zeros_like(late_s)
    nearest_exact()
    _in_trips(n_groups, far_sweep, trip)

    @pl.when(jnp.max(late_s[...]) > 0.5)
    def _():
        _reset_softmax_state(m_s, acc_s)
        nearest_exact()

        def exact(g, carry):
            far_sweep([g], streamed=False)
            return carry

        lax.fori_loop(0, n_groups, exact, 0)

    o_ref[...] = _normalised_heads(acc_s, MOBA_HEAD_DIM).T.astype(o_ref.dtype)


def _moba(qa, ka, vat, kmean, near, cfar, *, batch, seq, unroll, trip):
    n = qa.shape[0]
    blk = MOBA_BLOCK
    nb = seq // blk
    assert nb >= unroll and nb >= 2
    steps = MOBA_HEADS // HEADS_PER_STEP
    width = HEADS_PER_STEP * MOBA_HEAD_DIM
    return pl.pallas_call(
        functools.partial(_moba_kernel, unroll=unroll, trip=trip),
        grid=(batch, steps, nb),
        in_specs=[
            pl.BlockSpec((blk, width), lambda b, p, i: (b * nb + i, p)),
            pl.BlockSpec((blk, width), lambda b, p, i: (b * nb + i, p)),
            pl.BlockSpec((None, HEADS_PER_STEP * V_ROWS, blk), lambda b, p, i: (b, p, i)),
            pl.BlockSpec((None, nb, width), lambda b, p, i: (b, 0, p)),
            _per_head_group((HEADS_PER_STEP, 3, blk, blk), lambda b, p, i: (p, 0, 0, 0)),
            pl.BlockSpec((HEADS_PER_STEP, 1, LANES), lambda b, p, i: (p, 0, 0)),
        ],
        out_specs=pl.BlockSpec((blk, width), lambda b, p, i: (b * nb + i, p)),
        out_shape=jax.ShapeDtypeStruct((n, MOBA_WIDTH), BF16),
        scratch_shapes=[
            pltpu.VMEM((HEADS_PER_STEP, 1, blk), F32),
            pltpu.VMEM((HEADS_PER_STEP, V_ROWS, blk), F32),
            pltpu.VMEM((HEADS_PER_STEP, 1, blk), F32),
            pltpu.VMEM((HEADS_PER_STEP, nb, blk), F32),
            pltpu.VMEM((seq, width), BF16),
            pltpu.VMEM((HEADS_PER_STEP * V_ROWS, seq), BF16),
        ],
        compiler_params=pltpu.CompilerParams(
            dimension_semantics=("arbitrary", "arbitrary", "arbitrary"),
            vmem_limit_bytes=VMEM_LIMIT_BYTES),
        name="moba",
    )(qa, ka, vat, kmean, near, cfar)


def _mla_kernel(q_ref, k_new_ref, vt_new_ref, o_ref, m_s, acc_s, late_s, k_ref, vt_ref,
                *, tq, tk, unroll, trip):
    qi = pl.program_id(2)
    n_diag = tq // tk
    key = lax.broadcasted_iota(jnp.int32, (tk, tq), 0)
    qry = lax.broadcasted_iota(jnp.int32, (tk, tq), 1)
    q_heads = [q_ref[:, hh * LANES:(hh + 1) * LANES] for hh in range(HEADS_PER_STEP)]

    own = pl.ds(pl.multiple_of(qi * tq, tq), tq)
    k_ref[own, :] = k_new_ref[...]
    vt_ref[:, own] = vt_new_ref[...]

    _reset_softmax_state(m_s, acc_s)

    def k_tile(hh, j):
        return k_ref[pl.ds(pl.multiple_of(j * tk, tk), tk), hh * LANES:(hh + 1) * LANES]

    def vt_tiles(hh, j, n):
        return vt_ref[hh * V_ROWS:(hh + 1) * V_ROWS, pl.ds(pl.multiple_of(j * tk, tk), n * tk)]

    def group(first, n, masked):
        scores = [[_dot_nt(k_tile(hh, first + u), q_heads[hh]) for u in range(n)]
                  for hh in range(HEADS_PER_STEP)]
        for hh in range(HEADS_PER_STEP):
            blocks = scores[hh]
            if masked:
                blocks = [jnp.where(key + u * tk <= qry, st, NEG) for u, st in enumerate(blocks)]
            _online_softmax_group(m_s.at[hh], acc_s.at[hh], blocks, vt_tiles(hh, first, n))

    n_visible = qi * n_diag
    n_groups = (n_visible + unroll - 1) // unroll

    def visible_sweep(groups):
        def start_of(g):
            return jnp.maximum(jnp.minimum(g * unroll, n_visible - unroll), 0)

        def scores(stage):
            g, hh = stage
            start = pl.multiple_of(start_of(g) * tk, tk)
            st = _dot_nt(k_ref[pl.ds(start, unroll * tk), hh * LANES:(hh + 1) * LANES], q_heads[hh])
            return [st[u * tk:(u + 1) * tk, :] for u in range(unroll)]

        def softmax(stage, blocks):
            g, hh = stage
            start = start_of(g)
            chosen = None
            if unroll > 1:
                chosen = [jnp.where((start + u >= g * unroll) & (start + u < n_visible),
                                    jnp.ones((1, tq), F32), 0.0) > 0.5 for u in range(unroll)]
            _streamed_softmax_group(m_s.at[hh], acc_s.at[hh], late_s.at[hh], blocks,
                                    vt_tiles(hh, start, unroll), chosen)

        stages = [(g, hh) for g in groups for hh in range(HEADS_PER_STEP)]
        _pipelined(stages, scores, softmax, MATMUL_LOOKAHEAD)

    late_s[...] = jnp.zeros_like(late_s)
    group(qi * n_diag, n_diag, True)
    _in_trips(n_groups, visible_sweep, trip)

    @pl.when(jnp.max(late_s[...]) > 0.5)
    def _():
        _reset_softmax_state(m_s, acc_s)
        group(qi * n_diag, n_diag, True)

        def exact(j, carry):
            group(j, 1, False)
            return carry

        lax.fori_loop(0, n_visible, exact, 0)

    o_ref[...] = _normalised_heads(acc_s, MLA_V_DIM).T.astype(o_ref.dtype)


def _mla(qm, km, vmt, *, batch, seq, tq, tk, unroll, trip):
    n = qm.shape[0]
    nq = seq // tq
    assert tq % tk == 0 and seq // tk >= unroll
    pairs = MLA_HEADS // HEADS_PER_STEP
    return pl.pallas_call(
        functools.partial(_mla_kernel, tq=tq, tk=tk, unroll=unroll, trip=trip),
        grid=(batch, pairs, nq),
        in_specs=[
            pl.BlockSpec((tq, HEADS_PER_STEP * LANES), lambda b, p, i: (b * nq + i, p)),
            pl.BlockSpec((tq, HEADS_PER_STEP * LANES), lambda b, p, i: (b * nq + i, p)),
            pl.BlockSpec((None, HEADS_PER_STEP * V_ROWS, tq), lambda b, p, i: (b, p, i)),
        ],
        out_specs=pl.BlockSpec((tq, HEADS_PER_STEP * MLA_V_DIM), lambda b, p, i: (b * nq + i, p)),
        out_shape=jax.ShapeDtypeStruct((n, MLA_WIDTH), BF16),
        scratch_shapes=[
            pltpu.VMEM((HEADS_PER_STEP, 1, tq), F32),
            pltpu.VMEM((HEADS_PER_STEP, V_ROWS, tq), F32),
            pltpu.VMEM((HEADS_PER_STEP, 1, tq), F32),
            pltpu.VMEM((seq, HEADS_PER_STEP * LANES), BF16),
            pltpu.VMEM((HEADS_PER_STEP * V_ROWS, seq), BF16),
        ],
        compiler_params=pltpu.CompilerParams(
            dimension_semantics=("arbitrary", "arbitrary", "arbitrary"),
            vmem_limit_bytes=VMEM_LIMIT_BYTES),
        name="mla",
    )(qm, km, vmt)


def _back_kernel(x_ref, ya_ref, yb_ref, g_ref, wa_ref, wb_ref, wo_ref, gffn_ref, wup_ref,
                 cw_ref, cb_ref, wdn_ref, gfin_ref, o_ref,
                 carry_ref, act_ref, hn_ref, *, steps_per_seq, d_ff, fc, final):
    tm, d = x_ref.shape
    halo = SUBLANES

    @pl.when(pl.program_id(0) % steps_per_seq == 0)
    def _():
        carry_ref[...] = jnp.zeros_like(carry_ref)

    for c in range(0, d, fc):
        cols = slice(c, c + fc)
        mixed = (g_ref[:, cols].astype(F32) * _dot(ya_ref[...], wa_ref[:, cols])
                 + g_ref[:, d + c:d + c + fc].astype(F32) * _dot(yb_ref[...], wb_ref[:, cols]))
        hn_ref[:, cols] = mixed.astype(BF16)
    h1 = x_ref[...] + _dot(hn_ref[...], wo_ref[...])
    hn_ref[...] = _rms(h1, gffn_ref[...]).astype(BF16)

    def up_conv(col0):
        cols = slice(col0, col0 + fc)
        u = _dot(hn_ref[...], wup_ref[:, cols])
        rows = jnp.concatenate([carry_ref[:, cols], u], axis=0)
        carry_ref[:, cols] = u[tm - halo:tm, :]
        w = cw_ref[:, cols]
        y = cb_ref[:, cols]
        for t in range(CONV_WIDTH):
            back = CONV_WIDTH - 1 - t
            shifted = pltpu.roll(rows, back, axis=0) if back else rows
            y = y + w[t:t + 1, :] * shifted[halo:, :]
        return y

    for c in range(d_ff // fc):
        yg = up_conv(c * fc)
        yv = up_conv(d_ff + c * fc)
        act_ref[:, c * fc:(c + 1) * fc] = (yg * _sigmoid(yg) * yv).astype(BF16)

    h2 = h1 + _dot(act_ref[...], wdn_ref[...])
    o_ref[...] = _rms(h2, gfin_ref[...]) if final else h2


def _back(x2, ya, yb, g, wa, wb, wo, gffn, wup, cw, cb, wdn, gfin, *, seq, tm, fc, final):
    n, d = x2.shape
    d_ff = wdn.shape[0]
    row = lambda i: (i, 0)
    return pl.pallas_call(
        functools.partial(_back_kernel, steps_per_seq=seq // tm, d_ff=d_ff, fc=fc, final=final),
        grid=(n // tm,),
        in_specs=[
            pl.BlockSpec((tm, d), row),
            pl.BlockSpec((tm, ya.shape[1]), row),
            pl.BlockSpec((tm, yb.shape[1]), row),
            pl.BlockSpec((tm, g.shape[1]), row),
            _resident(wa.shape), _resident(wb.shape), _resident(wo.shape), _resident(gffn.shape),
            _resident(wup.shape), _resident(cw.shape), _resident(cb.shape), _resident(wdn.shape),
            _resident(gfin.shape),
        ],
        out_specs=pl.BlockSpec((tm, d), row),
        out_shape=jax.ShapeDtypeStruct((n, d), F32),
        scratch_shapes=[
            pltpu.VMEM((SUBLANES, 2 * d_ff), F32),
            pltpu.VMEM((tm, d_ff), BF16),
            pltpu.VMEM((tm, d), BF16),
        ],
        compiler_params=pltpu.CompilerParams(
            dimension_semantics=("arbitrary",), vmem_limit_bytes=VMEM_LIMIT_BYTES),
        name="back",
    )(x2, ya, yb, g, wa, wb, wo, gffn, wup, cw, cb, wdn, gfin)


def _t5_bucket_np(rel):
    n = np.maximum(rel, 0)
    max_exact = REL_BUCKETS // 2
    nf = np.maximum(n, 1).astype(np.float32)
    large = max_exact + (np.log(nf / np.float32(max_exact)) / np.float32(math.log(REL_MAX_DIST / max_exact))
                         * np.float32(REL_BUCKETS - max_exact)).astype(np.int32)
    large = np.minimum(large, REL_BUCKETS - 1)
    return np.where(n < max_exact, n, large)


def _moba_bias_tables(rel_bias):
    blk = MOBA_BLOCK
    period = 2 * blk
    assert REL_MAX_DIST <= blk + 1
    bias_h = rel_bias.T.astype(F32)
    slot = np.arange(period, dtype=np.int32)
    dist = np.where(slot < blk, slot, slot - period)

    def lookup(bucket):
        onehot = jnp.asarray(bucket)[None, :] == jnp.arange(REL_BUCKETS)[:, None]
        return jnp.sum(jnp.where(onehot[None], bias_h[:, :, None], 0.0), axis=1)

    def toeplitz(v):
        tiled = jnp.tile(v, (1, blk))[:, :blk * (period - 1)]
        return tiled.reshape(v.shape[0], blk, period - 1)[:, :, :blk]

    town = toeplitz(jnp.where(jnp.asarray(dist >= 0), lookup(_t5_bucket_np(dist)) * LOG2E, NEG))
    tprev = toeplitz(lookup(_t5_bucket_np(dist + blk)) * LOG2E)
    near = jnp.stack([tprev, town, jnp.zeros_like(town)], axis=1)
    cfar = bias_h[:, int(_t5_bucket_np(np.int32(blk + 1)))] * LOG2E
    cfar = jnp.broadcast_to(cfar[:, None, None], (MOBA_HEADS, 1, LANES))
    return near, cfar


def _pad_heads(w, n_heads, width, padded=LANES):
    r = w.shape[0]
    w = w.reshape(r, n_heads, width)
    out = jnp.zeros((r, n_heads, padded), w.dtype).at[:, :, :width].set(w)
    return out.reshape(r, n_heads * padded)


def _rope_slot(w_rope):
    r = w_rope.shape[0]
    return jnp.zeros((r, LANES), w_rope.dtype).at[:, MLA_NOPE_DIM:MLA_NOPE_DIM + MLA_ROPE_DIM].set(w_rope)


def _front_weights(w_in, w_uq, w_ukv):
    d = w_in.shape[0]
    s0 = MOBA_WIDTH
    s1 = 2 * MOBA_WIDTH
    s2 = 3 * MOBA_WIDTH
    s3 = s2 + MLA_Q_RANK
    s4 = s3 + MLA_KV_RANK
    s5 = s4 + MLA_ROPE_DIM
    w1 = jnp.concatenate([
        w_in[:, :s0], w_in[:, s0:s1],
        w_in[:, s2:s3], w_in[:, s3:s4],
        _rope_slot(w_in[:, s4:s5]),
        w_in[:, s5:],
    ], axis=1).astype(BF16)
    assert w1.shape == (d, _C_G + N_BRANCH * d)
    wvt = _pad_heads(w_in[:, s1:s2], MOBA_HEADS, MOBA_HEAD_DIM, V_ROWS).T.astype(BF16)

    wuq = _pad_heads(w_uq, MLA_HEADS, MLA_QK_DIM).astype(BF16)

    r = w_ukv.shape[0]
    ukv = w_ukv.reshape(r, MLA_HEADS, MLA_NOPE_DIM + MLA_V_DIM)
    wk = _pad_heads(ukv[:, :, :MLA_NOPE_DIM].reshape(r, -1), MLA_HEADS, MLA_NOPE_DIM).astype(BF16)
    wkvt = _pad_heads(ukv[:, :, MLA_NOPE_DIM:].reshape(r, -1), MLA_HEADS, MLA_V_DIM, V_ROWS).T.astype(BF16)
    return w1, wvt, wuq, wk, wkvt


def _rope_lane_tables(seq):
    dim = MLA_ROPE_DIM
    inv_freq = ROPE_THETA ** (-jnp.arange(0, dim, 2, dtype=F32) / dim)
    ang = jnp.arange(seq, dtype=F32)[:, None] * inv_freq[None, :]
    cos, sin = jnp.cos(ang), jnp.sin(ang)
    tail = jnp.zeros((seq, LANES - MLA_QK_DIM), F32)
    cosm = jnp.concatenate([jnp.ones((seq, MLA_NOPE_DIM), F32), cos, cos, tail], axis=1)
    sinm = jnp.concatenate([jnp.zeros((seq, MLA_NOPE_DIM), F32), -sin, sin, tail], axis=1)
    return cosm, sinm


class _Tiles:
    def __init__(self, seq):
        self.tm = 512 if seq % 512 == 0 else MOBA_BLOCK
        self.fc = 256
        self.mla_tq = 256
        self.mla_tk = 256
        self.mla_unroll = 2
        self.mla_trip = 8
        self.moba_unroll = 2
        self.moba_trip = 8


def kernel(x, norm_attn_g, w_in, b_gate, q_norm_g, w_uq, kv_norm_g, w_ukv, rel_bias,
           w_branch_moba, w_branch_mla, w_out, norm_ffn_g, w_up, conv_w, conv_b, w_down,
           norm_final_g):
    batch, seq, d = x.shape
    depth = w_in.shape[0]
    assert seq % MOBA_BLOCK == 0
    t = _Tiles(seq)
    d_ff = w_down.shape[1]
    assert d_ff % t.fc == 0
    n = batch * seq

    cosm, sinm = _rope_lane_tables(seq)
    near, cfar = _moba_bias_tables(rel_bias)
    row = lambda v: v.reshape(1, -1).astype(F32)

    h = x.reshape(n, d)
    for l in range(depth):
        w1, wvt, wuq, wk, wkvt = _front_weights(w_in[l], w_uq[l], w_ukv[l])
        qa, ka, kmean, vat, qm, km, vmt, g = _front(
            h, row(norm_attn_g[l]), w1, wvt, row(b_gate[l]), row(q_norm_g[l]), wuq,
            row(kv_norm_g[l]), wk, wkvt, cosm, sinm, seq=seq, tm=t.tm)
        kmean = kmean.reshape(batch, seq // MOBA_BLOCK, MOBA_WIDTH)
        ya = _moba(qa, ka, vat, kmean, near, cfar, batch=batch, seq=seq,
                   unroll=t.moba_unroll, trip=t.moba_trip)
        yb = _mla(qm, km, vmt, batch=batch, seq=seq, tq=t.mla_tq, tk=t.mla_tk,
                  unroll=t.mla_unroll, trip=t.mla_trip)
        h = _back(h, ya, yb, g, w_branch_moba[l].astype(BF16), w_branch_mla[l].astype(BF16),
                  w_out[l].astype(BF16), row(norm_ffn_g[l]), w_up[l].astype(BF16),
                  conv_w[l].astype(F32), row(conv_b[l]), w_down[l].astype(BF16),
                  row(norm_final_g), seq=seq, tm=t.tm, fc=t.fc, final=(l == depth - 1))
    return h.reshape(batch, seq, d)
```

```python
import functools
import math

import numpy as np
import jax
import jax.numpy as jnp
from jax import lax
from jax.experimental import pallas as pl
from jax.experimental.pallas import tpu as pltpu

MOBA_HEADS = 8
MOBA_HEAD_DIM = 64
MOBA_BLOCK = 256
MOBA_TOPK = 3
MLA_HEADS = 8
MLA_Q_RANK = 256
MLA_KV_RANK = 128
MLA_NOPE_DIM = 64
MLA_ROPE_DIM = 32
MLA_V_DIM = 64
ROPE_THETA = 10000.0
REL_BUCKETS = 32
REL_MAX_DIST = 128
CONV_WIDTH = 3
N_BRANCH = 2
EPS = 1e-6

MOBA_WIDTH = MOBA_HEADS * MOBA_HEAD_DIM
MLA_QK_DIM = MLA_NOPE_DIM + MLA_ROPE_DIM
MLA_WIDTH = MLA_HEADS * MLA_V_DIM

LANES = 128
SUBLANES = 8
VMEM_LIMIT_BYTES = 56 * 1024 * 1024

NEG = -1e30
LOG2E = math.log2(math.e)

F32 = jnp.float32
BF16 = jnp.bfloat16


def _dot(a, b):
    return jnp.dot(a, b, preferred_element_type=F32)


def _dot_nt(a, b):
    return lax.dot_general(a, b, (((1,), (1,)), ((), ())), preferred_element_type=F32)


def _rms(x, g):
    return x * lax.rsqrt(jnp.mean(x * x, axis=-1, keepdims=True) + EPS) * g


def _sigmoid(z):
    return 1.0 / (1.0 + jnp.exp(-z))


def _resident(shape):
    nd = len(shape)
    return pl.BlockSpec(shape, lambda *_: (0,) * nd, pipeline_mode=pl.Buffered(1))


def _per_head_group(shape, index_map):
    return pl.BlockSpec(shape, index_map, pipeline_mode=pl.Buffered(1))


_C_QA = 0
_C_KA = _C_QA + MOBA_WIDTH
_C_CQ = _C_KA + MOBA_WIDTH
_C_CKV = _C_CQ + MLA_Q_RANK
_C_KRA = _C_CKV + MLA_KV_RANK
_C_G = _C_KRA + LANES

BF16_SUBLANES = 16
V_ROWS = MLA_V_DIM + BF16_SUBLANES
assert MOBA_HEAD_DIM == MLA_V_DIM


def _ones_rows(n_rows):
    r = lax.broadcasted_iota(jnp.int32, (n_rows, 1), 0).astype(F32)
    within = r - jnp.floor((r + 0.5) * (1.0 / V_ROWS)) * V_ROWS
    return jnp.where(within == MLA_V_DIM, 1.0, 0.0).astype(F32)


def _front_kernel(x_ref, gattn_ref, w1_ref, wvt_ref, bg_ref, qng_ref, wuq_ref, kvng_ref, wkv_ref,
                  wkvt_ref, cos_ref, sin_ref,
                  qa_ref, ka_ref, sel_ref, vat_ref, qm_ref, km_ref, vmt_ref, g_ref,
                  kmean_s, *, mla_scale, steps_per_seq):
    tm = x_ref.shape[0]
    blk = MOBA_BLOCK
    nb = kmean_s.shape[0]
    tile = pl.program_id(0) % steps_per_seq
    xn = _rms(x_ref[...], gattn_ref[...]).astype(BF16)

    def proj(a, b):
        return _dot(xn, w1_ref[:, a:b])

    qa = (proj(_C_QA, _C_KA) * (MOBA_HEAD_DIM ** -0.5 * LOG2E)).astype(BF16)
    qa_ref[...] = qa
    ka = proj(_C_KA, _C_CQ)
    ka_ref[...] = ka.astype(BF16)

    @pl.when(tile == 0)
    def _():
        kmean_s[...] = jnp.zeros_like(kmean_s)

    for i in range(tm // blk):
        kmean_s[pl.ds(tile * (tm // blk) + i, 1), :] = jnp.mean(
            ka[i * blk:(i + 1) * blk], axis=0, keepdims=True)

    lane = lax.broadcasted_iota(jnp.int32, (blk, LANES), 1)
    blk_id = lax.broadcasted_iota(jnp.int32, (nb, blk), 0).astype(F32)
    for i in range(tm // blk):
        qbf = (tile * (tm // blk) + i).astype(F32)
        for h in range(MOBA_HEADS):
            group = slice((h // 2) * LANES, (h // 2 + 1) * LANES)
            in_head = (lane >= (h % 2) * MOBA_HEAD_DIM) & (lane < (h % 2 + 1) * MOBA_HEAD_DIM)
            qh = jnp.where(in_head, qa[i * blk:(i + 1) * blk, group].astype(F32), 0.0).astype(BF16)
            gate = jnp.where(blk_id < qbf, _dot_nt(kmean_s[:, group].astype(BF16), qh), -jnp.inf)
            sel = jnp.zeros((nb, blk), F32)
            for _ in range(min(MOBA_TOPK, nb)):
                best = jnp.max(gate, axis=0, keepdims=True)
                first = jnp.min(jnp.where(gate == best, blk_id, float(nb)), axis=0, keepdims=True)
                pick = blk_id == first
                sel = jnp.where(pick, 1.0, sel)
                gate = jnp.where(pick, -jnp.inf, gate)
            sel_ref[h, :, i * blk:(i + 1) * blk] = jnp.where(blk_id < qbf, sel, 0.0)

    ones = _ones_rows(vat_ref.shape[0])
    vat_ref[...] = (_dot_nt(wvt_ref[...], xn) + ones).astype(BF16)

    cosm = cos_ref[...]
    sinm = sin_ref[...]
    lane = lax.broadcasted_iota(jnp.int32, (tm, LANES), 1)
    half = MLA_ROPE_DIM // 2

    def rope(a):
        x2_on_x1 = pltpu.roll(a, LANES - half, axis=1)
        x1_on_x2 = pltpu.roll(a, half, axis=1)
        swapped = jnp.where(lane < MLA_NOPE_DIM + half, x2_on_x1, x1_on_x2)
        return a * cosm + swapped * sinm

    cqn = _rms(proj(_C_CQ, _C_CKV), qng_ref[...]).astype(BF16)
    qq = _dot(cqn, wuq_ref[...])
    for h in range(MLA_HEADS):
        qm_ref[:, h * LANES:(h + 1) * LANES] = (
            rope(qq[:, h * LANES:(h + 1) * LANES]) * mla_scale).astype(BF16)

    ckv_kra = proj(_C_CKV, _C_G)
    ckvn = _rms(ckv_kra[:, :MLA_KV_RANK], kvng_ref[...]).astype(BF16)
    kv = _dot(ckvn, wkv_ref[...])
    kr = rope(ckv_kra[:, MLA_KV_RANK:])
    for h in range(MLA_HEADS):
        km_ref[:, h * LANES:(h + 1) * LANES] = (kv[:, h * LANES:(h + 1) * LANES] + kr).astype(BF16)
    vmt_ref[...] = (_dot_nt(wkvt_ref[...], ckvn) + ones).astype(BF16)

    g_ref[...] = _sigmoid(proj(_C_G, w1_ref.shape[1]) + bg_ref[...]).astype(BF16)


def _front(x2, gattn, w1, wvt, bg, qng, wuq, kvng, wkv, wkvt, cosm, sinm, *, seq, tm):
    n, d = x2.shape
    batch = n // seq
    hw = MLA_HEADS * LANES
    nb = seq // MOBA_BLOCK
    steps_per_seq = seq // tm
    row = lambda i: (i, 0)
    tcol = lambda i: (i // steps_per_seq, 0, i % steps_per_seq)
    out_shape = (
        jax.ShapeDtypeStruct((n, MOBA_WIDTH), BF16),
        jax.ShapeDtypeStruct((n, MOBA_WIDTH), BF16),
        jax.ShapeDtypeStruct((batch, MOBA_HEADS, nb, seq), F32),
        jax.ShapeDtypeStruct((batch, MOBA_HEADS * V_ROWS, seq), BF16),
        jax.ShapeDtypeStruct((n, hw), BF16),
        jax.ShapeDtypeStruct((n, hw), BF16),
        jax.ShapeDtypeStruct((batch, MLA_HEADS * V_ROWS, seq), BF16),
        jax.ShapeDtypeStruct((n, N_BRANCH * d), BF16),
    )
    in_specs = [
        pl.BlockSpec((tm, d), row),
        _resident(gattn.shape), _resident(w1.shape), _resident(wvt.shape), _resident(bg.shape),
        _resident(qng.shape), _resident(wuq.shape), _resident(kvng.shape), _resident(wkv.shape),
        _resident(wkvt.shape),
        pl.BlockSpec((tm, LANES), lambda i: (i % steps_per_seq, 0)),
        pl.BlockSpec((tm, LANES), lambda i: (i % steps_per_seq, 0)),
    ]
    out_specs = (
        pl.BlockSpec((tm, MOBA_WIDTH), row),
        pl.BlockSpec((tm, MOBA_WIDTH), row),
        pl.BlockSpec((None, MOBA_HEADS, nb, tm), lambda i: (i // steps_per_seq, 0, 0, i % steps_per_seq)),
        pl.BlockSpec((None, MOBA_HEADS * V_ROWS, tm), tcol),
        pl.BlockSpec((tm, hw), row),
        pl.BlockSpec((tm, hw), row),
        pl.BlockSpec((None, MLA_HEADS * V_ROWS, tm), tcol),
        pl.BlockSpec((tm, N_BRANCH * d), row),
    )
    return pl.pallas_call(
        functools.partial(_front_kernel, mla_scale=MLA_QK_DIM ** -0.5 * LOG2E,
                          steps_per_seq=steps_per_seq),
        grid=(n // tm,),
        in_specs=in_specs,
        out_specs=out_specs,
        out_shape=out_shape,
        scratch_shapes=[pltpu.VMEM((nb, MOBA_WIDTH), F32)],
        compiler_params=pltpu.CompilerParams(
            dimension_semantics=("arbitrary",), vmem_limit_bytes=VMEM_LIMIT_BYTES),
        name="front",
    )(x2, gattn, w1, wvt, bg, qng, wuq, kvng, wkv, wkvt, cosm, sinm)


HEADS_PER_STEP = 8
BIG = 1e30
RESCALE_SLACK = 64.0
MATMUL_LOOKAHEAD = 2


def _online_softmax_group(m_ref, acc_ref, score_blocks, vt_grp, chosen=None, shifts=None):
    n = len(score_blocks)
    chosen = chosen or [None] * n
    shifts = shifts or [None] * n
    m_old = m_ref[...]
    m_new = m_old
    for st, ch, sh in zip(score_blocks, chosen, shifts):
        rm = jnp.max(st, axis=0, keepdims=True)
        if sh is not None:
            rm = rm + sh
        m_new = jnp.maximum(m_new, rm if ch is None else jnp.where(ch, rm, NEG))
    probs = []
    for st, ch, sh in zip(score_blocks, chosen, shifts):
        off = m_new if sh is None else m_new - sh
        if ch is not None:
            off = jnp.where(ch, off, BIG)
        probs.append(jnp.exp2(st - off).astype(BF16))
    ot = _dot(vt_grp, jnp.concatenate(probs, axis=0))
    acc_ref[...] = jnp.exp2(m_old - m_new) * acc_ref[...] + ot
    m_ref[...] = m_new


def _streamed_softmax_group(m_ref, acc_ref, late_ref, score_blocks, vt_grp, chosen=None, shifts=None):
    n = len(score_blocks)
    chosen = chosen or [None] * n
    shifts = shifts or [None] * n
    m_old = m_ref[...]
    seen = jnp.full(m_old.shape, NEG, F32)
    probs = []
    for st, ch, sh in zip(score_blocks, chosen, shifts):
        off = m_old if sh is None else m_old - sh
        if ch is not None:
            off = jnp.where(ch, off, BIG)
        probs.append(jnp.exp2(st - off).astype(BF16))
        rm = jnp.max(st, axis=0, keepdims=True)
        if sh is not None:
            rm = rm + sh
        seen = jnp.maximum(seen, rm if ch is None else jnp.where(ch, rm, NEG))
    acc_ref[...] = acc_ref[...] + _dot(vt_grp, jnp.concatenate(probs, axis=0))
    late_ref[...] = jnp.where(seen > m_old + RESCALE_SLACK, 1.0, late_ref[...])


def _pipelined(stages, issue, consume, lookahead):
    queue = []
    for n in range(len(stages) + lookahead):
        if n < len(stages):
            queue.append((stages[n], issue(stages[n])))
        if n >= lookahead:
            consume(*queue.pop(0))


def _in_trips(n, sweep, trip_groups):
    def trip(t, carry):
        sweep([trip_groups * t + i for i in range(trip_groups)])
        return carry

    lax.fori_loop(0, n // trip_groups, trip, 0)
    done = (n // trip_groups) * trip_groups
    size = trip_groups // 2
    while size >= 1:
        has = ((n - done) // size) % 2 == 1
        first = done + ((n - done) // (2 * size)) * (2 * size)
        pl.when(has)(functools.partial(sweep, [first + i for i in range(size)]))
        size //= 2


def _reset_softmax_state(m_s, acc_s):
    m_s[...] = jnp.full(m_s.shape, NEG, F32)
    acc_s[...] = jnp.zeros(acc_s.shape, F32)


def _normalised_heads(acc_s, v_dim):
    rows = []
    for hh in range(HEADS_PER_STEP):
        acc = acc_s[hh]
        rows.append(acc[0:v_dim, :] * (1.0 / acc[v_dim:v_dim + 1, :]))
    return jnp.concatenate(rows, axis=0)


def _moba_kernel(q_ref, k_new_ref, vt_new_ref, sel_ref, near_ref, cfar_ref, o_ref,
                 m_s, acc_s, late_s, sel_s, k_ref, vt_ref, *, unroll, trip):
    blk = MOBA_BLOCK
    qb = pl.program_id(2)
    nb = sel_ref.shape[1]
    lane = lax.broadcasted_iota(jnp.int32, (blk, LANES), 1)
    blk_id = lax.broadcasted_iota(jnp.int32, (nb, blk), 0).astype(F32)
    qbf = qb.astype(F32)
    has_prev = qb >= 1
    _reset_softmax_state(m_s, acc_s)

    @pl.when(qb == 0)
    def _():
        k_ref[...] = jnp.zeros_like(k_ref)
        vt_ref[...] = jnp.zeros_like(vt_ref)

    own = pl.ds(pl.multiple_of(qb * blk, blk), blk)
    k_ref[own, :] = k_new_ref[...]
    vt_ref[:, own] = vt_new_ref[...]

    def lane_group(hh):
        return slice((hh // 2) * LANES, (hh // 2 + 1) * LANES)

    def k_block(hh, j):
        return k_ref[pl.ds(pl.multiple_of(j * blk, blk), blk), lane_group(hh)]

    def vt_blocks(hh, j, n):
        return vt_ref[hh * V_ROWS:(hh + 1) * V_ROWS, pl.ds(pl.multiple_of(j * blk, blk), n * blk)]

    q_heads, prev_chosen = [], []
    for hh in range(HEADS_PER_STEP):
        in_head = (lane >= (hh % 2) * MOBA_HEAD_DIM) & (lane < (hh % 2 + 1) * MOBA_HEAD_DIM)
        qh = jnp.where(in_head, q_ref[:, lane_group(hh)].astype(F32), 0.0).astype(BF16)
        q_heads.append(qh)

        sel = sel_ref[hh]
        prev_chosen.append(jnp.max(jnp.where(blk_id == qbf - 1.0, sel, 0.0), axis=0, keepdims=True))
        sel_s[hh] = jnp.where(blk_id < qbf - 1.0, sel, 0.0)

    lo = jnp.maximum(qb - 1, 0)
    tab = jnp.where(has_prev, 0, 1)

    def nearest_exact():
        for first in range(0, HEADS_PER_STEP, HEADS_PER_STEP // 2):
            heads = range(first, first + HEADS_PER_STEP // 2)
            scores = {hh: [_dot_nt(k_block(hh, lo + u), q_heads[hh]) + near_ref[hh, tab + u]
                           for u in range(2)] for hh in heads}
            for hh in heads:
                chosen = [jnp.where(has_prev, prev_chosen[hh], 1.0) > 0.5,
                          jnp.where(has_prev, jnp.ones((1, blk), F32), 0.0) > 0.5]
                _online_softmax_group(m_s.at[hh], acc_s.at[hh], scores[hh], vt_blocks(hh, lo, 2), chosen)

    n_groups = (qb - 1 + unroll - 1) // unroll

    def far_sweep(groups, streamed=True):
        def start_of(g):
            return jnp.minimum(g * unroll, nb - unroll)

        def scores(stage):
            g, hh = stage
            keys = k_ref[pl.ds(pl.multiple_of(start_of(g) * blk, blk), unroll * blk), lane_group(hh)]
            st = _dot_nt(keys, q_heads[hh])
            return [st[u * blk:(u + 1) * blk, :] for u in range(unroll)]

        def softmax(stage, blocks):
            g, hh = stage
            start = start_of(g)
            chosen = [jnp.where(start + u >= g * unroll, sel_s[hh, pl.ds(start + u, 1), :], 0.0) > 0.5
                      for u in range(unroll)]
            operands = (blocks, vt_blocks(hh, start, unroll), chosen, [cfar_ref[hh][:, 0:1]] * unroll)
            if streamed:
                _streamed_softmax_group(m_s.at[hh], acc_s.at[hh], late_s.at[hh], *operands)
            else:
                _online_softmax_group(m_s.at[hh], acc_s.at[hh], *operands)

        stages = [(g, hh) for g in groups for hh in range(HEADS_PER_STEP)]
        _pipelined(stages, scores, softmax, MATMUL_LOOKAHEAD if streamed else len(stages))

    late_s[...] = jnp.zeros_like(late_s)
    nearest_exact()
    _in_trips(n_groups, far_sweep, trip)

    @pl.when(jnp.max(late_s[...]) > 0.5)
    def _():
        _reset_softmax_state(m_s, acc_s)
        nearest_exact()

        def exact(g, carry):
            far_sweep([g], streamed=False)
            return carry

        lax.fori_loop(0, n_groups, exact, 0)

    o_ref[...] = _normalised_heads(acc_s, MOBA_HEAD_DIM).T.astype(o_ref.dtype)


def _moba(qa, ka, vat, sel, near, cfar, *, batch, seq, unroll, trip):
    n = qa.shape[0]
    blk = MOBA_BLOCK
    nb = seq // blk
    assert nb >= unroll and nb >= 2
    steps = MOBA_HEADS // HEADS_PER_STEP
    width = HEADS_PER_STEP * MOBA_HEAD_DIM
    return pl.pallas_call(
        functools.partial(_moba_kernel, unroll=unroll, trip=trip),
        grid=(batch, steps, nb),
        in_specs=[
            pl.BlockSpec((blk, width), lambda b, p, i: (b * nb + i, p)),
            pl.BlockSpec((blk, width), lambda b, p, i: (b * nb + i, p)),
            pl.BlockSpec((None, HEADS_PER_STEP * V_ROWS, blk), lambda b, p, i: (b, p, i)),
            pl.BlockSpec((None, HEADS_PER_STEP, nb, blk), lambda b, p, i: (b, p, 0, i)),
            _per_head_group((HEADS_PER_STEP, 3, blk, blk), lambda b, p, i: (p, 0, 0, 0)),
            pl.BlockSpec((HEADS_PER_STEP, 1, LANES), lambda b, p, i: (p, 0, 0)),
        ],
        out_specs=pl.BlockSpec((blk, width), lambda b, p, i: (b * nb + i, p)),
        out_shape=jax.ShapeDtypeStruct((n, MOBA_WIDTH), BF16),
        scratch_shapes=[
            pltpu.VMEM((HEADS_PER_STEP, 1, blk), F32),
            pltpu.VMEM((HEADS_PER_STEP, V_ROWS, blk), F32),
            pltpu.VMEM((HEADS_PER_STEP, 1, blk), F32),
            pltpu.VMEM((HEADS_PER_STEP, nb, blk), F32),
            pltpu.VMEM((seq, width), BF16),
            pltpu.VMEM((HEADS_PER_STEP * V_ROWS, seq), BF16),
        ],
        compiler_params=pltpu.CompilerParams(
            dimension_semantics=("arbitrary", "arbitrary", "arbitrary"),
            vmem_limit_bytes=VMEM_LIMIT_BYTES),
        name="moba",
    )(qa, ka, vat, sel, near, cfar)


def _mla_kernel(q_ref, k_new_ref, vt_new_ref, o_ref, m_s, acc_s, late_s, k_ref, vt_ref,
                *, tq, tk, unroll, trip):
    qi = pl.program_id(2)
    n_diag = tq // tk
    key = lax.broadcasted_iota(jnp.int32, (tk, tq), 0)
    qry = lax.broadcasted_iota(jnp.int32, (tk, tq), 1)
    q_heads = [q_ref[:, hh * LANES:(hh + 1) * LANES] for hh in range(HEADS_PER_STEP)]

    own = pl.ds(pl.multiple_of(qi * tq, tq), tq)
    k_ref[own, :] = k_new_ref[...]
    vt_ref[:, own] = vt_new_ref[...]

    _reset_softmax_state(m_s, acc_s)

    def k_tile(hh, j):
        return k_ref[pl.ds(pl.multiple_of(j * tk, tk), tk), hh * LANES:(hh + 1) * LANES]

    def vt_tiles(hh, j, n):
        return vt_ref[hh * V_ROWS:(hh + 1) * V_ROWS, pl.ds(pl.multiple_of(j * tk, tk), n * tk)]

    def group(first, n, masked):
        scores = [[_dot_nt(k_tile(hh, first + u), q_heads[hh]) for u in range(n)]
                  for hh in range(HEADS_PER_STEP)]
        for hh in range(HEADS_PER_STEP):
            blocks = scores[hh]
            if masked:
                blocks = [jnp.where(key + u * tk <= qry, st, NEG) for u, st in enumerate(blocks)]
            _online_softmax_group(m_s.at[hh], acc_s.at[hh], blocks, vt_tiles(hh, first, n))

    n_visible = qi * n_diag
    n_groups = (n_visible + unroll - 1) // unroll

    def visible_sweep(groups):
        def start_of(g):
            return jnp.maximum(jnp.minimum(g * unroll, n_visible - unroll), 0)

        def scores(stage):
            g, hh = stage
            start = pl.multiple_of(start_of(g) * tk, tk)
            st = _dot_nt(k_ref[pl.ds(start, unroll * tk), hh * LANES:(hh + 1) * LANES], q_heads[hh])
            return [st[u * tk:(u + 1) * tk, :] for u in range(unroll)]

        def softmax(stage, blocks):
            g, hh = stage
            start = start_of(g)
            chosen = None
            if unroll > 1:
                chosen = [jnp.where((start + u >= g * unroll) & (start + u < n_visible),
                                    jnp.ones((1, tq), F32), 0.0) > 0.5 for u in range(unroll)]
            _streamed_softmax_group(m_s.at[hh], acc_s.at[hh], late_s.at[hh], blocks,
                                    vt_tiles(hh, start, unroll), chosen)

        stages = [(g, hh) for g in groups for hh in range(HEADS_PER_STEP)]
        _pipelined(stages, scores, softmax, MATMUL_LOOKAHEAD)

    late_s[...] = jnp.zeros_like(late_s)
    group(qi * n_diag, n_diag, True)
    _in_trips(n_groups, visible_sweep, trip)

    @pl.when(jnp.max(late_s[...]) > 0.5)
    def _():
        _reset_softmax_state(m_s, acc_s)
        group(qi * n_diag, n_diag, True)

        def exact(j, carry):
            group(j, 1, False)
            return carry

        lax.fori_loop(0, n_visible, exact, 0)

    o_ref[...] = _normalised_heads(acc_s, MLA_V_DIM).T.astype(o_ref.dtype)


def _mla(qm, km, vmt, *, batch, seq, tq, tk, unroll, trip):
    n = qm.shape[0]
    nq = seq // tq
    assert tq % tk == 0 and seq // tk >= unroll
    pairs = MLA_HEADS // HEADS_PER_STEP
    return pl.pallas_call(
        functools.partial(_mla_kernel, tq=tq, tk=tk, unroll=unroll, trip=trip),
        grid=(batch, pairs, nq),
        in_specs=[
            pl.BlockSpec((tq, HEADS_PER_STEP * LANES), lambda b, p, i: (b * nq + i, p)),
            pl.BlockSpec((tq, HEADS_PER_STEP * LANES), lambda b, p, i: (b * nq + i, p)),
            pl.BlockSpec((None, HEADS_PER_STEP * V_ROWS, tq), lambda b, p, i: (b, p, i)),
        ],
        out_specs=pl.BlockSpec((tq, HEADS_PER_STEP * MLA_V_DIM), lambda b, p, i: (b * nq + i, p)),
        out_shape=jax.ShapeDtypeStruct((n, MLA_WIDTH), BF16),
        scratch_shapes=[
            pltpu.VMEM((HEADS_PER_STEP, 1, tq), F32),
            pltpu.VMEM((HEADS_PER_STEP, V_ROWS, tq), F32),
            pltpu.VMEM((HEADS_PER_STEP, 1, tq), F32),
            pltpu.VMEM((seq, HEADS_PER_STEP * LANES), BF16),
            pltpu.VMEM((HEADS_PER_STEP * V_ROWS, seq), BF16),
        ],
        compiler_params=pltpu.CompilerParams(
            dimension_semantics=("arbitrary", "arbitrary", "arbitrary"),
            vmem_limit_bytes=VMEM_LIMIT_BYTES),
        name="mla",
    )(qm, km, vmt)


def _back_kernel(x_ref, ya_ref, yb_ref, g_ref, wa_ref, wb_ref, wo_ref, gffn_ref, wup_ref,
                 cw_ref, cb_ref, wdn_ref, gfin_ref, o_ref,
                 carry_ref, act_ref, hn_ref, *, steps_per_seq, d_ff, fc, final):
    tm, d = x_ref.shape
    halo = SUBLANES

    @pl.when(pl.program_id(0) % steps_per_seq == 0)
    def _():
        carry_ref[...] = jnp.zeros_like(carry_ref)

    g = g_ref[...].astype(F32)
    mixed = g[:, :d] * _dot(ya_ref[...], wa_ref[...]) + g[:, d:] * _dot(yb_ref[...], wb_ref[...])
    h1 = x_ref[...] + _dot(mixed.astype(BF16), wo_ref[...])
    hn_ref[...] = _rms(h1, gffn_ref[...]).astype(BF16)

    def up_conv(col0):
        cols = slice(col0, col0 + fc)
        u = _dot(hn_ref[...], wup_ref[:, cols])
        rows = jnp.concatenate([carry_ref[:, cols], u], axis=0)
        carry_ref[:, cols] = u[tm - halo:tm, :]
        w = cw_ref[:, cols]
        y = cb_ref[:, cols]
        for t in range(CONV_WIDTH):
            back = CONV_WIDTH - 1 - t
            shifted = pltpu.roll(rows, back, axis=0) if back else rows
            y = y + w[t:t + 1, :] * shifted[halo:, :]
        return y

    for c in range(d_ff // fc):
        yg = up_conv(c * fc)
        yv = up_conv(d_ff + c * fc)
        act_ref[:, c * fc:(c + 1) * fc] = (yg * _sigmoid(yg) * yv).astype(BF16)

    h2 = h1 + _dot(act_ref[...], wdn_ref[...])
    o_ref[...] = _rms(h2, gfin_ref[...]) if final else h2


def _back(x2, ya, yb, g, wa, wb, wo, gffn, wup, cw, cb, wdn, gfin, *, seq, tm, fc, final):
    n, d = x2.shape
    d_ff = wdn.shape[0]
    row = lambda i: (i, 0)
    return pl.pallas_call(
        functools.partial(_back_kernel, steps_per_seq=seq // tm, d_ff=d_ff, fc=fc, final=final),
        grid=(n // tm,),
        in_specs=[
            pl.BlockSpec((tm, d), row),
            pl.BlockSpec((tm, ya.shape[1]), row),
            pl.BlockSpec((tm, yb.shape[1]), row),
            pl.BlockSpec((tm, g.shape[1]), row),
            _resident(wa.shape), _resident(wb.shape), _resident(wo.shape), _resident(gffn.shape),
            _resident(wup.shape), _resident(cw.shape), _resident(cb.shape), _resident(wdn.shape),
            _resident(gfin.shape),
        ],
        out_specs=pl.BlockSpec((tm, d), row),
        out_shape=jax.ShapeDtypeStruct((n, d), F32),
        scratch_shapes=[
            pltpu.VMEM((SUBLANES, 2 * d_ff), F32),
            pltpu.VMEM((tm, d_ff), BF16),
            pltpu.VMEM((tm, d), BF16),
        ],
        compiler_params=pltpu.CompilerParams(
            dimension_semantics=("arbitrary",), vmem_limit_bytes=VMEM_LIMIT_BYTES),
        name="back",
    )(x2, ya, yb, g, wa, wb, wo, gffn, wup, cw, cb, wdn, gfin)


def _t5_bucket_np(rel):
    n = np.maximum(rel, 0)
    max_exact = REL_BUCKETS // 2
    nf = np.maximum(n, 1).astype(np.float32)
    large = max_exact + (np.log(nf / np.float32(max_exact)) / np.float32(math.log(REL_MAX_DIST / max_exact))
                         * np.float32(REL_BUCKETS - max_exact)).astype(np.int32)
    large = np.minimum(large, REL_BUCKETS - 1)
    return np.where(n < max_exact, n, large)


def _moba_bias_tables(rel_bias):
    blk = MOBA_BLOCK
    period = 2 * blk
    assert REL_MAX_DIST <= blk + 1
    bias_h = rel_bias.T.astype(F32)
    slot = np.arange(period, dtype=np.int32)
    dist = np.where(slot < blk, slot, slot - period)

    def lookup(bucket):
        onehot = jnp.asarray(bucket)[None, :] == jnp.arange(REL_BUCKETS)[:, None]
        return jnp.sum(jnp.where(onehot[None], bias_h[:, :, None], 0.0), axis=1)

    def toeplitz(v):
        tiled = jnp.tile(v, (1, blk))[:, :blk * (period - 1)]
        return tiled.reshape(v.shape[0], blk, period - 1)[:, :, :blk]

    town = toeplitz(jnp.where(jnp.asarray(dist >= 0), lookup(_t5_bucket_np(dist)) * LOG2E, NEG))
    tprev = toeplitz(lookup(_t5_bucket_np(dist + blk)) * LOG2E)
    near = jnp.stack([tprev, town, jnp.zeros_like(town)], axis=1)
    cfar = bias_h[:, int(_t5_bucket_np(np.int32(blk + 1)))] * LOG2E
    cfar = jnp.broadcast_to(cfar[:, None, None], (MOBA_HEADS, 1, LANES))
    return near, cfar


def _pad_heads(w, n_heads, width, padded=LANES):
    r = w.shape[0]
    w = w.reshape(r, n_heads, width)
    out = jnp.zeros((r, n_heads, padded), w.dtype).at[:, :, :width].set(w)
    return out.reshape(r, n_heads * padded)


def _rope_slot(w_rope):
    r = w_rope.shape[0]
    return jnp.zeros((r, LANES), w_rope.dtype).at[:, MLA_NOPE_DIM:MLA_NOPE_DIM + MLA_ROPE_DIM].set(w_rope)


def _front_weights(w_in, w_uq, w_ukv):
    d = w_in.shape[0]
    s0 = MOBA_WIDTH
    s1 = 2 * MOBA_WIDTH
    s2 = 3 * MOBA_WIDTH
    s3 = s2 + MLA_Q_RANK
    s4 = s3 + MLA_KV_RANK
    s5 = s4 + MLA_ROPE_DIM
    w1 = jnp.concatenate([
        w_in[:, :s0], w_in[:, s0:s1],
        w_in[:, s2:s3], w_in[:, s3:s4],
        _rope_slot(w_in[:, s4:s5]),
        w_in[:, s5:],
    ], axis=1).astype(BF16)
    assert w1.shape == (d, _C_G + N_BRANCH * d)
    wvt = _pad_heads(w_in[:, s1:s2], MOBA_HEADS, MOBA_HEAD_DIM, V_ROWS).T.astype(BF16)

    wuq = _pad_heads(w_uq, MLA_HEADS, MLA_QK_DIM).astype(BF16)

    r = w_ukv.shape[0]
    ukv = w_ukv.reshape(r, MLA_HEADS, MLA_NOPE_DIM + MLA_V_DIM)
    wk = _pad_heads(ukv[:, :, :MLA_NOPE_DIM].reshape(r, -1), MLA_HEADS, MLA_NOPE_DIM).astype(BF16)
    wkvt = _pad_heads(ukv[:, :, MLA_NOPE_DIM:].reshape(r, -1), MLA_HEADS, MLA_V_DIM, V_ROWS).T.astype(BF16)
    return w1, wvt, wuq, wk, wkvt


def _rope_lane_tables(seq):
    dim = MLA_ROPE_DIM
    inv_freq = ROPE_THETA ** (-jnp.arange(0, dim, 2, dtype=F32) / dim)
    ang = jnp.arange(seq, dtype=F32)[:, None] * inv_freq[None, :]
    cos, sin = jnp.cos(ang), jnp.sin(ang)
    tail = jnp.zeros((seq, LANES - MLA_QK_DIM), F32)
    cosm = jnp.concatenate([jnp.ones((seq, MLA_NOPE_DIM), F32), cos, cos, tail], axis=1)
    sinm = jnp.concatenate([jnp.zeros((seq, MLA_NOPE_DIM), F32), -sin, sin, tail], axis=1)
    return cosm, sinm


class _Tiles:
    def __init__(self, seq):
        self.tm = 512 if seq % 512 == 0 else MOBA_BLOCK
        self.fc = 256
        self.mla_tq = 256
        self.mla_tk = 256
        self.mla_unroll = 2
        self.mla_trip = 8
        self.moba_unroll = 2
        self.moba_trip = 8


def kernel(x, norm_attn_g, w_in, b_gate, q_norm_g, w_uq, kv_norm_g, w_ukv, rel_bias,
           w_branch_moba, w_branch_mla, w_out, norm_ffn_g, w_up, conv_w, conv_b, w_down,
           norm_final_g):
    batch, seq, d = x.shape
    depth = w_in.shape[0]
    assert seq % MOBA_BLOCK == 0
    t = _Tiles(seq)
    d_ff = w_down.shape[1]
    assert d_ff % t.fc == 0
    n = batch * seq

    cosm, sinm = _rope_lane_tables(seq)
    near, cfar = _moba_bias_tables(rel_bias)
    row = lambda v: v.reshape(1, -1).astype(F32)

    h = x.reshape(n, d)
    for l in range(depth):
        w1, wvt, wuq, wk, wkvt = _front_weights(w_in[l], w_uq[l], w_ukv[l])
        qa, ka, sel, vat, qm, km, vmt, g = _front(
            h, row(norm_attn_g[l]), w1, wvt, row(b_gate[l]), row(q_norm_g[l]), wuq,
            row(kv_norm_g[l]), wk, wkvt, cosm, sinm, seq=seq, tm=t.tm)
        ya = _moba(qa, ka, vat, sel, near, cfar, batch=batch, seq=seq,
                   unroll=t.moba_unroll, trip=t.moba_trip)
        yb = _mla(qm, km, vmt, batch=batch, seq=seq, tq=t.mla_tq, tk=t.mla_tk,
                  unroll=t.mla_unroll, trip=t.mla_trip)
        h = _back(h, ya, yb, g, w_branch_moba[l].astype(BF16), w_branch_mla[l].astype(BF16),
                  w_out[l].astype(BF16), row(norm_ffn_g[l]), w_up[l].astype(BF16),
                  conv_w[l].astype(F32), row(conv_b[l]), w_down[l].astype(BF16),
                  row(norm_final_g), seq=seq, tm=t.tm, fc=t.fc, final=(l == depth - 1))
    return h.reshape(batch, seq, d)
```

```python
import functools
import math

import numpy as np
import jax
import jax.numpy as jnp
from jax import lax
from jax.experimental import pallas as pl
from jax.experimental.pallas import tpu as pltpu

MOBA_HEADS = 8
MOBA_HEAD_DIM = 64
MOBA_BLOCK = 256
MOBA_TOPK = 3
MLA_HEADS = 8
MLA_Q_RANK = 256
MLA_KV_RANK = 128
MLA_NOPE_DIM = 64
MLA_ROPE_DIM = 32
MLA_V_DIM = 64
ROPE_THETA = 10000.0
REL_BUCKETS = 32
REL_MAX_DIST = 128
CONV_WIDTH = 3
N_BRANCH = 2
EPS = 1e-6

MOBA_WIDTH = MOBA_HEADS * MOBA_HEAD_DIM
MLA_QK_DIM = MLA_NOPE_DIM + MLA_ROPE_DIM
MLA_WIDTH = MLA_HEADS * MLA_V_DIM

LANES = 128
SUBLANES = 8
VMEM_LIMIT_BYTES = 56 * 1024 * 1024

NEG = -1e30
LOG2E = math.log2(math.e)

F32 = jnp.float32
BF16 = jnp.bfloat16


def _dot(a, b):
    return jnp.dot(a, b, preferred_element_type=F32)


def _dot_nt(a, b):
    return lax.dot_general(a, b, (((1,), (1,)), ((), ())), preferred_element_type=F32)


def _rms(x, g):
    return x * lax.rsqrt(jnp.mean(x * x, axis=-1, keepdims=True) + EPS) * g


def _sigmoid(z):
    return 1.0 / (1.0 + jnp.exp(-z))


def _resident(shape):
    nd = len(shape)
    return pl.BlockSpec(shape, lambda *_: (0,) * nd, pipeline_mode=pl.Buffered(1))


def _per_head_group(shape, index_map):
    return pl.BlockSpec(shape, index_map, pipeline_mode=pl.Buffered(1))


_C_QA = 0
_C_KA = _C_QA + MOBA_WIDTH
_C_CQ = _C_KA + MOBA_WIDTH
_C_CKV = _C_CQ + MLA_Q_RANK
_C_KRA = _C_CKV + MLA_KV_RANK
_C_G = _C_KRA + LANES

BF16_SUBLANES = 16
V_ROWS = MLA_V_DIM + BF16_SUBLANES
assert MOBA_HEAD_DIM == MLA_V_DIM


def _ones_rows(n_rows):
    r = lax.broadcasted_iota(jnp.int32, (n_rows, 1), 0).astype(F32)
    within = r - jnp.floor((r + 0.5) * (1.0 / V_ROWS)) * V_ROWS
    return jnp.where(within == MLA_V_DIM, 1.0, 0.0).astype(F32)


def _front_kernel(x_ref, gattn_ref, w1_ref, wvt_ref, bg_ref, qng_ref, wuq_ref, kvng_ref, wkv_ref,
                  wkvt_ref, cos_ref, sin_ref,
                  qa_ref, ka_ref, kmean_ref, vat_ref, qm_ref, km_ref, vmt_ref, g_ref,
                  *, mla_scale):
    tm = x_ref.shape[0]
    xn = _rms(x_ref[...], gattn_ref[...]).astype(BF16)

    def proj(a, b):
        return _dot(xn, w1_ref[:, a:b])

    qa_ref[...] = (proj(_C_QA, _C_KA) * (MOBA_HEAD_DIM ** -0.5 * LOG2E)).astype(BF16)
    ka = proj(_C_KA, _C_CQ)
    ka_ref[...] = ka.astype(BF16)
    for i in range(tm // MOBA_BLOCK):
        kmean_ref[i] = jnp.mean(ka[i * MOBA_BLOCK:(i + 1) * MOBA_BLOCK], axis=0, keepdims=True)
    ones = _ones_rows(vat_ref.shape[0])
    vat_ref[...] = (_dot_nt(wvt_ref[...], xn) + ones).astype(BF16)

    cosm = cos_ref[...]
    sinm = sin_ref[...]
    lane = lax.broadcasted_iota(jnp.int32, (tm, LANES), 1)
    half = MLA_ROPE_DIM // 2

    def rope(a):
        x2_on_x1 = pltpu.roll(a, LANES - half, axis=1)
        x1_on_x2 = pltpu.roll(a, half, axis=1)
        swapped = jnp.where(lane < MLA_NOPE_DIM + half, x2_on_x1, x1_on_x2)
        return a * cosm + swapped * sinm

    cqn = _rms(proj(_C_CQ, _C_CKV), qng_ref[...]).astype(BF16)
    qq = _dot(cqn, wuq_ref[...])
    for h in range(MLA_HEADS):
        qm_ref[:, h * LANES:(h + 1) * LANES] = (
            rope(qq[:, h * LANES:(h + 1) * LANES]) * mla_scale).astype(BF16)

    ckv_kra = proj(_C_CKV, _C_G)
    ckvn = _rms(ckv_kra[:, :MLA_KV_RANK], kvng_ref[...]).astype(BF16)
    kv = _dot(ckvn, wkv_ref[...])
    kr = rope(ckv_kra[:, MLA_KV_RANK:])
    for h in range(MLA_HEADS):
        km_ref[:, h * LANES:(h + 1) * LANES] = (kv[:, h * LANES:(h + 1) * LANES] + kr).astype(BF16)
    vmt_ref[...] = (_dot_nt(wkvt_ref[...], ckvn) + ones).astype(BF16)

    g_ref[...] = _sigmoid(proj(_C_G, w1_ref.shape[1]) + bg_ref[...]).astype(BF16)


def _front(x2, gattn, w1, wvt, bg, qng, wuq, kvng, wkv, wkvt, cosm, sinm, *, seq, tm):
    n, d = x2.shape
    batch = n // seq
    hw = MLA_HEADS * LANES
    nblk = n // MOBA_BLOCK
    steps_per_seq = seq // tm
    row = lambda i: (i, 0)
    tcol = lambda i: (i // steps_per_seq, 0, i % steps_per_seq)
    out_shape = (
        jax.ShapeDtypeStruct((n, MOBA_WIDTH), BF16),
        jax.ShapeDtypeStruct((n, MOBA_WIDTH), BF16),
        jax.ShapeDtypeStruct((nblk, 1, MOBA_WIDTH), F32),
        jax.ShapeDtypeStruct((batch, MOBA_HEADS * V_ROWS, seq), BF16),
        jax.ShapeDtypeStruct((n, hw), BF16),
        jax.ShapeDtypeStruct((n, hw), BF16),
        jax.ShapeDtypeStruct((batch, MLA_HEADS * V_ROWS, seq), BF16),
        jax.ShapeDtypeStruct((n, N_BRANCH * d), BF16),
    )
    in_specs = [
        pl.BlockSpec((tm, d), row),
        _resident(gattn.shape), _resident(w1.shape), _resident(wvt.shape), _resident(bg.shape),
        _resident(qng.shape), _resident(wuq.shape), _resident(kvng.shape), _resident(wkv.shape),
        _resident(wkvt.shape),
        pl.BlockSpec((tm, LANES), lambda i: (i % steps_per_seq, 0)),
        pl.BlockSpec((tm, LANES), lambda i: (i % steps_per_seq, 0)),
    ]
    out_specs = (
        pl.BlockSpec((tm, MOBA_WIDTH), row),
        pl.BlockSpec((tm, MOBA_WIDTH), row),
        pl.BlockSpec((tm // MOBA_BLOCK, 1, MOBA_WIDTH), lambda i: (i, 0, 0)),
        pl.BlockSpec((None, MOBA_HEADS * V_ROWS, tm), tcol),
        pl.BlockSpec((tm, hw), row),
        pl.BlockSpec((tm, hw), row),
        pl.BlockSpec((None, MLA_HEADS * V_ROWS, tm), tcol),
        pl.BlockSpec((tm, N_BRANCH * d), row),
    )
    return pl.pallas_call(
        functools.partial(_front_kernel, mla_scale=MLA_QK_DIM ** -0.5 * LOG2E),
        grid=(n // tm,),
        in_specs=in_specs,
        out_specs=out_specs,
        out_shape=out_shape,
        compiler_params=pltpu.CompilerParams(
            dimension_semantics=("arbitrary",), vmem_limit_bytes=VMEM_LIMIT_BYTES),
        name="front",
    )(x2, gattn, w1, wvt, bg, qng, wuq, kvng, wkv, wkvt, cosm, sinm)


HEADS_PER_STEP = 8
BIG = 1e30
RESCALE_SLACK = 64.0
MATMUL_LOOKAHEAD = 2


def _online_softmax_group(m_ref, acc_ref, score_blocks, vt_grp, chosen=None, shifts=None):
    n = len(score_blocks)
    chosen = chosen or [None] * n
    shifts = shifts or [None] * n
    m_old = m_ref[...]
    m_new = m_old
    for st, ch, sh in zip(score_blocks, chosen, shifts):
        rm = jnp.max(st, axis=0, keepdims=True)
        if sh is not None:
            rm = rm + sh
        m_new = jnp.maximum(m_new, rm if ch is None else jnp.where(ch, rm, NEG))
    probs = []
    for st, ch, sh in zip(score_blocks, chosen, shifts):
        off = m_new if sh is None else m_new - sh
        if ch is not None:
            off = jnp.where(ch, off, BIG)
        probs.append(jnp.exp2(st - off).astype(BF16))
    ot = _dot(vt_grp, jnp.concatenate(probs, axis=0))
    acc_ref[...] = jnp.exp2(m_old - m_new) * acc_ref[...] + ot
    m_ref[...] = m_new


def _streamed_softmax_group(m_ref, acc_ref, late_ref, score_blocks, vt_grp, chosen=None, shifts=None):
    n = len(score_blocks)
    chosen = chosen or [None] * n
    shifts = shifts or [None] * n
    m_old = m_ref[...]
    seen = jnp.full(m_old.shape, NEG, F32)
    probs = []
    for st, ch, sh in zip(score_blocks, chosen, shifts):
        off = m_old if sh is None else m_old - sh
        if ch is not None:
            off = jnp.where(ch, off, BIG)
        probs.append(jnp.exp2(st - off).astype(BF16))
        rm = jnp.max(st, axis=0, keepdims=True)
        if sh is not None:
            rm = rm + sh
        seen = jnp.maximum(seen, rm if ch is None else jnp.where(ch, rm, NEG))
    acc_ref[...] = acc_ref[...] + _dot(vt_grp, jnp.concatenate(probs, axis=0))
    late_ref[...] = jnp.where(seen > m_old + RESCALE_SLACK, 1.0, late_ref[...])


def _pipelined(stages, issue, consume, lookahead):
    queue = []
    for n in range(len(stages) + lookahead):
        if n < len(stages):
            queue.append((stages[n], issue(stages[n])))
        if n >= lookahead:
            consume(*queue.pop(0))


def _in_trips(n, sweep, trip_groups):
    def trip(t, carry):
        sweep([trip_groups * t + i for i in range(trip_groups)])
        return carry

    lax.fori_loop(0, n // trip_groups, trip, 0)
    done = (n // trip_groups) * trip_groups
    size = trip_groups // 2
    while size >= 1:
        has = ((n - done) // size) % 2 == 1
        first = done + ((n - done) // (2 * size)) * (2 * size)
        pl.when(has)(functools.partial(sweep, [first + i for i in range(size)]))
        size //= 2


def _reset_softmax_state(m_s, acc_s):
    m_s[...] = jnp.full(m_s.shape, NEG, F32)
    acc_s[...] = jnp.zeros(acc_s.shape, F32)


def _normalised_heads(acc_s, v_dim):
    rows = []
    for hh in range(HEADS_PER_STEP):
        acc = acc_s[hh]
        rows.append(acc[0:v_dim, :] * (1.0 / acc[v_dim:v_dim + 1, :]))
    return jnp.concatenate(rows, axis=0)


def _moba_kernel(q_ref, k_new_ref, vt_new_ref, kmean_ref, near_ref, cfar_ref, o_ref,
                 m_s, acc_s, late_s, sel_s, k_ref, vt_ref, *, unroll, trip):
    blk = MOBA_BLOCK
    qb = pl.program_id(2)
    nb = kmean_ref.shape[0]
    lane = lax.broadcasted_iota(jnp.int32, (blk, LANES), 1)
    blk_id = lax.broadcasted_iota(jnp.int32, (nb, blk), 0).astype(F32)
    qbf = qb.astype(F32)
    has_prev = qb >= 1
    _reset_softmax_state(m_s, acc_s)

    @pl.when(qb == 0)
    def _():
        k_ref[...] = jnp.zeros_like(k_ref)
        vt_ref[...] = jnp.zeros_like(vt_ref)

    own = pl.ds(pl.multiple_of(qb * blk, blk), blk)
    k_ref[own, :] = k_new_ref[...]
    vt_ref[:, own] = vt_new_ref[...]

    def lane_group(hh):
        return slice((hh // 2) * LANES, (hh // 2 + 1) * LANES)

    def k_block(hh, j):
        return k_ref[pl.ds(pl.multiple_of(j * blk, blk), blk), lane_group(hh)]

    def vt_blocks(hh, j, n):
        return vt_ref[hh * V_ROWS:(hh + 1) * V_ROWS, pl.ds(pl.multiple_of(j * blk, blk), n * blk)]

    q_heads, prev_chosen = [], []
    for hh in range(HEADS_PER_STEP):
        in_head = (lane >= (hh % 2) * MOBA_HEAD_DIM) & (lane < (hh % 2 + 1) * MOBA_HEAD_DIM)
        qh = jnp.where(in_head, q_ref[:, lane_group(hh)].astype(F32), 0.0).astype(BF16)
        q_heads.append(qh)

        kmean = kmean_ref[:, lane_group(hh)].astype(BF16)
        gate = jnp.where(blk_id < qbf, _dot_nt(kmean, qh), -jnp.inf)
        sel = jnp.zeros((nb, blk), F32)
        for _ in range(min(MOBA_TOPK, nb)):
            best = jnp.max(gate, axis=0, keepdims=True)
            first = jnp.min(jnp.where(gate == best, blk_id, float(nb)), axis=0, keepdims=True)
            pick = blk_id == first
            sel = jnp.where(pick, 1.0, sel)
            gate = jnp.where(pick, -jnp.inf, gate)
        prev_chosen.append(jnp.max(jnp.where(blk_id == qbf - 1.0, sel, 0.0), axis=0, keepdims=True))
        sel_s[hh] = jnp.where(blk_id < qbf - 1.0, sel, 0.0)

    lo = jnp.maximum(qb - 1, 0)
    tab = jnp.where(has_prev, 0, 1)

    def nearest_exact():
        for first in range(0, HEADS_PER_STEP, HEADS_PER_STEP // 2):
            heads = range(first, first + HEADS_PER_STEP // 2)
            scores = {hh: [_dot_nt(k_block(hh, lo + u), q_heads[hh]) + near_ref[hh, tab + u]
                           for u in range(2)] for hh in heads}
            for hh in heads:
                chosen = [jnp.where(has_prev, prev_chosen[hh], 1.0) > 0.5,
                          jnp.where(has_prev, jnp.ones((1, blk), F32), 0.0) > 0.5]
                _online_softmax_group(m_s.at[hh], acc_s.at[hh], scores[hh], vt_blocks(hh, lo, 2), chosen)

    n_groups = (qb - 1 + unroll - 1) // unroll

    def far_sweep(groups, streamed=True):
        def start_of(g):
            return jnp.minimum(g * unroll, nb - unroll)

        def scores(stage):
            g, hh = stage
            keys = k_ref[pl.ds(pl.multiple_of(start_of(g) * blk, blk), unroll * blk), lane_group(hh)]
            st = _dot_nt(keys, q_heads[hh])
            return [st[u * blk:(u + 1) * blk, :] for u in range(unroll)]

        def softmax(stage, blocks):
            g, hh = stage
            start = start_of(g)
            chosen = [jnp.where(start + u >= g * unroll, sel_s[hh, pl.ds(start + u, 1), :], 0.0) > 0.5
                      for u in range(unroll)]
            operands = (blocks, vt_blocks(hh, start, unroll), chosen, [cfar_ref[hh][:, 0:1]] * unroll)
            if streamed:
                _streamed_softmax_group(m_s.at[hh], acc_s.at[hh], late_s.at[hh], *operands)
            else:
                _online_softmax_group(m_s.at[hh], acc_s.at[hh], *operands)

        stages = [(g, hh) for hh in range(HEADS_PER_STEP) for g in groups]
        _pipelined(stages, scores, softmax, MATMUL_LOOKAHEAD if streamed else len(stages))

    late_s[...] = jnp.zeros_like(late_s)
    nearest_exact()
    _in_trips(n_groups, far_sweep, trip)

    @pl.when(jnp.max(late_s[...]) > 0.5)
    def _():
        _reset_softmax_state(m_s, acc_s)
        nearest_exact()

        def exact(g, carry):
            far_sweep([g], streamed=False)
            return carry

        lax.fori_loop(0, n_groups, exact, 0)

    o_ref[...] = _normalised_heads(acc_s, MOBA_HEAD_DIM).T.astype(o_ref.dtype)


def _moba(qa, ka, vat, kmean, near, cfar, *, batch, seq, unroll, trip):
    n = qa.shape[0]
    blk = MOBA_BLOCK
    nb = seq // blk
    assert nb >= unroll and nb >= 2
    steps = MOBA_HEADS // HEADS_PER_STEP
    width = HEADS_PER_STEP * MOBA_HEAD_DIM
    return pl.pallas_call(
        functools.partial(_moba_kernel, unroll=unroll, trip=trip),
        grid=(batch, steps, nb),
        in_specs=[
            pl.BlockSpec((blk, width), lambda b, p, i: (b * nb + i, p)),
            pl.BlockSpec((blk, width), lambda b, p, i: (b * nb + i, p)),
            pl.BlockSpec((None, HEADS_PER_STEP * V_ROWS, blk), lambda b, p, i: (b, p, i)),
            pl.BlockSpec((None, nb, width), lambda b, p, i: (b, 0, p)),
            _per_head_group((HEADS_PER_STEP, 3, blk, blk), lambda b, p, i: (p, 0, 0, 0)),
            pl.BlockSpec((HEADS_PER_STEP, 1, LANES), lambda b, p, i: (p, 0, 0)),
        ],
        out_specs=pl.BlockSpec((blk, width), lambda b, p, i: (b * nb + i, p)),
        out_shape=jax.ShapeDtypeStruct((n, MOBA_WIDTH), BF16),
        scratch_shapes=[
            pltpu.VMEM((HEADS_PER_STEP, 1, blk), F32),
            pltpu.VMEM((HEADS_PER_STEP, V_ROWS, blk), F32),
            pltpu.VMEM((HEADS_PER_STEP, 1, blk), F32),
            pltpu.VMEM((HEADS_PER_STEP, nb, blk), F32),
            pltpu.VMEM((seq, width), BF16),
            pltpu.VMEM((HEADS_PER_STEP * V_ROWS, seq), BF16),
        ],
        compiler_params=pltpu.CompilerParams(
            dimension_semantics=("arbitrary", "arbitrary", "arbitrary"),
            vmem_limit_bytes=VMEM_LIMIT_BYTES),
        name="moba",
    )(qa, ka, vat, kmean, near, cfar)


def _mla_kernel(q_ref, k_new_ref, vt_new_ref, o_ref, m_s, acc_s, late_s, k_ref, vt_ref,
                *, tq, tk, unroll, trip):
    qi = pl.program_id(2)
    n_diag = tq // tk
    key = lax.broadcasted_iota(jnp.int32, (tk, tq), 0)
    qry = lax.broadcasted_iota(jnp.int32, (tk, tq), 1)
    q_heads = [q_ref[:, hh * LANES:(hh + 1) * LANES] for hh in range(HEADS_PER_STEP)]

    own = pl.ds(pl.multiple_of(qi * tq, tq), tq)
    k_ref[own, :] = k_new_ref[...]
    vt_ref[:, own] = vt_new_ref[...]

    _reset_softmax_state(m_s, acc_s)

    def k_tile(hh, j):
        return k_ref[pl.ds(pl.multiple_of(j * tk, tk), tk), hh * LANES:(hh + 1) * LANES]

    def vt_tiles(hh, j, n):
        return vt_ref[hh * V_ROWS:(hh + 1) * V_ROWS, pl.ds(pl.multiple_of(j * tk, tk), n * tk)]

    def group(first, n, masked):
        scores = [[_dot_nt(k_tile(hh, first + u), q_heads[hh]) for u in range(n)]
                  for hh in range(HEADS_PER_STEP)]
        for hh in range(HEADS_PER_STEP):
            blocks = scores[hh]
            if masked:
                blocks = [jnp.where(key + u * tk <= qry, st, NEG) for u, st in enumerate(blocks)]
            _online_softmax_group(m_s.at[hh], acc_s.at[hh], blocks, vt_tiles(hh, first, n))

    n_visible = qi * n_diag
    n_groups = (n_visible + unroll - 1) // unroll

    def visible_sweep(groups):
        def start_of(g):
            return jnp.maximum(jnp.minimum(g * unroll, n_visible - unroll), 0)

        def scores(stage):
            g, hh = stage
            start = pl.multiple_of(start_of(g) * tk, tk)
            st = _dot_nt(k_ref[pl.ds(start, unroll * tk), hh * LANES:(hh + 1) * LANES], q_heads[hh])
            return [st[u * tk:(u + 1) * tk, :] for u in range(unroll)]

        def softmax(stage, blocks):
            g, hh = stage
            start = start_of(g)
            chosen = None
            if unroll > 1:
                chosen = [jnp.where((start + u >= g * unroll) & (start + u < n_visible),
                                    jnp.ones((1, tq), F32), 0.0) > 0.5 for u in range(unroll)]
            _streamed_softmax_group(m_s.at[hh], acc_s.at[hh], late_s.at[hh], blocks,
                                    vt_tiles(hh, start, unroll), chosen)

        stages = [(g, hh) for hh in range(HEADS_PER_STEP) for g in groups]
        _pipelined(stages, scores, softmax, MATMUL_LOOKAHEAD)

    late_s[...] = jnp.zeros_like(late_s)
    group(qi * n_diag, n_diag, True)
    _in_trips(n_groups, visible_sweep, trip)

    @pl.when(jnp.max(late_s[...]) > 0.5)
    def _():
        _reset_softmax_state(m_s, acc_s)
        group(qi * n_diag, n_diag, True)

        def exact(j, carry):
            group(j, 1, False)
            return carry

        lax.fori_loop(0, n_visible, exact, 0)

    o_ref[...] = _normalised_heads(acc_s, MLA_V_DIM).T.astype(o_ref.dtype)


def _mla(qm, km, vmt, *, batch, seq, tq, tk, unroll, trip):
    n = qm.shape[0]
    nq = seq // tq
    assert tq % tk == 0 and seq // tk >= unroll
    pairs = MLA_HEADS // HEADS_PER_STEP
    return pl.pallas_call(
        functools.partial(_mla_kernel, tq=tq, tk=tk, unroll=unroll, trip=trip),
        grid=(batch, pairs, nq),
        in_specs=[
            pl.BlockSpec((tq, HEADS_PER_STEP * LANES), lambda b, p, i: (b * nq + i, p)),
            pl.BlockSpec((tq, HEADS_PER_STEP * LANES), lambda b, p, i: (b * nq + i, p)),
            pl.BlockSpec((None, HEADS_PER_STEP * V_ROWS, tq), lambda b, p, i: (b, p, i)),
        ],
        out_specs=pl.BlockSpec((tq, HEADS_PER_STEP * MLA_V_DIM), lambda b, p, i: (b * nq + i, p)),
        out_shape=jax.ShapeDtypeStruct((n, MLA_WIDTH), BF16),
        scratch_shapes=[
            pltpu.VMEM((HEADS_PER_STEP, 1, tq), F32),
            pltpu.VMEM((HEADS_PER_STEP, V_ROWS, tq), F32),
            pltpu.VMEM((HEADS_PER_STEP, 1, tq), F32),
            pltpu.VMEM((seq, HEADS_PER_STEP * LANES), BF16),
            pltpu.VMEM((HEADS_PER_STEP * V_ROWS, seq), BF16),
        ],
        compiler_params=pltpu.CompilerParams(
            dimension_semantics=("arbitrary", "arbitrary", "arbitrary"),
            vmem_limit_bytes=VMEM_LIMIT_BYTES),
        name="mla",
    )(qm, km, vmt)


def _back_kernel(x_ref, ya_ref, yb_ref, g_ref, wa_ref, wb_ref, wo_ref, gffn_ref, wup_ref,
                 cw_ref, cb_ref, wdn_ref, gfin_ref, o_ref,
                 carry_ref, act_ref, hn_ref, *, steps_per_seq, d_ff, fc, final):
    tm, d = x_ref.shape
    halo = SUBLANES

    @pl.when(pl.program_id(0) % steps_per_seq == 0)
    def _():
        carry_ref[...] = jnp.zeros_like(carry_ref)

    g = g_ref[...].astype(F32)
    mixed = g[:, :d] * _dot(ya_ref[...], wa_ref[...]) + g[:, d:] * _dot(yb_ref[...], wb_ref[...])
    h1 = x_ref[...] + _dot(mixed.astype(BF16), wo_ref[...])
    hn_ref[...] = _rms(h1, gffn_ref[...]).astype(BF16)

    def up_conv(col0):
        cols = slice(col0, col0 + fc)
        u = _dot(hn_ref[...], wup_ref[:, cols])
        rows = jnp.concatenate([carry_ref[:, cols], u], axis=0)
        carry_ref[:, cols] = u[tm - halo:tm, :]
        w = cw_ref[:, cols]
        y = cb_ref[:, cols]
        for t in range(CONV_WIDTH):
            back = CONV_WIDTH - 1 - t
            shifted = pltpu.roll(rows, back, axis=0) if back else rows
            y = y + w[t:t + 1, :] * shifted[halo:, :]
        return y

    for c in range(d_ff // fc):
        yg = up_conv(c * fc)
        yv = up_conv(d_ff + c * fc)
        act_ref[:, c * fc:(c + 1) * fc] = (yg * _sigmoid(yg) * yv).astype(BF16)

    h2 = h1 + _dot(act_ref[...], wdn_ref[...])
    o_ref[...] = _rms(h2, gfin_ref[...]) if final else h2


def _back(x2, ya, yb, g, wa, wb, wo, gffn, wup, cw, cb, wdn, gfin, *, seq, tm, fc, final):
    n, d = x2.shape
    d_ff = wdn.shape[0]
    row = lambda i: (i, 0)
    return pl.pallas_call(
        functools.partial(_back_kernel, steps_per_seq=seq // tm, d_ff=d_ff, fc=fc, final=final),
        grid=(n // tm,),
        in_specs=[
            pl.BlockSpec((tm, d), row),
            pl.BlockSpec((tm, ya.shape[1]), row),
            pl.BlockSpec((tm, yb.shape[1]), row),
            pl.BlockSpec((tm, g.shape[1]), row),
            _resident(wa.shape), _resident(wb.shape), _resident(wo.shape), _resident(gffn.shape),
            _resident(wup.shape), _resident(cw.shape), _resident(cb.shape), _resident(wdn.shape),
            _resident(gfin.shape),
        ],
        out_specs=pl.BlockSpec((tm, d), row),
        out_shape=jax.ShapeDtypeStruct((n, d), F32),
        scratch_shapes=[
            pltpu.VMEM((SUBLANES, 2 * d_ff), F32),
            pltpu.VMEM((tm, d_ff), BF16),
            pltpu.VMEM((tm, d), BF16),
        ],
        compiler_params=pltpu.CompilerParams(
            dimension_semantics=("arbitrary",), vmem_limit_bytes=VMEM_LIMIT_BYTES),
        name="back",
    )(x2, ya, yb, g, wa, wb, wo, gffn, wup, cw, cb, wdn, gfin)


def _t5_bucket_np(rel):
    n = np.maximum(rel, 0)
    max_exact = REL_BUCKETS // 2
    nf = np.maximum(n, 1).astype(np.float32)
    large = max_exact + (np.log(nf / np.float32(max_exact)) / np.float32(math.log(REL_MAX_DIST / max_exact))
                         * np.float32(REL_BUCKETS - max_exact)).astype(np.int32)
    large = np.minimum(large, REL_BUCKETS - 1)
    return np.where(n < max_exact, n, large)


def _moba_bias_tables(rel_bias):
    blk = MOBA_BLOCK
    period = 2 * blk
    assert REL_MAX_DIST <= blk + 1
    bias_h = rel_bias.T.astype(F32)
    slot = np.arange(period, dtype=np.int32)
    dist = np.where(slot < blk, slot, slot - period)

    def lookup(bucket):
        onehot = jnp.asarray(bucket)[None, :] == jnp.arange(REL_BUCKETS)[:, None]
        return jnp.sum(jnp.where(onehot[None], bias_h[:, :, None], 0.0), axis=1)

    def toeplitz(v):
        tiled = jnp.tile(v, (1, blk))[:, :blk * (period - 1)]
        return tiled.reshape(v.shape[0], blk, period - 1)[:, :, :blk]

    town = toeplitz(jnp.where(jnp.asarray(dist >= 0), lookup(_t5_bucket_np(dist)) * LOG2E, NEG))
    tprev = toeplitz(lookup(_t5_bucket_np(dist + blk)) * LOG2E)
    near = jnp.stack([tprev, town, jnp.zeros_like(town)], axis=1)
    cfar = bias_h[:, int(_t5_bucket_np(np.int32(blk + 1)))] * LOG2E
    cfar = jnp.broadcast_to(cfar[:, None, None], (MOBA_HEADS, 1, LANES))
    return near, cfar


def _pad_heads(w, n_heads, width, padded=LANES):
    r = w.shape[0]
    w = w.reshape(r, n_heads, width)
    out = jnp.zeros((r, n_heads, padded), w.dtype).at[:, :, :width].set(w)
    return out.reshape(r, n_heads * padded)


def _rope_slot(w_rope):
    r = w_rope.shape[0]
    return jnp.zeros((r, LANES), w_rope.dtype).at[:, MLA_NOPE_DIM:MLA_NOPE_DIM + MLA_ROPE_DIM].set(w_rope)


def _front_weights(w_in, w_uq, w_ukv):
    d = w_in.shape[0]
    s0 = MOBA_WIDTH
    s1 = 2 * MOBA_WIDTH
    s2 = 3 * MOBA_WIDTH
    s3 = s2 + MLA_Q_RANK
    s4 = s3 + MLA_KV_RANK
    s5 = s4 + MLA_ROPE_DIM
    w1 = jnp.concatenate([
        w_in[:, :s0], w_in[:, s0:s1],
        w_in[:, s2:s3], w_in[:, s3:s4],
        _rope_slot(w_in[:, s4:s5]),
        w_in[:, s5:],
    ], axis=1).astype(BF16)
    assert w1.shape == (d, _C_G + N_BRANCH * d)
    wvt = _pad_heads(w_in[:, s1:s2], MOBA_HEADS, MOBA_HEAD_DIM, V_ROWS).T.astype(BF16)

    wuq = _pad_heads(w_uq, MLA_HEADS, MLA_QK_DIM).astype(BF16)

    r = w_ukv.shape[0]
    ukv = w_ukv.reshape(r, MLA_HEADS, MLA_NOPE_DIM + MLA_V_DIM)
    wk = _pad_heads(ukv[:, :, :MLA_NOPE_DIM].reshape(r, -1), MLA_HEADS, MLA_NOPE_DIM).astype(BF16)
    wkvt = _pad_heads(ukv[:, :, MLA_NOPE_DIM:].reshape(r, -1), MLA_HEADS, MLA_V_DIM, V_ROWS).T.astype(BF16)
    return w1, wvt, wuq, wk, wkvt


def _rope_lane_tables(seq):
    dim = MLA_ROPE_DIM
    inv_freq = ROPE_THETA ** (-jnp.arange(0, dim, 2, dtype=F32) / dim)
    ang = jnp.arange(seq, dtype=F32)[:, None] * inv_freq[None, :]
    cos, sin = jnp.cos(ang), jnp.sin(ang)
    tail = jnp.zeros((seq, LANES - MLA_QK_DIM), F32)
    cosm = jnp.concatenate([jnp.ones((seq, MLA_NOPE_DIM), F32), cos, cos, tail], axis=1)
    sinm = jnp.concatenate([jnp.zeros((seq, MLA_NOPE_DIM), F32), -sin, sin, tail], axis=1)
    return cosm, sinm


class _Tiles:
    def __init__(self, seq):
        self.tm = 512 if seq % 512 == 0 else MOBA_BLOCK
        self.fc = 256
        self.mla_tq = 256
        self.mla_tk = 256
        self.mla_unroll = 2
        self.mla_trip = 8
        self.moba_unroll = 2
        self.moba_trip = 8


def kernel(x, norm_attn_g, w_in, b_gate, q_norm_g, w_uq, kv_norm_g, w_ukv, rel_bias,
           w_branch_moba, w_branch_mla, w_out, norm_ffn_g, w_up, conv_w, conv_b, w_down,
           norm_final_g):
    batch, seq, d = x.shape
    depth = w_in.shape[0]
    assert seq % MOBA_BLOCK == 0
    t = _Tiles(seq)
    d_ff = w_down.shape[1]
    assert d_ff % t.fc == 0
    n = batch * seq

    cosm, sinm = _rope_lane_tables(seq)
    near, cfar = _moba_bias_tables(rel_bias)
    row = lambda v: v.reshape(1, -1).astype(F32)

    h = x.reshape(n, d)
    for l in range(depth):
        w1, wvt, wuq, wk, wkvt = _front_weights(w_in[l], w_uq[l], w_ukv[l])
        qa, ka, kmean, vat, qm, km, vmt, g = _front(
            h, row(norm_attn_g[l]), w1, wvt, row(b_gate[l]), row(q_norm_g[l]), wuq,
            row(kv_norm_g[l]), wk, wkvt, cosm, sinm, seq=seq, tm=t.tm)
        kmean = kmean.reshape(batch, seq // MOBA_BLOCK, MOBA_WIDTH)
        ya = _moba(qa, ka, vat, kmean, near, cfar, batch=batch, seq=seq,
                   unroll=t.moba_unroll, trip=t.moba_trip)
        yb = _mla(qm, km, vmt, batch=batch, seq=seq, tq=t.mla_tq, tk=t.mla_tk,
                  unroll=t.mla_unroll, trip=t.mla_trip)
        h = _back(h, ya, yb, g, w_branch_moba[l].astype(BF16), w_branch_mla[l].astype(BF16),
                  w_out[l].astype(BF16), row(norm_ffn_g[l]), w_up[l].astype(BF16),
                  conv_w[l].astype(F32), row(conv_b[l]), w_down[l].astype(BF16),
                  row(norm_final_g), seq=seq, tm=t.tm, fc=t.fc, final=(l == depth - 1))
    return h.reshape(batch, seq, d)
```

```python
import functools
import math

import numpy as np
import jax
import jax.numpy as jnp
from jax import lax
from jax.experimental import pallas as pl
from jax.experimental.pallas import tpu as pltpu

MOBA_HEADS = 8
MOBA_HEAD_DIM = 64
MOBA_BLOCK = 256
MOBA_TOPK = 3
MLA_HEADS = 8
MLA_Q_RANK = 256
MLA_KV_RANK = 128
MLA_NOPE_DIM = 64
MLA_ROPE_DIM = 32
MLA_V_DIM = 64
ROPE_THETA = 10000.0
REL_BUCKETS = 32
REL_MAX_DIST = 128
CONV_WIDTH = 3
N_BRANCH = 2
EPS = 1e-6

MOBA_WIDTH = MOBA_HEADS * MOBA_HEAD_DIM
MLA_QK_DIM = MLA_NOPE_DIM + MLA_ROPE_DIM
MLA_WIDTH = MLA_HEADS * MLA_V_DIM

LANES = 128
SUBLANES = 8
VMEM_LIMIT_BYTES = 56 * 1024 * 1024

NEG = -1e30
LOG2E = math.log2(math.e)

F32 = jnp.float32
BF16 = jnp.bfloat16


def _dot(a, b):
    return jnp.dot(a, b, preferred_element_type=F32)


def _dot_nt(a, b):
    return lax.dot_general(a, b, (((1,), (1,)), ((), ())), preferred_element_type=F32)


def _rms(x, g):
    return x * lax.rsqrt(jnp.mean(x * x, axis=-1, keepdims=True) + EPS) * g


def _sigmoid(z):
    return 1.0 / (1.0 + jnp.exp(-z))


def _resident(shape):
    nd = len(shape)
    return pl.BlockSpec(shape, lambda *_: (0,) * nd, pipeline_mode=pl.Buffered(1))


def _per_head_group(shape, index_map):
    return pl.BlockSpec(shape, index_map, pipeline_mode=pl.Buffered(1))


_C_QA = 0
_C_KA = _C_QA + MOBA_WIDTH
_C_CQ = _C_KA + MOBA_WIDTH
_C_CKV = _C_CQ + MLA_Q_RANK
_C_KRA = _C_CKV + MLA_KV_RANK
_C_G = _C_KRA + LANES

BF16_SUBLANES = 16
V_ROWS = MLA_V_DIM + BF16_SUBLANES
assert MOBA_HEAD_DIM == MLA_V_DIM


def _ones_rows(n_rows):
    r = lax.broadcasted_iota(jnp.int32, (n_rows, 1), 0).astype(F32)
    within = r - jnp.floor((r + 0.5) * (1.0 / V_ROWS)) * V_ROWS
    return jnp.where(within == MLA_V_DIM, 1.0, 0.0).astype(F32)


def _front_kernel(x_ref, gattn_ref, w1_ref, wvt_ref, bg_ref, qng_ref, wuq_ref, kvng_ref, wkv_ref,
                  wkvt_ref, cos_ref, sin_ref,
                  qa_ref, ka_ref, kmean_ref, vat_ref, qm_ref, km_ref, vmt_ref, g_ref,
                  *, mla_scale):
    tm = x_ref.shape[0]
    xn = _rms(x_ref[...], gattn_ref[...]).astype(BF16)

    def proj(a, b):
        return _dot(xn, w1_ref[:, a:b])

    qa_ref[...] = (proj(_C_QA, _C_KA) * (MOBA_HEAD_DIM ** -0.5 * LOG2E)).astype(BF16)
    ka = proj(_C_KA, _C_CQ)
    ka_ref[...] = ka.astype(BF16)
    for i in range(tm // MOBA_BLOCK):
        kmean_ref[i] = jnp.mean(ka[i * MOBA_BLOCK:(i + 1) * MOBA_BLOCK], axis=0, keepdims=True)
    ones = _ones_rows(vat_ref.shape[0])
    vat_ref[...] = (_dot_nt(wvt_ref[...], xn) + ones).astype(BF16)

    cosm = cos_ref[...]
    sinm = sin_ref[...]
    lane = lax.broadcasted_iota(jnp.int32, (tm, LANES), 1)
    half = MLA_ROPE_DIM // 2

    def rope(a):
        x2_on_x1 = pltpu.roll(a, LANES - half, axis=1)
        x1_on_x2 = pltpu.roll(a, half, axis=1)
        swapped = jnp.where(lane < MLA_NOPE_DIM + half, x2_on_x1, x1_on_x2)
        return a * cosm + swapped * sinm

    cqn = _rms(proj(_C_CQ, _C_CKV), qng_ref[...]).astype(BF16)
    qq = _dot(cqn, wuq_ref[...])
    for h in range(MLA_HEADS):
        qm_ref[:, h * LANES:(h + 1) * LANES] = (
            rope(qq[:, h * LANES:(h + 1) * LANES]) * mla_scale).astype(BF16)

    ckv_kra = proj(_C_CKV, _C_G)
    ckvn = _rms(ckv_kra[:, :MLA_KV_RANK], kvng_ref[...]).astype(BF16)
    kv = _dot(ckvn, wkv_ref[...])
    kr = rope(ckv_kra[:, MLA_KV_RANK:])
    for h in range(MLA_HEADS):
        km_ref[:, h * LANES:(h + 1) * LANES] = (kv[:, h * LANES:(h + 1) * LANES] + kr).astype(BF16)
    vmt_ref[...] = (_dot_nt(wkvt_ref[...], ckvn) + ones).astype(BF16)

    g_ref[...] = _sigmoid(proj(_C_G, w1_ref.shape[1]) + bg_ref[...]).astype(BF16)


def _front(x2, gattn, w1, wvt, bg, qng, wuq, kvng, wkv, wkvt, cosm, sinm, *, seq, tm):
    n, d = x2.shape
    batch = n // seq
    hw = MLA_HEADS * LANES
    nblk = n // MOBA_BLOCK
    steps_per_seq = seq // tm
    row = lambda i: (i, 0)
    tcol = lambda i: (i // steps_per_seq, 0, i % steps_per_seq)
    out_shape = (
        jax.ShapeDtypeStruct((n, MOBA_WIDTH), BF16),
        jax.ShapeDtypeStruct((n, MOBA_WIDTH), BF16),
        jax.ShapeDtypeStruct((nblk, 1, MOBA_WIDTH), F32),
        jax.ShapeDtypeStruct((batch, MOBA_HEADS * V_ROWS, seq), BF16),
        jax.ShapeDtypeStruct((n, hw), BF16),
        jax.ShapeDtypeStruct((n, hw), BF16),
        jax.ShapeDtypeStruct((batch, MLA_HEADS * V_ROWS, seq), BF16),
        jax.ShapeDtypeStruct((n, N_BRANCH * d), BF16),
    )
    in_specs = [
        pl.BlockSpec((tm, d), row),
        _resident(gattn.shape), _resident(w1.shape), _resident(wvt.shape), _resident(bg.shape),
        _resident(qng.shape), _resident(wuq.shape), _resident(kvng.shape), _resident(wkv.shape),
        _resident(wkvt.shape),
        pl.BlockSpec((tm, LANES), lambda i: (i % steps_per_seq, 0)),
        pl.BlockSpec((tm, LANES), lambda i: (i % steps_per_seq, 0)),
    ]
    out_specs = (
        pl.BlockSpec((tm, MOBA_WIDTH), row),
        pl.BlockSpec((tm, MOBA_WIDTH), row),
        pl.BlockSpec((tm // MOBA_BLOCK, 1, MOBA_WIDTH), lambda i: (i, 0, 0)),
        pl.BlockSpec((None, MOBA_HEADS * V_ROWS, tm), tcol),
        pl.BlockSpec((tm, hw), row),
        pl.BlockSpec((tm, hw), row),
        pl.BlockSpec((None, MLA_HEADS * V_ROWS, tm), tcol),
        pl.BlockSpec((tm, N_BRANCH * d), row),
    )
    return pl.pallas_call(
        functools.partial(_front_kernel, mla_scale=MLA_QK_DIM ** -0.5 * LOG2E),
        grid=(n // tm,),
        in_specs=in_specs,
        out_specs=out_specs,
        out_shape=out_shape,
        compiler_params=pltpu.CompilerParams(
            dimension_semantics=("arbitrary",), vmem_limit_bytes=VMEM_LIMIT_BYTES),
        name="front",
    )(x2, gattn, w1, wvt, bg, qng, wuq, kvng, wkv, wkvt, cosm, sinm)


HEADS_PER_STEP = 8
BIG = 1e30
RESCALE_SLACK = 64.0
MATMUL_LOOKAHEAD = 2


def _online_softmax_group(m_ref, acc_ref, score_blocks, vt_grp, chosen=None, shifts=None):
    n = len(score_blocks)
    chosen = chosen or [None] * n
    shifts = shifts or [None] * n
    m_old = m_ref[...]
    m_new = m_old
    for st, ch, sh in zip(score_blocks, chosen, shifts):
        rm = jnp.max(st, axis=0, keepdims=True)
        if sh is not None:
            rm = rm + sh
        m_new = jnp.maximum(m_new, rm if ch is None else jnp.where(ch, rm, NEG))
    probs = []
    for st, ch, sh in zip(score_blocks, chosen, shifts):
        off = m_new if sh is None else m_new - sh
        if ch is not None:
            off = jnp.where(ch, off, BIG)
        probs.append(jnp.exp2(st - off).astype(BF16))
    ot = _dot(vt_grp, jnp.concatenate(probs, axis=0))
    acc_ref[...] = jnp.exp2(m_old - m_new) * acc_ref[...] + ot
    m_ref[...] = m_new


def _streamed_softmax_group(m_ref, acc_ref, late_ref, score_blocks, vt_grp, chosen=None, shifts=None):
    n = len(score_blocks)
    chosen = chosen or [None] * n
    shifts = shifts or [None] * n
    m_old = m_ref[...]
    seen = jnp.full(m_old.shape, NEG, F32)
    probs = []
    for st, ch, sh in zip(score_blocks, chosen, shifts):
        off = m_old if sh is None else m_old - sh
        if ch is not None:
            off = jnp.where(ch, off, BIG)
        probs.append(jnp.exp2(st - off).astype(BF16))
        rm = jnp.max(st, axis=0, keepdims=True)
        if sh is not None:
            rm = rm + sh
        seen = jnp.maximum(seen, rm if ch is None else jnp.where(ch, rm, NEG))
    acc_ref[...] = acc_ref[...] + _dot(vt_grp, jnp.concatenate(probs, axis=0))
    late_ref[...] = jnp.where(seen > m_old + RESCALE_SLACK, 1.0, late_ref[...])


def _pipelined(stages, issue, consume, lookahead):
    queue = []
    for n in range(len(stages) + lookahead):
        if n < len(stages):
            queue.append((stages[n], issue(stages[n])))
        if n >= lookahead:
            consume(*queue.pop(0))


def _in_trips(n, sweep, trip_groups):
    def trip(t, carry):
        sweep([trip_groups * t + i for i in range(trip_groups)])
        return carry

    lax.fori_loop(0, n // trip_groups, trip, 0)
    done = (n // trip_groups) * trip_groups
    size = trip_groups // 2
    while size >= 1:
        has = ((n - done) // size) % 2 == 1
        first = done + ((n - done) // (2 * size)) * (2 * size)
        pl.when(has)(functools.partial(sweep, [first + i for i in range(size)]))
        size //= 2


def _reset_softmax_state(m_s, acc_s):
    m_s[...] = jnp.full(m_s.shape, NEG, F32)
    acc_s[...] = jnp.zeros(acc_s.shape, F32)


def _normalised_heads(acc_s, v_dim):
    rows = []
    for hh in range(HEADS_PER_STEP):
        acc = acc_s[hh]
        rows.append(acc[0:v_dim, :] * (1.0 / acc[v_dim:v_dim + 1, :]))
    return jnp.concatenate(rows, axis=0)


def _moba_kernel(q_ref, k_new_ref, vt_new_ref, kmean_ref, near_ref, cfar_ref, o_ref,
                 m_s, acc_s, late_s, sel_s, k_ref, vt_ref, *, unroll, trip):
    blk = MOBA_BLOCK
    qb = pl.program_id(2)
    nb = kmean_ref.shape[0]
    lane = lax.broadcasted_iota(jnp.int32, (blk, LANES), 1)
    blk_id = lax.broadcasted_iota(jnp.int32, (nb, blk), 0).astype(F32)
    qbf = qb.astype(F32)
    has_prev = qb >= 1
    _reset_softmax_state(m_s, acc_s)

    @pl.when(qb == 0)
    def _():
        k_ref[...] = jnp.zeros_like(k_ref)
        vt_ref[...] = jnp.zeros_like(vt_ref)

    own = pl.ds(pl.multiple_of(qb * blk, blk), blk)
    k_ref[own, :] = k_new_ref[...]
    vt_ref[:, own] = vt_new_ref[...]

    def lane_group(hh):
        return slice((hh // 2) * LANES, (hh // 2 + 1) * LANES)

    def k_block(hh, j):
        return k_ref[pl.ds(pl.multiple_of(j * blk, blk), blk), lane_group(hh)]

    def vt_blocks(hh, j, n):
        return vt_ref[hh * V_ROWS:(hh + 1) * V_ROWS, pl.ds(pl.multiple_of(j * blk, blk), n * blk)]

    q_heads, prev_chosen = [], []
    for hh in range(HEADS_PER_STEP):
        in_head = (lane >= (hh % 2) * MOBA_HEAD_DIM) & (lane < (hh % 2 + 1) * MOBA_HEAD_DIM)
        qh = jnp.where(in_head, q_ref[:, lane_group(hh)].astype(F32), 0.0).astype(BF16)
        q_heads.append(qh)

        kmean = kmean_ref[:, lane_group(hh)].astype(BF16)
        gate = jnp.where(blk_id < qbf, _dot_nt(kmean, qh), -jnp.inf)
        sel = jnp.zeros((nb, blk), F32)
        for _ in range(min(MOBA_TOPK, nb)):
            best = jnp.max(gate, axis=0, keepdims=True)
            first = jnp.min(jnp.where(gate == best, blk_id, float(nb)), axis=0, keepdims=True)
            pick = blk_id == first
            sel = jnp.where(pick, 1.0, sel)
            gate = jnp.where(pick, -jnp.inf, gate)
        prev_chosen.append(jnp.max(jnp.where(blk_id == qbf - 1.0, sel, 0.0), axis=0, keepdims=True))
        sel_s[hh] = jnp.where(blk_id < qbf - 1.0, sel, 0.0)

    lo = jnp.maximum(qb - 1, 0)
    tab = jnp.where(has_prev, 0, 1)

    def nearest_exact():
        for first in range(0, HEADS_PER_STEP, HEADS_PER_STEP // 2):
            heads = range(first, first + HEADS_PER_STEP // 2)
            scores = {hh: [_dot_nt(k_block(hh, lo + u), q_heads[hh]) + near_ref[hh, tab + u]
                           for u in range(2)] for hh in heads}
            for hh in heads:
                chosen = [jnp.where(has_prev, prev_chosen[hh], 1.0) > 0.5,
                          jnp.where(has_prev, jnp.ones((1, blk), F32), 0.0) > 0.5]
                _online_softmax_group(m_s.at[hh], acc_s.at[hh], scores[hh], vt_blocks(hh, lo, 2), chosen)

    n_groups = (qb - 1 + unroll - 1) // unroll

    def far_sweep(groups, streamed=True):
        def start_of(g):
            return jnp.minimum(g * unroll, nb - unroll)

        def scores(stage):
            g, hh = stage
            keys = k_ref[pl.ds(pl.multiple_of(start_of(g) * blk, blk), unroll * blk), lane_group(hh)]
            st = _dot_nt(keys, q_heads[hh])
            return [st[u * blk:(u + 1) * blk, :] for u in range(unroll)]

        def softmax(stage, blocks):
            g, hh = stage
            start = start_of(g)
            chosen = [jnp.where(start + u >= g * unroll, sel_s[hh, pl.ds(start + u, 1), :], 0.0) > 0.5
                      for u in range(unroll)]
            operands = (blocks, vt_blocks(hh, start, unroll), chosen, [cfar_ref[hh][:, 0:1]] * unroll)
            if streamed:
                _streamed_softmax_group(m_s.at[hh], acc_s.at[hh], late_s.at[hh], *operands)
            else:
                _online_softmax_group(m_s.at[hh], acc_s.at[hh], *operands)

        order = list(range(0, HEADS_PER_STEP, 2)) + list(range(1, HEADS_PER_STEP, 2))
        stages = [(g, hh) for g in groups for hh in order]
        _pipelined(stages, scores, softmax, MATMUL_LOOKAHEAD if streamed else len(stages))

    late_s[...] = jnp.zeros_like(late_s)
    nearest_exact()
    _in_trips(n_groups, far_sweep, trip)

    @pl.when(jnp.max(late_s[...]) > 0.5)
    def _():
        _reset_softmax_state(m_s, acc_s)
        nearest_exact()

        def exact(g, carry):
            far_sweep([g], streamed=False)
            return carry

        lax.fori_loop(0, n_groups, exact, 0)

    o_ref[...] = _normalised_heads(acc_s, MOBA_HEAD_DIM).T.astype(o_ref.dtype)


def _moba(qa, ka, vat, kmean, near, cfar, *, batch, seq, unroll, trip):
    n = qa.shape[0]
    blk = MOBA_BLOCK
    nb = seq // blk
    assert nb >= unroll and nb >= 2
    steps = MOBA_HEADS // HEADS_PER_STEP
    width = HEADS_PER_STEP * MOBA_HEAD_DIM
    return pl.pallas_call(
        functools.partial(_moba_kernel, unroll=unroll, trip=trip),
        grid=(batch, steps, nb),
        in_specs=[
            pl.BlockSpec((blk, width), lambda b, p, i: (b * nb + i, p)),
            pl.BlockSpec((blk, width), lambda b, p, i: (b * nb + i, p)),
            pl.BlockSpec((None, HEADS_PER_STEP * V_ROWS, blk), lambda b, p, i: (b, p, i)),
            pl.BlockSpec((None, nb, width), lambda b, p, i: (b, 0, p)),
            _per_head_group((HEADS_PER_STEP, 3, blk, blk), lambda b, p, i: (p, 0, 0, 0)),
            pl.BlockSpec((HEADS_PER_STEP, 1, LANES), lambda b, p, i: (p, 0, 0)),
        ],
        out_specs=pl.BlockSpec((blk, width), lambda b, p, i: (b * nb + i, p)),
        out_shape=jax.ShapeDtypeStruct((n, MOBA_WIDTH), BF16),
        scratch_shapes=[
            pltpu.VMEM((HEADS_PER_STEP, 1, blk), F32),
            pltpu.VMEM((HEADS_PER_STEP, V_ROWS, blk), F32),
            pltpu.VMEM((HEADS_PER_STEP, 1, blk), F32),
            pltpu.VMEM((HEADS_PER_STEP, nb, blk), F32),
            pltpu.VMEM((seq, width), BF16),
            pltpu.VMEM((HEADS_PER_STEP * V_ROWS, seq), BF16),
        ],
        compiler_params=pltpu.CompilerParams(
            dimension_semantics=("arbitrary", "arbitrary", "arbitrary"),
            vmem_limit_bytes=VMEM_LIMIT_BYTES),
        name="moba",
    )(qa, ka, vat, kmean, near, cfar)


def _mla_kernel(q_ref, k_new_ref, vt_new_ref, o_ref, m_s, acc_s, late_s, k_ref, vt_ref,
                *, tq, tk, unroll, trip):
    qi = pl.program_id(2)
    n_diag = tq // tk
    key = lax.broadcasted_iota(jnp.int32, (tk, tq), 0)
    qry = lax.broadcasted_iota(jnp.int32, (tk, tq), 1)
    q_heads = [q_ref[:, hh * LANES:(hh + 1) * LANES] for hh in range(HEADS_PER_STEP)]

    own = pl.ds(pl.multiple_of(qi * tq, tq), tq)
    k_ref[own, :] = k_new_ref[...]
    vt_ref[:, own] = vt_new_ref[...]

    _reset_softmax_state(m_s, acc_s)

    def k_tile(hh, j):
        return k_ref[pl.ds(pl.multiple_of(j * tk, tk), tk), hh * LANES:(hh + 1) * LANES]

    def vt_tiles(hh, j, n):
        return vt_ref[hh * V_ROWS:(hh + 1) * V_ROWS, pl.ds(pl.multiple_of(j * tk, tk), n * tk)]

    def group(first, n, masked):
        scores = [[_dot_nt(k_tile(hh, first + u), q_heads[hh]) for u in range(n)]
                  for hh in range(HEADS_PER_STEP)]
        for hh in range(HEADS_PER_STEP):
            blocks = scores[hh]
            if masked:
                blocks = [jnp.where(key + u * tk <= qry, st, NEG) for u, st in enumerate(blocks)]
            _online_softmax_group(m_s.at[hh], acc_s.at[hh], blocks, vt_tiles(hh, first, n))

    n_visible = qi * n_diag
    n_groups = (n_visible + unroll - 1) // unroll

    def visible_sweep(groups):
        def start_of(g):
            return jnp.maximum(jnp.minimum(g * unroll, n_visible - unroll), 0)

        def scores(stage):
            g, hh = stage
            start = pl.multiple_of(start_of(g) * tk, tk)
            st = _dot_nt(k_ref[pl.ds(start, unroll * tk), hh * LANES:(hh + 1) * LANES], q_heads[hh])
            return [st[u * tk:(u + 1) * tk, :] for u in range(unroll)]

        def softmax(stage, blocks):
            g, hh = stage
            start = start_of(g)
            chosen = None
            if unroll > 1:
                chosen = [jnp.where((start + u >= g * unroll) & (start + u < n_visible),
                                    jnp.ones((1, tq), F32), 0.0) > 0.5 for u in range(unroll)]
            _streamed_softmax_group(m_s.at[hh], acc_s.at[hh], late_s.at[hh], blocks,
                                    vt_tiles(hh, start, unroll), chosen)

        stages = [(g, hh) for g in groups for hh in range(HEADS_PER_STEP)]
        _pipelined(stages, scores, softmax, MATMUL_LOOKAHEAD)

    late_s[...] = jnp.zeros_like(late_s)
    group(qi * n_diag, n_diag, True)
    _in_trips(n_groups, visible_sweep, trip)

    @pl.when(jnp.max(late_s[...]) > 0.5)
    def _():
        _reset_softmax_state(m_s, acc_s)
        group(qi * n_diag, n_diag, True)

        def exact(j, carry):
            group(j, 1, False)
            return carry

        lax.fori_loop(0, n_visible, exact, 0)

    o_ref[...] = _normalised_heads(acc_s, MLA_V_DIM).T.astype(o_ref.dtype)


def _mla(qm, km, vmt, *, batch, seq, tq, tk, unroll, trip):
    n = qm.shape[0]
    nq = seq // tq
    assert tq % tk == 0 and seq // tk >= unroll
    pairs = MLA_HEADS // HEADS_PER_STEP
    return pl.pallas_call(
        functools.partial(_mla_kernel, tq=tq, tk=tk, unroll=unroll, trip=trip),
        grid=(batch, pairs, nq),
        in_specs=[
            pl.BlockSpec((tq, HEADS_PER_STEP * LANES), lambda b, p, i: (b * nq + i, p)),
            pl.BlockSpec((tq, HEADS_PER_STEP * LANES), lambda b, p, i: (b * nq + i, p)),
            pl.BlockSpec((None, HEADS_PER_STEP * V_ROWS, tq), lambda b, p, i: (b, p, i)),
        ],
        out_specs=pl.BlockSpec((tq, HEADS_PER_STEP * MLA_V_DIM), lambda b, p, i: (b * nq + i, p)),
        out_shape=jax.ShapeDtypeStruct((n, MLA_WIDTH), BF16),
        scratch_shapes=[
            pltpu.VMEM((HEADS_PER_STEP, 1, tq), F32),
            pltpu.VMEM((HEADS_PER_STEP, V_ROWS, tq), F32),
            pltpu.VMEM((HEADS_PER_STEP, 1, tq), F32),
            pltpu.VMEM((seq, HEADS_PER_STEP * LANES), BF16),
            pltpu.VMEM((HEADS_PER_STEP * V_ROWS, seq), BF16),
        ],
        compiler_params=pltpu.CompilerParams(
            dimension_semantics=("arbitrary", "arbitrary", "arbitrary"),
            vmem_limit_bytes=VMEM_LIMIT_BYTES),
        name="mla",
    )(qm, km, vmt)


def _back_kernel(x_ref, ya_ref, yb_ref, g_ref, wa_ref, wb_ref, wo_ref, gffn_ref, wup_ref,
                 cw_ref, cb_ref, wdn_ref, gfin_ref, o_ref,
                 carry_ref, act_ref, hn_ref, *, steps_per_seq, d_ff, fc, final):
    tm, d = x_ref.shape
    halo = SUBLANES

    @pl.when(pl.program_id(0) % steps_per_seq == 0)
    def _():
        carry_ref[...] = jnp.zeros_like(carry_ref)

    g = g_ref[...].astype(F32)
    mixed = g[:, :d] * _dot(ya_ref[...], wa_ref[...]) + g[:, d:] * _dot(yb_ref[...], wb_ref[...])
    h1 = x_ref[...] + _dot(mixed.astype(BF16), wo_ref[...])
    hn_ref[...] = _rms(h1, gffn_ref[...]).astype(BF16)

    def up_conv(col0):
        cols = slice(col0, col0 + fc)
        u = _dot(hn_ref[...], wup_ref[:, cols])
        rows = jnp.concatenate([carry_ref[:, cols], u], axis=0)
        carry_ref[:, cols] = u[tm - halo:tm, :]
        w = cw_ref[:, cols]
        y = cb_ref[:, cols]
        for t in range(CONV_WIDTH):
            back = CONV_WIDTH - 1 - t
            shifted = pltpu.roll(rows, back, axis=0) if back else rows
            y = y + w[t:t + 1, :] * shifted[halo:, :]
        return y

    for c in range(d_ff // fc):
        yg = up_conv(c * fc)
        yv = up_conv(d_ff + c * fc)
        act_ref[:, c * fc:(c + 1) * fc] = (yg * _sigmoid(yg) * yv).astype(BF16)

    h2 = h1 + _dot(act_ref[...], wdn_ref[...])
    o_ref[...] = _rms(h2, gfin_ref[...]) if final else h2


def _back(x2, ya, yb, g, wa, wb, wo, gffn, wup, cw, cb, wdn, gfin, *, seq, tm, fc, final):
    n, d = x2.shape
    d_ff = wdn.shape[0]
    row = lambda i: (i, 0)
    return pl.pallas_call(
        functools.partial(_back_kernel, steps_per_seq=seq // tm, d_ff=d_ff, fc=fc, final=final),
        grid=(n // tm,),
        in_specs=[
            pl.BlockSpec((tm, d), row),
            pl.BlockSpec((tm, ya.shape[1]), row),
            pl.BlockSpec((tm, yb.shape[1]), row),
            pl.BlockSpec((tm, g.shape[1]), row),
            _resident(wa.shape), _resident(wb.shape), _resident(wo.shape), _resident(gffn.shape),
            _resident(wup.shape), _resident(cw.shape), _resident(cb.shape), _resident(wdn.shape),
            _resident(gfin.shape),
        ],
        out_specs=pl.BlockSpec((tm, d), row),
        out_shape=jax.ShapeDtypeStruct((n, d), F32),
        scratch_shapes=[
            pltpu.VMEM((SUBLANES, 2 * d_ff), F32),
            pltpu.VMEM((tm, d_ff), BF16),
            pltpu.VMEM((tm, d), BF16),
        ],
        compiler_params=pltpu.CompilerParams(
            dimension_semantics=("arbitrary",), vmem_limit_bytes=VMEM_LIMIT_BYTES),
        name="back",
    )(x2, ya, yb, g, wa, wb, wo, gffn, wup, cw, cb, wdn, gfin)


def _t5_bucket_np(rel):
    n = np.maximum(rel, 0)
    max_exact = REL_BUCKETS // 2
    nf = np.maximum(n, 1).astype(np.float32)
    large = max_exact + (np.log(nf / np.float32(max_exact)) / np.float32(math.log(REL_MAX_DIST / max_exact))
                         * np.float32(REL_BUCKETS - max_exact)).astype(np.int32)
    large = np.minimum(large, REL_BUCKETS - 1)
    return np.where(n < max_exact, n, large)


def _moba_bias_tables(rel_bias):
    blk = MOBA_BLOCK
    period = 2 * blk
    assert REL_MAX_DIST <= blk + 1
    bias_h = rel_bias.T.astype(F32)
    slot = np.arange(period, dtype=np.int32)
    dist = np.where(slot < blk, slot, slot - period)

    def lookup(bucket):
        onehot = jnp.asarray(bucket)[None, :] == jnp.arange(REL_BUCKETS)[:, None]
        return jnp.sum(jnp.where(onehot[None], bias_h[:, :, None], 0.0), axis=1)

    def toeplitz(v):
        tiled = jnp.tile(v, (1, blk))[:, :blk * (period - 1)]
        return tiled.reshape(v.shape[0], blk, period - 1)[:, :, :blk]

    town = toeplitz(jnp.where(jnp.asarray(dist >= 0), lookup(_t5_bucket_np(dist)) * LOG2E, NEG))
    tprev = toeplitz(lookup(_t5_bucket_np(dist + blk)) * LOG2E)
    near = jnp.stack([tprev, town, jnp.zeros_like(town)], axis=1)
    cfar = bias_h[:, int(_t5_bucket_np(np.int32(blk + 1)))] * LOG2E
    cfar = jnp.broadcast_to(cfar[:, None, None], (MOBA_HEADS, 1, LANES))
    return near, cfar


def _pad_heads(w, n_heads, width, padded=LANES):
    r = w.shape[0]
    w = w.reshape(r, n_heads, width)
    out = jnp.zeros((r, n_heads, padded), w.dtype).at[:, :, :width].set(w)
    return out.reshape(r, n_heads * padded)


def _rope_slot(w_rope):
    r = w_rope.shape[0]
    return jnp.zeros((r, LANES), w_rope.dtype).at[:, MLA_NOPE_DIM:MLA_NOPE_DIM + MLA_ROPE_DIM].set(w_rope)


def _front_weights(w_in, w_uq, w_ukv):
    d = w_in.shape[0]
    s0 = MOBA_WIDTH
    s1 = 2 * MOBA_WIDTH
    s2 = 3 * MOBA_WIDTH
    s3 = s2 + MLA_Q_RANK
    s4 = s3 + MLA_KV_RANK
    s5 = s4 + MLA_ROPE_DIM
    w1 = jnp.concatenate([
        w_in[:, :s0], w_in[:, s0:s1],
        w_in[:, s2:s3], w_in[:, s3:s4],
        _rope_slot(w_in[:, s4:s5]),
        w_in[:, s5:],
    ], axis=1).astype(BF16)
    assert w1.shape == (d, _C_G + N_BRANCH * d)
    wvt = _pad_heads(w_in[:, s1:s2], MOBA_HEADS, MOBA_HEAD_DIM, V_ROWS).T.astype(BF16)

    wuq = _pad_heads(w_uq, MLA_HEADS, MLA_QK_DIM).astype(BF16)

    r = w_ukv.shape[0]
    ukv = w_ukv.reshape(r, MLA_HEADS, MLA_NOPE_DIM + MLA_V_DIM)
    wk = _pad_heads(ukv[:, :, :MLA_NOPE_DIM].reshape(r, -1), MLA_HEADS, MLA_NOPE_DIM).astype(BF16)
    wkvt = _pad_heads(ukv[:, :, MLA_NOPE_DIM:].reshape(r, -1), MLA_HEADS, MLA_V_DIM, V_ROWS).T.astype(BF16)
    return w1, wvt, wuq, wk, wkvt


def _rope_lane_tables(seq):
    dim = MLA_ROPE_DIM
    inv_freq = ROPE_THETA ** (-jnp.arange(0, dim, 2, dtype=F32) / dim)
    ang = jnp.arange(seq, dtype=F32)[:, None] * inv_freq[None, :]
    cos, sin = jnp.cos(ang), jnp.sin(ang)
    tail = jnp.zeros((seq, LANES - MLA_QK_DIM), F32)
    cosm = jnp.concatenate([jnp.ones((seq, MLA_NOPE_DIM), F32), cos, cos, tail], axis=1)
    sinm = jnp.concatenate([jnp.zeros((seq, MLA_NOPE_DIM), F32), -sin, sin, tail], axis=1)
    return cosm, sinm


class _Tiles:
    def __init__(self, seq):
        self.tm = 512 if seq % 512 == 0 else MOBA_BLOCK
        self.fc = 256
        self.mla_tq = 256
        self.mla_tk = 256
        self.mla_unroll = 2
        self.mla_trip = 8
        self.moba_unroll = 2
        self.moba_trip = 8


def kernel(x, norm_attn_g, w_in, b_gate, q_norm_g, w_uq, kv_norm_g, w_ukv, rel_bias,
           w_branch_moba, w_branch_mla, w_out, norm_ffn_g, w_up, conv_w, conv_b, w_down,
           norm_final_g):
    batch, seq, d = x.shape
    depth = w_in.shape[0]
    assert seq % MOBA_BLOCK == 0
    t = _Tiles(seq)
    d_ff = w_down.shape[1]
    assert d_ff % t.fc == 0
    n = batch * seq

    cosm, sinm = _rope_lane_tables(seq)
    near, cfar = _moba_bias_tables(rel_bias)
    row = lambda v: v.reshape(1, -1).astype(F32)

    h = x.reshape(n, d)
    for l in range(depth):
        w1, wvt, wuq, wk, wkvt = _front_weights(w_in[l], w_uq[l], w_ukv[l])
        qa, ka, kmean, vat, qm, km, vmt, g = _front(
            h, row(norm_attn_g[l]), w1, wvt, row(b_gate[l]), row(q_norm_g[l]), wuq,
            row(kv_norm_g[l]), wk, wkvt, cosm, sinm, seq=seq, tm=t.tm)
        kmean = kmean.reshape(batch, seq // MOBA_BLOCK, MOBA_WIDTH)
        ya = _moba(qa, ka, vat, kmean, near, cfar, batch=batch, seq=seq,
                   unroll=t.moba_unroll, trip=t.moba_trip)
        yb = _mla(qm, km, vmt, batch=batch, seq=seq, tq=t.mla_tq, tk=t.mla_tk,
                  unroll=t.mla_unroll, trip=t.mla_trip)
        h = _back(h, ya, yb, g, w_branch_moba[l].astype(BF16), w_branch_mla[l].astype(BF16),
                  w_out[l].astype(BF16), row(norm_ffn_g[l]), w_up[l].astype(BF16),
                  conv_w[l].astype(F32), row(conv_b[l]), w_down[l].astype(BF16),
                  row(norm_final_g), seq=seq, tm=t.tm, fc=t.fc, final=(l == depth - 1))
    return h.reshape(batch, seq, d)
```
